```python
import math
import jax
import jax.numpy as jnp
from jax import lax
import numpy as np


D_MODEL = 4096
BATCH = 2
SEQ = 4096
DEPTH = 2

HEAD_DIM = 128
ROPE_THETA = 10000.0
BLOCK_Q = 128
LN_EPS = 1e-5

DIL_GROUPS = ((128, 1), (512, 4), (2048, 16))
A_SLOTS = 4
A_HEADS = A_SLOTS * len(DIL_GROUPS)

B_HEADS = 12
B_KV_HEADS = 4
B_GROUP = B_HEADS // B_KV_HEADS
CMP_LEN = 32
CMP_STRIDE = 16
SEL_LEN = 64
SEL_TOPK = 16
WIN_LEN = 512
SEL_Q_BLOCK = 32

C_HEADS = 8

D_FF = 4 * D_MODEL
ALPHA = (2.0 * DEPTH) ** 0.25
BETA = (8.0 * DEPTH) ** -0.25

A_Q = A_HEADS * HEAD_DIM
A_KV = A_SLOTS * HEAD_DIM
B_Q = B_HEADS * HEAD_DIM
B_KV = B_KV_HEADS * HEAD_DIM
B_GATE = 3 * B_HEADS
C_W = C_HEADS * HEAD_DIM
IN_SPLITS = (A_Q, A_KV, A_KV,
             B_Q, B_KV, B_KV, B_KV, B_KV, B_KV, B_KV, B_GATE,
             C_W, C_W, C_W,
             D_MODEL, D_MODEL, D_MODEL)
IN_WIDTH = sum(IN_SPLITS)

kernel_name = 'hybrid_dilated_nsa_stickbreaking_block'


def split_columns(t):
    parts, start = [], 0
    for w in IN_SPLITS:
        parts.append(t[..., start:start + w])
        start += w
    return parts


def layer_norm(x, g, b):
    xf = x.astype(jnp.float32)
    mu = jnp.mean(xf, axis=-1, keepdims=True)
    var = jnp.mean(jnp.square(xf - mu), axis=-1, keepdims=True)
    return ((xf - mu) * lax.rsqrt(var + LN_EPS)).astype(x.dtype) * g + b


def rope(x, positions):
    half = HEAD_DIM // 2
    inv = 1.0 / (ROPE_THETA ** (jnp.arange(half, dtype=jnp.float32) / half))
    ang = positions.astype(jnp.float32)[:, None] * inv[None, :]
    cos = jnp.cos(ang).astype(x.dtype)
    sin = jnp.sin(ang).astype(x.dtype)
    x1, x2 = x[..., :half], x[..., half:]
    return jnp.concatenate([x1 * cos - x2 * sin, x1 * sin + x2 * cos], axis=-1)


def banded_attention(q, k, v, window):
    B, Hk, G, L, dh = q.shape
    nb = -(-L // BLOCK_Q)
    pad_t = nb * BLOCK_Q - L
    nprev = -(-window // BLOCK_Q)
    qp = jnp.pad(q * (dh ** -0.5), ((0, 0), (0, 0), (0, 0), (0, pad_t), (0, 0)))
    qb = qp.reshape(B, Hk, G, nb, BLOCK_Q, dh)
    kp = jnp.pad(k, ((0, 0), (0, 0), (nprev * BLOCK_Q, pad_t), (0, 0))).reshape(B, Hk, nb + nprev, BLOCK_Q, dh)
    vp = jnp.pad(v, ((0, 0), (0, 0), (nprev * BLOCK_Q, pad_t), (0, 0))).reshape(B, Hk, nb + nprev, BLOCK_Q, dh)
    kb = jnp.concatenate([kp[:, :, i:i + nb] for i in range(nprev + 1)], axis=3)
    vb = jnp.concatenate([vp[:, :, i:i + nb] for i in range(nprev + 1)], axis=3)
    s = jnp.einsum('bkgnqd,bknsd->bkgnqs', qb, kb).astype(jnp.float32)
    qpos = jnp.arange(nb)[:, None] * BLOCK_Q + jnp.arange(BLOCK_Q)[None, :]
    kpos = (jnp.arange(nb)[:, None] - nprev) * BLOCK_Q + jnp.arange((nprev + 1) * BLOCK_Q)[None, :]
    dist = qpos[:, :, None] - kpos[:, None, :]
    mask = (dist >= 0) & (dist <= window) & (kpos[:, None, :] >= 0)
    s = jnp.where(mask, s, -jnp.inf)
    lse = jax.nn.logsumexp(s, axis=-1)
    p = jnp.exp(s - lse[..., None])
    o = jnp.einsum('bkgnqs,bknsd->bkgnqd', p.astype(v.dtype), vb)
    o = o.reshape(B, Hk, G, nb * BLOCK_Q, dh)[:, :, :, :L]
    lse = lse.reshape(B, Hk, G, nb * BLOCK_Q)[:, :, :, :L]
    return o, lse


def dilated_mixer(q, k, v):
    B, _, H, T, dh = q.shape
    outs, lses = [], []
    for g, (window, d) in enumerate(DIL_GROUPS):
        L = T // d

        def by_stride(t):
            return t.reshape(B, H, L, d, dh).transpose(0, 1, 3, 2, 4).reshape(B, H * d, L, dh)

        o, lse = banded_attention(by_stride(q[:, g])[:, :, None], by_stride(k), by_stride(v), window // d)
        outs.append(o[:, :, 0].reshape(B, H, d, L, dh).transpose(0, 1, 3, 2, 4).reshape(B, H, T, dh))
        lses.append(lse[:, :, 0].reshape(B, H, d, L).transpose(0, 1, 3, 2).reshape(B, H, T))
    w = jax.nn.softmax(jnp.stack(lses, axis=0), axis=0)
    return jnp.einsum('gbht,gbhtd->bhtd', w.astype(q.dtype), jnp.stack(outs, axis=0))


def compress_blocks(t, pe, w1, w2):
    B, Hk, T, dh = t.shape
    ratio = CMP_LEN // CMP_STRIDE
    chunks = t.reshape(B, Hk, T // CMP_STRIDE, CMP_STRIDE, dh)
    nc = T // CMP_STRIDE - ratio + 1
    blocks = jnp.concatenate([chunks[:, :, i:i + nc] for i in range(ratio)], axis=3)
    flat = (blocks + pe).reshape(B, Hk, nc, CMP_LEN * dh)
    return jax.nn.gelu(flat @ w1) @ w2


def nsa_mixer(q, k_cmp, v_cmp, k_sel, v_sel, k_win, v_win, gates,
              pe_k, wk1, wk2, pe_v, wv1, wv2):
    B, Hk, G, T, dh = q.shape
    scale = dh ** -0.5
    qs = q * scale
    t_pos = jnp.arange(T)

    kc = compress_blocks(k_cmp, pe_k, wk1, wk2)
    vc = compress_blocks(v_cmp, pe_v, wv1, wv2)
    nc = kc.shape[2]
    c_start = jnp.arange(nc) * CMP_STRIDE
    c_end = c_start + CMP_LEN - 1
    s = jnp.einsum('bkgtd,bknd->bkgtn', qs, kc).astype(jnp.float32)
    s = jnp.where(c_end[None, :] <= t_pos[:, None], s, -jnp.inf)
    m = jnp.max(s, axis=-1, keepdims=True)
    m = jnp.where(jnp.isfinite(m), m, 0.0)
    e = jnp.exp(s - m)
    den = jnp.sum(e, axis=-1, keepdims=True)
    p_cmp = e / jnp.where(den > 0, den, 1.0)
    o_cmp = jnp.einsum('bkgtn,bknd->bkgtd', p_cmp.astype(vc.dtype), vc)

    ns = T // SEL_LEN
    s_start = jnp.arange(ns) * SEL_LEN
    overlap = ((c_start[:, None] <= s_start[None, :] + SEL_LEN - 1) &
               (c_end[:, None] >= s_start[None, :])).astype(jnp.float32)
    imp = jnp.einsum('bktn,nj->bktj', jnp.sum(p_cmp, axis=2), overlap)
    cur = t_pos // SEL_LEN
    j = jnp.arange(ns)
    forced = (j[None, :] == 0) | (j[None, :] == cur[:, None]) | (j[None, :] == cur[:, None] - 1)
    valid = j[None, :] <= cur[:, None]
    imp = jnp.where(forced, jnp.inf, jnp.where(valid, imp, -jnp.inf))
    n_top = min(SEL_TOPK, ns)
    _, sel_idx = lax.top_k(imp, n_top)

    k_blocks = k_sel.reshape(B, Hk, ns, SEL_LEN, dh)
    v_blocks = v_sel.reshape(B, Hk, ns, SEL_LEN, dh)
    bi = jnp.arange(B)[:, None, None]
    hi = jnp.arange(Hk)[None, :, None]

    def sel_block(i):
        t0 = i * SEL_Q_BLOCK
        qi = lax.dynamic_slice_in_dim(qs, t0, SEL_Q_BLOCK, axis=3)
        idx = lax.dynamic_slice_in_dim(sel_idx, t0, SEL_Q_BLOCK, axis=2)
        flat = idx.reshape(B, Hk, SEL_Q_BLOCK * n_top)
        kg = k_blocks[bi, hi, flat].reshape(B, Hk, SEL_Q_BLOCK, n_top * SEL_LEN, dh)
        vg = v_blocks[bi, hi, flat].reshape(B, Hk, SEL_Q_BLOCK, n_top * SEL_LEN, dh)
        sc = jnp.einsum('bkgqd,bkqsd->bkgqs', qi, kg).astype(jnp.float32)
        kpos = (idx[..., None] * SEL_LEN + jnp.arange(SEL_LEN)).reshape(B, Hk, SEL_Q_BLOCK, n_top * SEL_LEN)
        qpos = t0 + jnp.arange(SEL_Q_BLOCK)
        mask = kpos <= qpos[None, None, :, None]
        sc = jnp.where(mask[:, :, None], sc, -jnp.inf)
        p = jax.nn.softmax(sc, axis=-1)
        return jnp.einsum('bkgqs,bkqsd->bkgqd', p.astype(vg.dtype), vg)

    o_sel = lax.map(sel_block, jnp.arange(T // SEL_Q_BLOCK))
    o_sel = jnp.moveaxis(o_sel, 0, 3).reshape(B, Hk, G, T, dh)

    o_win, _ = banded_attention(q, k_win, v_win, WIN_LEN - 1)

    return gates[..., 0:1] * o_cmp + gates[..., 1:2] * o_sel + gates[..., 2:3] * o_win


def stick_breaking(q, k, v):
    B, H, T, dh = q.shape
    qs = q * (dh ** -0.5)
    kpos = jnp.arange(T)

    def block(i):
        t0 = i * BLOCK_Q
        qi = lax.dynamic_slice_in_dim(qs, t0, BLOCK_Q, axis=2)
        z = jnp.einsum('bhqd,bhsd->bhqs', qi, k).astype(jnp.float32)
        qpos = t0 + jnp.arange(BLOCK_Q)
        causal = kpos[None, :] < qpos[:, None]
        log_beta = jnp.where(causal, jax.nn.log_sigmoid(z), -jnp.inf)
        log_rest = jnp.where(causal, jax.nn.log_sigmoid(-z), 0.0)
        later = lax.cumsum(log_rest, axis=3, reverse=True) - log_rest
        a = jnp.exp(log_beta + later)
        return jnp.einsum('bhqs,bhsd->bhqd', a.astype(v.dtype), v)

    o = lax.map(block, jnp.arange(T // BLOCK_Q))
    return jnp.moveaxis(o, 0, 2).reshape(B, H, T, dh)


def hybrid_layer(x, w_in, cmp_pe_k, cmp_wk1, cmp_wk2, cmp_pe_v, cmp_wv1, cmp_wv2,
                 w_br_a, w_br_b, w_br_c, w_out, ln1_g, ln1_b,
                 w_up, w_down, ln2_g, ln2_b):
    B, T, _ = x.shape
    dh = HEAD_DIM
    pos = jnp.arange(T)
    (q_a, k_a, v_a, q_b, kc_b, vc_b, ks_b, vs_b, kw_b, vw_b, g_nsa,
     q_c, k_c, v_c, g_a, g_b, g_c) = split_columns(x @ w_in)

    def heads(t, n):
        return t.reshape(B, T, n, dh).transpose(0, 2, 1, 3)

    def unheads(t):
        return t.transpose(0, 2, 1, 3).reshape(B, T, -1)

    qa = rope(heads(q_a, A_HEADS), pos).reshape(B, len(DIL_GROUPS), A_SLOTS, T, dh)
    y_a = unheads(dilated_mixer(qa, rope(heads(k_a, A_SLOTS), pos), heads(v_a, A_SLOTS)))

    qb = rope(heads(q_b, B_HEADS), pos).reshape(B, B_KV_HEADS, B_GROUP, T, dh)
    gates = jax.nn.sigmoid(g_nsa).reshape(B, T, B_KV_HEADS, B_GROUP, 3).transpose(0, 2, 3, 1, 4)
    o_b = nsa_mixer(qb,
                    rope(heads(kc_b, B_KV_HEADS), pos), heads(vc_b, B_KV_HEADS),
                    rope(heads(ks_b, B_KV_HEADS), pos), heads(vs_b, B_KV_HEADS),
                    rope(heads(kw_b, B_KV_HEADS), pos), heads(vw_b, B_KV_HEADS),
                    gates, cmp_pe_k, cmp_wk1, cmp_wk2, cmp_pe_v, cmp_wv1, cmp_wv2)
    y_b = unheads(o_b.reshape(B, B_HEADS, T, dh))

    y_c = unheads(stick_breaking(heads(q_c, C_HEADS), heads(k_c, C_HEADS), heads(v_c, C_HEADS)))

    mixed = (jax.nn.sigmoid(g_a) * (y_a @ w_br_a)
             + jax.nn.sigmoid(g_b) * (y_b @ w_br_b)
             + jax.nn.sigmoid(g_c) * (y_c @ w_br_c)) @ w_out
    x = layer_norm(ALPHA * x + mixed, ln1_g, ln1_b)

    h = jnp.square(jax.nn.relu(x @ w_up)) @ w_down
    return layer_norm(ALPHA * x + h, ln2_g, ln2_b)


def setup_inputs(seed: int = 0) -> dict:
    key = jax.random.key(seed)
    ks = jax.random.split(key, 20)
    f32 = jnp.float32
    L = DEPTH

    def nrm(k, shape, scale):
        return jax.random.normal(k, shape, f32) * scale

    return {
        'x': nrm(ks[0], (BATCH, SEQ, D_MODEL), 1.0),
        'w_in': nrm(ks[1], (L, D_MODEL, IN_WIDTH), D_MODEL ** -0.5),
        'cmp_pe_k': nrm(ks[2], (L, CMP_LEN, HEAD_DIM), 0.02),
        'cmp_wk1': nrm(ks[3], (L, CMP_LEN * HEAD_DIM, HEAD_DIM), (CMP_LEN * HEAD_DIM) ** -0.5),
        'cmp_wk2': nrm(ks[4], (L, HEAD_DIM, HEAD_DIM), HEAD_DIM ** -0.5),
        'cmp_pe_v': nrm(ks[5], (L, CMP_LEN, HEAD_DIM), 0.02),
        'cmp_wv1': nrm(ks[6], (L, CMP_LEN * HEAD_DIM, HEAD_DIM), (CMP_LEN * HEAD_DIM) ** -0.5),
        'cmp_wv2': nrm(ks[7], (L, HEAD_DIM, HEAD_DIM), HEAD_DIM ** -0.5),
        'w_br_a': nrm(ks[8], (L, A_KV, D_MODEL), A_KV ** -0.5),
        'w_br_b': nrm(ks[9], (L, B_Q, D_MODEL), B_Q ** -0.5),
        'w_br_c': nrm(ks[10], (L, C_W, D_MODEL), C_W ** -0.5),
        'w_out': nrm(ks[11], (L, D_MODEL, D_MODEL), BETA * D_MODEL ** -0.5),
        'ln1_g': 1.0 + nrm(ks[12], (L, D_MODEL), 0.02),
        'ln1_b': nrm(ks[13], (L, D_MODEL), 0.02),
        'w_up': nrm(ks[14], (L, D_MODEL, D_FF), D_MODEL ** -0.5),
        'w_down': nrm(ks[15], (L, D_FF, D_MODEL), BETA * D_FF ** -0.5),
        'ln2_g': 1.0 + nrm(ks[16], (L, D_MODEL), 0.02),
        'ln2_b': nrm(ks[17], (L, D_MODEL), 0.02),
    }


def reference(x, w_in, cmp_pe_k, cmp_wk1, cmp_wk2, cmp_pe_v, cmp_wv1, cmp_wv2,
              w_br_a, w_br_b, w_br_c, w_out, ln1_g, ln1_b,
              w_up, w_down, ln2_g, ln2_b):
    for l in range(DEPTH):
        x = hybrid_layer(x, w_in[l], cmp_pe_k[l], cmp_wk1[l], cmp_wk2[l],
                         cmp_pe_v[l], cmp_wv1[l], cmp_wv2[l],
                         w_br_a[l], w_br_b[l], w_br_c[l], w_out[l], ln1_g[l], ln1_b[l],
                         w_up[l], w_down[l], ln2_g[l], ln2_b[l])
    return x
```

```python
import functools
import math

import numpy as np
import jax
import jax.numpy as jnp
from jax import lax
from jax.experimental import pallas as pl
from jax.experimental.pallas import tpu as pltpu

F32 = jnp.float32
BF16 = jnp.bfloat16

HEAD_DIM = 128
ROPE_THETA = 10000.0
LN_EPS = 1e-5
DEPTH = 2

DIL_GROUPS = ((128, 1), (512, 4), (2048, 16))
A_SLOTS = 4
A_HEADS = A_SLOTS * len(DIL_GROUPS)

B_HEADS = 12
B_KV_HEADS = 4
B_GROUP = B_HEADS // B_KV_HEADS
CMP_LEN = 32
CMP_STRIDE = 16
SEL_LEN = 64
SEL_TOPK = 16
WIN_LEN = 512

C_HEADS = 8

ALPHA = (2.0 * DEPTH) ** 0.25
Q_SCALE = HEAD_DIM ** -0.5

VMEM_LIMIT = 48 * 1024 * 1024
ATT_TQ = 128


def _params(semantics):
    return pltpu.CompilerParams(dimension_semantics=semantics, vmem_limit_bytes=VMEM_LIMIT)


def _dot(a, b):
    return jnp.dot(a, b, preferred_element_type=F32)


def _dot_nt(a, b):
    return lax.dot_general(a, b, (((1,), (1,)), ((), ())), preferred_element_type=F32)


def _split_bf16(x):
    hi = x.astype(BF16)
    lo = (x - hi.astype(F32)).astype(BF16)
    return hi, lo


def _matmul_kernel(*refs, mode, scale, head_major, n_k):
    if mode == 'rope':
        a_ref, b_ref, cos_ref, sin_ref, o_ref, acc_ref = refs
    else:
        a_ref, b_ref, o_ref, acc_ref = refs
    k = pl.program_id(2)
    part = _dot(a_ref[...], b_ref[...])

    @pl.when(k == 0)
    def _():
        acc_ref[...] = part

    @pl.when(k > 0)
    def _():
        acc_ref[...] += part

    @pl.when(k == n_k - 1)
    def _():
        tn = acc_ref.shape[1]
        if mode == 'rope' or head_major:
            for c in range(tn // HEAD_DIM):
                y = acc_ref[:, c * HEAD_DIM:(c + 1) * HEAD_DIM]
                if mode == 'rope':
                    y = y * cos_ref[...] + pltpu.roll(y, HEAD_DIM // 2, 1) * sin_ref[...]
                if scale != 1.0:
                    y = y * scale
                if head_major:
                    o_ref[c] = y.astype(o_ref.dtype)
                else:
                    o_ref[:, c * HEAD_DIM:(c + 1) * HEAD_DIM] = y.astype(o_ref.dtype)
        else:
            y = acc_ref[...]
            if mode == 'sigmoid':
                y = jax.nn.sigmoid(y)
            elif mode == 'relu2':
                y = jnp.square(jnp.maximum(y, 0.0))
            if scale != 1.0:
                y = y * scale
            o_ref[...] = y.astype(o_ref.dtype)


def _matmul(a, b, *, name, out_dtype, mode='none', scale=1.0, head_major=False,
            rope_tables=None, tm=1024, tn=1024, tk=1024):
    m, kdim = a.shape
    n = b.shape[1]
    tm, tn, tk = min(tm, m), min(tn, n), min(tk, kdim)
    assert m % tm == 0 and n % tn == 0 and kdim % tk == 0, (a.shape, b.shape, tm, tn, tk)
    n_k = kdim // tk
    in_specs = [pl.BlockSpec((tm, tk), lambda i, j, k: (i, k)),
                pl.BlockSpec((tk, tn), lambda i, j, k: (k, j))]
    operands = [a, b]
    if mode == 'rope':
        cos, sin = rope_tables
        seq_tiles = cos.shape[0] // tm
        assert cos.shape[0] % tm == 0
        spec = pl.BlockSpec((tm, HEAD_DIM), lambda i, j, k: (i % seq_tiles, 0))
        in_specs += [spec, spec]
        operands += [cos, sin]
    if head_major:
        out_shape = jax.ShapeDtypeStruct((n // HEAD_DIM, m, HEAD_DIM), out_dtype)
        out_spec = pl.BlockSpec((tn // HEAD_DIM, tm, HEAD_DIM), lambda i, j, k: (j, i, 0))
    else:
        out_shape = jax.ShapeDtypeStruct((m, n), out_dtype)
        out_spec = pl.BlockSpec((tm, tn), lambda i, j, k: (i, j))
    return pl.pallas_call(
        functools.partial(_matmul_kernel, mode=mode, scale=scale, head_major=head_major, n_k=n_k),
        grid=(m // tm, n // tn, n_k),
        in_specs=in_specs,
        out_specs=out_spec,
        out_shape=out_shape,
        scratch_shapes=[pltpu.VMEM((tm, tn), F32)],
        compiler_params=_params(("parallel", "parallel", "arbitrary")),
        name=name,
    )(*operands)


def _window_start(t0, back, span, seq, align):
    start = jnp.minimum(jnp.maximum(t0 - back, 0), seq - span)
    return pl.multiple_of(start, align)


def _dilated_kernel(q0_ref, q1_ref, q2_ref, k_ref, v_ref, o_ref, *, tq, seq):
    t0 = pl.program_id(2) * tq
    qpos = t0 + lax.broadcasted_iota(jnp.int32, (tq, 1), 0)
    outs, lses = [], []
    for q_ref, (window, dil) in zip((q0_ref, q1_ref, q2_ref), DIL_GROUPS):
        span = min(window + tq, seq)
        start = _window_start(t0, window, span, seq, tq)
        k = k_ref[pl.ds(start, span), :]
        v = v_ref[pl.ds(start, span), :]
        s = _dot_nt(q_ref[...], k)
        kpos = start + lax.broadcasted_iota(jnp.int32, (1, span), 1)
        dist = qpos - kpos
        keep = (dist >= 0) & (dist <= window) & ((dist & (dil - 1)) == 0)
        s = jnp.where(keep, s, -jnp.inf)
        m = jnp.max(s, axis=1, keepdims=True)
        p = jnp.exp(s - m)
        l = jnp.sum(p, axis=1, keepdims=True)
        outs.append(_dot(p.astype(BF16), v) / l)
        lses.append(m + jnp.log(l))
    top = jnp.maximum(jnp.maximum(lses[0], lses[1]), lses[2])
    ws = [jnp.exp(lse - top) for lse in lses]
    y = (ws[0] * outs[0] + ws[1] * outs[1] + ws[2] * outs[2]) / (ws[0] + ws[1] + ws[2])
    o_ref[...] = y.astype(o_ref.dtype)


def _dilated_mixer(q, q_head0, k, k_head0, v, v_head0):
    _, nb, seq, _ = q.shape
    tq = min(ATT_TQ, seq)
    q_specs = [
        pl.BlockSpec((None, None, tq, HEAD_DIM),
                     functools.partial(lambda b, s, n, g: (q_head0 + g * A_SLOTS + s, b, n, 0), g=g))
        for g in range(len(DIL_GROUPS))]
    k_spec = pl.BlockSpec((None, None, seq, HEAD_DIM), lambda b, s, n: (k_head0 + s, b, 0, 0))
    v_spec = pl.BlockSpec((None, None, seq, HEAD_DIM), lambda b, s, n: (v_head0 + s, b, 0, 0))
    return pl.pallas_call(
        functools.partial(_dilated_kernel, tq=tq, seq=seq),
        grid=(nb, A_SLOTS, seq // tq),
        in_specs=q_specs + [k_spec, v_spec],
        out_specs=pl.BlockSpec((None, tq, HEAD_DIM), lambda b, s, n: (b, n, s)),
        out_shape=jax.ShapeDtypeStruct((nb, seq, A_SLOTS * HEAD_DIM), BF16),
        compiler_params=_params(("parallel", "parallel", "arbitrary")),
        name="dilated_mixer",
    )(q, q, q, k, v)


def _gelu_tanh(x):
    return 0.5 * x * (1.0 + jnp.tanh(math.sqrt(2.0 / math.pi) * (x + 0.044715 * (x * x * x))))


def _compress_kernel(x_ref, pe_ref, w1_ref, w2_ref, o_ref):
    x = x_ref[...].astype(F32)
    first = _dot((x + pe_ref[0:1, :]).astype(BF16), w1_ref[0])
    second = _dot((x + pe_ref[1:2, :]).astype(BF16), w1_ref[1])
    chunks = x.shape[0]
    hidden = first + pltpu.roll(second, chunks - 1, 0)
    o_ref[...] = _dot(_gelu_tanh(hidden).astype(BF16), w2_ref[...]).astype(o_ref.dtype)


def _compress(t, pe, w1, w2, name):
    nh, nb, seq, _ = t.shape
    chunks = seq // CMP_STRIDE
    width = CMP_STRIDE * HEAD_DIM
    ratio = CMP_LEN // CMP_STRIDE
    x = t.reshape(nh, nb, chunks, width)
    return pl.pallas_call(
        _compress_kernel,
        grid=(nh, nb),
        in_specs=[pl.BlockSpec((None, None, chunks, width), lambda h, b: (h, b, 0, 0)),
                  pl.BlockSpec((ratio, width), lambda h, b: (0, 0)),
                  pl.BlockSpec((ratio, width, HEAD_DIM), lambda h, b: (0, 0, 0)),
                  pl.BlockSpec((HEAD_DIM, HEAD_DIM), lambda h, b: (0, 0))],
        out_specs=pl.BlockSpec((None, None, chunks, HEAD_DIM), lambda h, b: (h, b, 0, 0)),
        out_shape=jax.ShapeDtypeStruct((nh, nb, chunks, HEAD_DIM), BF16),
        compiler_params=_params(("parallel", "parallel")),
        name=name,
    )(x, pe.reshape(ratio, width), w1.reshape(ratio, width, HEAD_DIM).astype(BF16), w2.astype(BF16))


def _cmp_select_kernel(q_ref, kc_ref, vc_ref, ov_ref, o_ref, sel_ref, *, tq):
    t0 = pl.program_id(2) * tq
    n_cmp = kc_ref.shape[0]
    rows = B_GROUP * tq
    q = q_ref[...].reshape(rows, HEAD_DIM)
    s = _dot_nt(q, kc_ref[...])
    tpos3 = t0 + (lax.broadcasted_iota(jnp.int32, (rows, 1), 0) & (tq - 1))
    c_end = lax.broadcasted_iota(jnp.int32, (1, n_cmp), 1) * CMP_STRIDE + (CMP_LEN - 1)
    s = jnp.where(c_end - tpos3 <= 0, s, -jnp.inf)
    m = jnp.max(s, axis=1, keepdims=True)
    m = jnp.where(jnp.abs(m) < jnp.inf, m, 0.0)
    e = jnp.exp(s - m)
    den = jnp.sum(e, axis=1, keepdims=True)
    p = e / jnp.where(den > 0, den, 1.0)
    o_ref[...] = _dot(p.astype(BF16), vc_ref[...]).reshape(B_GROUP, tq, HEAD_DIM).astype(o_ref.dtype)

    p_sum = p[0:tq] + p[tq:2 * tq] + p[2 * tq:3 * tq]
    hi, lo = _split_bf16(p_sum)
    imp = _dot(hi, ov_ref[...]) + _dot(lo, ov_ref[...])
    tpos = t0 + lax.broadcasted_iota(jnp.int32, (tq, 1), 0)
    rel = lax.broadcasted_iota(jnp.int32, (1, HEAD_DIM), 1) - (tpos >> int(math.log2(SEL_LEN)))
    j_abs = jnp.broadcast_to(lax.broadcasted_iota(jnp.int32, (1, HEAD_DIM), 1), rel.shape)
    forced = (j_abs == 0) | (rel == 0) | (rel == -1)
    imp = jnp.where(forced, jnp.inf, jnp.where(rel <= 0, imp, -jnp.inf))

    n_sel = HEAD_DIM // 2
    imp_t = imp.T
    mine = imp_t[0:n_sel]
    j_idx = lax.broadcasted_iota(jnp.int32, (n_sel, tq), 0)
    beaten = jnp.zeros((n_sel, tq), F32)
    for kk in range(n_sel):
        other = imp_t[kk:kk + 1, :]
        wins = (other > mine) | ((other == mine) & (j_idx > kk))
        beaten = beaten + jnp.where(wins, 1.0, 0.0)
    chosen = jnp.where(beaten < SEL_TOPK, 1.0, 0.0)
    chosen = jnp.concatenate([chosen, jnp.zeros((HEAD_DIM - n_sel, tq), F32)], axis=0)
    sel_ref[...] = chosen.T.astype(sel_ref.dtype)


def _overlap_matrix(n_cmp_rows, n_sel):
    c_start = np.arange(n_cmp_rows) * CMP_STRIDE
    c_end = c_start + CMP_LEN - 1
    s_start = np.arange(HEAD_DIM) * SEL_LEN
    ov = (c_start[:, None] <= s_start[None, :] + SEL_LEN - 1) & (c_end[:, None] >= s_start[None, :])
    ov &= (np.arange(HEAD_DIM) < n_sel)[None, :]
    return jnp.asarray(ov.astype(np.float32), dtype=BF16)


def _group_q_spec(tq, head0):
    return pl.BlockSpec((B_GROUP, None, tq, HEAD_DIM), lambda b, h, n: (head0 // B_GROUP + h, b, n, 0))


def _cmp_select(q, q_head0, kc, vc):
    _, nb, seq, _ = q.shape
    assert seq // SEL_LEN <= HEAD_DIM // 2 and q_head0 % B_GROUP == 0
    tq = min(ATT_TQ, seq)
    n_cmp = kc.shape[2]
    kv_spec = pl.BlockSpec((None, None, n_cmp, HEAD_DIM), lambda b, h, n: (h, b, 0, 0))
    return pl.pallas_call(
        functools.partial(_cmp_select_kernel, tq=tq),
        grid=(nb, B_KV_HEADS, seq // tq),
        in_specs=[_group_q_spec(tq, q_head0), kv_spec, kv_spec,
                  pl.BlockSpec((n_cmp, HEAD_DIM), lambda b, h, n: (0, 0))],
        out_specs=[pl.BlockSpec((B_GROUP, None, tq, HEAD_DIM), lambda b, h, n: (h, b, n, 0)),
                   pl.BlockSpec((None, None, tq, HEAD_DIM), lambda b, h, n: (h, b, n, 0))],
        out_shape=[jax.ShapeDtypeStruct((B_HEADS, nb, seq, HEAD_DIM), BF16),
                   jax.ShapeDtypeStruct((B_KV_HEADS, nb, seq, HEAD_DIM), BF16)],
        compiler_params=_params(("parallel", "parallel", "arbitrary")),
        name="nsa_compressed_select",
    )(q, kc, vc, _overlap_matrix(n_cmp, seq // SEL_LEN))


def _selected_kernel(q_ref, k_ref, v_ref, sel_ref, o_ref, m_ref, l_ref, acc_ref, *, tq):
    n = pl.program_id(2)
    t0 = n * tq
    rows = B_GROUP * tq
    q = q_ref[...].reshape(rows, HEAD_DIM)
    sel = sel_ref[...]
    qpos = t0 + lax.broadcasted_iota(jnp.int32, (tq, 1), 0)
    blk = lax.broadcasted_iota(jnp.int32, (HEAD_DIM, 1), 0)
    m_ref[...] = jnp.full(m_ref.shape, -1e30, F32)
    l_ref[...] = jnp.zeros(l_ref.shape, F32)
    acc_ref[...] = jnp.zeros(acc_ref.shape, F32)

    def step(i, carry):
        k0 = pl.multiple_of(i * tq, tq)
        k = k_ref[pl.ds(k0, tq), :]
        v = v_ref[pl.ds(k0, tq), :]
        kpos = k0 + lax.broadcasted_iota(jnp.int32, (1, tq), 1)
        expand = jnp.where(blk == (kpos >> int(math.log2(SEL_LEN))), 1.0, 0.0).astype(BF16)
        picked = _dot(sel, expand)
        bias = jnp.where((picked > 0.5) & (kpos - qpos <= 0), 0.0, -jnp.inf)
        s = _dot_nt(q, k) + jnp.concatenate([bias] * B_GROUP, axis=0)
        m_old = m_ref[...]
        m_new = jnp.maximum(m_old, jnp.max(s, axis=1, keepdims=True))
        p = jnp.exp(s - m_new)
        alpha = jnp.exp(m_old - m_new)
        l_ref[...] = alpha * l_ref[...] + jnp.sum(p, axis=1, keepdims=True)
        acc_ref[...] = alpha * acc_ref[...] + _dot(p.astype(BF16), v)
        m_ref[...] = m_new
        return carry

    lax.fori_loop(0, n + 1, step, 0)
    o_ref[...] = (acc_ref[...] / l_ref[...]).reshape(B_GROUP, tq, HEAD_DIM).astype(o_ref.dtype)


def _selected(q, q_head0, k, k_head0, v, v_head0, sel):
    _, nb, seq, _ = q.shape
    tq = min(ATT_TQ, seq)
    rows = B_GROUP * tq
    return pl.pallas_call(
        functools.partial(_selected_kernel, tq=tq),
        grid=(nb, B_KV_HEADS, seq // tq),
        in_specs=[_group_q_spec(tq, q_head0),
                  pl.BlockSpec((None, None, seq, HEAD_DIM), lambda b, h, n: (k_head0 + h, b, 0, 0)),
                  pl.BlockSpec((None, None, seq, HEAD_DIM), lambda b, h, n: (v_head0 + h, b, 0, 0)),
                  pl.BlockSpec((None, None, tq, HEAD_DIM), lambda b, h, n: (h, b, n, 0))],
        out_specs=pl.BlockSpec((B_GROUP, None, tq, HEAD_DIM), lambda b, h, n: (h, b, n, 0)),
        out_shape=jax.ShapeDtypeStruct((B_HEADS, nb, seq, HEAD_DIM), BF16),
        scratch_shapes=[pltpu.VMEM((rows, 1), F32), pltpu.VMEM((rows, 1), F32),
                        pltpu.VMEM((rows, HEAD_DIM), F32)],
        compiler_params=_params(("parallel", "parallel", "arbitrary")),
        name="nsa_selected",
    )(q, k, v, sel)


def _window_kernel(q_ref, k_ref, v_ref, o_ref, *, tq, seq):
    t0 = pl.program_id(2) * tq
    rows = B_GROUP * tq
    span = min(WIN_LEN + tq, seq)
    start = _window_start(t0, WIN_LEN, span, seq, tq)
    q = q_ref[...].reshape(rows, HEAD_DIM)
    s = _dot_nt(q, k_ref[pl.ds(start, span), :])
    qpos = t0 + (lax.broadcasted_iota(jnp.int32, (rows, 1), 0) & (tq - 1))
    kpos = start + lax.broadcasted_iota(jnp.int32, (1, span), 1)
    dist = qpos - kpos
    s = jnp.where((dist >= 0) & (dist <= WIN_LEN - 1), s, -jnp.inf)
    m = jnp.max(s, axis=1, keepdims=True)
    p = jnp.exp(s - m)
    l = jnp.sum(p, axis=1, keepdims=True)
    o = _dot(p.astype(BF16), v_ref[pl.ds(start, span), :]) / l
    o_ref[...] = o.reshape(B_GROUP, tq, HEAD_DIM).astype(o_ref.dtype)


def _window(q, q_head0, k, k_head0, v, v_head0):
    _, nb, seq, _ = q.shape
    tq = min(ATT_TQ, seq)
    assert WIN_LEN % tq == 0
    return pl.pallas_call(
        functools.partial(_window_kernel, tq=tq, seq=seq),
        grid=(nb, B_KV_HEADS, seq // tq),
        in_specs=[_group_q_spec(tq, q_head0),
                  pl.BlockSpec((None, None, seq, HEAD_DIM), lambda b, h, n: (k_head0 + h, b, 0, 0)),
                  pl.BlockSpec((None, None, seq, HEAD_DIM), lambda b, h, n: (v_head0 + h, b, 0, 0))],
        out_specs=pl.BlockSpec((B_GROUP, None, tq, HEAD_DIM), lambda b, h, n: (h, b, n, 0)),
        out_shape=jax.ShapeDtypeStruct((B_HEADS, nb, seq, HEAD_DIM), BF16),
        compiler_params=_params(("parallel", "parallel", "arbitrary")),
        name="nsa_window",
    )(q, k, v)


def _nsa_gate_kernel(g_ref, oc_ref, os_ref, ow_ref, y_ref):
    g = g_ref[...]
    for h in range(B_HEADS):
        y = (g[:, 3 * h:3 * h + 1] * oc_ref[h].astype(F32)
             + g[:, 3 * h + 1:3 * h + 2] * os_ref[h].astype(F32)
             + g[:, 3 * h + 2:3 * h + 3] * ow_ref[h].astype(F32))
        y_ref[:, h * HEAD_DIM:(h + 1) * HEAD_DIM] = y.astype(y_ref.dtype)


def _nsa_gate(gates, o_cmp, o_sel, o_win):
    rows = gates.shape[0]
    tm = min(256, rows)
    o_spec = pl.BlockSpec((B_HEADS, tm, HEAD_DIM), lambda i: (0, i, 0))
    return pl.pallas_call(
        _nsa_gate_kernel,
        grid=(rows // tm,),
        in_specs=[pl.BlockSpec((tm, HEAD_DIM), lambda i: (i, 0)), o_spec, o_spec, o_spec],
        out_specs=pl.BlockSpec((tm, B_HEADS * HEAD_DIM), lambda i: (i, 0)),
        out_shape=jax.ShapeDtypeStruct((rows, B_HEADS * HEAD_DIM), BF16),
        compiler_params=_params(("parallel",)),
        name="nsa_gate",
    )(gates, o_cmp, o_sel, o_win)


def _stick_kernel(q_ref, k_ref, v_ref, o_ref, *, tq):
    n = pl.program_id(2)
    t0 = pl.multiple_of(n * tq, tq)
    q = q_ref[...]
    row = lax.broadcasted_iota(jnp.int32, (tq, tq), 0)
    col = lax.broadcasted_iota(jnp.int32, (tq, tq), 1)
    after = jnp.where(row > col, 1.0, 0.0).astype(BF16)
    causal = col < row

    def tile(k0, carried, acc, diagonal):
        z = _dot_nt(q, k_ref[pl.ds(k0, tq), :])
        soft = jnp.log1p(jnp.exp(-jnp.abs(z)))
        log_beta = jnp.minimum(z, 0.0) - soft
        log_rest = log_beta - z
        if diagonal:
            log_rest = jnp.where(causal, log_rest, 0.0)
        hi, lo = _split_bf16(log_rest)
        later = _dot(hi, after) + _dot(lo, after)
        a = jnp.exp(log_beta + (later + carried))
        if diagonal:
            a = jnp.where(causal, a, 0.0)
        acc = acc + _dot(a.astype(BF16), v_ref[pl.ds(k0, tq), :])
        return carried + jnp.sum(log_rest, axis=1, keepdims=True), acc

    carried, acc = tile(t0, jnp.zeros((tq, 1), F32), jnp.zeros((tq, HEAD_DIM), F32), True)

    def step(i, state):
        k0 = pl.multiple_of((n - 1 - i) * tq, tq)
        return tile(k0, state[0], state[1], False)

    _, acc = lax.fori_loop(0, n, step, (carried, acc))
    o_ref[...] = acc.astype(o_ref.dtype)


def _stick_breaking(q, q_head0, k, k_head0, v, v_head0):
    _, nb, seq, _ = q.shape
    tq = min(ATT_TQ, seq)
    return pl.pallas_call(
        functools.partial(_stick_kernel, tq=tq),
        grid=(nb, C_HEADS, seq // tq),
        in_specs=[pl.BlockSpec((None, None, tq, HEAD_DIM), lambda b, h, n: (q_head0 + h, b, n, 0)),
                  pl.BlockSpec((None, None, seq, HEAD_DIM), lambda b, h, n: (k_head0 + h, b, 0, 0)),
                  pl.BlockSpec((None, None, seq, HEAD_DIM), lambda b, h, n: (v_head0 + h, b, 0, 0))],
        out_specs=pl.BlockSpec((None, tq, HEAD_DIM), lambda b, h, n: (b, n, h)),
        out_shape=jax.ShapeDtypeStruct((nb, seq, C_HEADS * HEAD_DIM), BF16),
        compiler_params=_params(("parallel", "parallel", "arbitrary")),
        name="stick_breaking",
    )(q, k, v)


def _merge_kernel(ya_ref, yb_ref, yc_ref, wa_ref, wb_ref, wc_ref, ga_ref, gb_ref, gc_ref, o_ref):
    mixed = (ga_ref[...].astype(F32) * _dot(ya_ref[...], wa_ref[...])
             + gb_ref[...].astype(F32) * _dot(yb_ref[...], wb_ref[...])
             + gc_ref[...].astype(F32) * _dot(yc_ref[...], wc_ref[...]))
    o_ref[...] = mixed.astype(o_ref.dtype)


def _merge(ya, yb, yc, wa, wb, wc, gates, d_model):
    m = ya.shape[0]
    tm, tn = min(512, m), 1024
    col_tiles = d_model // tn

    def y_spec(y):
        return pl.BlockSpec((tm, y.shape[1]), lambda i, j: (i, 0))

    def w_spec(w):
        return pl.BlockSpec((w.shape[0], tn), lambda i, j: (0, j))

    def g_spec(branch):
        return pl.BlockSpec((tm, tn), lambda i, j: (i, branch * col_tiles + j))

    return pl.pallas_call(
        _merge_kernel,
        grid=(m // tm, col_tiles),
        in_specs=[y_spec(ya), y_spec(yb), y_spec(yc), w_spec(wa), w_spec(wb), w_spec(wc),
                  g_spec(0), g_spec(1), g_spec(2)],
        out_specs=pl.BlockSpec((tm, tn), lambda i, j: (i, j)),
        out_shape=jax.ShapeDtypeStruct((m, d_model), BF16),
        compiler_params=_params(("parallel", "arbitrary")),
        name="branch_merge",
    )(ya, yb, yc, wa, wb, wc, gates, gates, gates)


def _ln_kernel(x_ref, y_ref, g_ref, b_ref, o_ref, ob_ref):
    z = ALPHA * x_ref[...] + y_ref[...]
    mu = jnp.mean(z, axis=1, keepdims=True)
    zc = z - mu
    var = jnp.mean(zc * zc, axis=1, keepdims=True)
    out = zc * lax.rsqrt(var + LN_EPS) * g_ref[...] + b_ref[...]
    o_ref[...] = out
    ob_ref[...] = out.astype(BF16)


def _residual_ln(x, y, g, b):
    m, d = x.shape
    tm = min(128, m)
    row = pl.BlockSpec((tm, d), lambda i: (i, 0))
    vec = pl.BlockSpec((1, d), lambda i: (0, 0))
    return pl.pallas_call(
        _ln_kernel,
        grid=(m // tm,),
        in_specs=[row, row, vec, vec],
        out_specs=[row, row],
        out_shape=[jax.ShapeDtypeStruct((m, d), F32), jax.ShapeDtypeStruct((m, d), BF16)],
        compiler_params=_params(("parallel",)),
        name="residual_layer_norm",
    )(x, y, g.reshape(1, d), b.reshape(1, d))


def _rope_tables(seq):
    half = HEAD_DIM // 2
    inv = 1.0 / (ROPE_THETA ** (jnp.arange(half, dtype=F32) / half))
    ang = jnp.arange(seq).astype(F32)[:, None] * inv[None, :]
    cos, sin = jnp.cos(ang), jnp.sin(ang)
    return jnp.concatenate([cos, cos], axis=1), jnp.concatenate([-sin, sin], axis=1)


def _in_splits(d_model):
    a_q, a_kv = A_HEADS * HEAD_DIM, A_SLOTS * HEAD_DIM
    b_q, b_kv = B_HEADS * HEAD_DIM, B_KV_HEADS * HEAD_DIM
    c_w = C_HEADS * HEAD_DIM
    names = ('q_a', 'k_a', 'v_a', 'q_b', 'kc_b', 'vc_b', 'ks_b', 'vs_b', 'kw_b', 'vw_b', 'g_nsa',
             'q_c', 'k_c', 'v_c', 'g_a', 'g_b', 'g_c')
    widths = (a_q, a_kv, a_kv, b_q, b_kv, b_kv, b_kv, b_kv, b_kv, b_kv, 3 * B_HEADS,
              c_w, c_w, c_w, d_model, d_model, d_model)
    out, start = {}, 0
    for name, w in zip(names, widths):
        out[name] = (start, start + w)
        start += w
    return out


def _columns(w, splits, names):
    return jnp.concatenate([w[:, splits[n][0]:splits[n][1]] for n in names], axis=1).astype(BF16)


def _layer(x, xb, nb, seq, tables, w_in, cmp_pe_k, cmp_wk1, cmp_wk2, cmp_pe_v, cmp_wv1, cmp_wv2,
           w_br_a, w_br_b, w_br_c, w_out, ln1_g, ln1_b, w_up, w_down, ln2_g, ln2_b):
    m, d_model = x.shape
    sp = _in_splits(d_model)

    q_rope = _matmul(xb, _columns(w_in, sp, ('q_a', 'q_b')), name="proj_q_rope", out_dtype=BF16,
                     mode='rope', scale=Q_SCALE, head_major=True, rope_tables=tables)
    k_rope = _matmul(xb, _columns(w_in, sp, ('k_a', 'kc_b', 'ks_b', 'kw_b')), name="proj_k_rope",
                     out_dtype=BF16, mode='rope', head_major=True, rope_tables=tables)
    q_c = _matmul(xb, _columns(w_in, sp, ('q_c',)), name="proj_q_stick", out_dtype=BF16,
                  scale=Q_SCALE, head_major=True)
    plain = _matmul(xb, _columns(w_in, sp, ('v_a', 'vc_b', 'vs_b', 'vw_b', 'k_c', 'v_c')),
                    name="proj_plain", out_dtype=BF16, head_major=True)
    w_gn = _columns(w_in, sp, ('g_nsa',))
    w_gn = jnp.pad(w_gn, ((0, 0), (0, HEAD_DIM - w_gn.shape[1])))
    g_nsa = _matmul(xb, w_gn, name="proj_nsa_gates", out_dtype=F32, mode='sigmoid')
    gates = _matmul(xb, _columns(w_in, sp, ('g_a', 'g_b', 'g_c')), name="proj_branch_gates",
                    out_dtype=BF16, mode='sigmoid')

    def heads(t):
        return t.reshape(t.shape[0], nb, seq, HEAD_DIM)

    q_rope, k_rope, q_c, plain = heads(q_rope), heads(k_rope), heads(q_c), heads(plain)
    qa0, qb0 = 0, A_HEADS
    ka0, kc0, ks0, kw0 = 0, A_SLOTS, A_SLOTS + B_KV_HEADS, A_SLOTS + 2 * B_KV_HEADS
    va0, vc0, vs0, vw0 = 0, A_SLOTS, A_SLOTS + B_KV_HEADS, A_SLOTS + 2 * B_KV_HEADS
    kcs0 = A_SLOTS + 3 * B_KV_HEADS
    vcs0 = kcs0 + C_HEADS

    y_a = _dilated_mixer(q_rope, qa0, k_rope, ka0, plain, va0)

    kc = _compress(k_rope[kc0:kc0 + B_KV_HEADS], cmp_pe_k, cmp_wk1, cmp_wk2, "nsa_compress_k")
    vc = _compress(plain[vc0:vc0 + B_KV_HEADS], cmp_pe_v, cmp_wv1, cmp_wv2, "nsa_compress_v")
    o_cmp, sel = _cmp_select(q_rope, qb0, kc, vc)
    o_sel = _selected(q_rope, qb0, k_rope, ks0, plain, vs0, sel)
    o_win = _window(q_rope, qb0, k_rope, kw0, plain, vw0)
    y_b = _nsa_gate(g_nsa, o_cmp.reshape(B_HEADS, m, HEAD_DIM), o_sel.reshape(B_HEADS, m, HEAD_DIM),
                    o_win.reshape(B_HEADS, m, HEAD_DIM))

    y_c = _stick_breaking(q_c, 0, plain, kcs0, plain, vcs0)

    merged = _merge(y_a.reshape(m, -1), y_b, y_c.reshape(m, -1), w_br_a.astype(BF16),
                    w_br_b.astype(BF16), w_br_c.astype(BF16), gates, d_model)
    mixed = _matmul(merged, w_out.astype(BF16), name="out_proj", out_dtype=F32)
    x1, x1b = _residual_ln(x, mixed, ln1_g, ln1_b)

    hidden = _matmul(x1b, w_up.astype(BF16), name="mlp_up", out_dtype=BF16, mode='relu2')
    down = _matmul(hidden, w_down.astype(BF16), name="mlp_down", out_dtype=F32)
    return _residual_ln(x1, down, ln2_g, ln2_b)


def kernel(x, w_in, cmp_pe_k, cmp_wk1, cmp_wk2, cmp_pe_v, cmp_wv1, cmp_wv2, w_br_a, w_br_b, w_br_c,
           w_out, ln1_g, ln1_b, w_up, w_down, ln2_g, ln2_b):
    nb, seq, d_model = x.shape
    tables = _rope_tables(seq)
    xf = x.reshape(nb * seq, d_model)
    xb = xf.astype(BF16)
    for l in range(w_in.shape[0]):
        xf, xb = _layer(xf, xb, nb, seq, tables, w_in[l], cmp_pe_k[l], cmp_wk1[l], cmp_wk2[l],
                        cmp_pe_v[l], cmp_wv1[l], cmp_wv2[l], w_br_a[l], w_br_b[l], w_br_c[l],
                        w_out[l], ln1_g[l], ln1_b[l], w_up[l], w_down[l], ln2_g[l], ln2_b[l])
    return xf.reshape(nb, seq, d_model)
```

```python
import functools
import math

import numpy as np
import jax
import jax.numpy as jnp
from jax import lax
from jax.experimental import pallas as pl
from jax.experimental.pallas import tpu as pltpu

F32 = jnp.float32
BF16 = jnp.bfloat16

HEAD_DIM = 128
ROPE_THETA = 10000.0
LN_EPS = 1e-5
DEPTH = 2

DIL_GROUPS = ((128, 1), (512, 4), (2048, 16))
A_SLOTS = 4
A_HEADS = A_SLOTS * len(DIL_GROUPS)

B_HEADS = 12
B_KV_HEADS = 4
B_GROUP = B_HEADS // B_KV_HEADS
CMP_LEN = 32
CMP_STRIDE = 16
SEL_LEN = 64
SEL_SHIFT = 6
SEL_TOPK = 16
WIN_LEN = 512

C_HEADS = 8

ALPHA = (2.0 * DEPTH) ** 0.25
Q_SCALE = HEAD_DIM ** -0.5

VMEM_LIMIT = 48 * 1024 * 1024
ATT_TQ = 128
SEL_CHUNK = 512
STICK_TILE = 512
CUM_TILE = 128


def _params(semantics):
    return pltpu.CompilerParams(dimension_semantics=semantics, vmem_limit_bytes=VMEM_LIMIT)


def _dot(a, b):
    return jnp.dot(a, b, preferred_element_type=F32)


def _dot_nt(a, b):
    return lax.dot_general(a, b, (((1,), (1,)), ((), ())), preferred_element_type=F32)


def _split_bf16(x):
    hi = x.astype(BF16)
    lo = (x - hi.astype(F32)).astype(BF16)
    return hi, lo


def _apply_mode(y, mode, cos_ref, sin_ref):
    if mode in ('rope', 'rope_scale'):
        y = y * cos_ref[...] + pltpu.roll(y, HEAD_DIM // 2, 1) * sin_ref[...]
    if mode in ('rope_scale', 'scale'):
        y = y * Q_SCALE
    if mode == 'sigmoid':
        y = jax.nn.sigmoid(y)
    if mode == 'relu2':
        y = jnp.square(jnp.maximum(y, 0.0))
    return y


def _write_tile(acc, o_ref, mode, head_major, cos_ref, sin_ref):
    if head_major:
        for c in range(acc.shape[1] // HEAD_DIM):
            y = _apply_mode(acc[:, c * HEAD_DIM:(c + 1) * HEAD_DIM], mode, cos_ref, sin_ref)
            o_ref[c] = y.astype(o_ref.dtype)
    else:
        o_ref[...] = _apply_mode(acc, mode, cos_ref, sin_ref).astype(o_ref.dtype)


def _matmul_kernel(*refs, tile_modes, head_major, n_k, use_rope, acc_in_out):
    refs = list(refs)
    a_ref, b_ref = refs[0], refs[1]
    cos_ref, sin_ref = (refs[2], refs[3]) if use_rope else (None, None)
    o_ref = refs[4] if use_rope else refs[2]
    acc_ref = o_ref if acc_in_out else (refs[-1] if n_k > 1 else None)
    part = _dot(a_ref[...], b_ref[...].astype(BF16))

    def finish(acc):
        modes = sorted(set(tile_modes))
        if len(modes) == 1:
            _write_tile(acc, o_ref, modes[0], head_major, cos_ref, sin_ref)
            return
        j = pl.program_id(1)
        for mode in modes:
            hit = None
            for c, tile_mode in enumerate(tile_modes):
                if tile_mode == mode:
                    hit = (j == c) if hit is None else (hit | (j == c))
            pl.when(hit)(functools.partial(_write_tile, acc, o_ref, mode, head_major, cos_ref, sin_ref))

    if n_k == 1:
        finish(part)
        return
    k = pl.program_id(2)

    @pl.when(k == 0)
    def _():
        acc_ref[...] = part

    @pl.when(k > 0)
    def _():
        acc_ref[...] += part

    if not acc_in_out:
        pl.when(k == n_k - 1)(lambda: finish(acc_ref[...]))


def _matmul(a, b, *, name, out_dtype, tile_modes, tm, tn, tk=None, layer=None, col0=0,
            head_major=False, rope_tables=None):
    m, kdim = a.shape
    n = tn * len(tile_modes)
    tm = min(tm, m)
    tk = kdim if tk is None else min(tk, kdim)
    assert m % tm == 0 and kdim % tk == 0 and col0 % tn == 0
    n_k = kdim // tk
    col_tile0 = col0 // tn
    use_rope = any(mode.startswith('rope') for mode in tile_modes)
    acc_in_out = n_k > 1 and out_dtype == F32 and set(tile_modes) == {'none'} and not head_major
    if layer is None:
        b_spec = pl.BlockSpec((tk, tn), lambda i, j, k: (k, col_tile0 + j))
    else:
        b_spec = pl.BlockSpec((None, tk, tn), lambda i, j, k: (layer, k, col_tile0 + j))
    in_specs = [pl.BlockSpec((tm, tk), lambda i, j, k: (i, k)), b_spec]
    operands = [a, b]
    if use_rope:
        assert head_major
        cos, sin = rope_tables
        seq_tiles = cos.shape[0] // tm
        assert cos.shape[0] % tm == 0
        spec = pl.BlockSpec((tm, HEAD_DIM), lambda i, j, k: (i % seq_tiles, 0))
        in_specs += [spec, spec]
        operands += [cos, sin]
    if head_major:
        out_shape = jax.ShapeDtypeStruct((n // HEAD_DIM, m, HEAD_DIM), out_dtype)
        out_spec = pl.BlockSpec((tn // HEAD_DIM, tm, HEAD_DIM), lambda i, j, k: (j, i, 0))
    else:
        out_shape = jax.ShapeDtypeStruct((m, n), out_dtype)
        out_spec = pl.BlockSpec((tm, tn), lambda i, j, k: (i, j))
    scratch = [pltpu.VMEM((tm, tn), F32)] if (n_k > 1 and not acc_in_out) else []
    return pl.pallas_call(
        functools.partial(_matmul_kernel, tile_modes=tuple(tile_modes), head_major=head_major, n_k=n_k,
                          use_rope=use_rope, acc_in_out=acc_in_out),
        grid=(m // tm, len(tile_modes), n_k),
        in_specs=in_specs,
        out_specs=out_spec,
        out_shape=out_shape,
        scratch_shapes=scratch,
        compiler_params=_params(("parallel", "parallel", "arbitrary")),
        name=name,
    )(*operands)


def _shift_cast_kernel(w_ref, nxt_ref, o_ref, *, shift):
    both = jnp.concatenate([w_ref[...], nxt_ref[...]], axis=1)
    width = both.shape[1]
    o_ref[...] = pltpu.roll(both, width - shift, 1)[:, :o_ref.shape[1]].astype(o_ref.dtype)


def _shift_cast(w, layer, col0, shift, n, *, tk=512, tn=1024):
    kdim = w.shape[1]
    assert col0 % tn == 0 and n % tn == 0 and kdim % tk == 0 and 0 < shift < HEAD_DIM
    lanes_per_tile = tn // HEAD_DIM
    return pl.pallas_call(
        functools.partial(_shift_cast_kernel, shift=shift),
        grid=(kdim // tk, n // tn),
        in_specs=[pl.BlockSpec((None, tk, tn), lambda k, j: (layer, k, col0 // tn + j)),
                  pl.BlockSpec((None, tk, HEAD_DIM),
                               lambda k, j: (layer, k, col0 // HEAD_DIM + (j + 1) * lanes_per_tile))],
        out_specs=pl.BlockSpec((tk, tn), lambda k, j: (k, j)),
        out_shape=jax.ShapeDtypeStruct((kdim, n), BF16),
        compiler_params=_params(("parallel", "parallel")),
        name="shift_cast_weights",
    )(w, w)


def _window_start(t0, back, span, seq, align):
    start = jnp.minimum(jnp.maximum(t0 - back, 0), seq - span)
    return pl.multiple_of(start, align)


def _dilated_kernel(q0_ref, q1_ref, q2_ref, k_ref, v_ref, o_ref, *, tq, seq):
    t0 = pl.program_id(2) * tq
    qpos = t0 + lax.broadcasted_iota(jnp.int32, (tq, 1), 0)
    outs, lses = [], []
    for q_ref, (window, dil) in zip((q0_ref, q1_ref, q2_ref), DIL_GROUPS):
        span = min(window + tq, seq)
        start = _window_start(t0, window, span, seq, tq)
        k = k_ref[pl.ds(start, span), :]
        v = v_ref[pl.ds(start, span), :]
        s = _dot_nt(q_ref[...], k)
        kpos = start + lax.broadcasted_iota(jnp.int32, (1, span), 1)
        dist = qpos - kpos
        keep = (dist >= 0) & (dist <= window) & ((dist & (dil - 1)) == 0)
        s = jnp.where(keep, s, -jnp.inf)
        m = jnp.max(s, axis=1, keepdims=True)
        p = jnp.exp(s - m)
        l = jnp.sum(p, axis=1, keepdims=True)
        outs.append(_dot(p.astype(BF16), v) / l)
        lses.append(m + jnp.log(l))
    top = jnp.maximum(jnp.maximum(lses[0], lses[1]), lses[2])
    ws = [jnp.exp(lse - top) for lse in lses]
    y = (ws[0] * outs[0] + ws[1] * outs[1] + ws[2] * outs[2]) / (ws[0] + ws[1] + ws[2])
    o_ref[...] = y.astype(o_ref.dtype)


def _head_spec(rows, head0):
    return pl.BlockSpec((None, None, rows, HEAD_DIM), lambda b, h, n: (head0 + h, b, n, 0))


def _seq_spec(seq, head0):
    return pl.BlockSpec((None, None, seq, HEAD_DIM), lambda b, h, n: (head0 + h, b, 0, 0))


def _dilated_mixer(q, q_head0, k, k_head0, v, v_head0):
    _, nb, seq, _ = q.shape
    tq = min(ATT_TQ, seq)
    q_specs = [_head_spec(tq, q_head0 + g * A_SLOTS) for g in range(len(DIL_GROUPS))]
    return pl.pallas_call(
        functools.partial(_dilated_kernel, tq=tq, seq=seq),
        grid=(nb, A_SLOTS, seq // tq),
        in_specs=q_specs + [_seq_spec(seq, k_head0), _seq_spec(seq, v_head0)],
        out_specs=pl.BlockSpec((None, tq, HEAD_DIM), lambda b, s, n: (b, n, s)),
        out_shape=jax.ShapeDtypeStruct((nb, seq, A_SLOTS * HEAD_DIM), BF16),
        compiler_params=_params(("parallel", "parallel", "arbitrary")),
        name="dilated_mixer",
    )(q, q, q, k, v)


def _gelu_tanh(x):
    return 0.5 * x * (1.0 + jnp.tanh(math.sqrt(2.0 / math.pi) * (x + 0.044715 * (x * x * x))))


def _compress_kernel(x_ref, pe_ref, w1_ref, w2_ref, o_ref):
    x = x_ref[...].astype(F32)
    first = _dot((x + pe_ref[0:1, :]).astype(BF16), w1_ref[0])
    second = _dot((x + pe_ref[1:2, :]).astype(BF16), w1_ref[1])
    chunks = x.shape[0]
    hidden = first + pltpu.roll(second, chunks - 1, 0)
    o_ref[...] = _dot(_gelu_tanh(hidden).astype(BF16), w2_ref[...]).astype(o_ref.dtype)


def _compress(t, pe, w1, w2, name):
    nh, nb, seq, _ = t.shape
    chunks = seq // CMP_STRIDE
    width = CMP_STRIDE * HEAD_DIM
    ratio = CMP_LEN // CMP_STRIDE
    x = t.reshape(nh, nb, chunks, width)
    return pl.pallas_call(
        _compress_kernel,
        grid=(nh, nb),
        in_specs=[pl.BlockSpec((None, None, chunks, width), lambda h, b: (h, b, 0, 0)),
                  pl.BlockSpec((ratio, width), lambda h, b: (0, 0)),
                  pl.BlockSpec((ratio, width, HEAD_DIM), lambda h, b: (0, 0, 0)),
                  pl.BlockSpec((HEAD_DIM, HEAD_DIM), lambda h, b: (0, 0))],
        out_specs=pl.BlockSpec((None, None, chunks, HEAD_DIM), lambda h, b: (h, b, 0, 0)),
        out_shape=jax.ShapeDtypeStruct((nh, nb, chunks, HEAD_DIM), BF16),
        compiler_params=_params(("parallel", "parallel")),
        name=name,
    )(x, pe.reshape(ratio, width), w1.reshape(ratio, width, HEAD_DIM).astype(BF16), w2.astype(BF16))


def _group_q(q_refs):
    return jnp.concatenate([r[...] for r in q_refs], axis=0)


def _group_q_specs(tq, head0):
    return [pl.BlockSpec((None, None, tq, HEAD_DIM),
                         functools.partial(lambda b, h, n, g: (head0 + h * B_GROUP + g, b, n, 0), g=g))
            for g in range(B_GROUP)]


def _cmp_select_kernel(q0_ref, q1_ref, q2_ref, kc_ref, vc_ref, ov_ref, o_ref, sel_ref, *, tq):
    t0 = pl.program_id(2) * tq
    n_cmp = kc_ref.shape[0]
    rows = B_GROUP * tq
    s = _dot_nt(_group_q((q0_ref, q1_ref, q2_ref)), kc_ref[...])
    tpos3 = t0 + (lax.broadcasted_iota(jnp.int32, (rows, 1), 0) & (tq - 1))
    c_end = lax.broadcasted_iota(jnp.int32, (1, n_cmp), 1) * CMP_STRIDE + (CMP_LEN - 1)
    s = jnp.where(c_end - tpos3 <= 0, s, -jnp.inf)
    m = jnp.max(s, axis=1, keepdims=True)
    m = jnp.where(jnp.abs(m) < jnp.inf, m, 0.0)
    e = jnp.exp(s - m)
    den = jnp.sum(e, axis=1, keepdims=True)
    p = e / jnp.where(den > 0, den, 1.0)
    o_ref[...] = _dot(p.astype(BF16), vc_ref[...]).reshape(B_GROUP, tq, HEAD_DIM).astype(o_ref.dtype)

    p_sum = p[0:tq] + p[tq:2 * tq] + p[2 * tq:3 * tq]
    hi, lo = _split_bf16(p_sum)
    imp = _dot(hi, ov_ref[...]) + _dot(lo, ov_ref[...])
    tpos = t0 + lax.broadcasted_iota(jnp.int32, (tq, 1), 0)
    rel = lax.broadcasted_iota(jnp.int32, (1, HEAD_DIM), 1) - (tpos >> SEL_SHIFT)
    j_abs = jnp.broadcast_to(lax.broadcasted_iota(jnp.int32, (1, HEAD_DIM), 1), rel.shape)
    forced = (j_abs == 0) | (rel == 0) | (rel == -1)
    imp = jnp.where(forced, jnp.inf, jnp.where(rel <= 0, imp, -jnp.inf))

    n_sel = HEAD_DIM // 2
    imp_t = imp.T
    mine = imp_t[0:n_sel]
    j_idx = lax.broadcasted_iota(jnp.int32, (n_sel, tq), 0)
    beaten = jnp.zeros((n_sel, tq), F32)
    for kk in range(n_sel):
        other = imp_t[kk:kk + 1, :]
        wins = (other > mine) | ((other == mine) & (j_idx > kk))
        beaten = beaten + jnp.where(wins, 1.0, 0.0)
    chosen = jnp.where(beaten < SEL_TOPK, 1.0, 0.0)
    chosen = jnp.concatenate([chosen, jnp.zeros((HEAD_DIM - n_sel, tq), F32)], axis=0)
    sel_ref[...] = chosen.T.astype(sel_ref.dtype)


def _overlap_matrix(n_cmp_rows, n_sel):
    c_start = np.arange(n_cmp_rows) * CMP_STRIDE
    c_end = c_start + CMP_LEN - 1
    s_start = np.arange(HEAD_DIM) * SEL_LEN
    ov = (c_start[:, None] <= s_start[None, :] + SEL_LEN - 1) & (c_end[:, None] >= s_start[None, :])
    ov &= (np.arange(HEAD_DIM) < n_sel)[None, :]
    return jnp.asarray(ov.astype(np.float32), dtype=BF16)


def _group_o_spec(tq):
    return pl.BlockSpec((B_GROUP, None, tq, HEAD_DIM), lambda b, h, n: (h, b, n, 0))


def _cmp_select(q, q_head0, kc, vc):
    _, nb, seq, _ = q.shape
    assert seq // SEL_LEN <= HEAD_DIM // 2
    tq = min(ATT_TQ, seq)
    n_cmp = kc.shape[2]
    kv_spec = pl.BlockSpec((None, None, n_cmp, HEAD_DIM), lambda b, h, n: (h, b, 0, 0))
    return pl.pallas_call(
        functools.partial(_cmp_select_kernel, tq=tq),
        grid=(nb, B_KV_HEADS, seq // tq),
        in_specs=_group_q_specs(tq, q_head0) + [kv_spec, kv_spec,
                                                pl.BlockSpec((n_cmp, HEAD_DIM), lambda b, h, n: (0, 0))],
        out_specs=[_group_o_spec(tq), _head_spec(tq, 0)],
        out_shape=[jax.ShapeDtypeStruct((B_HEADS, nb, seq, HEAD_DIM), BF16),
                   jax.ShapeDtypeStruct((B_KV_HEADS, nb, seq, HEAD_DIM), BF16)],
        compiler_params=_params(("parallel", "parallel", "arbitrary")),
        name="nsa_compressed_select",
    )(q, q, q, kc, vc, _overlap_matrix(n_cmp, seq // SEL_LEN))


def _selected_kernel(q0_ref, q1_ref, q2_ref, k_ref, v_ref, sel_ref, o_ref, m_ref, l_ref, acc_ref, *, tq, kc):
    t0 = pl.program_id(2) * tq
    q = _group_q((q0_ref, q1_ref, q2_ref))
    sel = sel_ref[...]
    qpos = t0 + lax.broadcasted_iota(jnp.int32, (tq, 1), 0)
    blk = lax.broadcasted_iota(jnp.int32, (HEAD_DIM, 1), 0)
    key_in_chunk = lax.broadcasted_iota(jnp.int32, (1, kc), 1)
    m_ref[...] = jnp.full(m_ref.shape, -1e30, F32)
    l_ref[...] = jnp.zeros(l_ref.shape, F32)
    acc_ref[...] = jnp.zeros(acc_ref.shape, F32)

    def step(i, carry):
        k0 = pl.multiple_of(i * kc, kc)
        k = k_ref[pl.ds(k0, kc), :]
        v = v_ref[pl.ds(k0, kc), :]
        expand = jnp.where(blk - (k0 >> SEL_SHIFT) == (key_in_chunk >> SEL_SHIFT), 1.0, 0.0).astype(BF16)
        picked = _dot(sel, expand)
        bias = jnp.where((picked > 0.5) & (k0 + key_in_chunk - qpos <= 0), 0.0, -jnp.inf)
        s = _dot_nt(q, k) + jnp.concatenate([bias] * B_GROUP, axis=0)
        m_old = m_ref[...]
        m_new = jnp.maximum(m_old, jnp.max(s, axis=1, keepdims=True))
        p = jnp.exp(s - m_new)
        alpha = jnp.exp(m_old - m_new)
        l_ref[...] = alpha * l_ref[...] + jnp.sum(p, axis=1, keepdims=True)
        acc_ref[...] = alpha * acc_ref[...] + _dot(p.astype(BF16), v)
        m_ref[...] = m_new
        return carry

    lax.fori_loop(0, t0 // kc + 1, step, 0)
    o_ref[...] = (acc_ref[...] / l_ref[...]).reshape(B_GROUP, tq, HEAD_DIM).astype(o_ref.dtype)


def _selected(q, q_head0, k, k_head0, v, v_head0, sel):
    _, nb, seq, _ = q.shape
    tq = min(ATT_TQ, seq)
    kc = min(SEL_CHUNK, seq)
    assert kc % tq == 0 and seq % kc == 0
    rows = B_GROUP * tq
    return pl.pallas_call(
        functools.partial(_selected_kernel, tq=tq, kc=kc),
        grid=(nb, B_KV_HEADS, seq // tq),
        in_specs=_group_q_specs(tq, q_head0) + [_seq_spec(seq, k_head0), _seq_spec(seq, v_head0),
                                                _head_spec(tq, 0)],
        out_specs=_group_o_spec(tq),
        out_shape=jax.ShapeDtypeStruct((B_HEADS, nb, seq, HEAD_DIM), BF16),
        scratch_shapes=[pltpu.VMEM((rows, 1), F32), pltpu.VMEM((rows, 1), F32),
                        pltpu.VMEM((rows, HEAD_DIM), F32)],
        compiler_params=_params(("parallel", "parallel", "arbitrary")),
        name="nsa_selected",
    )(q, q, q, k, v, sel)


def _window_kernel(q0_ref, q1_ref, q2_ref, k_ref, v_ref, o_ref, *, tq, seq):
    t0 = pl.program_id(2) * tq
    rows = B_GROUP * tq
    span = min(WIN_LEN + tq, seq)
    start = _window_start(t0, WIN_LEN, span, seq, tq)
    s = _dot_nt(_group_q((q0_ref, q1_ref, q2_ref)), k_ref[pl.ds(start, span), :])
    qpos = t0 + (lax.broadcasted_iota(jnp.int32, (rows, 1), 0) & (tq - 1))
    kpos = start + lax.broadcasted_iota(jnp.int32, (1, span), 1)
    dist = qpos - kpos
    s = jnp.where((dist >= 0) & (dist <= WIN_LEN - 1), s, -jnp.inf)
    m = jnp.max(s, axis=1, keepdims=True)
    p = jnp.exp(s - m)
    l = jnp.sum(p, axis=1, keepdims=True)
    o = _dot(p.astype(BF16), v_ref[pl.ds(start, span), :]) / l
    o_ref[...] = o.reshape(B_GROUP, tq, HEAD_DIM).astype(o_ref.dtype)


def _window(q, q_head0, k, k_head0, v, v_head0):
    _, nb, seq, _ = q.shape
    tq = min(ATT_TQ, seq)
    assert WIN_LEN % tq == 0
    return pl.pallas_call(
        functools.partial(_window_kernel, tq=tq, seq=seq),
        grid=(nb, B_KV_HEADS, seq // tq),
        in_specs=_group_q_specs(tq, q_head0) + [_seq_spec(seq, k_head0), _seq_spec(seq, v_head0)],
        out_specs=_group_o_spec(tq),
        out_shape=jax.ShapeDtypeStruct((B_HEADS, nb, seq, HEAD_DIM), BF16),
        compiler_params=_params(("parallel", "parallel", "arbitrary")),
        name="nsa_window",
    )(q, q, q, k, v)


def _nsa_gate_kernel(g_ref, oc_ref, os_ref, ow_ref, y_ref):
    g = g_ref[...]
    for h in range(B_HEADS):
        y = (g[:, 3 * h:3 * h + 1] * oc_ref[h].astype(F32)
             + g[:, 3 * h + 1:3 * h + 2] * os_ref[h].astype(F32)
             + g[:, 3 * h + 2:3 * h + 3] * ow_ref[h].astype(F32))
        y_ref[:, h * HEAD_DIM:(h + 1) * HEAD_DIM] = y.astype(y_ref.dtype)


def _nsa_gate(gates, o_cmp, o_sel, o_win):
    rows = gates.shape[0]
    tm = min(256, rows)
    o_spec = pl.BlockSpec((B_HEADS, tm, HEAD_DIM), lambda i: (0, i, 0))
    return pl.pallas_call(
        _nsa_gate_kernel,
        grid=(rows // tm,),
        in_specs=[pl.BlockSpec((tm, HEAD_DIM), lambda i: (i, 0)), o_spec, o_spec, o_spec],
        out_specs=pl.BlockSpec((tm, B_HEADS * HEAD_DIM), lambda i: (i, 0)),
        out_shape=jax.ShapeDtypeStruct((rows, B_HEADS * HEAD_DIM), BF16),
        compiler_params=_params(("parallel",)),
        name="nsa_gate",
    )(gates, o_cmp, o_sel, o_win)


def _stick_kernel(q_ref, k_ref, v_ref, o_ref, acc_ref, car_ref, *, tq):
    n = pl.program_id(2)
    q = q_ref[...]
    sub = min(CUM_TILE, tq)
    n_sub = tq // sub
    r_idx = lax.broadcasted_iota(jnp.int32, (sub, 2 * sub), 0)
    c_idx = lax.broadcasted_iota(jnp.int32, (sub, 2 * sub), 1)
    suffix_and_total = jnp.where((c_idx >= sub) | (r_idx > c_idx), 1.0, 0.0).astype(BF16)
    q_in_tile = lax.broadcasted_iota(jnp.int32, (tq, 1), 0)
    k_in_sub = lax.broadcasted_iota(jnp.int32, (1, sub), 1)
    acc_ref[...] = jnp.zeros(acc_ref.shape, F32)
    car_ref[...] = jnp.zeros(car_ref.shape, F32)

    def tile(k0, diagonal):
        z = _dot_nt(q, k_ref[pl.ds(k0, tq), :])
        soft = jnp.log(1.0 + jnp.exp(-jnp.abs(z)))
        log_beta = jnp.minimum(z, 0.0) - soft
        log_rest = log_beta - z
        carried = car_ref[...]
        parts = [None] * n_sub
        for u in reversed(range(n_sub)):
            cols = slice(u * sub, (u + 1) * sub)
            rest_u = log_rest[:, cols]
            if diagonal:
                before = (u * sub + k_in_sub) - q_in_tile < 0
                rest_u = jnp.where(before, rest_u, 0.0)
            hi, lo = _split_bf16(rest_u)
            sums = _dot(hi, suffix_and_total) + _dot(lo, suffix_and_total)
            a = jnp.exp(log_beta[:, cols] + (sums[:, :sub] + carried))
            if diagonal:
                a = jnp.where(before, a, 0.0)
            carried = carried + sums[:, sub:]
            parts[u] = a.astype(BF16)
        car_ref[...] = carried
        acc_ref[...] += _dot(jnp.concatenate(parts, axis=1), v_ref[pl.ds(k0, tq), :])

    tile(pl.multiple_of(n * tq, tq), True)

    def step(i, carry):
        tile(pl.multiple_of((n - 1 - i) * tq, tq), False)
        return carry

    lax.fori_loop(0, n, step, 0)
    o_ref[...] = acc_ref[...].astype(o_ref.dtype)


def _stick_breaking(q, q_head0, k, k_head0, v, v_head0):
    _, nb, seq, _ = q.shape
    tq = min(STICK_TILE, seq)
    return pl.pallas_call(
        functools.partial(_stick_kernel, tq=tq),
        grid=(nb, C_HEADS, seq // tq),
        in_specs=[_head_spec(tq, q_head0), _seq_spec(seq, k_head0), _seq_spec(seq, v_head0)],
        out_specs=pl.BlockSpec((None, tq, HEAD_DIM), lambda b, h, n: (b, n, h)),
        out_shape=jax.ShapeDtypeStruct((nb, seq, C_HEADS * HEAD_DIM), BF16),
        scratch_shapes=[pltpu.VMEM((tq, HEAD_DIM), F32), pltpu.VMEM((tq, min(CUM_TILE, tq)), F32)],
        compiler_params=_params(("parallel", "parallel", "arbitrary")),
        name="stick_breaking",
    )(q, k, v)


def _merge_kernel(ya_ref, yb_ref, yc_ref, wa_ref, wb_ref, wc_ref, ga_ref, gb_ref, gc_ref, o_ref):
    mixed = (ga_ref[...].astype(F32) * _dot(ya_ref[...], wa_ref[...])
             + gb_ref[...].astype(F32) * _dot(yb_ref[...], wb_ref[...])
             + gc_ref[...].astype(F32) * _dot(yc_ref[...], wc_ref[...]))
    o_ref[...] = mixed.astype(o_ref.dtype)


def _merge(ya, yb, yc, wa, wb, wc, gates, d_model):
    m = ya.shape[0]
    tm, tn = min(512, m), 1024
    col_tiles = d_model // tn

    def y_spec(y):
        return pl.BlockSpec((tm, y.shape[1]), lambda i, j: (i, 0))

    def w_spec(w):
        return pl.BlockSpec((w.shape[0], tn), lambda i, j: (0, j))

    def g_spec(branch):
        return pl.BlockSpec((tm, tn), lambda i, j: (i, branch * col_tiles + j))

    return pl.pallas_call(
        _merge_kernel,
        grid=(m // tm, col_tiles),
        in_specs=[y_spec(ya), y_spec(yb), y_spec(yc), w_spec(wa), w_spec(wb), w_spec(wc),
                  g_spec(0), g_spec(1), g_spec(2)],
        out_specs=pl.BlockSpec((tm, tn), lambda i, j: (i, j)),
        out_shape=jax.ShapeDtypeStruct((m, d_model), BF16),
        compiler_params=_params(("parallel", "arbitrary")),
        name="branch_merge",
    )(ya, yb, yc, wa, wb, wc, gates, gates, gates)


def _ln_kernel(x_ref, y_ref, g_ref, b_ref, o_ref, ob_ref):
    z = ALPHA * x_ref[...] + y_ref[...]
    mu = jnp.mean(z, axis=1, keepdims=True)
    zc = z - mu
    var = jnp.mean(zc * zc, axis=1, keepdims=True)
    out = zc * lax.rsqrt(var + LN_EPS) * g_ref[...] + b_ref[...]
    o_ref[...] = out
    ob_ref[...] = out.astype(BF16)


def _residual_ln(x, y, g, b):
    m, d = x.shape
    tm = min(128, m)
    row = pl.BlockSpec((tm, d), lambda i: (i, 0))
    vec = pl.BlockSpec((1, d), lambda i: (0, 0))
    return pl.pallas_call(
        _ln_kernel,
        grid=(m // tm,),
        in_specs=[row, row, vec, vec],
        out_specs=[row, row],
        out_shape=[jax.ShapeDtypeStruct((m, d), F32), jax.ShapeDtypeStruct((m, d), BF16)],
        compiler_params=_params(("parallel",)),
        name="residual_layer_norm",
    )(x, y, g.reshape(1, d), b.reshape(1, d))


def _rope_tables(seq):
    half = HEAD_DIM // 2
    inv = 1.0 / (ROPE_THETA ** (jnp.arange(half, dtype=F32) / half))
    ang = jnp.arange(seq).astype(F32)[:, None] * inv[None, :]
    cos, sin = jnp.cos(ang), jnp.sin(ang)
    return jnp.concatenate([cos, cos], axis=1), jnp.concatenate([-sin, sin], axis=1)


_A_KV, _B_KV = A_SLOTS, B_KV_HEADS
_ALIGNED_FIELDS = (('q_a', A_HEADS, 'rope_scale'), ('k_a', _A_KV, 'rope'), ('v_a', _A_KV, 'none'),
                   ('q_b', B_HEADS, 'rope_scale'), ('kc_b', _B_KV, 'rope'), ('vc_b', _B_KV, 'none'),
                   ('ks_b', _B_KV, 'rope'), ('vs_b', _B_KV, 'none'), ('kw_b', _B_KV, 'rope'),
                   ('vw_b', _B_KV, 'none'))
_STICK_FIELDS = (('q_c', C_HEADS, 'scale'), ('k_c', C_HEADS, 'none'), ('v_c', C_HEADS, 'none'))
N_NSA_GATES = 3 * B_HEADS


def _field_layout(fields, heads_per_tile):
    head0, modes, start = {}, [], 0
    for name, heads, mode in fields:
        assert heads % heads_per_tile == 0
        head0[name] = start
        modes += [mode] * (heads // heads_per_tile)
        start += heads
    return head0, tuple(modes), start


def _layer(layer, x, xb, nb, seq, tables, w_in, cmp_pe_k, cmp_wk1, cmp_wk2, cmp_pe_v, cmp_wv1, cmp_wv2,
           w_br_a, w_br_b, w_br_c, w_out, ln1_g, ln1_b, w_up, w_down, ln2_g, ln2_b):
    m, d_model = x.shape

    tn_att = 2 * HEAD_DIM
    at, att_modes, att_heads = _field_layout(_ALIGNED_FIELDS, tn_att // HEAD_DIM)
    att = _matmul(xb, w_in, layer=layer, name="proj_attention", out_dtype=BF16, tile_modes=att_modes,
                  tm=1024, tn=tn_att, head_major=True, rope_tables=tables)
    gate_col0 = att_heads * HEAD_DIM
    g_nsa = _matmul(xb, w_in, layer=layer, col0=gate_col0, name="proj_nsa_gates", out_dtype=F32,
                    tile_modes=('sigmoid',), tm=1024, tn=HEAD_DIM)
    tn_tail = 1024
    st, stick_modes, stick_heads = _field_layout(_STICK_FIELDS, tn_tail // HEAD_DIM)
    tail_cols = stick_heads * HEAD_DIM + 3 * d_model
    w_tail = _shift_cast(w_in, layer, gate_col0, N_NSA_GATES, tail_cols)
    stick = _matmul(xb, w_tail, name="proj_stick", out_dtype=BF16, tile_modes=stick_modes,
                    tm=1024, tn=tn_tail, head_major=True)
    gates = _matmul(xb, w_tail, col0=stick_heads * HEAD_DIM, name="proj_branch_gates", out_dtype=BF16,
                    tile_modes=('sigmoid',) * (3 * d_model // tn_tail), tm=1024, tn=tn_tail)

    att = att.reshape(att_heads, nb, seq, HEAD_DIM)
    stick = stick.reshape(stick_heads, nb, seq, HEAD_DIM)

    y_a = _dilated_mixer(att, at['q_a'], att, at['k_a'], att, at['v_a'])

    kc = _compress(att[at['kc_b']:at['kc_b'] + B_KV_HEADS], cmp_pe_k, cmp_wk1, cmp_wk2, "nsa_compress_k")
    vc = _compress(att[at['vc_b']:at['vc_b'] + B_KV_HEADS], cmp_pe_v, cmp_wv1, cmp_wv2, "nsa_compress_v")
    o_cmp, sel = _cmp_select(att, at['q_b'], kc, vc)
    o_sel = _selected(att, at['q_b'], att, at['ks_b'], att, at['vs_b'], sel)
    o_win = _window(att, at['q_b'], att, at['kw_b'], att, at['vw_b'])
    y_b = _nsa_gate(g_nsa, o_cmp.reshape(B_HEADS, m, HEAD_DIM), o_sel.reshape(B_HEADS, m, HEAD_DIM),
                    o_win.reshape(B_HEADS, m, HEAD_DIM))

    y_c = _stick_breaking(stick, st['q_c'], stick, st['k_c'], stick, st['v_c'])

    merged = _merge(y_a.reshape(m, -1), y_b, y_c.reshape(m, -1), w_br_a.astype(BF16),
                    w_br_b.astype(BF16), w_br_c.astype(BF16), gates, d_model)
    mixed = _matmul(merged, w_out, layer=layer, name="out_proj", out_dtype=F32,
                    tile_modes=('none',) * (d_model // 256), tm=1024, tn=256)
    x1, x1b = _residual_ln(x, mixed, ln1_g, ln1_b)

    d_ff = w_up.shape[2]
    hidden = _matmul(x1b, w_up, layer=layer, name="mlp_up", out_dtype=BF16,
                     tile_modes=('relu2',) * (d_ff // 256), tm=1024, tn=256)
    down = _matmul(hidden, w_down, layer=layer, name="mlp_down", out_dtype=F32,
                   tile_modes=('none',) * (d_model // 1024), tm=1024, tn=1024, tk=2048)
    return _residual_ln(x1, down, ln2_g, ln2_b)


def kernel(x, w_in, cmp_pe_k, cmp_wk1, cmp_wk2, cmp_pe_v, cmp_wv1, cmp_wv2, w_br_a, w_br_b, w_br_c,
           w_out, ln1_g, ln1_b, w_up, w_down, ln2_g, ln2_b):
    nb, seq, d_model = x.shape
    tables = _rope_tables(seq)
    xf = x.reshape(nb * seq, d_model)
    xb = xf.astype(BF16)
    for l in range(w_in.shape[0]):
        xf, xb = _layer(l, xf, xb, nb, seq, tables, w_in, cmp_pe_k[l], cmp_wk1[l], cmp_wk2[l],
                        cmp_pe_v[l], cmp_wv1[l], cmp_wv2[l], w_br_a[l], w_br_b[l], w_br_c[l],
                        w_out, ln1_g[l], ln1_b[l], w_up, w_down, ln2_g[l], ln2_b[l])
    return xf.reshape(nb, seq, d_model)
```

```python
import functools
import math

import numpy as np
import jax
import jax.numpy as jnp
from jax import lax
from jax.experimental import pallas as pl
from jax.experimental.pallas import tpu as pltpu

F32 = jnp.float32
BF16 = jnp.bfloat16

HEAD_DIM = 128
ROPE_THETA = 10000.0
LN_EPS = 1e-5
DEPTH = 2

DIL_GROUPS = ((128, 1), (512, 4), (2048, 16))
A_SLOTS = 4
A_HEADS = A_SLOTS * len(DIL_GROUPS)

B_HEADS = 12
B_KV_HEADS = 4
B_GROUP = B_HEADS // B_KV_HEADS
CMP_LEN = 32
CMP_STRIDE = 16
SEL_LEN = 64
SEL_SHIFT = 6
SEL_TOPK = 16
WIN_LEN = 512

C_HEADS = 8

ALPHA = (2.0 * DEPTH) ** 0.25
Q_SCALE = HEAD_DIM ** -0.5

VMEM_LIMIT = 48 * 1024 * 1024
ATT_TQ = 128
SEL_TQ = 128
SEL_CHUNK = 512
STICK_TILE = 512
CUM_TILE = 128


def _params(semantics):
    return pltpu.CompilerParams(dimension_semantics=semantics, vmem_limit_bytes=VMEM_LIMIT)


def _dot(a, b):
    return jnp.dot(a, b, preferred_element_type=F32)


def _dot_nt(a, b):
    return lax.dot_general(a, b, (((1,), (1,)), ((), ())), preferred_element_type=F32)


def _split_bf16(x):
    hi = x.astype(BF16)
    lo = (x - hi.astype(F32)).astype(BF16)
    return hi, lo


def _apply_mode(y, mode, cos_ref, sin_ref):
    if mode in ('rope', 'rope_scale'):
        y = y * cos_ref[...] + pltpu.roll(y, HEAD_DIM // 2, 1) * sin_ref[...]
    if mode in ('rope_scale', 'scale'):
        y = y * Q_SCALE
    if mode == 'sigmoid':
        y = jax.nn.sigmoid(y)
    if mode == 'relu2':
        y = jnp.square(jnp.maximum(y, 0.0))
    return y


def _write_tile(acc, o_ref, mode, head_major, cos_ref, sin_ref):
    if head_major:
        for c in range(acc.shape[1] // HEAD_DIM):
            y = _apply_mode(acc[:, c * HEAD_DIM:(c + 1) * HEAD_DIM], mode, cos_ref, sin_ref)
            o_ref[c] = y.astype(o_ref.dtype)
    else:
        o_ref[...] = _apply_mode(acc, mode, cos_ref, sin_ref).astype(o_ref.dtype)


def _matmul_kernel(*refs, tile_modes, head_major, n_k, use_rope, acc_in_out, b_rows):
    refs = list(refs)
    a_ref, b_ref = refs[0], refs[1]
    cos_ref, sin_ref = (refs[2], refs[3]) if use_rope else (None, None)
    o_ref = refs[4] if use_rope else refs[2]
    acc_ref = o_ref if acc_in_out else (refs[-1] if n_k > 1 else None)
    if b_rows:
        part = _dot_nt(a_ref[...], b_ref[...].astype(BF16))
    else:
        part = _dot(a_ref[...], b_ref[...].astype(BF16))

    def finish(acc):
        modes = sorted(set(tile_modes))
        if len(modes) == 1:
            _write_tile(acc, o_ref, modes[0], head_major, cos_ref, sin_ref)
            return
        j = pl.program_id(1)
        for mode in modes:
            hit = None
            for c, tile_mode in enumerate(tile_modes):
                if tile_mode == mode:
                    hit = (j == c) if hit is None else (hit | (j == c))
            pl.when(hit)(functools.partial(_write_tile, acc, o_ref, mode, head_major, cos_ref, sin_ref))

    if n_k == 1:
        finish(part)
        return
    k = pl.program_id(2)

    @pl.when(k == 0)
    def _():
        acc_ref[...] = part

    @pl.when(k > 0)
    def _():
        acc_ref[...] += part

    if not acc_in_out:
        pl.when(k == n_k - 1)(lambda: finish(acc_ref[...]))


def _matmul(a, b, *, name, out_dtype, tile_modes, tm, tn, tk=None, layer=None, col0=0,
            head_major=False, rope_tables=None, b_rows=False):
    m, kdim = a.shape
    n = tn * len(tile_modes)
    tm = min(tm, m)
    tk = kdim if tk is None else min(tk, kdim)
    assert m % tm == 0 and kdim % tk == 0 and col0 % tn == 0
    n_k = kdim // tk
    col_tile0 = col0 // tn
    use_rope = any(mode.startswith('rope') for mode in tile_modes)
    acc_in_out = n_k > 1 and out_dtype == F32 and set(tile_modes) == {'none'} and not head_major
    if b_rows:
        assert layer is not None
        b_spec = pl.BlockSpec((None, tn, tk), lambda i, j, k: (layer, col_tile0 + j, k))
    elif layer is None:
        b_spec = pl.BlockSpec((tk, tn), lambda i, j, k: (k, col_tile0 + j))
    else:
        b_spec = pl.BlockSpec((None, tk, tn), lambda i, j, k: (layer, k, col_tile0 + j))
    in_specs = [pl.BlockSpec((tm, tk), lambda i, j, k: (i, k)), b_spec]
    operands = [a, b]
    if use_rope:
        assert head_major
        cos, sin = rope_tables
        seq_tiles = cos.shape[0] // tm
        assert cos.shape[0] % tm == 0
        spec = pl.BlockSpec((tm, HEAD_DIM), lambda i, j, k: (i % seq_tiles, 0))
        in_specs += [spec, spec]
        operands += [cos, sin]
    if head_major:
        out_shape = jax.ShapeDtypeStruct((n // HEAD_DIM, m, HEAD_DIM), out_dtype)
        out_spec = pl.BlockSpec((tn // HEAD_DIM, tm, HEAD_DIM), lambda i, j, k: (j, i, 0))
    else:
        out_shape = jax.ShapeDtypeStruct((m, n), out_dtype)
        out_spec = pl.BlockSpec((tm, tn), lambda i, j, k: (i, j))
    scratch = [pltpu.VMEM((tm, tn), F32)] if (n_k > 1 and not acc_in_out) else []
    return pl.pallas_call(
        functools.partial(_matmul_kernel, tile_modes=tuple(tile_modes), head_major=head_major, n_k=n_k,
                          use_rope=use_rope, acc_in_out=acc_in_out, b_rows=b_rows),
        grid=(m // tm, len(tile_modes), n_k),
        in_specs=in_specs,
        out_specs=out_spec,
        out_shape=out_shape,
        scratch_shapes=scratch,
        compiler_params=_params(("parallel", "parallel", "arbitrary")),
        name=name,
    )(*operands)


def _gather_rows_kernel(w_ref, o_ref, *, tn, k_chunks, n_layers):
    stride = k_chunks * n_layers
    for layer in range(n_layers):
        pieces = [w_ref[pl.ds(c * n_layers + layer, tn, stride=stride), :] for c in range(k_chunks)]
        o_ref[layer] = jnp.concatenate(pieces, axis=1).astype(o_ref.dtype)


def _input_weight_rows(w_in, aligned_cols, n_gates, tn=HEAD_DIM):
    n_layers, kdim, n = w_in.shape
    k_chunks = kdim // HEAD_DIM
    tail = n - aligned_cols - n_gates
    assert aligned_cols % tn == 0 and tail % tn == 0 and kdim % HEAD_DIM == 0
    aligned_tiles, tail_tiles = aligned_cols // tn, tail // tn
    rows_per_col = k_chunks * n_layers
    flat = w_in.reshape(n_layers, k_chunks, HEAD_DIM, n).transpose(3, 1, 0, 2).reshape(n * rows_per_col, HEAD_DIM)

    def source_row(j):
        col = jnp.where(j < aligned_tiles, tn * j,
                        jnp.where(j < aligned_tiles + tail_tiles,
                                  aligned_cols + n_gates + tn * (j - aligned_tiles), aligned_cols))
        return (col * rows_per_col, 0)

    tiles = aligned_tiles + tail_tiles + 1
    return pl.pallas_call(
        functools.partial(_gather_rows_kernel, tn=tn, k_chunks=k_chunks, n_layers=n_layers),
        grid=(tiles,),
        in_specs=[pl.BlockSpec((pl.Element(tn * rows_per_col), pl.Element(HEAD_DIM)), source_row)],
        out_specs=pl.BlockSpec((n_layers, tn, kdim), lambda j: (0, j, 0)),
        out_shape=jax.ShapeDtypeStruct((n_layers, tiles * tn, kdim), BF16),
        compiler_params=_params(("parallel",)),
        name="input_weight_rows",
    )(flat)


def _window_start(t0, back, span, seq, align):
    start = jnp.minimum(jnp.maximum(t0 - back, 0), seq - span)
    return pl.multiple_of(start, align)


def _dilated_kernel(q0_ref, q1_ref, q2_ref, k_ref, v_ref, o_ref, *, tq, seq):
    t0 = pl.program_id(2) * tq
    qpos = t0 + lax.broadcasted_iota(jnp.int32, (tq, 1), 0)
    outs, lses = [], []
    for q_ref, (window, dil) in zip((q0_ref, q1_ref, q2_ref), DIL_GROUPS):
        span = min(window + tq, seq)
        start = _window_start(t0, window, span, seq, tq)
        k = k_ref[pl.ds(start, span), :]
        v = v_ref[pl.ds(start, span), :]
        s = _dot_nt(q_ref[...], k)
        kpos = start + lax.broadcasted_iota(jnp.int32, (1, span), 1)
        dist = qpos - kpos
        keep = (dist >= 0) & (dist <= window) & ((dist & (dil - 1)) == 0)
        s = jnp.where(keep, s, -jnp.inf)
        m = jnp.max(s, axis=1, keepdims=True)
        p = jnp.exp(s - m)
        l = jnp.sum(p, axis=1, keepdims=True)
        outs.append(_dot(p.astype(BF16), v) / l)
        lses.append(m + jnp.log(l))
    top = jnp.maximum(jnp.maximum(lses[0], lses[1]), lses[2])
    ws = [jnp.exp(lse - top) for lse in lses]
    y = (ws[0] * outs[0] + ws[1] * outs[1] + ws[2] * outs[2]) / (ws[0] + ws[1] + ws[2])
    o_ref[...] = y.astype(o_ref.dtype)


def _head_spec(rows, head0):
    return pl.BlockSpec((None, None, rows, HEAD_DIM), lambda b, h, n: (head0 + h, b, n, 0))


def _seq_spec(seq, head0):
    return pl.BlockSpec((None, None, seq, HEAD_DIM), lambda b, h, n: (head0 + h, b, 0, 0))


def _dilated_mixer(q, q_head0, k, k_head0, v, v_head0):
    _, nb, seq, _ = q.shape
    tq = min(ATT_TQ, seq)
    q_specs = [_head_spec(tq, q_head0 + g * A_SLOTS) for g in range(len(DIL_GROUPS))]
    return pl.pallas_call(
        functools.partial(_dilated_kernel, tq=tq, seq=seq),
        grid=(nb, A_SLOTS, seq // tq),
        in_specs=q_specs + [_seq_spec(seq, k_head0), _seq_spec(seq, v_head0)],
        out_specs=pl.BlockSpec((None, tq, HEAD_DIM), lambda b, s, n: (b, n, s)),
        out_shape=jax.ShapeDtypeStruct((nb, seq, A_SLOTS * HEAD_DIM), BF16),
        compiler_params=_params(("parallel", "parallel", "arbitrary")),
        name="dilated_mixer",
    )(q, q, q, k, v)


def _gelu_tanh(x):
    return 0.5 * x * (1.0 + jnp.tanh(math.sqrt(2.0 / math.pi) * (x + 0.044715 * (x * x * x))))


def _compress_kernel(x_ref, pe_ref, w1_ref, w2_ref, o_ref):
    x = x_ref[...].astype(F32)
    first = _dot((x + pe_ref[0:1, :]).astype(BF16), w1_ref[0])
    second = _dot((x + pe_ref[1:2, :]).astype(BF16), w1_ref[1])
    chunks = x.shape[0]
    hidden = first + pltpu.roll(second, chunks - 1, 0)
    o_ref[...] = _dot(_gelu_tanh(hidden).astype(BF16), w2_ref[...]).astype(o_ref.dtype)


def _compress(t, pe, w1, w2, name):
    nh, nb, seq, _ = t.shape
    chunks = seq // CMP_STRIDE
    width = CMP_STRIDE * HEAD_DIM
    ratio = CMP_LEN // CMP_STRIDE
    x = t.reshape(nh, nb, chunks, width)
    return pl.pallas_call(
        _compress_kernel,
        grid=(nh, nb),
        in_specs=[pl.BlockSpec((None, None, chunks, width), lambda h, b: (h, b, 0, 0)),
                  pl.BlockSpec((ratio, width), lambda h, b: (0, 0)),
                  pl.BlockSpec((ratio, width, HEAD_DIM), lambda h, b: (0, 0, 0)),
                  pl.BlockSpec((HEAD_DIM, HEAD_DIM), lambda h, b: (0, 0))],
        out_specs=pl.BlockSpec((None, None, chunks, HEAD_DIM), lambda h, b: (h, b, 0, 0)),
        out_shape=jax.ShapeDtypeStruct((nh, nb, chunks, HEAD_DIM), BF16),
        compiler_params=_params(("parallel", "parallel")),
        name=name,
    )(x, pe.reshape(ratio, width), w1.reshape(ratio, width, HEAD_DIM).astype(BF16), w2.astype(BF16))


def _group_q(q_refs):
    return jnp.concatenate([r[...] for r in q_refs], axis=0)


def _group_q_specs(tq, head0):
    return [pl.BlockSpec((None, None, tq, HEAD_DIM),
                         functools.partial(lambda b, h, n, g: (head0 + h * B_GROUP + g, b, n, 0), g=g))
            for g in range(B_GROUP)]


def _cmp_select_kernel(q0_ref, q1_ref, q2_ref, kc_ref, vc_ref, ov_ref, o_ref, sel_ref, *, tq):
    t0 = pl.program_id(2) * tq
    n_cmp = kc_ref.shape[0]
    rows = B_GROUP * tq
    s = _dot_nt(_group_q((q0_ref, q1_ref, q2_ref)), kc_ref[...])
    tpos3 = t0 + (lax.broadcasted_iota(jnp.int32, (rows, 1), 0) & (tq - 1))
    c_end = lax.broadcasted_iota(jnp.int32, (1, n_cmp), 1) * CMP_STRIDE + (CMP_LEN - 1)
    s = jnp.where(c_end - tpos3 <= 0, s, -jnp.inf)
    m = jnp.max(s, axis=1, keepdims=True)
    m = jnp.where(jnp.abs(m) < jnp.inf, m, 0.0)
    e = jnp.exp(s - m)
    den = jnp.sum(e, axis=1, keepdims=True)
    p = e / jnp.where(den > 0, den, 1.0)
    o_ref[...] = _dot(p.astype(BF16), vc_ref[...]).reshape(B_GROUP, tq, HEAD_DIM).astype(o_ref.dtype)

    p_sum = p[0:tq] + p[tq:2 * tq] + p[2 * tq:3 * tq]
    hi, lo = _split_bf16(p_sum)
    imp = _dot(hi, ov_ref[...]) + _dot(lo, ov_ref[...])
    tpos = t0 + lax.broadcasted_iota(jnp.int32, (tq, 1), 0)
    rel = lax.broadcasted_iota(jnp.int32, (1, HEAD_DIM), 1) - (tpos >> SEL_SHIFT)
    j_abs = jnp.broadcast_to(lax.broadcasted_iota(jnp.int32, (1, HEAD_DIM), 1), rel.shape)
    forced = (j_abs == 0) | (rel == 0) | (rel == -1)
    imp = jnp.where(forced, jnp.inf, jnp.where(rel <= 0, imp, -jnp.inf))

    n_sel = HEAD_DIM // 2
    imp_t = imp.T
    mine = imp_t[0:n_sel]
    j_idx = lax.broadcasted_iota(jnp.int32, (n_sel, tq), 0)
    beaten = jnp.zeros((n_sel, tq), F32)
    for kk in range(n_sel):
        other = imp_t[kk:kk + 1, :]
        wins = (other > mine) | ((other == mine) & (j_idx > kk))
        beaten = beaten + jnp.where(wins, 1.0, 0.0)
    chosen = jnp.where(beaten < SEL_TOPK, 1.0, 0.0)
    chosen = jnp.concatenate([chosen, jnp.zeros((HEAD_DIM - n_sel, tq), F32)], axis=0)
    sel_ref[...] = chosen.T.astype(sel_ref.dtype)


def _overlap_matrix(n_cmp_rows, n_sel):
    c_start = np.arange(n_cmp_rows) * CMP_STRIDE
    c_end = c_start + CMP_LEN - 1
    s_start = np.arange(HEAD_DIM) * SEL_LEN
    ov = (c_start[:, None] <= s_start[None, :] + SEL_LEN - 1) & (c_end[:, None] >= s_start[None, :])
    ov &= (np.arange(HEAD_DIM) < n_sel)[None, :]
    return jnp.asarray(ov.astype(np.float32), dtype=BF16)


def _group_o_spec(tq):
    return pl.BlockSpec((B_GROUP, None, tq, HEAD_DIM), lambda b, h, n: (h, b, n, 0))


def _cmp_select(q, q_head0, kc, vc):
    _, nb, seq, _ = q.shape
    assert seq // SEL_LEN <= HEAD_DIM // 2
    tq = min(ATT_TQ, seq)
    n_cmp = kc.shape[2]
    kv_spec = pl.BlockSpec((None, None, n_cmp, HEAD_DIM), lambda b, h, n: (h, b, 0, 0))
    return pl.pallas_call(
        functools.partial(_cmp_select_kernel, tq=tq),
        grid=(nb, B_KV_HEADS, seq // tq),
        in_specs=_group_q_specs(tq, q_head0) + [kv_spec, kv_spec,
                                                pl.BlockSpec((n_cmp, HEAD_DIM), lambda b, h, n: (0, 0))],
        out_specs=[_group_o_spec(tq), _head_spec(tq, 0)],
        out_shape=[jax.ShapeDtypeStruct((B_HEADS, nb, seq, HEAD_DIM), BF16),
                   jax.ShapeDtypeStruct((B_KV_HEADS, nb, seq, HEAD_DIM), BF16)],
        compiler_params=_params(("parallel", "parallel", "arbitrary")),
        name="nsa_compressed_select",
    )(q, q, q, kc, vc, _overlap_matrix(n_cmp, seq // SEL_LEN))


def _selected_kernel(q0_ref, q1_ref, q2_ref, k_ref, v_ref, sel_ref, o_ref, m_ref, l_ref, acc_ref, *, tq, kc):
    t0 = pl.program_id(2) * tq
    q = _group_q((q0_ref, q1_ref, q2_ref))
    sel = sel_ref[...]
    qpos = t0 + lax.broadcasted_iota(jnp.int32, (tq, 1), 0)
    blk = lax.broadcasted_iota(jnp.int32, (HEAD_DIM, 1), 0)
    key_in_chunk = lax.broadcasted_iota(jnp.int32, (1, kc), 1)
    m_ref[...] = jnp.full(m_ref.shape, -1e30, F32)
    l_ref[...] = jnp.zeros(l_ref.shape, F32)
    acc_ref[...] = jnp.zeros(acc_ref.shape, F32)

    def step(i, carry):
        k0 = pl.multiple_of(i * kc, kc)
        k = k_ref[pl.ds(k0, kc), :]
        v = v_ref[pl.ds(k0, kc), :]
        expand = jnp.where(blk - (k0 >> SEL_SHIFT) == (key_in_chunk >> SEL_SHIFT), 1.0, 0.0).astype(BF16)
        picked = _dot(sel, expand)
        bias = jnp.where((picked > 0.5) & (k0 + key_in_chunk - qpos <= 0), 0.0, -jnp.inf)
        s = _dot_nt(q, k) + jnp.concatenate([bias] * B_GROUP, axis=0)
        m_old = m_ref[...]
        m_new = jnp.maximum(m_old, jnp.max(s, axis=1, keepdims=True))
        p = jnp.exp(s - m_new)
        alpha = jnp.exp(m_old - m_new)
        l_ref[...] = alpha * l_ref[...] + jnp.sum(p, axis=1, keepdims=True)
        acc_ref[...] = alpha * acc_ref[...] + _dot(p.astype(BF16), v)
        m_ref[...] = m_new
        return carry

    lax.fori_loop(0, t0 // kc + 1, step, 0)
    o_ref[...] = (acc_ref[...] / l_ref[...]).reshape(B_GROUP, tq, HEAD_DIM).astype(o_ref.dtype)


def _selected(q, q_head0, k, k_head0, v, v_head0, sel):
    _, nb, seq, _ = q.shape
    tq = min(SEL_TQ, seq)
    kc = min(SEL_CHUNK, seq)
    assert kc % tq == 0 and seq % kc == 0
    rows = B_GROUP * tq
    return pl.pallas_call(
        functools.partial(_selected_kernel, tq=tq, kc=kc),
        grid=(nb, B_KV_HEADS, seq // tq),
        in_specs=_group_q_specs(tq, q_head0) + [_seq_spec(seq, k_head0), _seq_spec(seq, v_head0),
                                                _head_spec(tq, 0)],
        out_specs=_group_o_spec(tq),
        out_shape=jax.ShapeDtypeStruct((B_HEADS, nb, seq, HEAD_DIM), BF16),
        scratch_shapes=[pltpu.VMEM((rows, 1), F32), pltpu.VMEM((rows, 1), F32),
                        pltpu.VMEM((rows, HEAD_DIM), F32)],
        compiler_params=_params(("parallel", "parallel", "arbitrary")),
        name="nsa_selected",
    )(q, q, q, k, v, sel)


def _window_kernel(q0_ref, q1_ref, q2_ref, k_ref, v_ref, o_ref, *, tq, seq):
    t0 = pl.program_id(2) * tq
    rows = B_GROUP * tq
    span = min(WIN_LEN + tq, seq)
    start = _window_start(t0, WIN_LEN, span, seq, tq)
    s = _dot_nt(_group_q((q0_ref, q1_ref, q2_ref)), k_ref[pl.ds(start, span), :])
    qpos = t0 + (lax.broadcasted_iota(jnp.int32, (rows, 1), 0) & (tq - 1))
    kpos = start + lax.broadcasted_iota(jnp.int32, (1, span), 1)
    dist = qpos - kpos
    s = jnp.where((dist >= 0) & (dist <= WIN_LEN - 1), s, -jnp.inf)
    m = jnp.max(s, axis=1, keepdims=True)
    p = jnp.exp(s - m)
    l = jnp.sum(p, axis=1, keepdims=True)
    o = _dot(p.astype(BF16), v_ref[pl.ds(start, span), :]) / l
    o_ref[...] = o.reshape(B_GROUP, tq, HEAD_DIM).astype(o_ref.dtype)


def _window(q, q_head0, k, k_head0, v, v_head0):
    _, nb, seq, _ = q.shape
    tq = min(ATT_TQ, seq)
    assert WIN_LEN % tq == 0
    return pl.pallas_call(
        functools.partial(_window_kernel, tq=tq, seq=seq),
        grid=(nb, B_KV_HEADS, seq // tq),
        in_specs=_group_q_specs(tq, q_head0) + [_seq_spec(seq, k_head0), _seq_spec(seq, v_head0)],
        out_specs=_group_o_spec(tq),
        out_shape=jax.ShapeDtypeStruct((B_HEADS, nb, seq, HEAD_DIM), BF16),
        compiler_params=_params(("parallel", "parallel", "arbitrary")),
        name="nsa_window",
    )(q, q, q, k, v)


def _nsa_gate_kernel(g_ref, oc_ref, os_ref, ow_ref, y_ref):
    g = g_ref[...]
    for h in range(B_HEADS):
        y = (g[:, 3 * h:3 * h + 1] * oc_ref[h].astype(F32)
             + g[:, 3 * h + 1:3 * h + 2] * os_ref[h].astype(F32)
             + g[:, 3 * h + 2:3 * h + 3] * ow_ref[h].astype(F32))
        y_ref[:, h * HEAD_DIM:(h + 1) * HEAD_DIM] = y.astype(y_ref.dtype)


def _nsa_gate(gates, o_cmp, o_sel, o_win):
    rows = gates.shape[0]
    tm = min(256, rows)
    o_spec = pl.BlockSpec((B_HEADS, tm, HEAD_DIM), lambda i: (0, i, 0))
    return pl.pallas_call(
        _nsa_gate_kernel,
        grid=(rows // tm,),
        in_specs=[pl.BlockSpec((tm, HEAD_DIM), lambda i: (i, 0)), o_spec, o_spec, o_spec],
        out_specs=pl.BlockSpec((tm, B_HEADS * HEAD_DIM), lambda i: (i, 0)),
        out_shape=jax.ShapeDtypeStruct((rows, B_HEADS * HEAD_DIM), BF16),
        compiler_params=_params(("parallel",)),
        name="nsa_gate",
    )(gates, o_cmp, o_sel, o_win)


def _stick_kernel(q_ref, k_ref, v_ref, o_ref, acc_ref, car_ref, *, tq):
    n = pl.program_id(2)
    q = q_ref[...]
    sub = min(CUM_TILE, tq)
    n_sub = tq // sub
    r_idx = lax.broadcasted_iota(jnp.int32, (2 * sub, 2 * sub), 0) & (sub - 1)
    c_idx = lax.broadcasted_iota(jnp.int32, (2 * sub, 2 * sub), 1)
    suffix_and_total = jnp.where((c_idx >= sub) | (r_idx > c_idx), 1.0, 0.0).astype(BF16)
    q_in_tile = lax.broadcasted_iota(jnp.int32, (tq, 1), 0)
    k_in_sub = lax.broadcasted_iota(jnp.int32, (1, sub), 1)
    acc_ref[...] = jnp.zeros(acc_ref.shape, F32)
    car_ref[...] = jnp.zeros(car_ref.shape, F32)

    def tile(k0, diagonal):
        z = _dot_nt(q, k_ref[pl.ds(k0, tq), :])
        soft = jnp.log(1.0 + jnp.exp(-jnp.abs(z)))
        log_beta = jnp.minimum(z, 0.0) - soft
        log_rest = log_beta - z
        carried = car_ref[...]
        parts = [None] * n_sub
        for u in reversed(range(n_sub)):
            cols = slice(u * sub, (u + 1) * sub)
            rest_u = log_rest[:, cols]
            if diagonal:
                before = (u * sub + k_in_sub) - q_in_tile < 0
                rest_u = jnp.where(before, rest_u, 0.0)
            sums = _dot(jnp.concatenate(_split_bf16(rest_u), axis=1), suffix_and_total)
            a = jnp.exp(log_beta[:, cols] + (sums[:, :sub] + carried))
            if diagonal:
                a = jnp.where(before, a, 0.0)
            carried = carried + sums[:, sub:]
            parts[u] = a.astype(BF16)
        car_ref[...] = carried
        acc_ref[...] += _dot(jnp.concatenate(parts, axis=1), v_ref[pl.ds(k0, tq), :])

    tile(pl.multiple_of(n * tq, tq), True)

    def step(i, carry):
        tile(pl.multiple_of((n - 1 - i) * tq, tq), False)
        return carry

    lax.fori_loop(0, n, step, 0)
    o_ref[...] = acc_ref[...].astype(o_ref.dtype)


def _stick_breaking(q, q_head0, k, k_head0, v, v_head0):
    _, nb, seq, _ = q.shape
    tq = min(STICK_TILE, seq)
    return pl.pallas_call(
        functools.partial(_stick_kernel, tq=tq),
        grid=(nb, C_HEADS, seq // tq),
        in_specs=[_head_spec(tq, q_head0), _seq_spec(seq, k_head0), _seq_spec(seq, v_head0)],
        out_specs=pl.BlockSpec((None, tq, HEAD_DIM), lambda b, h, n: (b, n, h)),
        out_shape=jax.ShapeDtypeStruct((nb, seq, C_HEADS * HEAD_DIM), BF16),
        scratch_shapes=[pltpu.VMEM((tq, HEAD_DIM), F32), pltpu.VMEM((tq, min(CUM_TILE, tq)), F32)],
        compiler_params=_params(("parallel", "parallel", "arbitrary")),
        name="stick_breaking",
    )(q, k, v)


def _merge_kernel(ya_ref, yb_ref, yc_ref, wa_ref, wb_ref, wc_ref, ga_ref, gb_ref, gc_ref, o_ref):
    mixed = (ga_ref[...].astype(F32) * _dot(ya_ref[...], wa_ref[...])
             + gb_ref[...].astype(F32) * _dot(yb_ref[...], wb_ref[...])
             + gc_ref[...].astype(F32) * _dot(yc_ref[...], wc_ref[...]))
    o_ref[...] = mixed.astype(o_ref.dtype)


def _merge(ya, yb, yc, wa, wb, wc, gates, d_model):
    m = ya.shape[0]
    tm, tn = min(512, m), 1024
    col_tiles = d_model // tn

    def y_spec(y):
        return pl.BlockSpec((tm, y.shape[1]), lambda i, j: (i, 0))

    def w_spec(w):
        return pl.BlockSpec((w.shape[0], tn), lambda i, j: (0, j))

    def g_spec(branch):
        return pl.BlockSpec((tm, tn), lambda i, j: (i, branch * col_tiles + j))

    return pl.pallas_call(
        _merge_kernel,
        grid=(m // tm, col_tiles),
        in_specs=[y_spec(ya), y_spec(yb), y_spec(yc), w_spec(wa), w_spec(wb), w_spec(wc),
                  g_spec(0), g_spec(1), g_spec(2)],
        out_specs=pl.BlockSpec((tm, tn), lambda i, j: (i, j)),
        out_shape=jax.ShapeDtypeStruct((m, d_model), BF16),
        compiler_params=_params(("parallel", "arbitrary")),
        name="branch_merge",
    )(ya, yb, yc, wa, wb, wc, gates, gates, gates)


def _ln_kernel(x_ref, y_ref, g_ref, b_ref, o_ref, ob_ref):
    z = ALPHA * x_ref[...] + y_ref[...]
    mu = jnp.mean(z, axis=1, keepdims=True)
    zc = z - mu
    var = jnp.mean(zc * zc, axis=1, keepdims=True)
    out = zc * lax.rsqrt(var + LN_EPS) * g_ref[...] + b_ref[...]
    o_ref[...] = out
    ob_ref[...] = out.astype(BF16)


def _residual_ln(x, y, g, b):
    m, d = x.shape
    tm = min(128, m)
    row = pl.BlockSpec((tm, d), lambda i: (i, 0))
    vec = pl.BlockSpec((1, d), lambda i: (0, 0))
    return pl.pallas_call(
        _ln_kernel,
        grid=(m // tm,),
        in_specs=[row, row, vec, vec],
        out_specs=[row, row],
        out_shape=[jax.ShapeDtypeStruct((m, d), F32), jax.ShapeDtypeStruct((m, d), BF16)],
        compiler_params=_params(("parallel",)),
        name="residual_layer_norm",
    )(x, y, g.reshape(1, d), b.reshape(1, d))


def _rope_tables(seq):
    half = HEAD_DIM // 2
    inv = 1.0 / (ROPE_THETA ** (jnp.arange(half, dtype=F32) / half))
    ang = jnp.arange(seq).astype(F32)[:, None] * inv[None, :]
    cos, sin = jnp.cos(ang), jnp.sin(ang)
    return jnp.concatenate([cos, cos], axis=1), jnp.concatenate([-sin, sin], axis=1)


_A_KV, _B_KV = A_SLOTS, B_KV_HEADS
_ALIGNED_FIELDS = (('q_a', A_HEADS, 'rope_scale'), ('k_a', _A_KV, 'rope'), ('v_a', _A_KV, 'none'),
                   ('q_b', B_HEADS, 'rope_scale'), ('kc_b', _B_KV, 'rope'), ('vc_b', _B_KV, 'none'),
                   ('ks_b', _B_KV, 'rope'), ('vs_b', _B_KV, 'none'), ('kw_b', _B_KV, 'rope'),
                   ('vw_b', _B_KV, 'none'))
_STICK_FIELDS = (('q_c', C_HEADS, 'scale'), ('k_c', C_HEADS, 'none'), ('v_c', C_HEADS, 'none'))
N_NSA_GATES = 3 * B_HEADS


def _field_layout(fields, heads_per_tile):
    head0, modes, start = {}, [], 0
    for name, heads, mode in fields:
        assert heads % heads_per_tile == 0
        head0[name] = start
        modes += [mode] * (heads // heads_per_tile)
        start += heads
    return head0, tuple(modes), start


def _layer(layer, x, xb, nb, seq, tables, w_rows, cmp_pe_k, cmp_wk1, cmp_wk2, cmp_pe_v, cmp_wv1, cmp_wv2,
           w_br_a, w_br_b, w_br_c, w_out, ln1_g, ln1_b, w_up, w_down, ln2_g, ln2_b):
    m, d_model = x.shape

    tn_att = 4 * HEAD_DIM
    at, att_modes, att_heads = _field_layout(_ALIGNED_FIELDS, tn_att // HEAD_DIM)
    att = _matmul(xb, w_rows, layer=layer, b_rows=True, name="proj_attention", out_dtype=BF16,
                  tile_modes=att_modes, tm=1024, tn=tn_att, head_major=True, rope_tables=tables)
    tn_tail = 1024
    st, stick_modes, stick_heads = _field_layout(_STICK_FIELDS, tn_tail // HEAD_DIM)
    stick_col0 = att_heads * HEAD_DIM
    gates_col0 = stick_col0 + stick_heads * HEAD_DIM
    nsa_col0 = gates_col0 + 3 * d_model
    stick = _matmul(xb, w_rows, layer=layer, b_rows=True, col0=stick_col0, name="proj_stick",
                    out_dtype=BF16, tile_modes=stick_modes, tm=1024, tn=tn_tail, head_major=True)
    gates = _matmul(xb, w_rows, layer=layer, b_rows=True, col0=gates_col0, name="proj_branch_gates",
                    out_dtype=BF16, tile_modes=('sigmoid',) * (3 * d_model // tn_tail), tm=1024, tn=tn_tail)
    g_nsa = _matmul(xb, w_rows, layer=layer, b_rows=True, col0=nsa_col0, name="proj_nsa_gates",
                    out_dtype=F32, tile_modes=('sigmoid',), tm=1024, tn=HEAD_DIM)

    att = att.reshape(att_heads, nb, seq, HEAD_DIM)
    stick = stick.reshape(stick_heads, nb, seq, HEAD_DIM)

    y_a = _dilated_mixer(att, at['q_a'], att, at['k_a'], att, at['v_a'])

    kc = _compress(att[at['kc_b']:at['kc_b'] + B_KV_HEADS], cmp_pe_k, cmp_wk1, cmp_wk2, "nsa_compress_k")
    vc = _compress(att[at['vc_b']:at['vc_b'] + B_KV_HEADS], cmp_pe_v, cmp_wv1, cmp_wv2, "nsa_compress_v")
    o_cmp, sel = _cmp_select(att, at['q_b'], kc, vc)
    o_sel = _selected(att, at['q_b'], att, at['ks_b'], att, at['vs_b'], sel)
    o_win = _window(att, at['q_b'], att, at['kw_b'], att, at['vw_b'])
    y_b = _nsa_gate(g_nsa, o_cmp.reshape(B_HEADS, m, HEAD_DIM), o_sel.reshape(B_HEADS, m, HEAD_DIM),
                    o_win.reshape(B_HEADS, m, HEAD_DIM))

    y_c = _stick_breaking(stick, st['q_c'], stick, st['k_c'], stick, st['v_c'])

    merged = _merge(y_a.reshape(m, -1), y_b, y_c.reshape(m, -1), w_br_a.astype(BF16),
                    w_br_b.astype(BF16), w_br_c.astype(BF16), gates, d_model)
    mixed = _matmul(merged, w_out, layer=layer, name="out_proj", out_dtype=F32,
                    tile_modes=('none',) * (d_model // 512), tm=1024, tn=512)
    x1, x1b = _residual_ln(x, mixed, ln1_g, ln1_b)

    d_ff = w_up.shape[2]
    hidden = _matmul(x1b, w_up, layer=layer, name="mlp_up", out_dtype=BF16,
                     tile_modes=('relu2',) * (d_ff // 512), tm=1024, tn=512)
    down = _matmul(hidden, w_down, layer=layer, name="mlp_down", out_dtype=F32,
                   tile_modes=('none',) * (d_model // 1024), tm=1024, tn=1024, tk=2048)
    return _residual_ln(x1, down, ln2_g, ln2_b)


def kernel(x, w_in, cmp_pe_k, cmp_wk1, cmp_wk2, cmp_pe_v, cmp_wv1, cmp_wv2, w_br_a, w_br_b, w_br_c,
           w_out, ln1_g, ln1_b, w_up, w_down, ln2_g, ln2_b):
    nb, seq, d_model = x.shape
    tables = _rope_tables(seq)
    xf = x.reshape(nb * seq, d_model)
    xb = xf.astype(BF16)
    aligned_cols = sum(heads for _, heads, _ in _ALIGNED_FIELDS) * HEAD_DIM
    w_rows = _input_weight_rows(w_in, aligned_cols, N_NSA_GATES)
    for l in range(w_in.shape[0]):
        xf, xb = _layer(l, xf, xb, nb, seq, tables, w_rows, cmp_pe_k[l], cmp_wk1[l], cmp_wk2[l],
                        cmp_pe_v[l], cmp_wv1[l], cmp_wv2[l], w_br_a[l], w_br_b[l], w_br_c[l],
                        w_out, ln1_g[l], ln1_b[l], w_up, w_down, ln2_g[l], ln2_b[l])
    return xf.reshape(nb, seq, d_model)
```

```python
import functools
import math

import numpy as np
import jax
import jax.numpy as jnp
from jax import lax
from jax.experimental import pallas as pl
from jax.experimental.pallas import tpu as pltpu

F32 = jnp.float32
BF16 = jnp.bfloat16

HEAD_DIM = 128
ROPE_THETA = 10000.0
LN_EPS = 1e-5
DEPTH = 2

DIL_GROUPS = ((128, 1), (512, 4), (2048, 16))
A_SLOTS = 4
A_HEADS = A_SLOTS * len(DIL_GROUPS)

B_HEADS = 12
B_KV_HEADS = 4
B_GROUP = B_HEADS // B_KV_HEADS
CMP_LEN = 32
CMP_STRIDE = 16
SEL_LEN = 64
SEL_SHIFT = 6
SEL_TOPK = 16
WIN_LEN = 512

C_HEADS = 8

ALPHA = (2.0 * DEPTH) ** 0.25
Q_SCALE = HEAD_DIM ** -0.5

VMEM_LIMIT = 48 * 1024 * 1024
ATT_TQ = 128
DIL_ROWS = 128
DIL_ROWS_DENSE = 512
SEL_TQ = 128
SEL_CHUNK = 1024
STICK_TILE = 512
CUM_TILE = 128


def _params(semantics):
    return pltpu.CompilerParams(dimension_semantics=semantics, vmem_limit_bytes=VMEM_LIMIT)


def _dot(a, b):
    return jnp.dot(a, b, preferred_element_type=F32)


def _dot_nt(a, b):
    return lax.dot_general(a, b, (((1,), (1,)), ((), ())), preferred_element_type=F32)


def _split_bf16(x):
    hi = x.astype(BF16)
    lo = (x - hi.astype(F32)).astype(BF16)
    return hi, lo


def _apply_mode(y, mode, cos_ref, sin_ref):
    if mode in ('rope', 'rope_scale'):
        y = y * cos_ref[...] + pltpu.roll(y, HEAD_DIM // 2, 1) * sin_ref[...]
    if mode in ('rope_scale', 'scale'):
        y = y * Q_SCALE
    if mode == 'sigmoid':
        y = jax.nn.sigmoid(y)
    if mode == 'relu2':
        y = jnp.square(jnp.maximum(y, 0.0))
    return y


def _write_tile(acc, o_ref, mode, head_major, cos_ref, sin_ref):
    if head_major:
        for c in range(acc.shape[1] // HEAD_DIM):
            y = _apply_mode(acc[:, c * HEAD_DIM:(c + 1) * HEAD_DIM], mode, cos_ref, sin_ref)
            o_ref[c] = y.astype(o_ref.dtype)
    else:
        o_ref[...] = _apply_mode(acc, mode, cos_ref, sin_ref).astype(o_ref.dtype)


def _matmul_kernel(*refs, tile_modes, head_major, n_k, use_rope, acc_in_out, b_rows):
    refs = list(refs)
    a_ref, b_ref = refs[0], refs[1]
    cos_ref, sin_ref = (refs[2], refs[3]) if use_rope else (None, None)
    o_ref = refs[4] if use_rope else refs[2]
    acc_ref = o_ref if acc_in_out else (refs[-1] if n_k > 1 else None)
    if n_k > 1:
        @pl.when(pl.program_id(2) == 0)
        def _():
            acc_ref[...] = jnp.zeros(acc_ref.shape, F32)

    if b_rows:
        part = _dot_nt(a_ref[...], b_ref[...].astype(BF16))
    else:
        part = _dot(a_ref[...], b_ref[...].astype(BF16))

    def finish(acc):
        modes = sorted(set(tile_modes))
        if len(modes) == 1:
            _write_tile(acc, o_ref, modes[0], head_major, cos_ref, sin_ref)
            return
        j = pl.program_id(1)
        for mode in modes:
            hit = None
            for c, tile_mode in enumerate(tile_modes):
                if tile_mode == mode:
                    hit = (j == c) if hit is None else (hit | (j == c))
            pl.when(hit)(functools.partial(_write_tile, acc, o_ref, mode, head_major, cos_ref, sin_ref))

    if n_k == 1:
        finish(part)
        return
    acc_ref[...] += part
    if not acc_in_out:
        pl.when(pl.program_id(2) == n_k - 1)(lambda: finish(acc_ref[...]))


def _matmul(a, b, *, name, out_dtype, tile_modes, tm, tn, tk=None, layer=None, col0=0,
            head_major=False, rope_tables=None, b_rows=False):
    m, kdim = a.shape
    n = tn * len(tile_modes)
    tm = min(tm, m)
    tk = kdim if tk is None else min(tk, kdim)
    assert m % tm == 0 and kdim % tk == 0 and col0 % tn == 0
    n_k = kdim // tk
    col_tile0 = col0 // tn
    use_rope = any(mode.startswith('rope') for mode in tile_modes)
    acc_in_out = n_k > 1 and out_dtype == F32 and set(tile_modes) == {'none'} and not head_major
    if b_rows:
        assert layer is not None
        b_spec = pl.BlockSpec((None, tn, tk), lambda i, j, k: (layer, col_tile0 + j, k))
    elif layer is None:
        b_spec = pl.BlockSpec((tk, tn), lambda i, j, k: (k, col_tile0 + j))
    else:
        b_spec = pl.BlockSpec((None, tk, tn), lambda i, j, k: (layer, k, col_tile0 + j))
    in_specs = [pl.BlockSpec((tm, tk), lambda i, j, k: (i, k)), b_spec]
    operands = [a, b]
    if use_rope:
        assert head_major
        cos, sin = rope_tables
        seq_tiles = cos.shape[0] // tm
        assert cos.shape[0] % tm == 0
        spec = pl.BlockSpec((tm, HEAD_DIM), lambda i, j, k: (i % seq_tiles, 0))
        in_specs += [spec, spec]
        operands += [cos, sin]
    if head_major:
        out_shape = jax.ShapeDtypeStruct((n // HEAD_DIM, m, HEAD_DIM), out_dtype)
        out_spec = pl.BlockSpec((tn // HEAD_DIM, tm, HEAD_DIM), lambda i, j, k: (j, i, 0))
    else:
        out_shape = jax.ShapeDtypeStruct((m, n), out_dtype)
        out_spec = pl.BlockSpec((tm, tn), lambda i, j, k: (i, j))
    scratch = [pltpu.VMEM((tm, tn), F32)] if (n_k > 1 and not acc_in_out) else []
    return pl.pallas_call(
        functools.partial(_matmul_kernel, tile_modes=tuple(tile_modes), head_major=head_major, n_k=n_k,
                          use_rope=use_rope, acc_in_out=acc_in_out, b_rows=b_rows),
        grid=(m // tm, len(tile_modes), n_k),
        in_specs=in_specs,
        out_specs=out_spec,
        out_shape=out_shape,
        scratch_shapes=scratch,
        compiler_params=_params(("parallel", "parallel", "arbitrary")),
        name=name,
    )(*operands)


def _gather_rows_kernel(w_ref, o_ref, *, tn, k_chunks, n_layers):
    stride = k_chunks * n_layers
    for layer in range(n_layers):
        pieces = [w_ref[pl.ds(c * n_layers + layer, tn, stride=stride), :] for c in range(k_chunks)]
        o_ref[layer] = jnp.concatenate(pieces, axis=1).astype(o_ref.dtype)


def _input_weight_rows(w_in, aligned_cols, n_gates, tn=HEAD_DIM):
    n_layers, kdim, n = w_in.shape
    k_chunks = kdim // HEAD_DIM
    tail = n - aligned_cols - n_gates
    assert aligned_cols % tn == 0 and tail % tn == 0 and kdim % HEAD_DIM == 0
    aligned_tiles, tail_tiles = aligned_cols // tn, tail // tn
    rows_per_col = k_chunks * n_layers
    flat = w_in.reshape(n_layers, k_chunks, HEAD_DIM, n).transpose(3, 1, 0, 2).reshape(n * rows_per_col, HEAD_DIM)

    def source_row(j):
        col = jnp.where(j < aligned_tiles, tn * j,
                        jnp.where(j < aligned_tiles + tail_tiles,
                                  aligned_cols + n_gates + tn * (j - aligned_tiles), aligned_cols))
        return (col * rows_per_col, 0)

    tiles = aligned_tiles + tail_tiles + 1
    return pl.pallas_call(
        functools.partial(_gather_rows_kernel, tn=tn, k_chunks=k_chunks, n_layers=n_layers),
        grid=(tiles,),
        in_specs=[pl.BlockSpec((pl.Element(tn * rows_per_col), pl.Element(HEAD_DIM)), source_row)],
        out_specs=pl.BlockSpec((n_layers, tn, kdim), lambda j: (0, j, 0)),
        out_shape=jax.ShapeDtypeStruct((n_layers, tiles * tn, kdim), BF16),
        compiler_params=_params(("parallel",)),
        name="input_weight_rows",
    )(flat)


def _window_start(t0, back, span, seq, align):
    start = jnp.minimum(jnp.maximum(t0 - back, 0), seq - span)
    return pl.multiple_of(start, align)


def _dilated_group_kernel(q_ref, k_ref, v_ref, o_ref, lse_ref, *, dil, rows, band, n_rows):
    l0 = pl.program_id(2) * rows
    span = min(band + rows, n_rows)
    start = _window_start(l0, band, span, n_rows, math.gcd(rows, band))
    dist = (l0 + lax.broadcasted_iota(jnp.int32, (rows, 1), 0)) - (start + lax.broadcasted_iota(jnp.int32, (1, span), 1))
    keep = (dist >= 0) & (dist <= band)
    for r in range(dil):
        lanes = slice(r * HEAD_DIM, (r + 1) * HEAD_DIM)
        s = jnp.where(keep, _dot_nt(q_ref[:, lanes], k_ref[pl.ds(start, span), lanes]), -jnp.inf)
        m = jnp.max(s, axis=1, keepdims=True)
        p = jnp.exp(s - m)
        l = jnp.sum(p, axis=1, keepdims=True)
        o = _dot(p.astype(BF16), v_ref[pl.ds(start, span), lanes]) / l
        lse = jnp.broadcast_to(m + jnp.log(l), (rows, HEAD_DIM))
        if dil == 1:
            o_ref[...] = o
            lse_ref[...] = lse
        else:
            o_ref[pl.ds(r, rows, stride=dil), :] = o
            lse_ref[pl.ds(r, rows, stride=dil), :] = lse


def _head_spec(rows, head0):
    return pl.BlockSpec((None, None, rows, HEAD_DIM), lambda b, h, n: (head0 + h, b, n, 0))


def _seq_spec(seq, head0):
    return pl.BlockSpec((None, None, seq, HEAD_DIM), lambda b, h, n: (head0 + h, b, 0, 0))


def _dilated_group(q, q_head0, k, k_head0, v, v_head0, window, dil):
    _, nb, n_rows, width = q.shape
    seq = n_rows * dil
    band = window // dil
    rows = min(DIL_ROWS if dil > 1 else DIL_ROWS_DENSE, n_rows)
    assert width == dil * HEAD_DIM and n_rows % rows == 0
    view_spec = pl.BlockSpec((None, None, rows, width), lambda b, s, j: (q_head0 + s, b, j, 0))
    out_spec = pl.BlockSpec((None, None, rows * dil, HEAD_DIM), lambda b, s, j: (s, b, j, 0))
    out_shape = jax.ShapeDtypeStruct((A_SLOTS, nb, seq, HEAD_DIM), F32)
    return pl.pallas_call(
        functools.partial(_dilated_group_kernel, dil=dil, rows=rows, band=band, n_rows=n_rows),
        grid=(nb, A_SLOTS, n_rows // rows),
        in_specs=[view_spec,
                  pl.BlockSpec((None, None, n_rows, width), lambda b, s, j: (k_head0 + s, b, 0, 0)),
                  pl.BlockSpec((None, None, n_rows, width), lambda b, s, j: (v_head0 + s, b, 0, 0))],
        out_specs=[out_spec, out_spec],
        out_shape=[out_shape, out_shape],
        compiler_params=_params(("parallel", "parallel", "arbitrary")),
        name="dilated_group_%d" % dil,
    )(q, k, v)


def _dilated_merge_kernel(o0_ref, o1_ref, o2_ref, l0_ref, l1_ref, l2_ref, y_ref):
    for s in range(A_SLOTS):
        lses = (l0_ref[s], l1_ref[s], l2_ref[s])
        top = jnp.maximum(jnp.maximum(lses[0], lses[1]), lses[2])
        ws = [jnp.exp(lse - top) for lse in lses]
        y = (ws[0] * o0_ref[s] + ws[1] * o1_ref[s] + ws[2] * o2_ref[s]) / (ws[0] + ws[1] + ws[2])
        y_ref[:, s * HEAD_DIM:(s + 1) * HEAD_DIM] = y.astype(y_ref.dtype)


def _dilated_mixer(att, q_head0, k_head0, v_head0):
    _, nb, seq, _ = att.shape
    outs, lses = [], []
    for g, (window, dil) in enumerate(DIL_GROUPS):
        qh = q_head0 + g * A_SLOTS
        if dil == 1:
            o, lse = _dilated_group(att, qh, att, k_head0, att, v_head0, window, dil)
        else:
            def strided(head0):
                return att[head0:head0 + A_SLOTS].reshape(A_SLOTS, nb, seq // dil, dil * HEAD_DIM)
            o, lse = _dilated_group(strided(qh), 0, strided(k_head0), 0, strided(v_head0), 0, window, dil)
        outs.append(o)
        lses.append(lse)
    tq = min(256, seq)
    spec = pl.BlockSpec((A_SLOTS, None, tq, HEAD_DIM), lambda b, n: (0, b, n, 0))
    return pl.pallas_call(
        _dilated_merge_kernel,
        grid=(nb, seq // tq),
        in_specs=[spec] * 6,
        out_specs=pl.BlockSpec((None, tq, A_SLOTS * HEAD_DIM), lambda b, n: (b, n, 0)),
        out_shape=jax.ShapeDtypeStruct((nb, seq, A_SLOTS * HEAD_DIM), BF16),
        compiler_params=_params(("parallel", "parallel")),
        name="dilated_merge",
    )(*outs, *lses)


def _gelu_tanh(x):
    return 0.5 * x * (1.0 + jnp.tanh(math.sqrt(2.0 / math.pi) * (x + 0.044715 * (x * x * x))))


def _compress_kernel(x_ref, pe_ref, w1_ref, w2_ref, o_ref):
    x = x_ref[...].astype(F32)
    first = _dot((x + pe_ref[0:1, :]).astype(BF16), w1_ref[0])
    second = _dot((x + pe_ref[1:2, :]).astype(BF16), w1_ref[1])
    chunks = x.shape[0]
    hidden = first + pltpu.roll(second, chunks - 1, 0)
    o_ref[...] = _dot(_gelu_tanh(hidden).astype(BF16), w2_ref[...]).astype(o_ref.dtype)


def _compress(t, pe, w1, w2, name):
    nh, nb, seq, _ = t.shape
    chunks = seq // CMP_STRIDE
    width = CMP_STRIDE * HEAD_DIM
    ratio = CMP_LEN // CMP_STRIDE
    x = t.reshape(nh, nb, chunks, width)
    return pl.pallas_call(
        _compress_kernel,
        grid=(nh, nb),
        in_specs=[pl.BlockSpec((None, None, chunks, width), lambda h, b: (h, b, 0, 0)),
                  pl.BlockSpec((ratio, width), lambda h, b: (0, 0)),
                  pl.BlockSpec((ratio, width, HEAD_DIM), lambda h, b: (0, 0, 0)),
                  pl.BlockSpec((HEAD_DIM, HEAD_DIM), lambda h, b: (0, 0))],
        out_specs=pl.BlockSpec((None, None, chunks, HEAD_DIM), lambda h, b: (h, b, 0, 0)),
        out_shape=jax.ShapeDtypeStruct((nh, nb, chunks, HEAD_DIM), BF16),
        compiler_params=_params(("parallel", "parallel")),
        name=name,
    )(x, pe.reshape(ratio, width), w1.reshape(ratio, width, HEAD_DIM).astype(BF16), w2.astype(BF16))


def _group_q(q_refs):
    return jnp.concatenate([r[...] for r in q_refs], axis=0)


def _group_q_specs(tq, head0):
    return [pl.BlockSpec((None, None, tq, HEAD_DIM),
                         functools.partial(lambda b, h, n, g: (head0 + h * B_GROUP + g, b, n, 0), g=g))
            for g in range(B_GROUP)]


def _cmp_select_kernel(q0_ref, q1_ref, q2_ref, kc_ref, vc_ref, ov_ref, o_ref, sel_ref, *, tq):
    t0 = pl.program_id(2) * tq
    n_cmp = kc_ref.shape[0]
    rows = B_GROUP * tq
    s = _dot_nt(_group_q((q0_ref, q1_ref, q2_ref)), kc_ref[...])
    tpos3 = t0 + (lax.broadcasted_iota(jnp.int32, (rows, 1), 0) & (tq - 1))
    c_end = lax.broadcasted_iota(jnp.int32, (1, n_cmp), 1) * CMP_STRIDE + (CMP_LEN - 1)
    s = jnp.where(c_end - tpos3 <= 0, s, -jnp.inf)
    m = jnp.max(s, axis=1, keepdims=True)
    m = jnp.where(jnp.abs(m) < jnp.inf, m, 0.0)
    e = jnp.exp(s - m)
    den = jnp.sum(e, axis=1, keepdims=True)
    p = e / jnp.where(den > 0, den, 1.0)
    o_ref[...] = _dot(p.astype(BF16), vc_ref[...]).reshape(B_GROUP, tq, HEAD_DIM).astype(o_ref.dtype)

    p_sum = p[0:tq] + p[tq:2 * tq] + p[2 * tq:3 * tq]
    hi, lo = _split_bf16(p_sum)
    imp = _dot(hi, ov_ref[...]) + _dot(lo, ov_ref[...])
    tpos = t0 + lax.broadcasted_iota(jnp.int32, (tq, 1), 0)
    rel = lax.broadcasted_iota(jnp.int32, (1, HEAD_DIM), 1) - (tpos >> SEL_SHIFT)
    j_abs = jnp.broadcast_to(lax.broadcasted_iota(jnp.int32, (1, HEAD_DIM), 1), rel.shape)
    forced = (j_abs == 0) | (rel == 0) | (rel == -1)
    imp = jnp.where(forced, jnp.inf, jnp.where(rel <= 0, imp, -jnp.inf))

    n_sel = HEAD_DIM // 2
    imp_t = imp.T
    mine = imp_t[0:n_sel]
    j_idx = lax.broadcasted_iota(jnp.int32, (n_sel, tq), 0)
    beaten = jnp.zeros((n_sel, tq), F32)
    for kk in range(n_sel):
        other = imp_t[kk:kk + 1, :]
        wins = (other > mine) | ((other == mine) & (j_idx > kk))
        beaten = beaten + jnp.where(wins, 1.0, 0.0)
    chosen = jnp.where(beaten < SEL_TOPK, 1.0, 0.0)
    chosen = jnp.concatenate([chosen, jnp.zeros((HEAD_DIM - n_sel, tq), F32)], axis=0)
    sel_ref[...] = chosen.astype(sel_ref.dtype)


def _overlap_matrix(n_cmp_rows, n_sel):
    c_start = np.arange(n_cmp_rows) * CMP_STRIDE
    c_end = c_start + CMP_LEN - 1
    s_start = np.arange(HEAD_DIM) * SEL_LEN
    ov = (c_start[:, None] <= s_start[None, :] + SEL_LEN - 1) & (c_end[:, None] >= s_start[None, :])
    ov &= (np.arange(HEAD_DIM) < n_sel)[None, :]
    return jnp.asarray(ov.astype(np.float32), dtype=BF16)


def _group_o_spec(tq):
    return pl.BlockSpec((B_GROUP, None, tq, HEAD_DIM), lambda b, h, n: (h, b, n, 0))


def _sel_spec(tq):
    return pl.BlockSpec((None, None, HEAD_DIM, tq), lambda b, h, n: (h, b, 0, n))


def _cmp_select(q, q_head0, kc, vc):
    _, nb, seq, _ = q.shape
    assert seq // SEL_LEN <= HEAD_DIM // 2
    tq = min(ATT_TQ, seq)
    n_cmp = kc.shape[2]
    kv_spec = pl.BlockSpec((None, None, n_cmp, HEAD_DIM), lambda b, h, n: (h, b, 0, 0))
    return pl.pallas_call(
        functools.partial(_cmp_select_kernel, tq=tq),
        grid=(nb, B_KV_HEADS, seq // tq),
        in_specs=_group_q_specs(tq, q_head0) + [kv_spec, kv_spec,
                                                pl.BlockSpec((n_cmp, HEAD_DIM), lambda b, h, n: (0, 0))],
        out_specs=[_group_o_spec(tq), _sel_spec(tq)],
        out_shape=[jax.ShapeDtypeStruct((B_HEADS, nb, seq, HEAD_DIM), BF16),
                   jax.ShapeDtypeStruct((B_KV_HEADS, nb, HEAD_DIM, seq), BF16)],
        compiler_params=_params(("parallel", "parallel", "arbitrary")),
        name="nsa_compressed_select",
    )(q, q, q, kc, vc, _overlap_matrix(n_cmp, seq // SEL_LEN))


def _selected_kernel(q0_ref, q1_ref, q2_ref, k_ref, v_ref, sel_ref, o_ref, vt_ref, acc_ref, *, tq, kc):
    n = pl.program_id(2)
    t0 = n * tq
    cols = B_GROUP * tq

    @pl.when(n == 0)
    def _():
        for c in range(v_ref.shape[0] // kc):
            vt_ref[c] = v_ref[c * kc:(c + 1) * kc, :].astype(F32).T.astype(BF16)

    q = _group_q((q0_ref, q1_ref, q2_ref))
    sel = sel_ref[...]
    qpos = t0 + lax.broadcasted_iota(jnp.int32, (1, tq), 1)
    key_in_chunk = lax.broadcasted_iota(jnp.int32, (kc, 1), 0)
    blk = lax.broadcasted_iota(jnp.int32, (1, HEAD_DIM), 1)
    acc_ref[...] = jnp.zeros(acc_ref.shape, F32)

    def step(i, carry):
        m_old, l_old = carry
        k0 = pl.multiple_of(i * kc, kc)
        kpos = k0 + key_in_chunk
        expand = jnp.where((kpos >> SEL_SHIFT) == blk, 1.0, 0.0).astype(BF16)
        picked = _dot(expand, sel)
        bias = jnp.where((picked > 0.5) & (kpos - qpos <= 0), 0.0, -jnp.inf)
        s = _dot_nt(k_ref[pl.ds(k0, kc), :], q) + jnp.concatenate([bias] * B_GROUP, axis=1)
        m_new = jnp.maximum(m_old, jnp.max(s, axis=0, keepdims=True))
        p = jnp.exp(s - m_new)
        alpha = jnp.exp(m_old - m_new)
        acc_ref[...] = alpha * acc_ref[...] + _dot(vt_ref[i], p.astype(BF16))
        return m_new, alpha * l_old + jnp.sum(p, axis=0, keepdims=True)

    init = (jnp.full((1, cols), -1e30, F32), jnp.zeros((1, cols), F32))
    _, l = lax.fori_loop(0, t0 // kc + 1, step, init)
    o_t = acc_ref[...] / l
    for g in range(B_GROUP):
        o_ref[g] = o_t[:, g * tq:(g + 1) * tq].T.astype(o_ref.dtype)


def _selected(q, q_head0, k, k_head0, v, v_head0, sel):
    _, nb, seq, _ = q.shape
    tq = min(SEL_TQ, seq)
    kc = min(SEL_CHUNK, seq)
    assert kc % tq == 0 and seq % kc == 0 and tq == HEAD_DIM
    return pl.pallas_call(
        functools.partial(_selected_kernel, tq=tq, kc=kc),
        grid=(nb, B_KV_HEADS, seq // tq),
        in_specs=_group_q_specs(tq, q_head0) + [_seq_spec(seq, k_head0), _seq_spec(seq, v_head0),
                                                _sel_spec(tq)],
        out_specs=_group_o_spec(tq),
        out_shape=jax.ShapeDtypeStruct((B_HEADS, nb, seq, HEAD_DIM), BF16),
        scratch_shapes=[pltpu.VMEM((seq // kc, HEAD_DIM, kc), BF16),
                        pltpu.VMEM((HEAD_DIM, B_GROUP * tq), F32)],
        compiler_params=_params(("parallel", "parallel", "arbitrary")),
        name="nsa_selected",
    )(q, q, q, k, v, sel)


def _window_kernel(q0_ref, q1_ref, q2_ref, k_ref, v_ref, o_ref, *, tq, seq):
    t0 = pl.program_id(2) * tq
    rows = B_GROUP * tq
    span = min(WIN_LEN + tq, seq)
    start = _window_start(t0, WIN_LEN, span, seq, tq)
    s = _dot_nt(_group_q((q0_ref, q1_ref, q2_ref)), k_ref[pl.ds(start, span), :])
    qpos = t0 + (lax.broadcasted_iota(jnp.int32, (rows, 1), 0) & (tq - 1))
    kpos = start + lax.broadcasted_iota(jnp.int32, (1, span), 1)
    dist = qpos - kpos
    s = jnp.where((dist >= 0) & (dist <= WIN_LEN - 1), s, -jnp.inf)
    m = jnp.max(s, axis=1, keepdims=True)
    p = jnp.exp(s - m)
    l = jnp.sum(p, axis=1, keepdims=True)
    o = _dot(p.astype(BF16), v_ref[pl.ds(start, span), :]) / l
    o_ref[...] = o.reshape(B_GROUP, tq, HEAD_DIM).astype(o_ref.dtype)


def _window(q, q_head0, k, k_head0, v, v_head0):
    _, nb, seq, _ = q.shape
    tq = min(ATT_TQ, seq)
    assert WIN_LEN % tq == 0
    return pl.pallas_call(
        functools.partial(_window_kernel, tq=tq, seq=seq),
        grid=(nb, B_KV_HEADS, seq // tq),
        in_specs=_group_q_specs(tq, q_head0) + [_seq_spec(seq, k_head0), _seq_spec(seq, v_head0)],
        out_specs=_group_o_spec(tq),
        out_shape=jax.ShapeDtypeStruct((B_HEADS, nb, seq, HEAD_DIM), BF16),
        compiler_params=_params(("parallel", "parallel", "arbitrary")),
        name="nsa_window",
    )(q, q, q, k, v)


def _nsa_gate_kernel(g_ref, oc_ref, os_ref, ow_ref, y_ref):
    g = g_ref[...]
    for h in range(B_HEADS):
        y = (g[:, 3 * h:3 * h + 1] * oc_ref[h].astype(F32)
             + g[:, 3 * h + 1:3 * h + 2] * os_ref[h].astype(F32)
             + g[:, 3 * h + 2:3 * h + 3] * ow_ref[h].astype(F32))
        y_ref[:, h * HEAD_DIM:(h + 1) * HEAD_DIM] = y.astype(y_ref.dtype)


def _nsa_gate(gates, o_cmp, o_sel, o_win):
    rows = gates.shape[0]
    tm = min(256, rows)
    o_spec = pl.BlockSpec((B_HEADS, tm, HEAD_DIM), lambda i: (0, i, 0))
    return pl.pallas_call(
        _nsa_gate_kernel,
        grid=(rows // tm,),
        in_specs=[pl.BlockSpec((tm, HEAD_DIM), lambda i: (i, 0)), o_spec, o_spec, o_spec],
        out_specs=pl.BlockSpec((tm, B_HEADS * HEAD_DIM), lambda i: (i, 0)),
        out_shape=jax.ShapeDtypeStruct((rows, B_HEADS * HEAD_DIM), BF16),
        compiler_params=_params(("parallel",)),
        name="nsa_gate",
    )(gates, o_cmp, o_sel, o_win)


def _stick_kernel(q_ref, k_ref, v_ref, o_ref, acc_ref, car_ref, *, tq):
    n = pl.program_id(2)
    q = q_ref[...]
    sub = min(CUM_TILE, tq)
    n_sub = tq // sub
    r_idx = lax.broadcasted_iota(jnp.int32, (2 * sub, 2 * sub), 0) & (sub - 1)
    c_idx = lax.broadcasted_iota(jnp.int32, (2 * sub, 2 * sub), 1)
    suffix_and_total = jnp.where((c_idx >= sub) | (r_idx > c_idx), 1.0, 0.0).astype(BF16)
    q_in_tile = lax.broadcasted_iota(jnp.int32, (tq, 1), 0)
    k_in_sub = lax.broadcasted_iota(jnp.int32, (1, sub), 1)
    acc_ref[...] = jnp.zeros(acc_ref.shape, F32)
    car_ref[...] = jnp.zeros(car_ref.shape, F32)

    def tile(k0, diagonal):
        z = _dot_nt(q, k_ref[pl.ds(k0, tq), :])
        soft = jnp.log(1.0 + jnp.exp(-jnp.abs(z)))
        log_beta = jnp.minimum(z, 0.0) - soft
        log_rest = log_beta - z
        carried = car_ref[...]
        parts = [None] * n_sub
        for u in reversed(range(n_sub)):
            cols = slice(u * sub, (u + 1) * sub)
            rest_u = log_rest[:, cols]
            if diagonal:
                before = (u * sub + k_in_sub) - q_in_tile < 0
                rest_u = jnp.where(before, rest_u, 0.0)
            sums = _dot(jnp.concatenate(_split_bf16(rest_u), axis=1), suffix_and_total)
            a = jnp.exp(log_beta[:, cols] + (sums[:, :sub] + carried))
            if diagonal:
                a = jnp.where(before, a, 0.0)
            carried = carried + sums[:, sub:]
            parts[u] = a.astype(BF16)
        car_ref[...] = carried
        acc_ref[...] += _dot(jnp.concatenate(parts, axis=1), v_ref[pl.ds(k0, tq), :])

    tile(pl.multiple_of(n * tq, tq), True)

    def step(i, carry):
        tile(pl.multiple_of((n - 1 - i) * tq, tq), False)
        return carry

    lax.fori_loop(0, n, step, 0)
    o_ref[...] = acc_ref[...].astype(o_ref.dtype)


def _stick_breaking(q, q_head0, k, k_head0, v, v_head0):
    _, nb, seq, _ = q.shape
    tq = min(STICK_TILE, seq)
    return pl.pallas_call(
        functools.partial(_stick_kernel, tq=tq),
        grid=(nb, C_HEADS, seq // tq),
        in_specs=[_head_spec(tq, q_head0), _seq_spec(seq, k_head0), _seq_spec(seq, v_head0)],
        out_specs=pl.BlockSpec((None, tq, HEAD_DIM), lambda b, h, n: (b, n, h)),
        out_shape=jax.ShapeDtypeStruct((nb, seq, C_HEADS * HEAD_DIM), BF16),
        scratch_shapes=[pltpu.VMEM((tq, HEAD_DIM), F32), pltpu.VMEM((tq, min(CUM_TILE, tq)), F32)],
        compiler_params=_params(("parallel", "parallel", "arbitrary")),
        name="stick_breaking",
    )(q, k, v)


def _merge_kernel(ya_ref, yb_ref, yc_ref, wa_ref, wb_ref, wc_ref, ga_ref, gb_ref, gc_ref, o_ref):
    mixed = (ga_ref[...].astype(F32) * _dot(ya_ref[...], wa_ref[...])
             + gb_ref[...].astype(F32) * _dot(yb_ref[...], wb_ref[...])
             + gc_ref[...].astype(F32) * _dot(yc_ref[...], wc_ref[...]))
    o_ref[...] = mixed.astype(o_ref.dtype)


def _merge(ya, yb, yc, wa, wb, wc, gates, d_model):
    m = ya.shape[0]
    tm, tn = min(512, m), 1024
    col_tiles = d_model // tn

    def y_spec(y):
        return pl.BlockSpec((tm, y.shape[1]), lambda i, j: (i, 0))

    def w_spec(w):
        return pl.BlockSpec((w.shape[0], tn), lambda i, j: (0, j))

    def g_spec(branch):
        return pl.BlockSpec((tm, tn), lambda i, j: (i, branch * col_tiles + j))

    return pl.pallas_call(
        _merge_kernel,
        grid=(m // tm, col_tiles),
        in_specs=[y_spec(ya), y_spec(yb), y_spec(yc), w_spec(wa), w_spec(wb), w_spec(wc),
                  g_spec(0), g_spec(1), g_spec(2)],
        out_specs=pl.BlockSpec((tm, tn), lambda i, j: (i, j)),
        out_shape=jax.ShapeDtypeStruct((m, d_model), BF16),
        compiler_params=_params(("parallel", "arbitrary")),
        name="branch_merge",
    )(ya, yb, yc, wa, wb, wc, gates, gates, gates)


def _ln_kernel(x_ref, y_ref, g_ref, b_ref, o_ref, ob_ref):
    z = ALPHA * x_ref[...] + y_ref[...]
    mu = jnp.mean(z, axis=1, keepdims=True)
    zc = z - mu
    var = jnp.mean(zc * zc, axis=1, keepdims=True)
    out = zc * lax.rsqrt(var + LN_EPS) * g_ref[...] + b_ref[...]
    o_ref[...] = out
    ob_ref[...] = out.astype(BF16)


def _residual_ln(x, y, g, b):
    m, d = x.shape
    tm = min(128, m)
    row = pl.BlockSpec((tm, d), lambda i: (i, 0))
    vec = pl.BlockSpec((1, d), lambda i: (0, 0))
    return pl.pallas_call(
        _ln_kernel,
        grid=(m // tm,),
        in_specs=[row, row, vec, vec],
        out_specs=[row, row],
        out_shape=[jax.ShapeDtypeStruct((m, d), F32), jax.ShapeDtypeStruct((m, d), BF16)],
        compiler_params=_params(("parallel",)),
        name="residual_layer_norm",
    )(x, y, g.reshape(1, d), b.reshape(1, d))


def _rope_tables(seq):
    half = HEAD_DIM // 2
    inv = 1.0 / (ROPE_THETA ** (jnp.arange(half, dtype=F32) / half))
    ang = jnp.arange(seq).astype(F32)[:, None] * inv[None, :]
    cos, sin = jnp.cos(ang), jnp.sin(ang)
    return jnp.concatenate([cos, cos], axis=1), jnp.concatenate([-sin, sin], axis=1)


_A_KV, _B_KV = A_SLOTS, B_KV_HEADS
_ALIGNED_FIELDS = (('q_a', A_HEADS, 'rope_scale'), ('k_a', _A_KV, 'rope'), ('v_a', _A_KV, 'none'),
                   ('q_b', B_HEADS, 'rope_scale'), ('kc_b', _B_KV, 'rope'), ('vc_b', _B_KV, 'none'),
                   ('ks_b', _B_KV, 'rope'), ('vs_b', _B_KV, 'none'), ('kw_b', _B_KV, 'rope'),
                   ('vw_b', _B_KV, 'none'))
_STICK_FIELDS = (('q_c', C_HEADS, 'scale'), ('k_c', C_HEADS, 'none'), ('v_c', C_HEADS, 'none'))
N_NSA_GATES = 3 * B_HEADS


def _field_layout(fields, heads_per_tile):
    head0, modes, start = {}, [], 0
    for name, heads, mode in fields:
        assert heads % heads_per_tile == 0
        head0[name] = start
        modes += [mode] * (heads // heads_per_tile)
        start += heads
    return head0, tuple(modes), start


def _layer(layer, x, xb, nb, seq, tables, w_rows, cmp_pe_k, cmp_wk1, cmp_wk2, cmp_pe_v, cmp_wv1, cmp_wv2,
           w_br_a, w_br_b, w_br_c, w_out, ln1_g, ln1_b, w_up, w_down, ln2_g, ln2_b):
    m, d_model = x.shape

    tn_att = 4 * HEAD_DIM
    at, att_modes, att_heads = _field_layout(_ALIGNED_FIELDS, tn_att // HEAD_DIM)
    att = _matmul(xb, w_rows, layer=layer, b_rows=True, name="proj_attention", out_dtype=BF16,
                  tile_modes=att_modes, tm=1024, tn=tn_att, head_major=True, rope_tables=tables)
    tn_tail = 1024
    st, stick_modes, stick_heads = _field_layout(_STICK_FIELDS, tn_tail // HEAD_DIM)
    stick_col0 = att_heads * HEAD_DIM
    gates_col0 = stick_col0 + stick_heads * HEAD_DIM
    nsa_col0 = gates_col0 + 3 * d_model
    stick = _matmul(xb, w_rows, layer=layer, b_rows=True, col0=stick_col0, name="proj_stick",
                    out_dtype=BF16, tile_modes=stick_modes, tm=1024, tn=tn_tail, head_major=True)
    gates = _matmul(xb, w_rows, layer=layer, b_rows=True, col0=gates_col0, name="proj_branch_gates",
                    out_dtype=BF16, tile_modes=('sigmoid',) * (3 * d_model // tn_tail), tm=1024, tn=tn_tail)
    g_nsa = _matmul(xb, w_rows, layer=layer, b_rows=True, col0=nsa_col0, name="proj_nsa_gates",
                    out_dtype=F32, tile_modes=('sigmoid',), tm=1024, tn=HEAD_DIM)

    att = att.reshape(att_heads, nb, seq, HEAD_DIM)
    stick = stick.reshape(stick_heads, nb, seq, HEAD_DIM)

    y_a = _dilated_mixer(att, at['q_a'], at['k_a'], at['v_a'])

    kc = _compress(att[at['kc_b']:at['kc_b'] + B_KV_HEADS], cmp_pe_k, cmp_wk1, cmp_wk2, "nsa_compress_k")
    vc = _compress(att[at['vc_b']:at['vc_b'] + B_KV_HEADS], cmp_pe_v, cmp_wv1, cmp_wv2, "nsa_compress_v")
    o_cmp, sel = _cmp_select(att, at['q_b'], kc, vc)
    o_sel = _selected(att, at['q_b'], att, at['ks_b'], att, at['vs_b'], sel)
    o_win = _window(att, at['q_b'], att, at['kw_b'], att, at['vw_b'])
    y_b = _nsa_gate(g_nsa, o_cmp.reshape(B_HEADS, m, HEAD_DIM), o_sel.reshape(B_HEADS, m, HEAD_DIM),
                    o_win.reshape(B_HEADS, m, HEAD_DIM))

    y_c = _stick_breaking(stick, st['q_c'], stick, st['k_c'], stick, st['v_c'])

    merged = _merge(y_a.reshape(m, -1), y_b, y_c.reshape(m, -1), w_br_a.astype(BF16),
                    w_br_b.astype(BF16), w_br_c.astype(BF16), gates, d_model)
    mixed = _matmul(merged, w_out, layer=layer, name="out_proj", out_dtype=F32,
                    tile_modes=('none',) * (d_model // 512), tm=1024, tn=512)
    x1, x1b = _residual_ln(x, mixed, ln1_g, ln1_b)

    d_ff = w_up.shape[2]
    hidden = _matmul(x1b, w_up, layer=layer, name="mlp_up", out_dtype=BF16,
                     tile_modes=('relu2',) * (d_ff // 512), tm=1024, tn=512)
    down = _matmul(hidden, w_down, layer=layer, name="mlp_down", out_dtype=F32,
                   tile_modes=('none',) * (d_model // 1024), tm=1024, tn=1024, tk=2048)
    return _residual_ln(x1, down, ln2_g, ln2_b)


def kernel(x, w_in, cmp_pe_k, cmp_wk1, cmp_wk2, cmp_pe_v, cmp_wv1, cmp_wv2, w_br_a, w_br_b, w_br_c,
           w_out, ln1_g, ln1_b, w_up, w_down, ln2_g, ln2_b):
    nb, seq, d_model = x.shape
    tables = _rope_tables(seq)
    xf = x.reshape(nb * seq, d_model)
    xb = xf.astype(BF16)
    aligned_cols = sum(heads for _, heads, _ in _ALIGNED_FIELDS) * HEAD_DIM
    w_rows = _input_weight_rows(w_in, aligned_cols, N_NSA_GATES)
    for l in range(w_in.shape[0]):
        xf, xb = _layer(l, xf, xb, nb, seq, tables, w_rows, cmp_pe_k[l], cmp_wk1[l], cmp_wk2[l],
                        cmp_pe_v[l], cmp_wv1[l], cmp_wv2[l], w_br_a[l], w_br_b[l], w_br_c[l],
                        w_out, ln1_g[l], ln1_b[l], w_up, w_down, ln2_g[l], ln2_b[l])
    return xf.reshape(nb, seq, d_model)
```

```python
import functools
import math

import numpy as np
import jax
import jax.numpy as jnp
from jax import lax
from jax.experimental import pallas as pl
from jax.experimental.pallas import tpu as pltpu

F32 = jnp.float32
BF16 = jnp.bfloat16

HEAD_DIM = 128
ROPE_THETA = 10000.0
LN_EPS = 1e-5
DEPTH = 2

DIL_GROUPS = ((128, 1), (512, 4), (2048, 16))
A_SLOTS = 4
A_HEADS = A_SLOTS * len(DIL_GROUPS)

B_HEADS = 12
B_KV_HEADS = 4
B_GROUP = B_HEADS // B_KV_HEADS
CMP_LEN = 32
CMP_STRIDE = 16
SEL_LEN = 64
SEL_SHIFT = 6
SEL_TOPK = 16
WIN_LEN = 512

C_HEADS = 8

ALPHA = (2.0 * DEPTH) ** 0.25
Q_SCALE = HEAD_DIM ** -0.5

VMEM_LIMIT = 48 * 1024 * 1024
MXU_COLS = 256
ATT_TQ = 128
ATT_TILES = 4
DIL_ROWS = 128
DIL_ROWS_DENSE = 512
SEL_TQ = 128
SEL_TILES = 2
SEL_CHUNK = 1024
STICK_TILE = 512
CUM_TILE = 128


def _params(semantics):
    return pltpu.CompilerParams(dimension_semantics=semantics, vmem_limit_bytes=VMEM_LIMIT)


def _dot(a, b):
    return jnp.dot(a, b, preferred_element_type=F32)


def _dot_nt(a, b):
    return lax.dot_general(a, b, (((1,), (1,)), ((), ())), preferred_element_type=F32)


def _split_bf16(x):
    hi = x.astype(BF16)
    lo = (x - hi.astype(F32)).astype(BF16)
    return hi, lo


def _tile_flag(tile_modes, wanted):
    hits = [mode in wanted for mode in tile_modes]
    if all(hits) or not any(hits):
        return hits[0]
    j = pl.program_id(1)
    flag = None
    for c, hit in enumerate(hits):
        if hit:
            flag = (j == c) if flag is None else (flag | (j == c))
    return flag


def _write_columns(acc, o_ref, col0, tile_modes, head_major, cos_ref, sin_ref):
    is_rope = _tile_flag(tile_modes, ('rope', 'rope_scale'))
    is_scaled = _tile_flag(tile_modes, ('rope_scale', 'scale'))
    uniform = tile_modes[0] if len(set(tile_modes)) == 1 else None
    assert uniform is not None or not (set(tile_modes) & {'sigmoid', 'relu2'})

    def epilogue(y):
        if is_rope is not False:
            roped = y * cos_ref[...] + pltpu.roll(y, HEAD_DIM // 2, 1) * sin_ref[...]
            y = roped if is_rope is True else jnp.where(is_rope, roped, y)
        if is_scaled is not False:
            y = y * (Q_SCALE if is_scaled is True else jnp.where(is_scaled, Q_SCALE, 1.0))
        if uniform == 'sigmoid':
            y = jax.nn.sigmoid(y)
        if uniform == 'relu2':
            y = jnp.square(jnp.maximum(y, 0.0))
        return y

    width = acc.shape[1]
    if head_major:
        for c in range(width // HEAD_DIM):
            y = epilogue(acc[:, c * HEAD_DIM:(c + 1) * HEAD_DIM])
            o_ref[col0 // HEAD_DIM + c] = y.astype(o_ref.dtype)
    else:
        o_ref[:, col0:col0 + width] = epilogue(acc).astype(o_ref.dtype)


def _matmul_kernel(*refs, tile_modes, head_major, n_k, use_rope, acc_in_out, b_rows):
    refs = list(refs)
    a_ref, b_ref = refs[0], refs[1]
    cos_ref, sin_ref = (refs[2], refs[3]) if use_rope else (None, None)
    o_ref = refs[4] if use_rope else refs[2]
    acc_ref = o_ref if acc_in_out else (refs[-1] if n_k > 1 else None)
    tn = b_ref.shape[0] if b_rows else b_ref.shape[1]

    def product(col0, width):
        if b_rows:
            return _dot_nt(a_ref[...], b_ref[col0:col0 + width, :].astype(BF16))
        return _dot(a_ref[...], b_ref[:, col0:col0 + width].astype(BF16))

    if n_k == 1:
        width = min(MXU_COLS, tn)
        for col0 in range(0, tn, width):
            _write_columns(product(col0, width), o_ref, col0, tile_modes, head_major, cos_ref, sin_ref)
        return

    @pl.when(pl.program_id(2) == 0)
    def _():
        acc_ref[...] = jnp.zeros(acc_ref.shape, F32)

    acc_ref[...] += product(0, tn)
    if not acc_in_out:
        pl.when(pl.program_id(2) == n_k - 1)(
            lambda: _write_columns(acc_ref[...], o_ref, 0, tile_modes, head_major, cos_ref, sin_ref))


def _matmul(a, b, *, name, out_dtype, tile_modes, tm, tn, tk=None, layer=None, col0=0,
            head_major=False, rope_tables=None, b_rows=False):
    m, kdim = a.shape
    n = tn * len(tile_modes)
    tm = min(tm, m)
    tk = kdim if tk is None else min(tk, kdim)
    assert m % tm == 0 and kdim % tk == 0 and col0 % tn == 0
    n_k = kdim // tk
    col_tile0 = col0 // tn
    use_rope = any(mode.startswith('rope') for mode in tile_modes)
    acc_in_out = n_k > 1 and out_dtype == F32 and set(tile_modes) == {'none'} and not head_major
    if b_rows:
        assert layer is not None
        b_spec = pl.BlockSpec((None, tn, tk), lambda i, j, k: (layer, col_tile0 + j, k))
    elif layer is None:
        b_spec = pl.BlockSpec((tk, tn), lambda i, j, k: (k, col_tile0 + j))
    else:
        b_spec = pl.BlockSpec((None, tk, tn), lambda i, j, k: (layer, k, col_tile0 + j))
    in_specs = [pl.BlockSpec((tm, tk), lambda i, j, k: (i, k)), b_spec]
    operands = [a, b]
    if use_rope:
        assert head_major
        cos, sin = rope_tables
        seq_tiles = cos.shape[0] // tm
        assert cos.shape[0] % tm == 0
        spec = pl.BlockSpec((tm, HEAD_DIM), lambda i, j, k: (i % seq_tiles, 0))
        in_specs += [spec, spec]
        operands += [cos, sin]
    if head_major:
        out_shape = jax.ShapeDtypeStruct((n // HEAD_DIM, m, HEAD_DIM), out_dtype)
        out_spec = pl.BlockSpec((tn // HEAD_DIM, tm, HEAD_DIM), lambda i, j, k: (j, i, 0))
    else:
        out_shape = jax.ShapeDtypeStruct((m, n), out_dtype)
        out_spec = pl.BlockSpec((tm, tn), lambda i, j, k: (i, j))
    scratch = [pltpu.VMEM((tm, tn), F32)] if (n_k > 1 and not acc_in_out) else []
    return pl.pallas_call(
        functools.partial(_matmul_kernel, tile_modes=tuple(tile_modes), head_major=head_major, n_k=n_k,
                          use_rope=use_rope, acc_in_out=acc_in_out, b_rows=b_rows),
        grid=(m // tm, len(tile_modes), n_k),
        in_specs=in_specs,
        out_specs=out_spec,
        out_shape=out_shape,
        scratch_shapes=scratch,
        compiler_params=_params(("parallel", "parallel", "arbitrary")),
        name=name,
    )(*operands)


def _gather_rows_kernel(w_ref, o_ref, *, tn, k_chunks, n_layers):
    stride = k_chunks * n_layers
    for layer in range(n_layers):
        pieces = [w_ref[pl.ds(c * n_layers + layer, tn, stride=stride), :] for c in range(k_chunks)]
        o_ref[layer] = jnp.concatenate(pieces, axis=1).astype(o_ref.dtype)


def _input_weight_rows(w_in, aligned_cols, n_gates, tn=HEAD_DIM):
    n_layers, kdim, n = w_in.shape
    k_chunks = kdim // HEAD_DIM
    tail = n - aligned_cols - n_gates
    assert aligned_cols % tn == 0 and tail % tn == 0 and kdim % HEAD_DIM == 0
    aligned_tiles, tail_tiles = aligned_cols // tn, tail // tn
    rows_per_col = k_chunks * n_layers
    flat = w_in.reshape(n_layers, k_chunks, HEAD_DIM, n).transpose(3, 1, 0, 2).reshape(n * rows_per_col, HEAD_DIM)

    def source_row(j):
        col = jnp.where(j < aligned_tiles, tn * j,
                        jnp.where(j < aligned_tiles + tail_tiles,
                                  aligned_cols + n_gates + tn * (j - aligned_tiles), aligned_cols))
        return (col * rows_per_col, 0)

    tiles = aligned_tiles + tail_tiles + 1
    return pl.pallas_call(
        functools.partial(_gather_rows_kernel, tn=tn, k_chunks=k_chunks, n_layers=n_layers),
        grid=(tiles,),
        in_specs=[pl.BlockSpec((pl.Element(tn * rows_per_col), pl.Element(HEAD_DIM)), source_row)],
        out_specs=pl.BlockSpec((n_layers, tn, kdim), lambda j: (0, j, 0)),
        out_shape=jax.ShapeDtypeStruct((n_layers, tiles * tn, kdim), BF16),
        compiler_params=_params(("parallel",)),
        name="input_weight_rows",
    )(flat)


def _window_start(t0, back, span, seq, align):
    start = jnp.minimum(jnp.maximum(t0 - back, 0), seq - span)
    return pl.multiple_of(start, align)


def _dilated_group_kernel(q_ref, k_ref, v_ref, o_ref, lse_ref, *, dil, rows, band, n_rows):
    l0 = pl.program_id(2) * rows
    span = min(band + rows, n_rows)
    start = _window_start(l0, band, span, n_rows, math.gcd(rows, band))
    dist = (l0 + lax.broadcasted_iota(jnp.int32, (rows, 1), 0)) - (start + lax.broadcasted_iota(jnp.int32, (1, span), 1))
    keep = (dist >= 0) & (dist <= band)
    for r in range(dil):
        lanes = slice(r * HEAD_DIM, (r + 1) * HEAD_DIM)
        s = jnp.where(keep, _dot_nt(q_ref[:, lanes], k_ref[pl.ds(start, span), lanes]), -jnp.inf)
        m = jnp.max(s, axis=1, keepdims=True)
        p = jnp.exp(s - m)
        l = jnp.sum(p, axis=1, keepdims=True)
        o = _dot(p.astype(BF16), v_ref[pl.ds(start, span), lanes]) / l
        lse = jnp.broadcast_to(m + jnp.log(l), (rows, HEAD_DIM))
        if dil == 1:
            o_ref[...] = o
            lse_ref[...] = lse
        else:
            o_ref[pl.ds(r, rows, stride=dil), :] = o
            lse_ref[pl.ds(r, rows, stride=dil), :] = lse


def _head_spec(rows, head0):
    return pl.BlockSpec((None, None, rows, HEAD_DIM), lambda b, h, n: (head0 + h, b, n, 0))


def _seq_spec(seq, head0):
    return pl.BlockSpec((None, None, seq, HEAD_DIM), lambda b, h, n: (head0 + h, b, 0, 0))


def _dilated_group(q, q_head0, k, k_head0, v, v_head0, window, dil):
    _, nb, n_rows, width = q.shape
    seq = n_rows * dil
    band = window // dil
    rows = min(DIL_ROWS if dil > 1 else DIL_ROWS_DENSE, n_rows)
    assert width == dil * HEAD_DIM and n_rows % rows == 0
    view_spec = pl.BlockSpec((None, None, rows, width), lambda b, s, j: (q_head0 + s, b, j, 0))
    out_spec = pl.BlockSpec((None, None, rows * dil, HEAD_DIM), lambda b, s, j: (s, b, j, 0))
    out_shape = jax.ShapeDtypeStruct((A_SLOTS, nb, seq, HEAD_DIM), F32)
    return pl.pallas_call(
        functools.partial(_dilated_group_kernel, dil=dil, rows=rows, band=band, n_rows=n_rows),
        grid=(nb, A_SLOTS, n_rows // rows),
        in_specs=[view_spec,
                  pl.BlockSpec((None, None, n_rows, width), lambda b, s, j: (k_head0 + s, b, 0, 0)),
                  pl.BlockSpec((None, None, n_rows, width), lambda b, s, j: (v_head0 + s, b, 0, 0))],
        out_specs=[out_spec, out_spec],
        out_shape=[out_shape, out_shape],
        compiler_params=_params(("parallel", "parallel", "arbitrary")),
        name="dilated_group_%d" % dil,
    )(q, k, v)


def _dilated_merge_kernel(o0_ref, o1_ref, o2_ref, l0_ref, l1_ref, l2_ref, y_ref):
    for s in range(A_SLOTS):
        lses = (l0_ref[s], l1_ref[s], l2_ref[s])
        top = jnp.maximum(jnp.maximum(lses[0], lses[1]), lses[2])
        ws = [jnp.exp(lse - top) for lse in lses]
        y = (ws[0] * o0_ref[s] + ws[1] * o1_ref[s] + ws[2] * o2_ref[s]) / (ws[0] + ws[1] + ws[2])
        y_ref[:, s * HEAD_DIM:(s + 1) * HEAD_DIM] = y.astype(y_ref.dtype)


def _dilated_mixer(att, q_head0, k_head0, v_head0):
    _, nb, seq, _ = att.shape
    outs, lses = [], []
    for g, (window, dil) in enumerate(DIL_GROUPS):
        qh = q_head0 + g * A_SLOTS
        if dil == 1:
            o, lse = _dilated_group(att, qh, att, k_head0, att, v_head0, window, dil)
        else:
            def strided(head0):
                return att[head0:head0 + A_SLOTS].reshape(A_SLOTS, nb, seq // dil, dil * HEAD_DIM)
            o, lse = _dilated_group(strided(qh), 0, strided(k_head0), 0, strided(v_head0), 0, window, dil)
        outs.append(o)
        lses.append(lse)
    tq = min(256, seq)
    spec = pl.BlockSpec((A_SLOTS, None, tq, HEAD_DIM), lambda b, n: (0, b, n, 0))
    return pl.pallas_call(
        _dilated_merge_kernel,
        grid=(nb, seq // tq),
        in_specs=[spec] * 6,
        out_specs=pl.BlockSpec((None, tq, A_SLOTS * HEAD_DIM), lambda b, n: (b, n, 0)),
        out_shape=jax.ShapeDtypeStruct((nb, seq, A_SLOTS * HEAD_DIM), BF16),
        compiler_params=_params(("parallel", "parallel")),
        name="dilated_merge",
    )(*outs, *lses)


def _gelu_tanh(x):
    return 0.5 * x * (1.0 + jnp.tanh(math.sqrt(2.0 / math.pi) * (x + 0.044715 * (x * x * x))))


def _compress_kernel(x_ref, pe_ref, w1_ref, w2_ref, o_ref):
    x = x_ref[...].astype(F32)
    first = _dot((x + pe_ref[0:1, :]).astype(BF16), w1_ref[0])
    second = _dot((x + pe_ref[1:2, :]).astype(BF16), w1_ref[1])
    chunks = x.shape[0]
    hidden = first + pltpu.roll(second, chunks - 1, 0)
    o_ref[...] = _dot(_gelu_tanh(hidden).astype(BF16), w2_ref[...]).astype(o_ref.dtype)


def _compress(t, pe, w1, w2, name):
    nh, nb, seq, _ = t.shape
    chunks = seq // CMP_STRIDE
    width = CMP_STRIDE * HEAD_DIM
    ratio = CMP_LEN // CMP_STRIDE
    x = t.reshape(nh, nb, chunks, width)
    return pl.pallas_call(
        _compress_kernel,
        grid=(nh, nb),
        in_specs=[pl.BlockSpec((None, None, chunks, width), lambda h, b: (h, b, 0, 0)),
                  pl.BlockSpec((ratio, width), lambda h, b: (0, 0)),
                  pl.BlockSpec((ratio, width, HEAD_DIM), lambda h, b: (0, 0, 0)),
                  pl.BlockSpec((HEAD_DIM, HEAD_DIM), lambda h, b: (0, 0))],
        out_specs=pl.BlockSpec((None, None, chunks, HEAD_DIM), lambda h, b: (h, b, 0, 0)),
        out_shape=jax.ShapeDtypeStruct((nh, nb, chunks, HEAD_DIM), BF16),
        compiler_params=_params(("parallel", "parallel")),
        name=name,
    )(x, pe.reshape(ratio, width), w1.reshape(ratio, width, HEAD_DIM).astype(BF16), w2.astype(BF16))


def _group_q(q_refs, row0=0, rows=None):
    rows = q_refs[0].shape[0] if rows is None else rows
    return jnp.concatenate([r[row0:row0 + rows, :] for r in q_refs], axis=0)


def _group_q_specs(tq, head0):
    return [pl.BlockSpec((None, None, tq, HEAD_DIM),
                         functools.partial(lambda b, h, n, g: (head0 + h * B_GROUP + g, b, n, 0), g=g))
            for g in range(B_GROUP)]


def _cmp_select_kernel(q0_ref, q1_ref, q2_ref, kc_ref, vc_ref, ov_ref, o_ref, sel_ref, *, tq, tiles):
    for u in range(tiles):
        _cmp_select_tile((q0_ref, q1_ref, q2_ref), kc_ref, vc_ref, ov_ref, o_ref, sel_ref,
                         (pl.program_id(2) * tiles + u) * tq, u * tq, tq)


def _cmp_select_tile(q_refs, kc_ref, vc_ref, ov_ref, o_ref, sel_ref, t0, row0, tq):
    n_cmp = kc_ref.shape[0]
    rows = B_GROUP * tq
    s = _dot_nt(_group_q(q_refs, row0, tq), kc_ref[...])
    tpos3 = t0 + (lax.broadcasted_iota(jnp.int32, (rows, 1), 0) & (tq - 1))
    c_end = lax.broadcasted_iota(jnp.int32, (1, n_cmp), 1) * CMP_STRIDE + (CMP_LEN - 1)
    s = jnp.where(c_end - tpos3 <= 0, s, -jnp.inf)
    m = jnp.max(s, axis=1, keepdims=True)
    m = jnp.where(jnp.abs(m) < jnp.inf, m, 0.0)
    e = jnp.exp(s - m)
    den = jnp.sum(e, axis=1, keepdims=True)
    p = e / jnp.where(den > 0, den, 1.0)
    o = _dot(p.astype(BF16), vc_ref[...]).reshape(B_GROUP, tq, HEAD_DIM)
    o_ref[:, row0:row0 + tq, :] = o.astype(o_ref.dtype)

    p_sum = p[0:tq] + p[tq:2 * tq] + p[2 * tq:3 * tq]
    hi, lo = _split_bf16(p_sum)
    imp = _dot(hi, ov_ref[...]) + _dot(lo, ov_ref[...])
    tpos = t0 + lax.broadcasted_iota(jnp.int32, (tq, 1), 0)
    rel = lax.broadcasted_iota(jnp.int32, (1, HEAD_DIM), 1) - (tpos >> SEL_SHIFT)
    j_abs = jnp.broadcast_to(lax.broadcasted_iota(jnp.int32, (1, HEAD_DIM), 1), rel.shape)
    forced = (j_abs == 0) | (rel == 0) | (rel == -1)
    imp = jnp.where(forced, jnp.inf, jnp.where(rel <= 0, imp, -jnp.inf))

    n_sel = HEAD_DIM // 2
    imp_t = imp.T
    mine = imp_t[0:n_sel]
    j_idx = lax.broadcasted_iota(jnp.int32, (n_sel, tq), 0)
    beaten = jnp.zeros((n_sel, tq), F32)
    for kk in range(n_sel):
        other = imp_t[kk:kk + 1, :]
        wins = (other > mine) | ((other == mine) & (j_idx > kk))
        beaten = beaten + jnp.where(wins, 1.0, 0.0)
    chosen = jnp.where(beaten < SEL_TOPK, 1.0, 0.0)
    chosen = jnp.concatenate([chosen, jnp.zeros((HEAD_DIM - n_sel, tq), F32)], axis=0)
    sel_ref[:, row0:row0 + tq] = chosen.astype(sel_ref.dtype)


def _overlap_matrix(n_cmp_rows, n_sel):
    c_start = np.arange(n_cmp_rows) * CMP_STRIDE
    c_end = c_start + CMP_LEN - 1
    s_start = np.arange(HEAD_DIM) * SEL_LEN
    ov = (c_start[:, None] <= s_start[None, :] + SEL_LEN - 1) & (c_end[:, None] >= s_start[None, :])
    ov &= (np.arange(HEAD_DIM) < n_sel)[None, :]
    return jnp.asarray(ov.astype(np.float32), dtype=BF16)


def _group_o_spec(tq):
    return pl.BlockSpec((B_GROUP, None, tq, HEAD_DIM), lambda b, h, n: (h, b, n, 0))


def _sel_spec(tq):
    return pl.BlockSpec((None, None, HEAD_DIM, tq), lambda b, h, n: (h, b, 0, n))


def _cmp_select(q, q_head0, kc, vc):
    _, nb, seq, _ = q.shape
    assert seq // SEL_LEN <= HEAD_DIM // 2
    tq = min(ATT_TQ, seq)
    tiles = min(ATT_TILES, seq // tq)
    n_cmp = kc.shape[2]
    kv_spec = pl.BlockSpec((None, None, n_cmp, HEAD_DIM), lambda b, h, n: (h, b, 0, 0))
    return pl.pallas_call(
        functools.partial(_cmp_select_kernel, tq=tq, tiles=tiles),
        grid=(nb, B_KV_HEADS, seq // (tq * tiles)),
        in_specs=_group_q_specs(tq * tiles, q_head0) + [kv_spec, kv_spec,
                                                        pl.BlockSpec((n_cmp, HEAD_DIM), lambda b, h, n: (0, 0))],
        out_specs=[_group_o_spec(tq * tiles), _sel_spec(tq * tiles)],
        out_shape=[jax.ShapeDtypeStruct((B_HEADS, nb, seq, HEAD_DIM), BF16),
                   jax.ShapeDtypeStruct((B_KV_HEADS, nb, HEAD_DIM, seq), BF16)],
        compiler_params=_params(("parallel", "parallel", "arbitrary")),
        name="nsa_compressed_select",
    )(q, q, q, kc, vc, _overlap_matrix(n_cmp, seq // SEL_LEN))


def _selected_kernel(q0_ref, q1_ref, q2_ref, k_ref, v_ref, sel_ref, o_ref, vt_ref, acc_ref, *, tq, kc, tiles):
    n = pl.program_id(2)
    t_first = n * (tq * tiles)
    cols = B_GROUP * tq

    @pl.when(n == 0)
    def _():
        for c in range(v_ref.shape[0] // kc):
            vt_ref[c] = v_ref[c * kc:(c + 1) * kc, :].astype(F32).T.astype(BF16)

    qs = [_group_q((q0_ref, q1_ref, q2_ref), u * tq, tq) for u in range(tiles)]
    sels = [sel_ref[:, u * tq:(u + 1) * tq] for u in range(tiles)]
    qposs = [t_first + u * tq + lax.broadcasted_iota(jnp.int32, (1, tq), 1) for u in range(tiles)]
    key_in_chunk = lax.broadcasted_iota(jnp.int32, (kc, 1), 0)
    blk = lax.broadcasted_iota(jnp.int32, (1, HEAD_DIM), 1)
    acc_ref[...] = jnp.zeros(acc_ref.shape, F32)

    def step(i, carry):
        k0 = pl.multiple_of(i * kc, kc)
        kpos = k0 + key_in_chunk
        keys = k_ref[pl.ds(k0, kc), :]
        expand = jnp.where((kpos >> SEL_SHIFT) == blk, 1.0, 0.0).astype(BF16)
        out = []
        for u in range(tiles):
            m_old, l_old = carry[u]
            picked = _dot(expand, sels[u])
            bias = jnp.where((picked > 0.5) & (kpos - qposs[u] <= 0), 0.0, -jnp.inf)
            s = _dot_nt(keys, qs[u]) + jnp.concatenate([bias] * B_GROUP, axis=1)
            m_new = jnp.maximum(m_old, jnp.max(s, axis=0, keepdims=True))
            p = jnp.exp(s - m_new)
            alpha = jnp.exp(m_old - m_new)
            acc_ref[u] = alpha * acc_ref[u] + _dot(vt_ref[i], p.astype(BF16))
            out.append((m_new, alpha * l_old + jnp.sum(p, axis=0, keepdims=True)))
        return tuple(out)

    init = tuple((jnp.full((1, cols), -1e30, F32), jnp.zeros((1, cols), F32)) for _ in range(tiles))
    final = lax.fori_loop(0, t_first // kc + 1, step, init)
    for u in range(tiles):
        o_t = acc_ref[u] / final[u][1]
        for g in range(B_GROUP):
            o_ref[g, u * tq:(u + 1) * tq, :] = o_t[:, g * tq:(g + 1) * tq].T.astype(o_ref.dtype)


def _selected(q, q_head0, k, k_head0, v, v_head0, sel):
    _, nb, seq, _ = q.shape
    tq = min(SEL_TQ, seq)
    kc = min(SEL_CHUNK, seq)
    tiles = min(SEL_TILES, kc // tq)
    assert kc % (tq * tiles) == 0 and seq % kc == 0 and tq == HEAD_DIM
    return pl.pallas_call(
        functools.partial(_selected_kernel, tq=tq, kc=kc, tiles=tiles),
        grid=(nb, B_KV_HEADS, seq // (tq * tiles)),
        in_specs=_group_q_specs(tq * tiles, q_head0) + [_seq_spec(seq, k_head0), _seq_spec(seq, v_head0),
                                                        _sel_spec(tq * tiles)],
        out_specs=_group_o_spec(tq * tiles),
        out_shape=jax.ShapeDtypeStruct((B_HEADS, nb, seq, HEAD_DIM), BF16),
        scratch_shapes=[pltpu.VMEM((seq // kc, HEAD_DIM, kc), BF16),
                        pltpu.VMEM((tiles, HEAD_DIM, B_GROUP * tq), F32)],
        compiler_params=_params(("parallel", "parallel", "arbitrary")),
        name="nsa_selected",
    )(q, q, q, k, v, sel)


def _window_kernel(q0_ref, q1_ref, q2_ref, k_ref, v_ref, o_ref, *, tq, tiles, seq):
    rows = B_GROUP * tq
    span = min(WIN_LEN + tq, seq)
    for u in range(tiles):
        t0 = (pl.program_id(2) * tiles + u) * tq
        start = _window_start(t0, WIN_LEN, span, seq, tq)
        s = _dot_nt(_group_q((q0_ref, q1_ref, q2_ref), u * tq, tq), k_ref[pl.ds(start, span), :])
        qpos = t0 + (lax.broadcasted_iota(jnp.int32, (rows, 1), 0) & (tq - 1))
        kpos = start + lax.broadcasted_iota(jnp.int32, (1, span), 1)
        dist = qpos - kpos
        s = jnp.where((dist >= 0) & (dist <= WIN_LEN - 1), s, -jnp.inf)
        m = jnp.max(s, axis=1, keepdims=True)
        p = jnp.exp(s - m)
        l = jnp.sum(p, axis=1, keepdims=True)
        o = _dot(p.astype(BF16), v_ref[pl.ds(start, span), :]) / l
        o_ref[:, u * tq:(u + 1) * tq, :] = o.reshape(B_GROUP, tq, HEAD_DIM).astype(o_ref.dtype)


def _window(q, q_head0, k, k_head0, v, v_head0):
    _, nb, seq, _ = q.shape
    tq = min(ATT_TQ, seq)
    tiles = min(ATT_TILES, seq // tq)
    assert WIN_LEN % tq == 0
    return pl.pallas_call(
        functools.partial(_window_kernel, tq=tq, tiles=tiles, seq=seq),
        grid=(nb, B_KV_HEADS, seq // (tq * tiles)),
        in_specs=_group_q_specs(tq * tiles, q_head0) + [_seq_spec(seq, k_head0), _seq_spec(seq, v_head0)],
        out_specs=_group_o_spec(tq * tiles),
        out_shape=jax.ShapeDtypeStruct((B_HEADS, nb, seq, HEAD_DIM), BF16),
        compiler_params=_params(("parallel", "parallel", "arbitrary")),
        name="nsa_window",
    )(q, q, q, k, v)


def _nsa_gate_kernel(g_ref, oc_ref, os_ref, ow_ref, y_ref):
    g = g_ref[...]
    for h in range(B_HEADS):
        y = (g[:, 3 * h:3 * h + 1] * oc_ref[h].astype(F32)
             + g[:, 3 * h + 1:3 * h + 2] * os_ref[h].astype(F32)
             + g[:, 3 * h + 2:3 * h + 3] * ow_ref[h].astype(F32))
        y_ref[:, h * HEAD_DIM:(h + 1) * HEAD_DIM] = y.astype(y_ref.dtype)


def _nsa_gate(gates, o_cmp, o_sel, o_win):
    rows = gates.shape[0]
    tm = min(256, rows)
    o_spec = pl.BlockSpec((B_HEADS, tm, HEAD_DIM), lambda i: (0, i, 0))
    return pl.pallas_call(
        _nsa_gate_kernel,
        grid=(rows // tm,),
        in_specs=[pl.BlockSpec((tm, HEAD_DIM), lambda i: (i, 0)), o_spec, o_spec, o_spec],
        out_specs=pl.BlockSpec((tm, B_HEADS * HEAD_DIM), lambda i: (i, 0)),
        out_shape=jax.ShapeDtypeStruct((rows, B_HEADS * HEAD_DIM), BF16),
        compiler_params=_params(("parallel",)),
        name="nsa_gate",
    )(gates, o_cmp, o_sel, o_win)


def _stick_kernel(q_ref, k_ref, v_ref, o_ref, acc_ref, car_ref, *, tq):
    n = pl.program_id(2)
    q = q_ref[...]
    sub = min(CUM_TILE, tq)
    n_sub = tq // sub
    r_idx = lax.broadcasted_iota(jnp.int32, (2 * sub, 2 * sub), 0) & (sub - 1)
    c_idx = lax.broadcasted_iota(jnp.int32, (2 * sub, 2 * sub), 1)
    suffix_and_total = jnp.where((c_idx >= sub) | (r_idx > c_idx), 1.0, 0.0).astype(BF16)
    q_in_tile = lax.broadcasted_iota(jnp.int32, (tq, 1), 0)
    k_in_sub = lax.broadcasted_iota(jnp.int32, (1, sub), 1)
    acc_ref[...] = jnp.zeros(acc_ref.shape, F32)
    car_ref[...] = jnp.zeros(car_ref.shape, F32)

    def tile(k0, diagonal):
        z = _dot_nt(q, k_ref[pl.ds(k0, tq), :])
        soft = jnp.log(1.0 + jnp.exp(-jnp.abs(z)))
        log_beta = jnp.minimum(z, 0.0) - soft
        log_rest = log_beta - z
        carried = car_ref[...]
        parts = [None] * n_sub
        for u in reversed(range(n_sub)):
            cols = slice(u * sub, (u + 1) * sub)
            rest_u = log_rest[:, cols]
            if diagonal:
                before = (u * sub + k_in_sub) - q_in_tile < 0
                rest_u = jnp.where(before, rest_u, 0.0)
            sums = _dot(jnp.concatenate(_split_bf16(rest_u), axis=1), suffix_and_total)
            a = jnp.exp(log_beta[:, cols] + (sums[:, :sub] + carried))
            if diagonal:
                a = jnp.where(before, a, 0.0)
            carried = carried + sums[:, sub:]
            parts[u] = a.astype(BF16)
        car_ref[...] = carried
        acc_ref[...] += _dot(jnp.concatenate(parts, axis=1), v_ref[pl.ds(k0, tq), :])

    tile(pl.multiple_of(n * tq, tq), True)

    def step(i, carry):
        tile(pl.multiple_of((n - 1 - i) * tq, tq), False)
        return carry

    lax.fori_loop(0, n, step, 0)
    o_ref[...] = acc_ref[...].astype(o_ref.dtype)


def _stick_breaking(q, q_head0, k, k_head0, v, v_head0):
    _, nb, seq, _ = q.shape
    tq = min(STICK_TILE, seq)
    return pl.pallas_call(
        functools.partial(_stick_kernel, tq=tq),
        grid=(nb, C_HEADS, seq // tq),
        in_specs=[_head_spec(tq, q_head0), _seq_spec(seq, k_head0), _seq_spec(seq, v_head0)],
        out_specs=pl.BlockSpec((None, tq, HEAD_DIM), lambda b, h, n: (b, n, h)),
        out_shape=jax.ShapeDtypeStruct((nb, seq, C_HEADS * HEAD_DIM), BF16),
        scratch_shapes=[pltpu.VMEM((tq, HEAD_DIM), F32), pltpu.VMEM((tq, min(CUM_TILE, tq)), F32)],
        compiler_params=_params(("parallel", "parallel", "arbitrary")),
        name="stick_breaking",
    )(q, k, v)


def _merge_kernel(ya_ref, yb_ref, yc_ref, wa_ref, wb_ref, wc_ref, ga_ref, gb_ref, gc_ref, o_ref):
    mixed = (ga_ref[...].astype(F32) * _dot(ya_ref[...], wa_ref[...])
             + gb_ref[...].astype(F32) * _dot(yb_ref[...], wb_ref[...])
             + gc_ref[...].astype(F32) * _dot(yc_ref[...], wc_ref[...]))
    o_ref[...] = mixed.astype(o_ref.dtype)


def _merge(ya, yb, yc, wa, wb, wc, gates, d_model):
    m = ya.shape[0]
    tm, tn = min(512, m), 1024
    col_tiles = d_model // tn

    def y_spec(y):
        return pl.BlockSpec((tm, y.shape[1]), lambda i, j: (i, 0))

    def w_spec(w):
        return pl.BlockSpec((w.shape[0], tn), lambda i, j: (0, j))

    def g_spec(branch):
        return pl.BlockSpec((tm, tn), lambda i, j: (i, branch * col_tiles + j))

    return pl.pallas_call(
        _merge_kernel,
        grid=(m // tm, col_tiles),
        in_specs=[y_spec(ya), y_spec(yb), y_spec(yc), w_spec(wa), w_spec(wb), w_spec(wc),
                  g_spec(0), g_spec(1), g_spec(2)],
        out_specs=pl.BlockSpec((tm, tn), lambda i, j: (i, j)),
        out_shape=jax.ShapeDtypeStruct((m, d_model), BF16),
        compiler_params=_params(("parallel", "arbitrary")),
        name="branch_merge",
    )(ya, yb, yc, wa, wb, wc, gates, gates, gates)


def _ln_kernel(x_ref, y_ref, g_ref, b_ref, o_ref, ob_ref):
    z = ALPHA * x_ref[...] + y_ref[...]
    mu = jnp.mean(z, axis=1, keepdims=True)
    zc = z - mu
    var = jnp.mean(zc * zc, axis=1, keepdims=True)
    out = zc * lax.rsqrt(var + LN_EPS) * g_ref[...] + b_ref[...]
    o_ref[...] = out
    ob_ref[...] = out.astype(BF16)


def _residual_ln(x, y, g, b):
    m, d = x.shape
    tm = min(128, m)
    row = pl.BlockSpec((tm, d), lambda i: (i, 0))
    vec = pl.BlockSpec((1, d), lambda i: (0, 0))
    return pl.pallas_call(
        _ln_kernel,
        grid=(m // tm,),
        in_specs=[row, row, vec, vec],
        out_specs=[row, row],
        out_shape=[jax.ShapeDtypeStruct((m, d), F32), jax.ShapeDtypeStruct((m, d), BF16)],
        compiler_params=_params(("parallel",)),
        name="residual_layer_norm",
    )(x, y, g.reshape(1, d), b.reshape(1, d))


def _rope_tables(seq):
    half = HEAD_DIM // 2
    inv = 1.0 / (ROPE_THETA ** (jnp.arange(half, dtype=F32) / half))
    ang = jnp.arange(seq).astype(F32)[:, None] * inv[None, :]
    cos, sin = jnp.cos(ang), jnp.sin(ang)
    return jnp.concatenate([cos, cos], axis=1), jnp.concatenate([-sin, sin], axis=1)


_A_KV, _B_KV = A_SLOTS, B_KV_HEADS
_ALIGNED_FIELDS = (('q_a', A_HEADS, 'rope_scale'), ('k_a', _A_KV, 'rope'), ('v_a', _A_KV, 'none'),
                   ('q_b', B_HEADS, 'rope_scale'), ('kc_b', _B_KV, 'rope'), ('vc_b', _B_KV, 'none'),
                   ('ks_b', _B_KV, 'rope'), ('vs_b', _B_KV, 'none'), ('kw_b', _B_KV, 'rope'),
                   ('vw_b', _B_KV, 'none'))
_STICK_FIELDS = (('q_c', C_HEADS, 'scale'), ('k_c', C_HEADS, 'none'), ('v_c', C_HEADS, 'none'))
N_NSA_GATES = 3 * B_HEADS


def _field_layout(fields, heads_per_tile):
    head0, modes, start = {}, [], 0
    for name, heads, mode in fields:
        assert heads % heads_per_tile == 0
        head0[name] = start
        modes += [mode] * (heads // heads_per_tile)
        start += heads
    return head0, tuple(modes), start


def _layer(layer, x, xb, nb, seq, tables, w_rows, cmp_pe_k, cmp_wk1, cmp_wk2, cmp_pe_v, cmp_wv1, cmp_wv2,
           w_br_a, w_br_b, w_br_c, w_out, ln1_g, ln1_b, w_up, w_down, ln2_g, ln2_b):
    m, d_model = x.shape

    tn_att = 4 * HEAD_DIM
    at, att_modes, att_heads = _field_layout(_ALIGNED_FIELDS, tn_att // HEAD_DIM)
    att = _matmul(xb, w_rows, layer=layer, b_rows=True, name="proj_attention", out_dtype=BF16,
                  tile_modes=att_modes, tm=1024, tn=tn_att, head_major=True, rope_tables=tables)
    tn_tail = 1024
    st, stick_modes, stick_heads = _field_layout(_STICK_FIELDS, tn_tail // HEAD_DIM)
    stick_col0 = att_heads * HEAD_DIM
    gates_col0 = stick_col0 + stick_heads * HEAD_DIM
    nsa_col0 = gates_col0 + 3 * d_model
    stick = _matmul(xb, w_rows, layer=layer, b_rows=True, col0=stick_col0, name="proj_stick",
                    out_dtype=BF16, tile_modes=stick_modes, tm=1024, tn=tn_tail, head_major=True)
    gates = _matmul(xb, w_rows, layer=layer, b_rows=True, col0=gates_col0, name="proj_branch_gates",
                    out_dtype=BF16, tile_modes=('sigmoid',) * (3 * d_model // tn_tail), tm=1024, tn=tn_tail)
    g_nsa = _matmul(xb, w_rows, layer=layer, b_rows=True, col0=nsa_col0, name="proj_nsa_gates",
                    out_dtype=F32, tile_modes=('sigmoid',), tm=1024, tn=HEAD_DIM)

    att = att.reshape(att_heads, nb, seq, HEAD_DIM)
    stick = stick.reshape(stick_heads, nb, seq, HEAD_DIM)

    y_a = _dilated_mixer(att, at['q_a'], at['k_a'], at['v_a'])

    kc = _compress(att[at['kc_b']:at['kc_b'] + B_KV_HEADS], cmp_pe_k, cmp_wk1, cmp_wk2, "nsa_compress_k")
    vc = _compress(att[at['vc_b']:at['vc_b'] + B_KV_HEADS], cmp_pe_v, cmp_wv1, cmp_wv2, "nsa_compress_v")
    o_cmp, sel = _cmp_select(att, at['q_b'], kc, vc)
    o_sel = _selected(att, at['q_b'], att, at['ks_b'], att, at['vs_b'], sel)
    o_win = _window(att, at['q_b'], att, at['kw_b'], att, at['vw_b'])
    y_b = _nsa_gate(g_nsa, o_cmp.reshape(B_HEADS, m, HEAD_DIM), o_sel.reshape(B_HEADS, m, HEAD_DIM),
                    o_win.reshape(B_HEADS, m, HEAD_DIM))

    y_c = _stick_breaking(stick, st['q_c'], stick, st['k_c'], stick, st['v_c'])

    merged = _merge(y_a.reshape(m, -1), y_b, y_c.reshape(m, -1), w_br_a.astype(BF16),
                    w_br_b.astype(BF16), w_br_c.astype(BF16), gates, d_model)
    mixed = _matmul(merged, w_out, layer=layer, name="out_proj", out_dtype=F32,
                    tile_modes=('none',) * (d_model // 512), tm=1024, tn=512)
    x1, x1b = _residual_ln(x, mixed, ln1_g, ln1_b)

    d_ff = w_up.shape[2]
    hidden = _matmul(x1b, w_up, layer=layer, name="mlp_up", out_dtype=BF16,
                     tile_modes=('relu2',) * (d_ff // 512), tm=1024, tn=512)
    down = _matmul(hidden, w_down, layer=layer, name="mlp_down", out_dtype=F32,
                   tile_modes=('none',) * (d_model // 1024), tm=1024, tn=1024, tk=2048)
    return _residual_ln(x1, down, ln2_g, ln2_b)


def kernel(x, w_in, cmp_pe_k, cmp_wk1, cmp_wk2, cmp_pe_v, cmp_wv1, cmp_wv2, w_br_a, w_br_b, w_br_c,
           w_out, ln1_g, ln1_b, w_up, w_down, ln2_g, ln2_b):
    nb, seq, d_model = x.shape
    tables = _rope_tables(seq)
    xf = x.reshape(nb * seq, d_model)
    xb = xf.astype(BF16)
    aligned_cols = sum(heads for _, heads, _ in _ALIGNED_FIELDS) * HEAD_DIM
    w_rows = _input_weight_rows(w_in, aligned_cols, N_NSA_GATES)
    for l in range(w_in.shape[0]):
        xf, xb = _layer(l, xf, xb, nb, seq, tables, w_rows, cmp_pe_k[l], cmp_wk1[l], cmp_wk2[l],
                        cmp_pe_v[l], cmp_wv1[l], cmp_wv2[l], w_br_a[l], w_br_b[l], w_br_c[l],
                        w_out, ln1_g[l], ln1_b[l], w_up, w_down, ln2_g[l], ln2_b[l])
    return xf.reshape(nb, seq, d_model)
```

```python
import functools
import math

import numpy as np
import jax
import jax.numpy as jnp
from jax import lax
from jax.experimental import pallas as pl
from jax.experimental.pallas import tpu as pltpu

F32 = jnp.float32
BF16 = jnp.bfloat16

HEAD_DIM = 128
ROPE_THETA = 10000.0
LN_EPS = 1e-5
DEPTH = 2

DIL_GROUPS = ((128, 1), (512, 4), (2048, 16))
A_SLOTS = 4
A_HEADS = A_SLOTS * len(DIL_GROUPS)

B_HEADS = 12
B_KV_HEADS = 4
B_GROUP = B_HEADS // B_KV_HEADS
CMP_LEN = 32
CMP_STRIDE = 16
SEL_LEN = 64
SEL_SHIFT = 6
SEL_TOPK = 16
WIN_LEN = 512

C_HEADS = 8

MASKED = -1e30
ALPHA = (2.0 * DEPTH) ** 0.25
Q_SCALE = HEAD_DIM ** -0.5

VMEM_LIMIT = 48 * 1024 * 1024
MXU_COLS = 256
ATT_TQ = 128
ATT_TILES = 4
DIL_ROWS = 128
DIL_ROWS_DENSE = 512
SEL_TQ = 128
SEL_TILES = 4
SEL_CHUNK = 1024
STICK_TILE = 512
CUM_TILE = 128


def _params(semantics):
    return pltpu.CompilerParams(dimension_semantics=semantics, vmem_limit_bytes=VMEM_LIMIT)


def _dot(a, b):
    return jnp.dot(a, b, preferred_element_type=F32)


def _dot_nt(a, b):
    return lax.dot_general(a, b, (((1,), (1,)), ((), ())), preferred_element_type=F32)


def _split_bf16(x):
    hi = x.astype(BF16)
    lo = (x - hi.astype(F32)).astype(BF16)
    return hi, lo


def _tile_flag(tile_modes, wanted):
    hits = [mode in wanted for mode in tile_modes]
    if all(hits) or not any(hits):
        return hits[0]
    j = pl.program_id(1)
    flag = None
    for c, hit in enumerate(hits):
        if hit:
            flag = (j == c) if flag is None else (flag | (j == c))
    return flag


def _write_columns(acc, o_ref, col0, tile_modes, head_major, cos_ref, sin_ref):
    is_rope = _tile_flag(tile_modes, ('rope', 'rope_scale'))
    is_scaled = _tile_flag(tile_modes, ('rope_scale', 'scale'))
    uniform = tile_modes[0] if len(set(tile_modes)) == 1 else None
    assert uniform is not None or not (set(tile_modes) & {'sigmoid', 'relu2'})

    def epilogue(y):
        if is_rope is not False:
            roped = y * cos_ref[...] + pltpu.roll(y, HEAD_DIM // 2, 1) * sin_ref[...]
            y = roped if is_rope is True else jnp.where(is_rope, roped, y)
        if is_scaled is not False:
            y = y * (Q_SCALE if is_scaled is True else jnp.where(is_scaled, Q_SCALE, 1.0))
        if uniform == 'sigmoid':
            y = jax.nn.sigmoid(y)
        if uniform == 'relu2':
            y = jnp.square(jnp.maximum(y, 0.0))
        return y

    width = acc.shape[1]
    if head_major:
        for c in range(width // HEAD_DIM):
            y = epilogue(acc[:, c * HEAD_DIM:(c + 1) * HEAD_DIM])
            o_ref[col0 // HEAD_DIM + c] = y.astype(o_ref.dtype)
    else:
        o_ref[:, col0:col0 + width] = epilogue(acc).astype(o_ref.dtype)


def _matmul_kernel(*refs, tile_modes, head_major, n_k, use_rope, acc_in_out, b_rows):
    refs = list(refs)
    a_ref, b_ref = refs[0], refs[1]
    cos_ref, sin_ref = (refs[2], refs[3]) if use_rope else (None, None)
    o_ref = refs[4] if use_rope else refs[2]
    acc_ref = o_ref if acc_in_out else (refs[-1] if n_k > 1 else None)
    tn = b_ref.shape[0] if b_rows else b_ref.shape[1]

    def product(col0, width):
        if b_rows:
            return _dot_nt(a_ref[...], b_ref[col0:col0 + width, :].astype(BF16))
        return _dot(a_ref[...], b_ref[:, col0:col0 + width].astype(BF16))

    if n_k == 1:
        width = min(MXU_COLS, tn)
        for col0 in range(0, tn, width):
            _write_columns(product(col0, width), o_ref, col0, tile_modes, head_major, cos_ref, sin_ref)
        return

    @pl.when(pl.program_id(2) == 0)
    def _():
        acc_ref[...] = jnp.zeros(acc_ref.shape, F32)

    acc_ref[...] += product(0, tn)
    if not acc_in_out:
        pl.when(pl.program_id(2) == n_k - 1)(
            lambda: _write_columns(acc_ref[...], o_ref, 0, tile_modes, head_major, cos_ref, sin_ref))


def _matmul(a, b, *, name, out_dtype, tile_modes, tm, tn, tk=None, layer=None, col0=0,
            head_major=False, rope_tables=None, b_rows=False):
    m, kdim = a.shape
    n = tn * len(tile_modes)
    tm = min(tm, m)
    tk = kdim if tk is None else min(tk, kdim)
    assert m % tm == 0 and kdim % tk == 0 and col0 % tn == 0
    n_k = kdim // tk
    col_tile0 = col0 // tn
    use_rope = any(mode.startswith('rope') for mode in tile_modes)
    acc_in_out = n_k > 1 and out_dtype == F32 and set(tile_modes) == {'none'} and not head_major
    if b_rows:
        assert layer is not None
        b_spec = pl.BlockSpec((None, tn, tk), lambda i, j, k: (layer, col_tile0 + j, k))
    elif layer is None:
        b_spec = pl.BlockSpec((tk, tn), lambda i, j, k: (k, col_tile0 + j))
    else:
        b_spec = pl.BlockSpec((None, tk, tn), lambda i, j, k: (layer, k, col_tile0 + j))
    in_specs = [pl.BlockSpec((tm, tk), lambda i, j, k: (i, k)), b_spec]
    operands = [a, b]
    if use_rope:
        assert head_major
        cos, sin = rope_tables
        seq_tiles = cos.shape[0] // tm
        assert cos.shape[0] % tm == 0
        spec = pl.BlockSpec((tm, HEAD_DIM), lambda i, j, k: (i % seq_tiles, 0))
        in_specs += [spec, spec]
        operands += [cos, sin]
    if head_major:
        out_shape = jax.ShapeDtypeStruct((n // HEAD_DIM, m, HEAD_DIM), out_dtype)
        out_spec = pl.BlockSpec((tn // HEAD_DIM, tm, HEAD_DIM), lambda i, j, k: (j, i, 0))
    else:
        out_shape = jax.ShapeDtypeStruct((m, n), out_dtype)
        out_spec = pl.BlockSpec((tm, tn), lambda i, j, k: (i, j))
    scratch = [pltpu.VMEM((tm, tn), F32)] if (n_k > 1 and not acc_in_out) else []
    return pl.pallas_call(
        functools.partial(_matmul_kernel, tile_modes=tuple(tile_modes), head_major=head_major, n_k=n_k,
                          use_rope=use_rope, acc_in_out=acc_in_out, b_rows=b_rows),
        grid=(m // tm, len(tile_modes), n_k),
        in_specs=in_specs,
        out_specs=out_spec,
        out_shape=out_shape,
        scratch_shapes=scratch,
        compiler_params=_params(("parallel", "parallel", "arbitrary")),
        name=name,
    )(*operands)


def _gather_rows_kernel(w_ref, o_ref, *, tn, k_chunks, n_layers):
    stride = k_chunks * n_layers
    for layer in range(n_layers):
        pieces = [w_ref[pl.ds(c * n_layers + layer, tn, stride=stride), :] for c in range(k_chunks)]
        o_ref[layer] = jnp.concatenate(pieces, axis=1).astype(o_ref.dtype)


def _input_weight_rows(w_in, aligned_cols, n_gates, tn=HEAD_DIM):
    n_layers, kdim, n = w_in.shape
    k_chunks = kdim // HEAD_DIM
    tail = n - aligned_cols - n_gates
    assert aligned_cols % tn == 0 and tail % tn == 0 and kdim % HEAD_DIM == 0
    aligned_tiles, tail_tiles = aligned_cols // tn, tail // tn
    rows_per_col = k_chunks * n_layers
    flat = w_in.reshape(n_layers, k_chunks, HEAD_DIM, n).transpose(3, 1, 0, 2).reshape(n * rows_per_col, HEAD_DIM)

    def source_row(j):
        col = jnp.where(j < aligned_tiles, tn * j,
                        jnp.where(j < aligned_tiles + tail_tiles,
                                  aligned_cols + n_gates + tn * (j - aligned_tiles), aligned_cols))
        return (col * rows_per_col, 0)

    tiles = aligned_tiles + tail_tiles + 1
    return pl.pallas_call(
        functools.partial(_gather_rows_kernel, tn=tn, k_chunks=k_chunks, n_layers=n_layers),
        grid=(tiles,),
        in_specs=[pl.BlockSpec((pl.Element(tn * rows_per_col), pl.Element(HEAD_DIM)), source_row)],
        out_specs=pl.BlockSpec((n_layers, tn, kdim), lambda j: (0, j, 0)),
        out_shape=jax.ShapeDtypeStruct((n_layers, tiles * tn, kdim), BF16),
        compiler_params=_params(("parallel",)),
        name="input_weight_rows",
    )(flat)


def _window_start(t0, back, span, seq, align):
    start = jnp.minimum(jnp.maximum(t0 - back, 0), seq - span)
    return pl.multiple_of(start, align)


def _dilated_group_kernel(q_ref, k_ref, v_ref, o_ref, lse_ref, *, dil, rows, band, n_rows):
    l0 = pl.program_id(2) * rows
    span = min(band + rows, n_rows)
    start = _window_start(l0, band, span, n_rows, math.gcd(rows, band))
    dist = (l0 + lax.broadcasted_iota(jnp.int32, (rows, 1), 0)) - (start + lax.broadcasted_iota(jnp.int32, (1, span), 1))
    keep = (dist >= 0) & (dist <= band)
    for r in range(dil):
        lanes = slice(r * HEAD_DIM, (r + 1) * HEAD_DIM)
        s = jnp.where(keep, _dot_nt(q_ref[:, lanes], k_ref[pl.ds(start, span), lanes]), -jnp.inf)
        m = jnp.max(s, axis=1, keepdims=True)
        p = jnp.exp(s - m)
        l = jnp.sum(p, axis=1, keepdims=True)
        o = _dot(p.astype(BF16), v_ref[pl.ds(start, span), lanes]) / l
        lse = jnp.broadcast_to(m + jnp.log(l), (rows, HEAD_DIM))
        if dil == 1:
            o_ref[...] = o
            lse_ref[...] = lse
        else:
            o_ref[pl.ds(r, rows, stride=dil), :] = o
            lse_ref[pl.ds(r, rows, stride=dil), :] = lse


def _head_spec(rows, head0):
    return pl.BlockSpec((None, None, rows, HEAD_DIM), lambda b, h, n: (head0 + h, b, n, 0))


def _seq_spec(seq, head0):
    return pl.BlockSpec((None, None, seq, HEAD_DIM), lambda b, h, n: (head0 + h, b, 0, 0))


def _dilated_group(q, q_head0, k, k_head0, v, v_head0, window, dil):
    _, nb, n_rows, width = q.shape
    seq = n_rows * dil
    band = window // dil
    rows = min(DIL_ROWS if dil > 1 else DIL_ROWS_DENSE, n_rows)
    assert width == dil * HEAD_DIM and n_rows % rows == 0
    view_spec = pl.BlockSpec((None, None, rows, width), lambda b, s, j: (q_head0 + s, b, j, 0))
    out_spec = pl.BlockSpec((None, None, rows * dil, HEAD_DIM), lambda b, s, j: (s, b, j, 0))
    out_shape = jax.ShapeDtypeStruct((A_SLOTS, nb, seq, HEAD_DIM), F32)
    return pl.pallas_call(
        functools.partial(_dilated_group_kernel, dil=dil, rows=rows, band=band, n_rows=n_rows),
        grid=(nb, A_SLOTS, n_rows // rows),
        in_specs=[view_spec,
                  pl.BlockSpec((None, None, n_rows, width), lambda b, s, j: (k_head0 + s, b, 0, 0)),
                  pl.BlockSpec((None, None, n_rows, width), lambda b, s, j: (v_head0 + s, b, 0, 0))],
        out_specs=[out_spec, out_spec],
        out_shape=[out_shape, out_shape],
        compiler_params=_params(("parallel", "parallel", "arbitrary")),
        name="dilated_group_%d" % dil,
    )(q, k, v)


def _dilated_merge_kernel(o0_ref, o1_ref, o2_ref, l0_ref, l1_ref, l2_ref, y_ref):
    for s in range(A_SLOTS):
        lses = (l0_ref[s], l1_ref[s], l2_ref[s])
        top = jnp.maximum(jnp.maximum(lses[0], lses[1]), lses[2])
        ws = [jnp.exp(lse - top) for lse in lses]
        y = (ws[0] * o0_ref[s] + ws[1] * o1_ref[s] + ws[2] * o2_ref[s]) / (ws[0] + ws[1] + ws[2])
        y_ref[:, s * HEAD_DIM:(s + 1) * HEAD_DIM] = y.astype(y_ref.dtype)


def _dilated_mixer(att, q_head0, k_head0, v_head0):
    _, nb, seq, _ = att.shape
    outs, lses = [], []
    for g, (window, dil) in enumerate(DIL_GROUPS):
        qh = q_head0 + g * A_SLOTS
        if dil == 1:
            o, lse = _dilated_group(att, qh, att, k_head0, att, v_head0, window, dil)
        else:
            lo = min(q_head0, k_head0, v_head0)
            hi = max(q_head0 + A_HEADS, k_head0 + A_SLOTS, v_head0 + A_SLOTS)
            view = att[lo:hi].reshape(hi - lo, nb, seq // dil, dil * HEAD_DIM)
            o, lse = _dilated_group(view, qh - lo, view, k_head0 - lo, view, v_head0 - lo, window, dil)
        outs.append(o)
        lses.append(lse)
    tq = min(256, seq)
    spec = pl.BlockSpec((A_SLOTS, None, tq, HEAD_DIM), lambda b, n: (0, b, n, 0))
    return pl.pallas_call(
        _dilated_merge_kernel,
        grid=(nb, seq // tq),
        in_specs=[spec] * 6,
        out_specs=pl.BlockSpec((None, tq, A_SLOTS * HEAD_DIM), lambda b, n: (b, n, 0)),
        out_shape=jax.ShapeDtypeStruct((nb, seq, A_SLOTS * HEAD_DIM), BF16),
        compiler_params=_params(("parallel", "parallel")),
        name="dilated_merge",
    )(*outs, *lses)


def _gelu_tanh(x):
    return 0.5 * x * (1.0 + jnp.tanh(math.sqrt(2.0 / math.pi) * (x + 0.044715 * (x * x * x))))


def _compress_kernel(x_ref, pe_ref, w1_ref, w2_ref, o_ref):
    x = x_ref[...].astype(F32)
    first = _dot((x + pe_ref[0:1, :]).astype(BF16), w1_ref[0])
    second = _dot((x + pe_ref[1:2, :]).astype(BF16), w1_ref[1])
    chunks = x.shape[0]
    hidden = first + pltpu.roll(second, chunks - 1, 0)
    o_ref[...] = _dot(_gelu_tanh(hidden).astype(BF16), w2_ref[...]).astype(o_ref.dtype)


def _compress(t, pe, w1, w2, name):
    nh, nb, seq, _ = t.shape
    chunks = seq // CMP_STRIDE
    width = CMP_STRIDE * HEAD_DIM
    ratio = CMP_LEN // CMP_STRIDE
    x = t.reshape(nh, nb, chunks, width)
    return pl.pallas_call(
        _compress_kernel,
        grid=(nh, nb),
        in_specs=[pl.BlockSpec((None, None, chunks, width), lambda h, b: (h, b, 0, 0)),
                  pl.BlockSpec((ratio, width), lambda h, b: (0, 0)),
                  pl.BlockSpec((ratio, width, HEAD_DIM), lambda h, b: (0, 0, 0)),
                  pl.BlockSpec((HEAD_DIM, HEAD_DIM), lambda h, b: (0, 0))],
        out_specs=pl.BlockSpec((None, None, chunks, HEAD_DIM), lambda h, b: (h, b, 0, 0)),
        out_shape=jax.ShapeDtypeStruct((nh, nb, chunks, HEAD_DIM), BF16),
        compiler_params=_params(("parallel", "parallel")),
        name=name,
    )(x, pe.reshape(ratio, width), w1.reshape(ratio, width, HEAD_DIM).astype(BF16), w2.astype(BF16))


def _group_q(q_refs, row0=0, rows=None):
    rows = q_refs[0].shape[0] if rows is None else rows
    return jnp.concatenate([r[row0:row0 + rows, :] for r in q_refs], axis=0)


def _group_q_specs(tq, head0):
    return [pl.BlockSpec((None, None, tq, HEAD_DIM),
                         functools.partial(lambda b, h, n, g: (head0 + h * B_GROUP + g, b, n, 0), g=g))
            for g in range(B_GROUP)]


def _cmp_select_kernel(q0_ref, q1_ref, q2_ref, kc_ref, vc_ref, ov_ref, o_ref, sel_ref, *, tq, tiles):
    for u in range(tiles):
        _cmp_select_tile((q0_ref, q1_ref, q2_ref), kc_ref, vc_ref, ov_ref, o_ref, sel_ref,
                         (pl.program_id(2) * tiles + u) * tq, u * tq, tq)


def _cmp_select_tile(q_refs, kc_ref, vc_ref, ov_ref, o_ref, sel_ref, t0, row0, tq):
    n_cmp = kc_ref.shape[0]
    rows = B_GROUP * tq
    s = _dot_nt(_group_q(q_refs, row0, tq), kc_ref[...])
    tpos3 = t0 + (lax.broadcasted_iota(jnp.int32, (rows, 1), 0) & (tq - 1))
    c_end = lax.broadcasted_iota(jnp.int32, (1, n_cmp), 1) * CMP_STRIDE + (CMP_LEN - 1)
    s = jnp.where(c_end - tpos3 <= 0, s, -jnp.inf)
    m = jnp.max(s, axis=1, keepdims=True)
    m = jnp.where(jnp.abs(m) < jnp.inf, m, 0.0)
    e = jnp.exp(s - m)
    den = jnp.sum(e, axis=1, keepdims=True)
    p = e / jnp.where(den > 0, den, 1.0)
    o = _dot(p.astype(BF16), vc_ref[...]).reshape(B_GROUP, tq, HEAD_DIM)
    o_ref[:, row0:row0 + tq, :] = o.astype(o_ref.dtype)

    p_sum = p[0:tq] + p[tq:2 * tq] + p[2 * tq:3 * tq]
    hi, lo = _split_bf16(p_sum)
    imp = _dot(hi, ov_ref[...]) + _dot(lo, ov_ref[...])
    tpos = t0 + lax.broadcasted_iota(jnp.int32, (tq, 1), 0)
    rel = lax.broadcasted_iota(jnp.int32, (1, HEAD_DIM), 1) - (tpos >> SEL_SHIFT)
    j_abs = jnp.broadcast_to(lax.broadcasted_iota(jnp.int32, (1, HEAD_DIM), 1), rel.shape)
    forced = (j_abs == 0) | (rel == 0) | (rel == -1)
    imp = jnp.where(forced, jnp.inf, jnp.where(rel <= 0, imp, -jnp.inf))

    n_sel = HEAD_DIM // 2
    imp_t = imp.T
    mine = imp_t[0:n_sel]
    j_idx = lax.broadcasted_iota(jnp.int32, (n_sel, tq), 0)
    beaten = jnp.zeros((n_sel, tq), F32)
    for kk in range(n_sel):
        other = imp_t[kk:kk + 1, :]
        wins = (other > mine) | ((other == mine) & (j_idx > kk))
        beaten = beaten + jnp.where(wins, 1.0, 0.0)
    chosen = jnp.where(beaten < SEL_TOPK, 0.0, MASKED)
    chosen = jnp.concatenate([chosen, jnp.full((HEAD_DIM - n_sel, tq), MASKED, F32)], axis=0)
    sel_ref[:, row0:row0 + tq] = chosen.astype(sel_ref.dtype)


def _overlap_matrix(n_cmp_rows, n_sel):
    c_start = np.arange(n_cmp_rows) * CMP_STRIDE
    c_end = c_start + CMP_LEN - 1
    s_start = np.arange(HEAD_DIM) * SEL_LEN
    ov = (c_start[:, None] <= s_start[None, :] + SEL_LEN - 1) & (c_end[:, None] >= s_start[None, :])
    ov &= (np.arange(HEAD_DIM) < n_sel)[None, :]
    return jnp.asarray(ov.astype(np.float32), dtype=BF16)


def _group_o_spec(tq):
    return pl.BlockSpec((B_GROUP, None, tq, HEAD_DIM), lambda b, h, n: (h, b, n, 0))


def _sel_spec(tq):
    return pl.BlockSpec((None, None, HEAD_DIM, tq), lambda b, h, n: (h, b, 0, n))


def _cmp_select(q, q_head0, kc, vc):
    _, nb, seq, _ = q.shape
    assert seq // SEL_LEN <= HEAD_DIM // 2
    tq = min(ATT_TQ, seq)
    tiles = min(ATT_TILES, seq // tq)
    n_cmp = kc.shape[2]
    kv_spec = pl.BlockSpec((None, None, n_cmp, HEAD_DIM), lambda b, h, n: (h, b, 0, 0))
    return pl.pallas_call(
        functools.partial(_cmp_select_kernel, tq=tq, tiles=tiles),
        grid=(nb, B_KV_HEADS, seq // (tq * tiles)),
        in_specs=_group_q_specs(tq * tiles, q_head0) + [kv_spec, kv_spec,
                                                        pl.BlockSpec((n_cmp, HEAD_DIM), lambda b, h, n: (0, 0))],
        out_specs=[_group_o_spec(tq * tiles), _sel_spec(tq * tiles)],
        out_shape=[jax.ShapeDtypeStruct((B_HEADS, nb, seq, HEAD_DIM), BF16),
                   jax.ShapeDtypeStruct((B_KV_HEADS, nb, HEAD_DIM, seq), BF16)],
        compiler_params=_params(("parallel", "parallel", "arbitrary")),
        name="nsa_compressed_select",
    )(q, q, q, kc, vc, _overlap_matrix(n_cmp, seq // SEL_LEN))


def _selected_kernel(q0_ref, q1_ref, q2_ref, k_ref, v_ref, sel_ref, o_ref, vt_ref, acc_ref, *, tq, kc, tiles):
    n = pl.program_id(2)
    t_first = n * (tq * tiles)
    cols = B_GROUP * tq

    @pl.when(n == 0)
    def _():
        for c in range(v_ref.shape[0] // kc):
            vt_ref[c] = v_ref[c * kc:(c + 1) * kc, :].astype(F32).T.astype(BF16)

    qs = [_group_q((q0_ref, q1_ref, q2_ref), u * tq, tq) for u in range(tiles)]
    qposs = [t_first + u * tq + lax.broadcasted_iota(jnp.int32, (1, tq), 1) for u in range(tiles)]
    key_in_chunk = lax.broadcasted_iota(jnp.int32, (kc, 1), 0)
    blocks = kc // SEL_LEN
    acc_ref[...] = jnp.zeros(acc_ref.shape, F32)

    def chunk(i, carry, diagonal):
        k0 = pl.multiple_of(i * kc, kc)
        kpos = k0 + key_in_chunk
        keys = k_ref[pl.ds(k0, kc), :]
        blk0 = pl.multiple_of(i * blocks, blocks)
        out = []
        for u in range(tiles):
            m_old, l_old = carry[u]
            rows = sel_ref[pl.ds(blk0, blocks), u * tq:(u + 1) * tq].astype(F32)
            bias = jnp.broadcast_to(rows[:, None, :], (blocks, SEL_LEN, tq)).reshape(kc, tq)
            if diagonal:
                bias = jnp.where(kpos - qposs[u] <= 0, bias, MASKED)
            s = _dot_nt(keys, qs[u]) + jnp.concatenate([bias] * B_GROUP, axis=1)
            m_new = jnp.maximum(m_old, jnp.max(s, axis=0, keepdims=True))
            p = jnp.exp(s - m_new)
            alpha = jnp.exp(m_old - m_new)
            acc_ref[u] = alpha * acc_ref[u] + _dot(vt_ref[i], p.astype(BF16))
            out.append((m_new, alpha * l_old + jnp.sum(p, axis=0, keepdims=True)))
        return tuple(out)

    init = tuple((jnp.full((1, cols), MASKED, F32), jnp.zeros((1, cols), F32)) for _ in range(tiles))
    last = t_first // kc
    carry = lax.fori_loop(0, last, lambda i, c: chunk(i, c, False), init)
    final = chunk(last, carry, True)
    for u in range(tiles):
        o_t = acc_ref[u] / final[u][1]
        for g in range(B_GROUP):
            o_ref[g, u * tq:(u + 1) * tq, :] = o_t[:, g * tq:(g + 1) * tq].T.astype(o_ref.dtype)


def _selected(q, q_head0, k, k_head0, v, v_head0, sel):
    _, nb, seq, _ = q.shape
    tq = min(SEL_TQ, seq)
    kc = min(SEL_CHUNK, seq)
    tiles = min(SEL_TILES, kc // tq)
    assert kc % (tq * tiles) == 0 and seq % kc == 0 and tq == HEAD_DIM
    return pl.pallas_call(
        functools.partial(_selected_kernel, tq=tq, kc=kc, tiles=tiles),
        grid=(nb, B_KV_HEADS, seq // (tq * tiles)),
        in_specs=_group_q_specs(tq * tiles, q_head0) + [_seq_spec(seq, k_head0), _seq_spec(seq, v_head0),
                                                        _sel_spec(tq * tiles)],
        out_specs=_group_o_spec(tq * tiles),
        out_shape=jax.ShapeDtypeStruct((B_HEADS, nb, seq, HEAD_DIM), BF16),
        scratch_shapes=[pltpu.VMEM((seq // kc, HEAD_DIM, kc), BF16),
                        pltpu.VMEM((tiles, HEAD_DIM, B_GROUP * tq), F32)],
        compiler_params=_params(("parallel", "parallel", "arbitrary")),
        name="nsa_selected",
    )(q, q, q, k, v, sel)


def _window_kernel(q0_ref, q1_ref, q2_ref, k_ref, v_ref, o_ref, *, tq, tiles, seq):
    rows = B_GROUP * tq
    span = min(WIN_LEN + tq, seq)
    for u in range(tiles):
        t0 = (pl.program_id(2) * tiles + u) * tq
        start = _window_start(t0, WIN_LEN, span, seq, tq)
        s = _dot_nt(_group_q((q0_ref, q1_ref, q2_ref), u * tq, tq), k_ref[pl.ds(start, span), :])
        qpos = t0 + (lax.broadcasted_iota(jnp.int32, (rows, 1), 0) & (tq - 1))
        kpos = start + lax.broadcasted_iota(jnp.int32, (1, span), 1)
        dist = qpos - kpos
        s = jnp.where((dist >= 0) & (dist <= WIN_LEN - 1), s, -jnp.inf)
        m = jnp.max(s, axis=1, keepdims=True)
        p = jnp.exp(s - m)
        l = jnp.sum(p, axis=1, keepdims=True)
        o = _dot(p.astype(BF16), v_ref[pl.ds(start, span), :]) / l
        o_ref[:, u * tq:(u + 1) * tq, :] = o.reshape(B_GROUP, tq, HEAD_DIM).astype(o_ref.dtype)


def _window(q, q_head0, k, k_head0, v, v_head0):
    _, nb, seq, _ = q.shape
    tq = min(ATT_TQ, seq)
    tiles = min(ATT_TILES, seq // tq)
    assert WIN_LEN % tq == 0
    return pl.pallas_call(
        functools.partial(_window_kernel, tq=tq, tiles=tiles, seq=seq),
        grid=(nb, B_KV_HEADS, seq // (tq * tiles)),
        in_specs=_group_q_specs(tq * tiles, q_head0) + [_seq_spec(seq, k_head0), _seq_spec(seq, v_head0)],
        out_specs=_group_o_spec(tq * tiles),
        out_shape=jax.ShapeDtypeStruct((B_HEADS, nb, seq, HEAD_DIM), BF16),
        compiler_params=_params(("parallel", "parallel", "arbitrary")),
        name="nsa_window",
    )(q, q, q, k, v)


def _nsa_gate_kernel(g_ref, oc_ref, os_ref, ow_ref, y_ref):
    g = g_ref[...]
    for h in range(B_HEADS):
        y = (g[:, 3 * h:3 * h + 1] * oc_ref[h].astype(F32)
             + g[:, 3 * h + 1:3 * h + 2] * os_ref[h].astype(F32)
             + g[:, 3 * h + 2:3 * h + 3] * ow_ref[h].astype(F32))
        y_ref[:, h * HEAD_DIM:(h + 1) * HEAD_DIM] = y.astype(y_ref.dtype)


def _nsa_gate(gates, o_cmp, o_sel, o_win):
    rows = gates.shape[0]
    tm = min(256, rows)
    o_spec = pl.BlockSpec((B_HEADS, tm, HEAD_DIM), lambda i: (0, i, 0))
    return pl.pallas_call(
        _nsa_gate_kernel,
        grid=(rows // tm,),
        in_specs=[pl.BlockSpec((tm, HEAD_DIM), lambda i: (i, 0)), o_spec, o_spec, o_spec],
        out_specs=pl.BlockSpec((tm, B_HEADS * HEAD_DIM), lambda i: (i, 0)),
        out_shape=jax.ShapeDtypeStruct((rows, B_HEADS * HEAD_DIM), BF16),
        compiler_params=_params(("parallel",)),
        name="nsa_gate",
    )(gates, o_cmp, o_sel, o_win)


def _stick_kernel(q_ref, k_ref, v_ref, o_ref, acc_ref, car_ref, *, tq):
    n = pl.program_id(2)
    q = q_ref[...]
    sub = min(CUM_TILE, tq)
    n_sub = tq // sub
    r_idx = lax.broadcasted_iota(jnp.int32, (2 * sub, 2 * sub), 0) & (sub - 1)
    c_idx = lax.broadcasted_iota(jnp.int32, (2 * sub, 2 * sub), 1)
    suffix_and_total = jnp.where((c_idx >= sub) | (r_idx > c_idx), 1.0, 0.0).astype(BF16)
    q_in_tile = lax.broadcasted_iota(jnp.int32, (tq, 1), 0)
    k_in_sub = lax.broadcasted_iota(jnp.int32, (1, sub), 1)
    acc_ref[...] = jnp.zeros(acc_ref.shape, F32)
    car_ref[...] = jnp.zeros(car_ref.shape, F32)

    def tile(k0, diagonal):
        z = _dot_nt(q, k_ref[pl.ds(k0, tq), :])
        soft = jnp.log(1.0 + jnp.exp(-jnp.abs(z)))
        log_beta = jnp.minimum(z, 0.0) - soft
        log_rest = log_beta - z
        carried = car_ref[...]
        parts = [None] * n_sub
        for u in reversed(range(n_sub)):
            cols = slice(u * sub, (u + 1) * sub)
            rest_u = log_rest[:, cols]
            if diagonal:
                before = (u * sub + k_in_sub) - q_in_tile < 0
                rest_u = jnp.where(before, rest_u, 0.0)
            sums = _dot(jnp.concatenate(_split_bf16(rest_u), axis=1), suffix_and_total)
            a = jnp.exp(log_beta[:, cols] + (sums[:, :sub] + carried))
            if diagonal:
                a = jnp.where(before, a, 0.0)
            carried = carried + sums[:, sub:]
            parts[u] = a.astype(BF16)
        car_ref[...] = carried
        acc_ref[...] += _dot(jnp.concatenate(parts, axis=1), v_ref[pl.ds(k0, tq), :])

    tile(pl.multiple_of(n * tq, tq), True)

    def step(i, carry):
        tile(pl.multiple_of((n - 1 - i) * tq, tq), False)
        return carry

    lax.fori_loop(0, n, step, 0)
    o_ref[...] = acc_ref[...].astype(o_ref.dtype)


def _stick_breaking(q, q_head0, k, k_head0, v, v_head0):
    _, nb, seq, _ = q.shape
    tq = min(STICK_TILE, seq)
    return pl.pallas_call(
        functools.partial(_stick_kernel, tq=tq),
        grid=(nb, C_HEADS, seq // tq),
        in_specs=[_head_spec(tq, q_head0), _seq_spec(seq, k_head0), _seq_spec(seq, v_head0)],
        out_specs=pl.BlockSpec((None, tq, HEAD_DIM), lambda b, h, n: (b, n, h)),
        out_shape=jax.ShapeDtypeStruct((nb, seq, C_HEADS * HEAD_DIM), BF16),
        scratch_shapes=[pltpu.VMEM((tq, HEAD_DIM), F32), pltpu.VMEM((tq, min(CUM_TILE, tq)), F32)],
        compiler_params=_params(("parallel", "parallel", "arbitrary")),
        name="stick_breaking",
    )(q, k, v)


def _merge_kernel(ya_ref, yb_ref, yc_ref, wa_ref, wb_ref, wc_ref, ga_ref, gb_ref, gc_ref, o_ref):
    mixed = (ga_ref[...].astype(F32) * _dot(ya_ref[...], wa_ref[...])
             + gb_ref[...].astype(F32) * _dot(yb_ref[...], wb_ref[...])
             + gc_ref[...].astype(F32) * _dot(yc_ref[...], wc_ref[...]))
    o_ref[...] = mixed.astype(o_ref.dtype)


def _merge(ya, yb, yc, wa, wb, wc, gates, d_model):
    m = ya.shape[0]
    tm, tn = min(512, m), 1024
    col_tiles = d_model // tn

    def y_spec(y):
        return pl.BlockSpec((tm, y.shape[1]), lambda i, j: (i, 0))

    def w_spec(w):
        return pl.BlockSpec((w.shape[0], tn), lambda i, j: (0, j))

    def g_spec(branch):
        return pl.BlockSpec((tm, tn), lambda i, j: (i, branch * col_tiles + j))

    return pl.pallas_call(
        _merge_kernel,
        grid=(m // tm, col_tiles),
        in_specs=[y_spec(ya), y_spec(yb), y_spec(yc), w_spec(wa), w_spec(wb), w_spec(wc),
                  g_spec(0), g_spec(1), g_spec(2)],
        out_specs=pl.BlockSpec((tm, tn), lambda i, j: (i, j)),
        out_shape=jax.ShapeDtypeStruct((m, d_model), BF16),
        compiler_params=_params(("parallel", "arbitrary")),
        name="branch_merge",
    )(ya, yb, yc, wa, wb, wc, gates, gates, gates)


def _ln_kernel(x_ref, y_ref, g_ref, b_ref, o_ref, ob_ref):
    z = ALPHA * x_ref[...] + y_ref[...]
    mu = jnp.mean(z, axis=1, keepdims=True)
    zc = z - mu
    var = jnp.mean(zc * zc, axis=1, keepdims=True)
    out = zc * lax.rsqrt(var + LN_EPS) * g_ref[...] + b_ref[...]
    o_ref[...] = out
    ob_ref[...] = out.astype(BF16)


def _residual_ln(x, y, g, b):
    m, d = x.shape
    tm = min(128, m)
    row = pl.BlockSpec((tm, d), lambda i: (i, 0))
    vec = pl.BlockSpec((1, d), lambda i: (0, 0))
    return pl.pallas_call(
        _ln_kernel,
        grid=(m // tm,),
        in_specs=[row, row, vec, vec],
        out_specs=[row, row],
        out_shape=[jax.ShapeDtypeStruct((m, d), F32), jax.ShapeDtypeStruct((m, d), BF16)],
        compiler_params=_params(("parallel",)),
        name="residual_layer_norm",
    )(x, y, g.reshape(1, d), b.reshape(1, d))


def _rope_tables(seq):
    half = HEAD_DIM // 2
    inv = 1.0 / (ROPE_THETA ** (jnp.arange(half, dtype=F32) / half))
    ang = jnp.arange(seq).astype(F32)[:, None] * inv[None, :]
    cos, sin = jnp.cos(ang), jnp.sin(ang)
    return jnp.concatenate([cos, cos], axis=1), jnp.concatenate([-sin, sin], axis=1)


_A_KV, _B_KV = A_SLOTS, B_KV_HEADS
_ALIGNED_FIELDS = (('q_a', A_HEADS, 'rope_scale'), ('k_a', _A_KV, 'rope'), ('v_a', _A_KV, 'none'),
                   ('q_b', B_HEADS, 'rope_scale'), ('kc_b', _B_KV, 'rope'), ('vc_b', _B_KV, 'none'),
                   ('ks_b', _B_KV, 'rope'), ('vs_b', _B_KV, 'none'), ('kw_b', _B_KV, 'rope'),
                   ('vw_b', _B_KV, 'none'))
_STICK_FIELDS = (('q_c', C_HEADS, 'scale'), ('k_c', C_HEADS, 'none'), ('v_c', C_HEADS, 'none'))
N_NSA_GATES = 3 * B_HEADS


def _field_layout(fields, heads_per_tile):
    head0, modes, start = {}, [], 0
    for name, heads, mode in fields:
        assert heads % heads_per_tile == 0
        head0[name] = start
        modes += [mode] * (heads // heads_per_tile)
        start += heads
    return head0, tuple(modes), start


def _layer(layer, x, xb, nb, seq, tables, w_rows, cmp_pe_k, cmp_wk1, cmp_wk2, cmp_pe_v, cmp_wv1, cmp_wv2,
           w_br_a, w_br_b, w_br_c, w_out, ln1_g, ln1_b, w_up, w_down, ln2_g, ln2_b):
    m, d_model = x.shape

    tn_att = 4 * HEAD_DIM
    at, att_modes, att_heads = _field_layout(_ALIGNED_FIELDS, tn_att // HEAD_DIM)
    att = _matmul(xb, w_rows, layer=layer, b_rows=True, name="proj_attention", out_dtype=BF16,
                  tile_modes=att_modes, tm=1024, tn=tn_att, head_major=True, rope_tables=tables)
    tn_tail = 1024
    st, stick_modes, stick_heads = _field_layout(_STICK_FIELDS, tn_tail // HEAD_DIM)
    stick_col0 = att_heads * HEAD_DIM
    gates_col0 = stick_col0 + stick_heads * HEAD_DIM
    nsa_col0 = gates_col0 + 3 * d_model
    stick = _matmul(xb, w_rows, layer=layer, b_rows=True, col0=stick_col0, name="proj_stick",
                    out_dtype=BF16, tile_modes=stick_modes, tm=1024, tn=tn_tail, head_major=True)
    gates = _matmul(xb, w_rows, layer=layer, b_rows=True, col0=gates_col0, name="proj_branch_gates",
                    out_dtype=BF16, tile_modes=('sigmoid',) * (3 * d_model // tn_tail), tm=1024, tn=tn_tail)
    g_nsa = _matmul(xb, w_rows, layer=layer, b_rows=True, col0=nsa_col0, name="proj_nsa_gates",
                    out_dtype=F32, tile_modes=('sigmoid',), tm=1024, tn=HEAD_DIM)

    att = att.reshape(att_heads, nb, seq, HEAD_DIM)
    stick = stick.reshape(stick_heads, nb, seq, HEAD_DIM)

    y_a = _dilated_mixer(att, at['q_a'], at['k_a'], at['v_a'])

    kc = _compress(att[at['kc_b']:at['kc_b'] + B_KV_HEADS], cmp_pe_k, cmp_wk1, cmp_wk2, "nsa_compress_k")
    vc = _compress(att[at['vc_b']:at['vc_b'] + B_KV_HEADS], cmp_pe_v, cmp_wv1, cmp_wv2, "nsa_compress_v")
    o_cmp, sel = _cmp_select(att, at['q_b'], kc, vc)
    o_sel = _selected(att, at['q_b'], att, at['ks_b'], att, at['vs_b'], sel)
    o_win = _window(att, at['q_b'], att, at['kw_b'], att, at['vw_b'])
    y_b = _nsa_gate(g_nsa, o_cmp.reshape(B_HEADS, m, HEAD_DIM), o_sel.reshape(B_HEADS, m, HEAD_DIM),
                    o_win.reshape(B_HEADS, m, HEAD_DIM))

    y_c = _stick_breaking(stick, st['q_c'], stick, st['k_c'], stick, st['v_c'])

    merged = _merge(y_a.reshape(m, -1), y_b, y_c.reshape(m, -1), w_br_a.astype(BF16),
                    w_br_b.astype(BF16), w_br_c.astype(BF16), gates, d_model)
    mixed = _matmul(merged, w_out, layer=layer, name="out_proj", out_dtype=F32,
                    tile_modes=('none',) * (d_model // 512), tm=1024, tn=512)
    x1, x1b = _residual_ln(x, mixed, ln1_g, ln1_b)

    d_ff = w_up.shape[2]
    hidden = _matmul(x1b, w_up, layer=layer, name="mlp_up", out_dtype=BF16,
                     tile_modes=('relu2',) * (d_ff // 512), tm=1024, tn=512)
    down = _matmul(hidden, w_down, layer=layer, name="mlp_down", out_dtype=F32,
                   tile_modes=('none',) * (d_model // 1024), tm=1024, tn=1024, tk=2048)
    return _residual_ln(x1, down, ln2_g, ln2_b)


def kernel(x, w_in, cmp_pe_k, cmp_wk1, cmp_wk2, cmp_pe_v, cmp_wv1, cmp_wv2, w_br_a, w_br_b, w_br_c,
           w_out, ln1_g, ln1_b, w_up, w_down, ln2_g, ln2_b):
    nb, seq, d_model = x.shape
    tables = _rope_tables(seq)
    xf = x.reshape(nb * seq, d_model)
    xb = xf.astype(BF16)
    aligned_cols = sum(heads for _, heads, _ in _ALIGNED_FIELDS) * HEAD_DIM
    w_rows = _input_weight_rows(w_in, aligned_cols, N_NSA_GATES)
    for l in range(w_in.shape[0]):
        xf, xb = _layer(l, xf, xb, nb, seq, tables, w_rows, cmp_pe_k[l], cmp_wk1[l], cmp_wk2[l],
                        cmp_pe_v[l], cmp_wv1[l], cmp_wv2[l], w_br_a[l], w_br_b[l], w_br_c[l],
                        w_out, ln1_g[l], ln1_b[l], w_up, w_down, ln2_g[l], ln2_b[l])
    return xf.reshape(nb, seq, d_model)
```

```python
import functools
import math

import numpy as np
import jax
import jax.numpy as jnp
from jax import lax
from jax.experimental import pallas as pl
from jax.experimental.pallas import tpu as pltpu

F32 = jnp.float32
BF16 = jnp.bfloat16

HEAD_DIM = 128
ROPE_THETA = 10000.0
LN_EPS = 1e-5
DEPTH = 2

DIL_GROUPS = ((128, 1), (512, 4), (2048, 16))
A_SLOTS = 4
A_HEADS = A_SLOTS * len(DIL_GROUPS)

B_HEADS = 12
B_KV_HEADS = 4
B_GROUP = B_HEADS // B_KV_HEADS
CMP_LEN = 32
CMP_STRIDE = 16
SEL_LEN = 64
SEL_SHIFT = 6
SEL_TOPK = 16
WIN_LEN = 512

C_HEADS = 8

MASKED = -3e38
ALPHA = (2.0 * DEPTH) ** 0.25
Q_SCALE = HEAD_DIM ** -0.5

VMEM_LIMIT = 48 * 1024 * 1024
MXU_COLS = 256
ATT_TQ = 128
ATT_TILES = 4
DIL_ROWS = 128
DIL_ROWS_DENSE = 512
SEL_TQ = 128
SEL_TILES = 4
SEL_CHUNK = 1024
STICK_TILE = 512
STICK_HEADS = 2
CUM_TILE = 128


def _params(semantics):
    return pltpu.CompilerParams(dimension_semantics=semantics, vmem_limit_bytes=VMEM_LIMIT)


def _dot(a, b):
    return jnp.dot(a, b, preferred_element_type=F32)


def _dot_nt(a, b):
    return lax.dot_general(a, b, (((1,), (1,)), ((), ())), preferred_element_type=F32)


def _split_bf16(x):
    hi = x.astype(BF16)
    lo = (x - hi.astype(F32)).astype(BF16)
    return hi, lo


def _tile_flag(tile_modes, wanted):
    hits = [mode in wanted for mode in tile_modes]
    if all(hits) or not any(hits):
        return hits[0]
    j = pl.program_id(1)
    flag = None
    for c, hit in enumerate(hits):
        if hit:
            flag = (j == c) if flag is None else (flag | (j == c))
    return flag


def _write_columns(acc, o_ref, col0, tile_modes, head_major, cos_ref, sin_ref):
    is_rope = _tile_flag(tile_modes, ('rope', 'rope_scale'))
    is_scaled = _tile_flag(tile_modes, ('rope_scale', 'scale'))
    uniform = tile_modes[0] if len(set(tile_modes)) == 1 else None
    assert uniform is not None or not (set(tile_modes) & {'sigmoid', 'relu2'})

    def epilogue(y):
        if is_rope is not False:
            roped = y * cos_ref[...] + pltpu.roll(y, HEAD_DIM // 2, 1) * sin_ref[...]
            y = roped if is_rope is True else jnp.where(is_rope, roped, y)
        if is_scaled is not False:
            y = y * (Q_SCALE if is_scaled is True else jnp.where(is_scaled, Q_SCALE, 1.0))
        if uniform == 'sigmoid':
            y = jax.nn.sigmoid(y)
        if uniform == 'relu2':
            y = jnp.square(jnp.maximum(y, 0.0))
        return y

    width = acc.shape[1]
    if head_major:
        for c in range(width // HEAD_DIM):
            y = epilogue(acc[:, c * HEAD_DIM:(c + 1) * HEAD_DIM])
            o_ref[col0 // HEAD_DIM + c] = y.astype(o_ref.dtype)
    else:
        o_ref[:, col0:col0 + width] = epilogue(acc).astype(o_ref.dtype)


def _matmul_kernel(*refs, tile_modes, head_major, n_k, use_rope, acc_in_out, b_rows):
    refs = list(refs)
    a_ref, b_ref = refs[0], refs[1]
    cos_ref, sin_ref = (refs[2], refs[3]) if use_rope else (None, None)
    o_ref = refs[4] if use_rope else refs[2]
    acc_ref = o_ref if acc_in_out else (refs[-1] if n_k > 1 else None)
    tn = b_ref.shape[0] if b_rows else b_ref.shape[1]

    def product(col0, width):
        if b_rows:
            return _dot_nt(a_ref[...], b_ref[col0:col0 + width, :].astype(BF16))
        return _dot(a_ref[...], b_ref[:, col0:col0 + width].astype(BF16))

    if n_k == 1:
        width = min(MXU_COLS, tn)
        for col0 in range(0, tn, width):
            _write_columns(product(col0, width), o_ref, col0, tile_modes, head_major, cos_ref, sin_ref)
        return

    @pl.when(pl.program_id(2) == 0)
    def _():
        acc_ref[...] = jnp.zeros(acc_ref.shape, F32)

    acc_ref[...] += product(0, tn)
    if not acc_in_out:
        pl.when(pl.program_id(2) == n_k - 1)(
            lambda: _write_columns(acc_ref[...], o_ref, 0, tile_modes, head_major, cos_ref, sin_ref))


def _matmul(a, b, *, name, out_dtype, tile_modes, tm, tn, tk=None, layer=None, col0=0,
            head_major=False, rope_tables=None, b_rows=False):
    m, kdim = a.shape
    n = tn * len(tile_modes)
    tm = min(tm, m)
    tk = kdim if tk is None else min(tk, kdim)
    assert m % tm == 0 and kdim % tk == 0 and col0 % tn == 0
    n_k = kdim // tk
    col_tile0 = col0 // tn
    use_rope = any(mode.startswith('rope') for mode in tile_modes)
    acc_in_out = n_k > 1 and out_dtype == F32 and set(tile_modes) == {'none'} and not head_major
    if b_rows:
        assert layer is not None
        b_spec = pl.BlockSpec((None, tn, tk), lambda i, j, k: (layer, col_tile0 + j, k))
    elif layer is None:
        b_spec = pl.BlockSpec((tk, tn), lambda i, j, k: (k, col_tile0 + j))
    else:
        b_spec = pl.BlockSpec((None, tk, tn), lambda i, j, k: (layer, k, col_tile0 + j))
    in_specs = [pl.BlockSpec((tm, tk), lambda i, j, k: (i, k)), b_spec]
    operands = [a, b]
    if use_rope:
        assert head_major
        cos, sin = rope_tables
        seq_tiles = cos.shape[0] // tm
        assert cos.shape[0] % tm == 0
        spec = pl.BlockSpec((tm, HEAD_DIM), lambda i, j, k: (i % seq_tiles, 0))
        in_specs += [spec, spec]
        operands += [cos, sin]
    if head_major:
        out_shape = jax.ShapeDtypeStruct((n // HEAD_DIM, m, HEAD_DIM), out_dtype)
        out_spec = pl.BlockSpec((tn // HEAD_DIM, tm, HEAD_DIM), lambda i, j, k: (j, i, 0))
    else:
        out_shape = jax.ShapeDtypeStruct((m, n), out_dtype)
        out_spec = pl.BlockSpec((tm, tn), lambda i, j, k: (i, j))
    scratch = [pltpu.VMEM((tm, tn), F32)] if (n_k > 1 and not acc_in_out) else []
    return pl.pallas_call(
        functools.partial(_matmul_kernel, tile_modes=tuple(tile_modes), head_major=head_major, n_k=n_k,
                          use_rope=use_rope, acc_in_out=acc_in_out, b_rows=b_rows),
        grid=(m // tm, len(tile_modes), n_k),
        in_specs=in_specs,
        out_specs=out_spec,
        out_shape=out_shape,
        scratch_shapes=scratch,
        compiler_params=_params(("parallel", "parallel", "arbitrary")),
        name=name,
    )(*operands)


def _gather_rows_kernel(w_ref, o_ref, *, tn, k_chunks, n_layers):
    stride = k_chunks * n_layers
    for layer in range(n_layers):
        pieces = [w_ref[pl.ds(c * n_layers + layer, tn, stride=stride), :] for c in range(k_chunks)]
        o_ref[layer] = jnp.concatenate(pieces, axis=1).astype(o_ref.dtype)


def _input_weight_rows(w_in, aligned_cols, n_gates, tn=HEAD_DIM):
    n_layers, kdim, n = w_in.shape
    k_chunks = kdim // HEAD_DIM
    tail = n - aligned_cols - n_gates
    assert aligned_cols % tn == 0 and tail % tn == 0 and kdim % HEAD_DIM == 0
    aligned_tiles, tail_tiles = aligned_cols // tn, tail // tn
    rows_per_col = k_chunks * n_layers
    flat = w_in.reshape(n_layers, k_chunks, HEAD_DIM, n).transpose(3, 1, 0, 2).reshape(n * rows_per_col, HEAD_DIM)

    def source_row(j):
        col = jnp.where(j < aligned_tiles, tn * j,
                        jnp.where(j < aligned_tiles + tail_tiles,
                                  aligned_cols + n_gates + tn * (j - aligned_tiles), aligned_cols))
        return (col * rows_per_col, 0)

    tiles = aligned_tiles + tail_tiles + 1
    return pl.pallas_call(
        functools.partial(_gather_rows_kernel, tn=tn, k_chunks=k_chunks, n_layers=n_layers),
        grid=(tiles,),
        in_specs=[pl.BlockSpec((pl.Element(tn * rows_per_col), pl.Element(HEAD_DIM)), source_row)],
        out_specs=pl.BlockSpec((n_layers, tn, kdim), lambda j: (0, j, 0)),
        out_shape=jax.ShapeDtypeStruct((n_layers, tiles * tn, kdim), BF16),
        compiler_params=_params(("parallel",)),
        name="input_weight_rows",
    )(flat)


def _window_start(t0, back, span, seq, align):
    start = jnp.minimum(jnp.maximum(t0 - back, 0), seq - span)
    return pl.multiple_of(start, align)


def _dilated_group_kernel(q_ref, k_ref, v_ref, o_ref, lse_ref, *, dil, rows, band, n_rows):
    l0 = pl.program_id(2) * rows
    span = min(band + rows, n_rows)
    start = _window_start(l0, band, span, n_rows, math.gcd(rows, band))
    dist = (l0 + lax.broadcasted_iota(jnp.int32, (rows, 1), 0)) - (start + lax.broadcasted_iota(jnp.int32, (1, span), 1))
    keep = (dist >= 0) & (dist <= band)
    for r in range(dil):
        lanes = slice(r * HEAD_DIM, (r + 1) * HEAD_DIM)
        s = jnp.where(keep, _dot_nt(q_ref[:, lanes], k_ref[pl.ds(start, span), lanes]), -jnp.inf)
        m = jnp.max(s, axis=1, keepdims=True)
        p = jnp.exp(s - m)
        l = jnp.sum(p, axis=1, keepdims=True)
        o = _dot(p.astype(BF16), v_ref[pl.ds(start, span), lanes]) / l
        lse = jnp.broadcast_to(m + jnp.log(l), (rows, HEAD_DIM))
        if dil == 1:
            o_ref[...] = o
            lse_ref[...] = lse
        else:
            o_ref[pl.ds(r, rows, stride=dil), :] = o
            lse_ref[pl.ds(r, rows, stride=dil), :] = lse


def _head_spec(rows, head0):
    return pl.BlockSpec((None, None, rows, HEAD_DIM), lambda b, h, n: (head0 + h, b, n, 0))


def _seq_spec(seq, head0):
    return pl.BlockSpec((None, None, seq, HEAD_DIM), lambda b, h, n: (head0 + h, b, 0, 0))


def _dilated_group(q, q_head0, k, k_head0, v, v_head0, window, dil):
    _, nb, n_rows, width = q.shape
    seq = n_rows * dil
    band = window // dil
    rows = min(DIL_ROWS if dil > 1 else DIL_ROWS_DENSE, n_rows)
    assert width == dil * HEAD_DIM and n_rows % rows == 0
    view_spec = pl.BlockSpec((None, None, rows, width), lambda b, s, j: (q_head0 + s, b, j, 0))
    out_spec = pl.BlockSpec((None, None, rows * dil, HEAD_DIM), lambda b, s, j: (s, b, j, 0))
    out_shape = jax.ShapeDtypeStruct((A_SLOTS, nb, seq, HEAD_DIM), F32)
    return pl.pallas_call(
        functools.partial(_dilated_group_kernel, dil=dil, rows=rows, band=band, n_rows=n_rows),
        grid=(nb, A_SLOTS, n_rows // rows),
        in_specs=[view_spec,
                  pl.BlockSpec((None, None, n_rows, width), lambda b, s, j: (k_head0 + s, b, 0, 0)),
                  pl.BlockSpec((None, None, n_rows, width), lambda b, s, j: (v_head0 + s, b, 0, 0))],
        out_specs=[out_spec, out_spec],
        out_shape=[out_shape, out_shape],
        compiler_params=_params(("parallel", "parallel", "arbitrary")),
        name="dilated_group_%d" % dil,
    )(q, k, v)


def _dilated_merge_kernel(o0_ref, o1_ref, o2_ref, l0_ref, l1_ref, l2_ref, y_ref):
    for s in range(A_SLOTS):
        lses = (l0_ref[s], l1_ref[s], l2_ref[s])
        top = jnp.maximum(jnp.maximum(lses[0], lses[1]), lses[2])
        ws = [jnp.exp(lse - top) for lse in lses]
        y = (ws[0] * o0_ref[s] + ws[1] * o1_ref[s] + ws[2] * o2_ref[s]) / (ws[0] + ws[1] + ws[2])
        y_ref[:, s * HEAD_DIM:(s + 1) * HEAD_DIM] = y.astype(y_ref.dtype)


def _dilated_mixer(att, q_head0, k_head0, v_head0):
    _, nb, seq, _ = att.shape
    outs, lses = [], []
    for g, (window, dil) in enumerate(DIL_GROUPS):
        qh = q_head0 + g * A_SLOTS
        if dil == 1:
            o, lse = _dilated_group(att, qh, att, k_head0, att, v_head0, window, dil)
        else:
            lo = min(q_head0, k_head0, v_head0)
            hi = max(q_head0 + A_HEADS, k_head0 + A_SLOTS, v_head0 + A_SLOTS)
            view = att[lo:hi].reshape(hi - lo, nb, seq // dil, dil * HEAD_DIM)
            o, lse = _dilated_group(view, qh - lo, view, k_head0 - lo, view, v_head0 - lo, window, dil)
        outs.append(o)
        lses.append(lse)
    tq = min(256, seq)
    spec = pl.BlockSpec((A_SLOTS, None, tq, HEAD_DIM), lambda b, n: (0, b, n, 0))
    return pl.pallas_call(
        _dilated_merge_kernel,
        grid=(nb, seq // tq),
        in_specs=[spec] * 6,
        out_specs=pl.BlockSpec((None, tq, A_SLOTS * HEAD_DIM), lambda b, n: (b, n, 0)),
        out_shape=jax.ShapeDtypeStruct((nb, seq, A_SLOTS * HEAD_DIM), BF16),
        compiler_params=_params(("parallel", "parallel")),
        name="dilated_merge",
    )(*outs, *lses)


def _gelu_tanh(x):
    return 0.5 * x * (1.0 + jnp.tanh(math.sqrt(2.0 / math.pi) * (x + 0.044715 * (x * x * x))))


def _compress_kernel(x_ref, pe_ref, w1_ref, w2_ref, o_ref):
    x = x_ref[...].astype(F32)
    first = _dot((x + pe_ref[0:1, :]).astype(BF16), w1_ref[0])
    second = _dot((x + pe_ref[1:2, :]).astype(BF16), w1_ref[1])
    chunks = x.shape[0]
    hidden = first + pltpu.roll(second, chunks - 1, 0)
    o_ref[...] = _dot(_gelu_tanh(hidden).astype(BF16), w2_ref[...]).astype(o_ref.dtype)


def _compress(t, pe, w1, w2, name):
    nh, nb, seq, _ = t.shape
    chunks = seq // CMP_STRIDE
    width = CMP_STRIDE * HEAD_DIM
    ratio = CMP_LEN // CMP_STRIDE
    x = t.reshape(nh, nb, chunks, width)
    return pl.pallas_call(
        _compress_kernel,
        grid=(nh, nb),
        in_specs=[pl.BlockSpec((None, None, chunks, width), lambda h, b: (h, b, 0, 0)),
                  pl.BlockSpec((ratio, width), lambda h, b: (0, 0)),
                  pl.BlockSpec((ratio, width, HEAD_DIM), lambda h, b: (0, 0, 0)),
                  pl.BlockSpec((HEAD_DIM, HEAD_DIM), lambda h, b: (0, 0))],
        out_specs=pl.BlockSpec((None, None, chunks, HEAD_DIM), lambda h, b: (h, b, 0, 0)),
        out_shape=jax.ShapeDtypeStruct((nh, nb, chunks, HEAD_DIM), BF16),
        compiler_params=_params(("parallel", "parallel")),
        name=name,
    )(x, pe.reshape(ratio, width), w1.reshape(ratio, width, HEAD_DIM).astype(BF16), w2.astype(BF16))


def _group_q(q_refs, row0=0, rows=None):
    rows = q_refs[0].shape[0] if rows is None else rows
    return jnp.concatenate([r[row0:row0 + rows, :] for r in q_refs], axis=0)


def _group_q_specs(tq, head0):
    return [pl.BlockSpec((None, None, tq, HEAD_DIM),
                         functools.partial(lambda b, h, n, g: (head0 + h * B_GROUP + g, b, n, 0), g=g))
            for g in range(B_GROUP)]


def _cmp_select_kernel(q0_ref, q1_ref, q2_ref, kc_ref, vc_ref, ov_ref, o_ref, sel_ref, *, tq, tiles):
    for u in range(tiles):
        _cmp_select_tile((q0_ref, q1_ref, q2_ref), kc_ref, vc_ref, ov_ref, o_ref, sel_ref,
                         (pl.program_id(2) * tiles + u) * tq, u * tq, tq)


def _cmp_select_tile(q_refs, kc_ref, vc_ref, ov_ref, o_ref, sel_ref, t0, row0, tq):
    n_cmp = kc_ref.shape[0]
    rows = B_GROUP * tq
    s = _dot_nt(_group_q(q_refs, row0, tq), kc_ref[...])
    tpos3 = t0 + (lax.broadcasted_iota(jnp.int32, (rows, 1), 0) & (tq - 1))
    c_end = lax.broadcasted_iota(jnp.int32, (1, n_cmp), 1) * CMP_STRIDE + (CMP_LEN - 1)
    s = jnp.where(c_end - tpos3 <= 0, s, -jnp.inf)
    m = jnp.max(s, axis=1, keepdims=True)
    m = jnp.where(jnp.abs(m) < jnp.inf, m, 0.0)
    e = jnp.exp(s - m)
    den = jnp.sum(e, axis=1, keepdims=True)
    p = e / jnp.where(den > 0, den, 1.0)
    o = _dot(p.astype(BF16), vc_ref[...]).reshape(B_GROUP, tq, HEAD_DIM)
    o_ref[:, row0:row0 + tq, :] = o.astype(o_ref.dtype)

    p_sum = p[0:tq] + p[tq:2 * tq] + p[2 * tq:3 * tq]
    hi, lo = _split_bf16(p_sum)
    imp = _dot(hi, ov_ref[...]) + _dot(lo, ov_ref[...])
    tpos = t0 + lax.broadcasted_iota(jnp.int32, (tq, 1), 0)
    rel = lax.broadcasted_iota(jnp.int32, (1, HEAD_DIM), 1) - (tpos >> SEL_SHIFT)
    j_abs = jnp.broadcast_to(lax.broadcasted_iota(jnp.int32, (1, HEAD_DIM), 1), rel.shape)
    forced = (j_abs == 0) | (rel == 0) | (rel == -1)
    imp = jnp.where(forced, jnp.inf, jnp.where(rel <= 0, imp, -jnp.inf))

    n_sel = HEAD_DIM // 2
    imp_t = imp.T
    mine = imp_t[0:n_sel]
    j_idx = lax.broadcasted_iota(jnp.int32, (n_sel, tq), 0)
    beaten = jnp.zeros((n_sel, tq), F32)
    for kk in range(n_sel):
        other = imp_t[kk:kk + 1, :]
        wins = (other > mine) | ((other == mine) & (j_idx > kk))
        beaten = beaten + jnp.where(wins, 1.0, 0.0)
    chosen = jnp.where(beaten < SEL_TOPK, 0.0, MASKED)
    chosen = jnp.concatenate([chosen, jnp.full((HEAD_DIM - n_sel, tq), MASKED, F32)], axis=0)
    sel_ref[:, row0:row0 + tq] = chosen.astype(sel_ref.dtype)


def _overlap_matrix(n_cmp_rows, n_sel):
    c_start = np.arange(n_cmp_rows) * CMP_STRIDE
    c_end = c_start + CMP_LEN - 1
    s_start = np.arange(HEAD_DIM) * SEL_LEN
    ov = (c_start[:, None] <= s_start[None, :] + SEL_LEN - 1) & (c_end[:, None] >= s_start[None, :])
    ov &= (np.arange(HEAD_DIM) < n_sel)[None, :]
    return jnp.asarray(ov.astype(np.float32), dtype=BF16)


def _group_o_spec(tq):
    return pl.BlockSpec((B_GROUP, None, tq, HEAD_DIM), lambda b, h, n: (h, b, n, 0))


def _sel_spec(tq):
    return pl.BlockSpec((None, None, HEAD_DIM, tq), lambda b, h, n: (h, b, 0, n))


def _cmp_select(q, q_head0, kc, vc):
    _, nb, seq, _ = q.shape
    assert seq // SEL_LEN <= HEAD_DIM // 2
    tq = min(ATT_TQ, seq)
    tiles = min(ATT_TILES, seq // tq)
    n_cmp = kc.shape[2]
    kv_spec = pl.BlockSpec((None, None, n_cmp, HEAD_DIM), lambda b, h, n: (h, b, 0, 0))
    return pl.pallas_call(
        functools.partial(_cmp_select_kernel, tq=tq, tiles=tiles),
        grid=(nb, B_KV_HEADS, seq // (tq * tiles)),
        in_specs=_group_q_specs(tq * tiles, q_head0) + [kv_spec, kv_spec,
                                                        pl.BlockSpec((n_cmp, HEAD_DIM), lambda b, h, n: (0, 0))],
        out_specs=[_group_o_spec(tq * tiles), _sel_spec(tq * tiles)],
        out_shape=[jax.ShapeDtypeStruct((B_HEADS, nb, seq, HEAD_DIM), BF16),
                   jax.ShapeDtypeStruct((B_KV_HEADS, nb, HEAD_DIM, seq), BF16)],
        compiler_params=_params(("parallel", "parallel", "arbitrary")),
        name="nsa_compressed_select",
    )(q, q, q, kc, vc, _overlap_matrix(n_cmp, seq // SEL_LEN))


def _selected_kernel(q0_ref, q1_ref, q2_ref, k_ref, v_ref, sel_ref, o_ref, vt_ref, acc_ref, *, tq, kc, tiles):
    n = pl.program_id(2)
    t_first = n * (tq * tiles)
    cols = B_GROUP * tq

    @pl.when(n == 0)
    def _():
        for c in range(v_ref.shape[0] // kc):
            vt_ref[c] = v_ref[c * kc:(c + 1) * kc, :].astype(F32).T.astype(BF16)

    qs = [_group_q((q0_ref, q1_ref, q2_ref), u * tq, tq) for u in range(tiles)]
    qposs = [t_first + u * tq + lax.broadcasted_iota(jnp.int32, (1, tq), 1) for u in range(tiles)]
    key_in_chunk = lax.broadcasted_iota(jnp.int32, (kc, 1), 0)
    blocks = kc // SEL_LEN
    acc_ref[...] = jnp.zeros(acc_ref.shape, F32)

    def chunk(i, carry, diagonal):
        k0 = pl.multiple_of(i * kc, kc)
        kpos = k0 + key_in_chunk
        keys = k_ref[pl.ds(k0, kc), :]
        blk0 = pl.multiple_of(i * blocks, blocks)
        out = []
        for u in range(tiles):
            m_old, l_old = carry[u]
            rows = sel_ref[pl.ds(blk0, blocks), u * tq:(u + 1) * tq].astype(F32)
            bias = jnp.broadcast_to(rows[:, None, :], (blocks, SEL_LEN, tq)).reshape(kc, tq)
            if diagonal:
                bias = jnp.where(kpos - qposs[u] <= 0, bias, MASKED)
            s = _dot_nt(keys, qs[u]) + jnp.concatenate([bias] * B_GROUP, axis=1)
            m_new = jnp.maximum(m_old, jnp.max(s, axis=0, keepdims=True))
            p = jnp.exp(s - m_new)
            alpha = jnp.exp(m_old - m_new)
            acc_ref[u] = alpha * acc_ref[u] + _dot(vt_ref[i], p.astype(BF16))
            out.append((m_new, alpha * l_old + jnp.sum(p, axis=0, keepdims=True)))
        return tuple(out)

    init = tuple((jnp.full((1, cols), MASKED, F32), jnp.zeros((1, cols), F32)) for _ in range(tiles))
    last = t_first // kc
    carry = lax.fori_loop(0, last, lambda i, c: chunk(i, c, False), init)
    final = chunk(last, carry, True)
    for u in range(tiles):
        o_t = acc_ref[u] / final[u][1]
        for g in range(B_GROUP):
            o_ref[g, u * tq:(u + 1) * tq, :] = o_t[:, g * tq:(g + 1) * tq].T.astype(o_ref.dtype)


def _selected(q, q_head0, k, k_head0, v, v_head0, sel):
    _, nb, seq, _ = q.shape
    tq = min(SEL_TQ, seq)
    kc = min(SEL_CHUNK, seq)
    tiles = min(SEL_TILES, kc // tq)
    assert kc % (tq * tiles) == 0 and seq % kc == 0 and tq == HEAD_DIM
    return pl.pallas_call(
        functools.partial(_selected_kernel, tq=tq, kc=kc, tiles=tiles),
        grid=(nb, B_KV_HEADS, seq // (tq * tiles)),
        in_specs=_group_q_specs(tq * tiles, q_head0) + [_seq_spec(seq, k_head0), _seq_spec(seq, v_head0),
                                                        _sel_spec(tq * tiles)],
        out_specs=_group_o_spec(tq * tiles),
        out_shape=jax.ShapeDtypeStruct((B_HEADS, nb, seq, HEAD_DIM), BF16),
        scratch_shapes=[pltpu.VMEM((seq // kc, HEAD_DIM, kc), BF16),
                        pltpu.VMEM((tiles, HEAD_DIM, B_GROUP * tq), F32)],
        compiler_params=_params(("parallel", "parallel", "arbitrary")),
        name="nsa_selected",
    )(q, q, q, k, v, sel)


def _window_kernel(q0_ref, q1_ref, q2_ref, k_ref, v_ref, o_ref, *, tq, tiles, seq):
    rows = B_GROUP * tq
    span = min(WIN_LEN + tq, seq)
    for u in range(tiles):
        t0 = (pl.program_id(2) * tiles + u) * tq
        start = _window_start(t0, WIN_LEN, span, seq, tq)
        s = _dot_nt(_group_q((q0_ref, q1_ref, q2_ref), u * tq, tq), k_ref[pl.ds(start, span), :])
        qpos = t0 + (lax.broadcasted_iota(jnp.int32, (rows, 1), 0) & (tq - 1))
        kpos = start + lax.broadcasted_iota(jnp.int32, (1, span), 1)
        dist = qpos - kpos
        s = jnp.where((dist >= 0) & (dist <= WIN_LEN - 1), s, -jnp.inf)
        m = jnp.max(s, axis=1, keepdims=True)
        p = jnp.exp(s - m)
        l = jnp.sum(p, axis=1, keepdims=True)
        o = _dot(p.astype(BF16), v_ref[pl.ds(start, span), :]) / l
        o_ref[:, u * tq:(u + 1) * tq, :] = o.reshape(B_GROUP, tq, HEAD_DIM).astype(o_ref.dtype)


def _window(q, q_head0, k, k_head0, v, v_head0):
    _, nb, seq, _ = q.shape
    tq = min(ATT_TQ, seq)
    tiles = min(ATT_TILES, seq // tq)
    assert WIN_LEN % tq == 0
    return pl.pallas_call(
        functools.partial(_window_kernel, tq=tq, tiles=tiles, seq=seq),
        grid=(nb, B_KV_HEADS, seq // (tq * tiles)),
        in_specs=_group_q_specs(tq * tiles, q_head0) + [_seq_spec(seq, k_head0), _seq_spec(seq, v_head0)],
        out_specs=_group_o_spec(tq * tiles),
        out_shape=jax.ShapeDtypeStruct((B_HEADS, nb, seq, HEAD_DIM), BF16),
        compiler_params=_params(("parallel", "parallel", "arbitrary")),
        name="nsa_window",
    )(q, q, q, k, v)


def _nsa_gate_kernel(g_ref, oc_ref, os_ref, ow_ref, y_ref):
    g = g_ref[...]
    for h in range(B_HEADS):
        y = (g[:, 3 * h:3 * h + 1] * oc_ref[h].astype(F32)
             + g[:, 3 * h + 1:3 * h + 2] * os_ref[h].astype(F32)
             + g[:, 3 * h + 2:3 * h + 3] * ow_ref[h].astype(F32))
        y_ref[:, h * HEAD_DIM:(h + 1) * HEAD_DIM] = y.astype(y_ref.dtype)


def _nsa_gate(gates, o_cmp, o_sel, o_win):
    rows = gates.shape[0]
    tm = min(256, rows)
    o_spec = pl.BlockSpec((B_HEADS, tm, HEAD_DIM), lambda i: (0, i, 0))
    return pl.pallas_call(
        _nsa_gate_kernel,
        grid=(rows // tm,),
        in_specs=[pl.BlockSpec((tm, HEAD_DIM), lambda i: (i, 0)), o_spec, o_spec, o_spec],
        out_specs=pl.BlockSpec((tm, B_HEADS * HEAD_DIM), lambda i: (i, 0)),
        out_shape=jax.ShapeDtypeStruct((rows, B_HEADS * HEAD_DIM), BF16),
        compiler_params=_params(("parallel",)),
        name="nsa_gate",
    )(gates, o_cmp, o_sel, o_win)


def _stick_kernel(*refs, tq, heads):
    q_refs, k_refs, v_refs = refs[:heads], refs[heads:2 * heads], refs[2 * heads:3 * heads]
    o_ref, acc_ref, car_ref = refs[3 * heads:]
    n = pl.program_id(2)
    sub = min(CUM_TILE, tq)
    n_sub = tq // sub
    r_idx = lax.broadcasted_iota(jnp.int32, (2 * sub, 2 * sub), 0) & (sub - 1)
    c_idx = lax.broadcasted_iota(jnp.int32, (2 * sub, 2 * sub), 1)
    suffix_and_total = jnp.where((c_idx >= sub) | (r_idx > c_idx), 1.0, 0.0).astype(BF16)
    q_in_tile = lax.broadcasted_iota(jnp.int32, (tq, 1), 0)
    k_in_sub = lax.broadcasted_iota(jnp.int32, (1, sub), 1)
    acc_ref[...] = jnp.zeros(acc_ref.shape, F32)
    car_ref[...] = jnp.zeros(car_ref.shape, F32)

    def tile(k0, diagonal):
        for h in range(heads):
            z = _dot_nt(q_refs[h][...], k_refs[h][pl.ds(k0, tq), :])
            soft = jnp.log(1.0 + jnp.exp(-jnp.abs(z)))
            log_beta = jnp.minimum(z, 0.0) - soft
            log_rest = log_beta - z
            carried = car_ref[h]
            parts = [None] * n_sub
            for u in reversed(range(n_sub)):
                cols = slice(u * sub, (u + 1) * sub)
                rest_u = log_rest[:, cols]
                if diagonal:
                    before = (u * sub + k_in_sub) - q_in_tile < 0
                    rest_u = jnp.where(before, rest_u, 0.0)
                sums = _dot(jnp.concatenate(_split_bf16(rest_u), axis=1), suffix_and_total)
                a = jnp.exp(log_beta[:, cols] + (sums[:, :sub] + carried))
                if diagonal:
                    a = jnp.where(before, a, 0.0)
                carried = carried + sums[:, sub:]
                parts[u] = a.astype(BF16)
            car_ref[h] = carried
            acc_ref[h] += _dot(jnp.concatenate(parts, axis=1), v_refs[h][pl.ds(k0, tq), :])

    tile(pl.multiple_of(n * tq, tq), True)

    def step(i, carry):
        tile(pl.multiple_of((n - 1 - i) * tq, tq), False)
        return carry

    lax.fori_loop(0, n, step, 0)
    for h in range(heads):
        o_ref[:, h * HEAD_DIM:(h + 1) * HEAD_DIM] = acc_ref[h].astype(o_ref.dtype)


def _stick_breaking(q, q_head0, k, k_head0, v, v_head0):
    _, nb, seq, _ = q.shape
    tq = min(STICK_TILE, seq)
    heads = STICK_HEADS
    sub = min(CUM_TILE, tq)

    def per_head(make, head0):
        return [make(head0 + h) for h in range(heads)]

    def q_spec(head):
        return pl.BlockSpec((None, None, tq, HEAD_DIM), lambda b, g, n: (head + g * heads, b, n, 0))

    def kv_spec(head):
        return pl.BlockSpec((None, None, seq, HEAD_DIM), lambda b, g, n: (head + g * heads, b, 0, 0))

    return pl.pallas_call(
        functools.partial(_stick_kernel, tq=tq, heads=heads),
        grid=(nb, C_HEADS // heads, seq // tq),
        in_specs=per_head(q_spec, q_head0) + per_head(kv_spec, k_head0) + per_head(kv_spec, v_head0),
        out_specs=pl.BlockSpec((None, tq, heads * HEAD_DIM), lambda b, g, n: (b, n, g)),
        out_shape=jax.ShapeDtypeStruct((nb, seq, C_HEADS * HEAD_DIM), BF16),
        scratch_shapes=[pltpu.VMEM((heads, tq, HEAD_DIM), F32), pltpu.VMEM((heads, tq, sub), F32)],
        compiler_params=_params(("parallel", "parallel", "arbitrary")),
        name="stick_breaking",
    )(*([q] * heads + [k] * heads + [v] * heads))


def _merge_kernel(ya_ref, yb_ref, yc_ref, wa_ref, wb_ref, wc_ref, ga_ref, gb_ref, gc_ref, o_ref):
    mixed = (ga_ref[...].astype(F32) * _dot(ya_ref[...], wa_ref[...])
             + gb_ref[...].astype(F32) * _dot(yb_ref[...], wb_ref[...])
             + gc_ref[...].astype(F32) * _dot(yc_ref[...], wc_ref[...]))
    o_ref[...] = mixed.astype(o_ref.dtype)


def _merge(ya, yb, yc, wa, wb, wc, gates, d_model):
    m = ya.shape[0]
    tm, tn = min(512, m), 1024
    col_tiles = d_model // tn

    def y_spec(y):
        return pl.BlockSpec((tm, y.shape[1]), lambda i, j: (i, 0))

    def w_spec(w):
        return pl.BlockSpec((w.shape[0], tn), lambda i, j: (0, j))

    def g_spec(branch):
        return pl.BlockSpec((tm, tn), lambda i, j: (i, branch * col_tiles + j))

    return pl.pallas_call(
        _merge_kernel,
        grid=(m // tm, col_tiles),
        in_specs=[y_spec(ya), y_spec(yb), y_spec(yc), w_spec(wa), w_spec(wb), w_spec(wc),
                  g_spec(0), g_spec(1), g_spec(2)],
        out_specs=pl.BlockSpec((tm, tn), lambda i, j: (i, j)),
        out_shape=jax.ShapeDtypeStruct((m, d_model), BF16),
        compiler_params=_params(("parallel", "arbitrary")),
        name="branch_merge",
    )(ya, yb, yc, wa, wb, wc, gates, gates, gates)


def _ln_kernel(x_ref, y_ref, g_ref, b_ref, o_ref, ob_ref):
    z = ALPHA * x_ref[...] + y_ref[...]
    mu = jnp.mean(z, axis=1, keepdims=True)
    zc = z - mu
    var = jnp.mean(zc * zc, axis=1, keepdims=True)
    out = zc * lax.rsqrt(var + LN_EPS) * g_ref[...] + b_ref[...]
    o_ref[...] = out
    ob_ref[...] = out.astype(BF16)


def _residual_ln(x, y, g, b):
    m, d = x.shape
    tm = min(128, m)
    row = pl.BlockSpec((tm, d), lambda i: (i, 0))
    vec = pl.BlockSpec((1, d), lambda i: (0, 0))
    return pl.pallas_call(
        _ln_kernel,
        grid=(m // tm,),
        in_specs=[row, row, vec, vec],
        out_specs=[row, row],
        out_shape=[jax.ShapeDtypeStruct((m, d), F32), jax.ShapeDtypeStruct((m, d), BF16)],
        compiler_params=_params(("parallel",)),
        name="residual_layer_norm",
    )(x, y, g.reshape(1, d), b.reshape(1, d))


def _rope_tables(seq):
    half = HEAD_DIM // 2
    inv = 1.0 / (ROPE_THETA ** (jnp.arange(half, dtype=F32) / half))
    ang = jnp.arange(seq).astype(F32)[:, None] * inv[None, :]
    cos, sin = jnp.cos(ang), jnp.sin(ang)
    return jnp.concatenate([cos, cos], axis=1), jnp.concatenate([-sin, sin], axis=1)


_A_KV, _B_KV = A_SLOTS, B_KV_HEADS
_ALIGNED_FIELDS = (('q_a', A_HEADS, 'rope_scale'), ('k_a', _A_KV, 'rope'), ('v_a', _A_KV, 'none'),
                   ('q_b', B_HEADS, 'rope_scale'), ('kc_b', _B_KV, 'rope'), ('vc_b', _B_KV, 'none'),
                   ('ks_b', _B_KV, 'rope'), ('vs_b', _B_KV, 'none'), ('kw_b', _B_KV, 'rope'),
                   ('vw_b', _B_KV, 'none'))
_STICK_FIELDS = (('q_c', C_HEADS, 'scale'), ('k_c', C_HEADS, 'none'), ('v_c', C_HEADS, 'none'))
N_NSA_GATES = 3 * B_HEADS


def _field_layout(fields, heads_per_tile):
    head0, modes, start = {}, [], 0
    for name, heads, mode in fields:
        assert heads % heads_per_tile == 0
        head0[name] = start
        modes += [mode] * (heads // heads_per_tile)
        start += heads
    return head0, tuple(modes), start


def _layer(layer, x, xb, nb, seq, tables, w_rows, cmp_pe_k, cmp_wk1, cmp_wk2, cmp_pe_v, cmp_wv1, cmp_wv2,
           w_br_a, w_br_b, w_br_c, w_out, ln1_g, ln1_b, w_up, w_down, ln2_g, ln2_b):
    m, d_model = x.shape

    tn_att = 4 * HEAD_DIM
    at, att_modes, att_heads = _field_layout(_ALIGNED_FIELDS, tn_att // HEAD_DIM)
    att = _matmul(xb, w_rows, layer=layer, b_rows=True, name="proj_attention", out_dtype=BF16,
                  tile_modes=att_modes, tm=1024, tn=tn_att, head_major=True, rope_tables=tables)
    tn_tail = 1024
    st, stick_modes, stick_heads = _field_layout(_STICK_FIELDS, tn_tail // HEAD_DIM)
    stick_col0 = att_heads * HEAD_DIM
    gates_col0 = stick_col0 + stick_heads * HEAD_DIM
    nsa_col0 = gates_col0 + 3 * d_model
    stick = _matmul(xb, w_rows, layer=layer, b_rows=True, col0=stick_col0, name="proj_stick",
                    out_dtype=BF16, tile_modes=stick_modes, tm=1024, tn=tn_tail, head_major=True)
    gates = _matmul(xb, w_rows, layer=layer, b_rows=True, col0=gates_col0, name="proj_branch_gates",
                    out_dtype=BF16, tile_modes=('sigmoid',) * (3 * d_model // tn_tail), tm=1024, tn=tn_tail)
    g_nsa = _matmul(xb, w_rows, layer=layer, b_rows=True, col0=nsa_col0, name="proj_nsa_gates",
                    out_dtype=F32, tile_modes=('sigmoid',), tm=1024, tn=HEAD_DIM)

    att = att.reshape(att_heads, nb, seq, HEAD_DIM)
    stick = stick.reshape(stick_heads, nb, seq, HEAD_DIM)

    y_a = _dilated_mixer(att, at['q_a'], at['k_a'], at['v_a'])

    kc = _compress(att[at['kc_b']:at['kc_b'] + B_KV_HEADS], cmp_pe_k, cmp_wk1, cmp_wk2, "nsa_compress_k")
    vc = _compress(att[at['vc_b']:at['vc_b'] + B_KV_HEADS], cmp_pe_v, cmp_wv1, cmp_wv2, "nsa_compress_v")
    o_cmp, sel = _cmp_select(att, at['q_b'], kc, vc)
    o_sel = _selected(att, at['q_b'], att, at['ks_b'], att, at['vs_b'], sel)
    o_win = _window(att, at['q_b'], att, at['kw_b'], att, at['vw_b'])
    y_b = _nsa_gate(g_nsa, o_cmp.reshape(B_HEADS, m, HEAD_DIM), o_sel.reshape(B_HEADS, m, HEAD_DIM),
                    o_win.reshape(B_HEADS, m, HEAD_DIM))

    y_c = _stick_breaking(stick, st['q_c'], stick, st['k_c'], stick, st['v_c'])

    merged = _merge(y_a.reshape(m, -1), y_b, y_c.reshape(m, -1), w_br_a.astype(BF16),
                    w_br_b.astype(BF16), w_br_c.astype(BF16), gates, d_model)
    mixed = _matmul(merged, w_out, layer=layer, name="out_proj", out_dtype=F32,
                    tile_modes=('none',) * (d_model // 512), tm=1024, tn=512)
    x1, x1b = _residual_ln(x, mixed, ln1_g, ln1_b)

    d_ff = w_up.shape[2]
    hidden = _matmul(x1b, w_up, layer=layer, name="mlp_up", out_dtype=BF16,
                     tile_modes=('relu2',) * (d_ff // 512), tm=1024, tn=512)
    down = _matmul(hidden, w_down, layer=layer, name="mlp_down", out_dtype=F32,
                   tile_modes=('none',) * (d_model // 1024), tm=2048, tn=1024, tk=1024)
    return _residual_ln(x1, down, ln2_g, ln2_b)


def kernel(x, w_in, cmp_pe_k, cmp_wk1, cmp_wk2, cmp_pe_v, cmp_wv1, cmp_wv2, w_br_a, w_br_b, w_br_c,
           w_out, ln1_g, ln1_b, w_up, w_down, ln2_g, ln2_b):
    nb, seq, d_model = x.shape
    tables = _rope_tables(seq)
    xf = x.reshape(nb * seq, d_model)
    xb = xf.astype(BF16)
    aligned_cols = sum(heads for _, heads, _ in _ALIGNED_FIELDS) * HEAD_DIM
    w_rows = _input_weight_rows(w_in, aligned_cols, N_NSA_GATES)
    for l in range(w_in.shape[0]):
        xf, xb = _layer(l, xf, xb, nb, seq, tables, w_rows, cmp_pe_k[l], cmp_wk1[l], cmp_wk2[l],
                        cmp_pe_v[l], cmp_wv1[l], cmp_wv2[l], w_br_a[l], w_br_b[l], w_br_c[l],
                        w_out, ln1_g[l], ln1_b[l], w_up, w_down, ln2_g[l], ln2_b[l])
    return xf.reshape(nb, seq, d_model)
```

```python
import functools
import math

import numpy as np
import jax
import jax.numpy as jnp
from jax import lax
from jax.experimental import pallas as pl
from jax.experimental.pallas import tpu as pltpu

F32 = jnp.float32
BF16 = jnp.bfloat16

HEAD_DIM = 128
ROPE_THETA = 10000.0
LN_EPS = 1e-5
DEPTH = 2

DIL_GROUPS = ((128, 1), (512, 4), (2048, 16))
A_SLOTS = 4
A_HEADS = A_SLOTS * len(DIL_GROUPS)

B_HEADS = 12
B_KV_HEADS = 4
B_GROUP = B_HEADS // B_KV_HEADS
CMP_LEN = 32
CMP_STRIDE = 16
SEL_LEN = 64
SEL_SHIFT = 6
SEL_TOPK = 16
WIN_LEN = 512

C_HEADS = 8

MASKED = -3e38
ALPHA = (2.0 * DEPTH) ** 0.25
Q_SCALE = HEAD_DIM ** -0.5

VMEM_LIMIT = 48 * 1024 * 1024
MXU_COLS = 256
ATT_TQ = 128
ATT_TILES = 8
DIL_ROWS = 128
DIL_ROWS_DENSE = 512
SEL_TQ = 128
SEL_TILES = 4
SEL_CHUNK = 1024
STICK_TILE = 512
STICK_HEADS = 2
CUM_TILE = 128


def _params(semantics):
    return pltpu.CompilerParams(dimension_semantics=semantics, vmem_limit_bytes=VMEM_LIMIT)


def _dot(a, b):
    return jnp.dot(a, b, preferred_element_type=F32)


def _dot_nt(a, b):
    return lax.dot_general(a, b, (((1,), (1,)), ((), ())), preferred_element_type=F32)


def _split_bf16(x):
    hi = x.astype(BF16)
    lo = (x - hi.astype(F32)).astype(BF16)
    return hi, lo


def _tile_flag(tile_modes, wanted):
    hits = [mode in wanted for mode in tile_modes]
    if all(hits) or not any(hits):
        return hits[0]
    j = pl.program_id(1)
    flag = None
    for c, hit in enumerate(hits):
        if hit:
            flag = (j == c) if flag is None else (flag | (j == c))
    return flag


def _write_columns(acc, o_ref, col0, tile_modes, head_major, cos_ref, sin_ref):
    is_rope = _tile_flag(tile_modes, ('rope', 'rope_scale'))
    is_scaled = _tile_flag(tile_modes, ('rope_scale', 'scale'))
    uniform = tile_modes[0] if len(set(tile_modes)) == 1 else None
    assert uniform is not None or not (set(tile_modes) & {'sigmoid', 'relu2'})

    def epilogue(y):
        if is_rope is not False:
            roped = y * cos_ref[...] + pltpu.roll(y, HEAD_DIM // 2, 1) * sin_ref[...]
            y = roped if is_rope is True else jnp.where(is_rope, roped, y)
        if is_scaled is not False:
            y = y * (Q_SCALE if is_scaled is True else jnp.where(is_scaled, Q_SCALE, 1.0))
        if uniform == 'sigmoid':
            y = jax.nn.sigmoid(y)
        if uniform == 'relu2':
            y = jnp.square(jnp.maximum(y, 0.0))
        return y

    width = acc.shape[1]
    if head_major:
        for c in range(width // HEAD_DIM):
            y = epilogue(acc[:, c * HEAD_DIM:(c + 1) * HEAD_DIM])
            o_ref[col0 // HEAD_DIM + c] = y.astype(o_ref.dtype)
    else:
        o_ref[:, col0:col0 + width] = epilogue(acc).astype(o_ref.dtype)


def _matmul_kernel(*refs, tile_modes, head_major, n_k, use_rope, acc_in_out, b_rows):
    refs = list(refs)
    a_ref, b_ref = refs[0], refs[1]
    cos_ref, sin_ref = (refs[2], refs[3]) if use_rope else (None, None)
    o_ref = refs[4] if use_rope else refs[2]
    acc_ref = o_ref if acc_in_out else (refs[-1] if n_k > 1 else None)
    tn = b_ref.shape[0] if b_rows else b_ref.shape[1]

    def product(col0, width):
        if b_rows:
            return _dot_nt(a_ref[...], b_ref[col0:col0 + width, :].astype(BF16))
        return _dot(a_ref[...], b_ref[:, col0:col0 + width].astype(BF16))

    if n_k == 1:
        width = min(MXU_COLS, tn)
        for col0 in range(0, tn, width):
            _write_columns(product(col0, width), o_ref, col0, tile_modes, head_major, cos_ref, sin_ref)
        return

    @pl.when(pl.program_id(2) == 0)
    def _():
        acc_ref[...] = jnp.zeros(acc_ref.shape, F32)

    acc_ref[...] += product(0, tn)
    if not acc_in_out:
        pl.when(pl.program_id(2) == n_k - 1)(
            lambda: _write_columns(acc_ref[...], o_ref, 0, tile_modes, head_major, cos_ref, sin_ref))


def _matmul(a, b, *, name, out_dtype, tile_modes, tm, tn, tk=None, layer=None, col0=0,
            head_major=False, rope_tables=None, b_rows=False):
    m, kdim = a.shape
    n = tn * len(tile_modes)
    tm = min(tm, m)
    tk = kdim if tk is None else min(tk, kdim)
    assert m % tm == 0 and kdim % tk == 0 and col0 % tn == 0
    n_k = kdim // tk
    col_tile0 = col0 // tn
    use_rope = any(mode.startswith('rope') for mode in tile_modes)
    acc_in_out = n_k > 1 and out_dtype == F32 and set(tile_modes) == {'none'} and not head_major
    if b_rows:
        assert layer is not None
        b_spec = pl.BlockSpec((None, tn, tk), lambda i, j, k: (layer, col_tile0 + j, k))
    elif layer is None:
        b_spec = pl.BlockSpec((tk, tn), lambda i, j, k: (k, col_tile0 + j))
    else:
        b_spec = pl.BlockSpec((None, tk, tn), lambda i, j, k: (layer, k, col_tile0 + j))
    in_specs = [pl.BlockSpec((tm, tk), lambda i, j, k: (i, k)), b_spec]
    operands = [a, b]
    if use_rope:
        assert head_major
        cos, sin = rope_tables
        seq_tiles = cos.shape[0] // tm
        assert cos.shape[0] % tm == 0
        spec = pl.BlockSpec((tm, HEAD_DIM), lambda i, j, k: (i % seq_tiles, 0))
        in_specs += [spec, spec]
        operands += [cos, sin]
    if head_major:
        out_shape = jax.ShapeDtypeStruct((n // HEAD_DIM, m, HEAD_DIM), out_dtype)
        out_spec = pl.BlockSpec((tn // HEAD_DIM, tm, HEAD_DIM), lambda i, j, k: (j, i, 0))
    else:
        out_shape = jax.ShapeDtypeStruct((m, n), out_dtype)
        out_spec = pl.BlockSpec((tm, tn), lambda i, j, k: (i, j))
    scratch = [pltpu.VMEM((tm, tn), F32)] if (n_k > 1 and not acc_in_out) else []
    return pl.pallas_call(
        functools.partial(_matmul_kernel, tile_modes=tuple(tile_modes), head_major=head_major, n_k=n_k,
                          use_rope=use_rope, acc_in_out=acc_in_out, b_rows=b_rows),
        grid=(m // tm, len(tile_modes), n_k),
        in_specs=in_specs,
        out_specs=out_spec,
        out_shape=out_shape,
        scratch_shapes=scratch,
        compiler_params=_params(("parallel", "parallel", "arbitrary")),
        name=name,
    )(*operands)


def _gather_rows_kernel(w_ref, o_ref, *, tn, k_chunks, n_layers):
    stride = k_chunks * n_layers
    by_chunk = pltpu.einshape("nck->cnk", w_ref[...].reshape(tn, stride, HEAD_DIM))
    for layer in range(n_layers):
        pieces = [by_chunk[c * n_layers + layer] for c in range(k_chunks)]
        o_ref[layer] = jnp.concatenate(pieces, axis=1).astype(o_ref.dtype)


def _input_weight_rows(w_in, aligned_cols, n_gates, tn=HEAD_DIM):
    n_layers, kdim, n = w_in.shape
    k_chunks = kdim // HEAD_DIM
    tail = n - aligned_cols - n_gates
    assert aligned_cols % tn == 0 and tail % tn == 0 and kdim % HEAD_DIM == 0
    aligned_tiles, tail_tiles = aligned_cols // tn, tail // tn
    rows_per_col = k_chunks * n_layers
    flat = w_in.reshape(n_layers, k_chunks, HEAD_DIM, n).transpose(3, 1, 0, 2).reshape(n * rows_per_col, HEAD_DIM)

    def source_row(j):
        col = jnp.where(j < aligned_tiles, tn * j,
                        jnp.where(j < aligned_tiles + tail_tiles,
                                  aligned_cols + n_gates + tn * (j - aligned_tiles), aligned_cols))
        return (col * rows_per_col, 0)

    tiles = aligned_tiles + tail_tiles + 1
    return pl.pallas_call(
        functools.partial(_gather_rows_kernel, tn=tn, k_chunks=k_chunks, n_layers=n_layers),
        grid=(tiles,),
        in_specs=[pl.BlockSpec((pl.Element(tn * rows_per_col), pl.Element(HEAD_DIM)), source_row)],
        out_specs=pl.BlockSpec((n_layers, tn, kdim), lambda j: (0, j, 0)),
        out_shape=jax.ShapeDtypeStruct((n_layers, tiles * tn, kdim), BF16),
        compiler_params=_params(("parallel",)),
        name="input_weight_rows",
    )(flat)


def _window_start(t0, back, span, seq, align):
    start = jnp.minimum(jnp.maximum(t0 - back, 0), seq - span)
    return pl.multiple_of(start, align)


def _dilated_group_kernel(q_ref, k_ref, v_ref, o_ref, lse_ref, *, dil, rows, band, n_rows):
    l0 = pl.program_id(2) * rows
    span = min(band + rows, n_rows)
    start = _window_start(l0, band, span, n_rows, math.gcd(rows, band))
    dist = (l0 + lax.broadcasted_iota(jnp.int32, (rows, 1), 0)) - (start + lax.broadcasted_iota(jnp.int32, (1, span), 1))
    keep = (dist >= 0) & (dist <= band)
    for r in range(dil):
        lanes = slice(r * HEAD_DIM, (r + 1) * HEAD_DIM)
        s = jnp.where(keep, _dot_nt(q_ref[:, lanes], k_ref[pl.ds(start, span), lanes]), -jnp.inf)
        m = jnp.max(s, axis=1, keepdims=True)
        p = jnp.exp(s - m)
        l = jnp.sum(p, axis=1, keepdims=True)
        o = _dot(p.astype(BF16), v_ref[pl.ds(start, span), lanes]) / l
        lse = jnp.broadcast_to(m + jnp.log(l), (rows, HEAD_DIM))
        if dil == 1:
            o_ref[...] = o
            lse_ref[...] = lse
        else:
            o_ref[pl.ds(r, rows, stride=dil), :] = o
            lse_ref[pl.ds(r, rows, stride=dil), :] = lse


def _head_spec(rows, head0):
    return pl.BlockSpec((None, None, rows, HEAD_DIM), lambda b, h, n: (head0 + h, b, n, 0))


def _seq_spec(seq, head0):
    return pl.BlockSpec((None, None, seq, HEAD_DIM), lambda b, h, n: (head0 + h, b, 0, 0))


def _dilated_group(q, q_head0, k, k_head0, v, v_head0, window, dil):
    _, nb, n_rows, width = q.shape
    seq = n_rows * dil
    band = window // dil
    rows = min(DIL_ROWS if dil > 1 else DIL_ROWS_DENSE, n_rows)
    assert width == dil * HEAD_DIM and n_rows % rows == 0
    view_spec = pl.BlockSpec((None, None, rows, width), lambda b, s, j: (q_head0 + s, b, j, 0))
    out_spec = pl.BlockSpec((None, None, rows * dil, HEAD_DIM), lambda b, s, j: (s, b, j, 0))
    out_shape = jax.ShapeDtypeStruct((A_SLOTS, nb, seq, HEAD_DIM), F32)
    return pl.pallas_call(
        functools.partial(_dilated_group_kernel, dil=dil, rows=rows, band=band, n_rows=n_rows),
        grid=(nb, A_SLOTS, n_rows // rows),
        in_specs=[view_spec,
                  pl.BlockSpec((None, None, n_rows, width), lambda b, s, j: (k_head0 + s, b, 0, 0)),
                  pl.BlockSpec((None, None, n_rows, width), lambda b, s, j: (v_head0 + s, b, 0, 0))],
        out_specs=[out_spec, out_spec],
        out_shape=[out_shape, out_shape],
        compiler_params=_params(("parallel", "parallel", "arbitrary")),
        name="dilated_group_%d" % dil,
    )(q, k, v)


def _dilated_merge_kernel(o0_ref, o1_ref, o2_ref, l0_ref, l1_ref, l2_ref, y_ref):
    for s in range(A_SLOTS):
        lses = (l0_ref[s], l1_ref[s], l2_ref[s])
        top = jnp.maximum(jnp.maximum(lses[0], lses[1]), lses[2])
        ws = [jnp.exp(lse - top) for lse in lses]
        y = (ws[0] * o0_ref[s] + ws[1] * o1_ref[s] + ws[2] * o2_ref[s]) / (ws[0] + ws[1] + ws[2])
        y_ref[:, s * HEAD_DIM:(s + 1) * HEAD_DIM] = y.astype(y_ref.dtype)


def _dilated_mixer(att, q_head0, k_head0, v_head0):
    _, nb, seq, _ = att.shape
    outs, lses = [], []
    for g, (window, dil) in enumerate(DIL_GROUPS):
        qh = q_head0 + g * A_SLOTS
        if dil == 1:
            o, lse = _dilated_group(att, qh, att, k_head0, att, v_head0, window, dil)
        else:
            lo = min(q_head0, k_head0, v_head0)
            hi = max(q_head0 + A_HEADS, k_head0 + A_SLOTS, v_head0 + A_SLOTS)
            view = att[lo:hi].reshape(hi - lo, nb, seq // dil, dil * HEAD_DIM)
            o, lse = _dilated_group(view, qh - lo, view, k_head0 - lo, view, v_head0 - lo, window, dil)
        outs.append(o)
        lses.append(lse)
    tq = min(256, seq)
    spec = pl.BlockSpec((A_SLOTS, None, tq, HEAD_DIM), lambda b, n: (0, b, n, 0))
    return pl.pallas_call(
        _dilated_merge_kernel,
        grid=(nb, seq // tq),
        in_specs=[spec] * 6,
        out_specs=pl.BlockSpec((None, tq, A_SLOTS * HEAD_DIM), lambda b, n: (b, n, 0)),
        out_shape=jax.ShapeDtypeStruct((nb, seq, A_SLOTS * HEAD_DIM), BF16),
        compiler_params=_params(("parallel", "parallel")),
        name="dilated_merge",
    )(*outs, *lses)


def _gelu_tanh(x):
    return 0.5 * x * (1.0 + jnp.tanh(math.sqrt(2.0 / math.pi) * (x + 0.044715 * (x * x * x))))


def _compress_kernel(x_ref, pe_ref, w1_ref, w2_ref, o_ref):
    x = x_ref[...].astype(F32)
    first = _dot((x + pe_ref[0:1, :]).astype(BF16), w1_ref[0])
    second = _dot((x + pe_ref[1:2, :]).astype(BF16), w1_ref[1])
    chunks = x.shape[0]
    hidden = first + pltpu.roll(second, chunks - 1, 0)
    o_ref[...] = _dot(_gelu_tanh(hidden).astype(BF16), w2_ref[...]).astype(o_ref.dtype)


def _compress(t, pe, w1, w2, name):
    nh, nb, seq, _ = t.shape
    chunks = seq // CMP_STRIDE
    width = CMP_STRIDE * HEAD_DIM
    ratio = CMP_LEN // CMP_STRIDE
    x = t.reshape(nh, nb, chunks, width)
    return pl.pallas_call(
        _compress_kernel,
        grid=(nh, nb),
        in_specs=[pl.BlockSpec((None, None, chunks, width), lambda h, b: (h, b, 0, 0)),
                  pl.BlockSpec((ratio, width), lambda h, b: (0, 0)),
                  pl.BlockSpec((ratio, width, HEAD_DIM), lambda h, b: (0, 0, 0)),
                  pl.BlockSpec((HEAD_DIM, HEAD_DIM), lambda h, b: (0, 0))],
        out_specs=pl.BlockSpec((None, None, chunks, HEAD_DIM), lambda h, b: (h, b, 0, 0)),
        out_shape=jax.ShapeDtypeStruct((nh, nb, chunks, HEAD_DIM), BF16),
        compiler_params=_params(("parallel", "parallel")),
        name=name,
    )(x, pe.reshape(ratio, width), w1.reshape(ratio, width, HEAD_DIM).astype(BF16), w2.astype(BF16))


def _group_q(q_refs, row0=0, rows=None):
    rows = q_refs[0].shape[0] if rows is None else rows
    return jnp.concatenate([r[row0:row0 + rows, :] for r in q_refs], axis=0)


def _group_q_specs(tq, head0):
    return [pl.BlockSpec((None, None, tq, HEAD_DIM),
                         functools.partial(lambda b, h, n, g: (head0 + h * B_GROUP + g, b, n, 0), g=g))
            for g in range(B_GROUP)]


def _cmp_select_kernel(q0_ref, q1_ref, q2_ref, kc_ref, vc_ref, ov_ref, o_ref, sel_ref, *, tq, tiles):
    for u in range(tiles):
        _cmp_select_tile((q0_ref, q1_ref, q2_ref), kc_ref, vc_ref, ov_ref, o_ref, sel_ref,
                         (pl.program_id(2) * tiles + u) * tq, u * tq, tq)


def _cmp_select_tile(q_refs, kc_ref, vc_ref, ov_ref, o_ref, sel_ref, t0, row0, tq):
    n_cmp = kc_ref.shape[0]
    rows = B_GROUP * tq
    s = _dot_nt(_group_q(q_refs, row0, tq), kc_ref[...])
    tpos3 = t0 + (lax.broadcasted_iota(jnp.int32, (rows, 1), 0) & (tq - 1))
    c_end = lax.broadcasted_iota(jnp.int32, (1, n_cmp), 1) * CMP_STRIDE + (CMP_LEN - 1)
    s = jnp.where(c_end - tpos3 <= 0, s, -jnp.inf)
    m = jnp.max(s, axis=1, keepdims=True)
    m = jnp.where(jnp.abs(m) < jnp.inf, m, 0.0)
    e = jnp.exp(s - m)
    den = jnp.sum(e, axis=1, keepdims=True)
    p = e / jnp.where(den > 0, den, 1.0)
    o = _dot(p.astype(BF16), vc_ref[...]).reshape(B_GROUP, tq, HEAD_DIM)
    o_ref[:, row0:row0 + tq, :] = o.astype(o_ref.dtype)

    p_sum = p[0:tq] + p[tq:2 * tq] + p[2 * tq:3 * tq]
    hi, lo = _split_bf16(p_sum)
    imp = _dot(hi, ov_ref[...]) + _dot(lo, ov_ref[...])
    tpos = t0 + lax.broadcasted_iota(jnp.int32, (tq, 1), 0)
    rel = lax.broadcasted_iota(jnp.int32, (1, HEAD_DIM), 1) - (tpos >> SEL_SHIFT)
    j_abs = jnp.broadcast_to(lax.broadcasted_iota(jnp.int32, (1, HEAD_DIM), 1), rel.shape)
    forced = (j_abs == 0) | (rel == 0) | (rel == -1)
    imp = jnp.where(forced, jnp.inf, jnp.where(rel <= 0, imp, -jnp.inf))

    n_sel = HEAD_DIM // 2
    imp_t = imp.T
    mine = imp_t[0:n_sel]
    j_idx = lax.broadcasted_iota(jnp.int32, (n_sel, tq), 0)
    beaten = jnp.zeros((n_sel, tq), F32)
    for kk in range(n_sel):
        other = imp_t[kk:kk + 1, :]
        wins = (other > mine) | ((other == mine) & (j_idx > kk))
        beaten = beaten + jnp.where(wins, 1.0, 0.0)
    chosen = jnp.where(beaten < SEL_TOPK, 0.0, MASKED)
    chosen = jnp.concatenate([chosen, jnp.full((HEAD_DIM - n_sel, tq), MASKED, F32)], axis=0)
    sel_ref[:, row0:row0 + tq] = chosen.astype(sel_ref.dtype)


def _overlap_matrix(n_cmp_rows, n_sel):
    c_start = np.arange(n_cmp_rows) * CMP_STRIDE
    c_end = c_start + CMP_LEN - 1
    s_start = np.arange(HEAD_DIM) * SEL_LEN
    ov = (c_start[:, None] <= s_start[None, :] + SEL_LEN - 1) & (c_end[:, None] >= s_start[None, :])
    ov &= (np.arange(HEAD_DIM) < n_sel)[None, :]
    return jnp.asarray(ov.astype(np.float32), dtype=BF16)


def _group_o_spec(tq):
    return pl.BlockSpec((B_GROUP, None, tq, HEAD_DIM), lambda b, h, n: (h, b, n, 0))


def _sel_spec(tq):
    return pl.BlockSpec((None, None, HEAD_DIM, tq), lambda b, h, n: (h, b, 0, n))


def _cmp_select(q, q_head0, kc, vc):
    _, nb, seq, _ = q.shape
    assert seq // SEL_LEN <= HEAD_DIM // 2
    tq = min(ATT_TQ, seq)
    tiles = min(ATT_TILES, seq // tq)
    n_cmp = kc.shape[2]
    kv_spec = pl.BlockSpec((None, None, n_cmp, HEAD_DIM), lambda b, h, n: (h, b, 0, 0))
    return pl.pallas_call(
        functools.partial(_cmp_select_kernel, tq=tq, tiles=tiles),
        grid=(nb, B_KV_HEADS, seq // (tq * tiles)),
        in_specs=_group_q_specs(tq * tiles, q_head0) + [kv_spec, kv_spec,
                                                        pl.BlockSpec((n_cmp, HEAD_DIM), lambda b, h, n: (0, 0))],
        out_specs=[_group_o_spec(tq * tiles), _sel_spec(tq * tiles)],
        out_shape=[jax.ShapeDtypeStruct((B_HEADS, nb, seq, HEAD_DIM), BF16),
                   jax.ShapeDtypeStruct((B_KV_HEADS, nb, HEAD_DIM, seq), BF16)],
        compiler_params=_params(("parallel", "parallel", "arbitrary")),
        name="nsa_compressed_select",
    )(q, q, q, kc, vc, _overlap_matrix(n_cmp, seq // SEL_LEN))


def _selected_kernel(q0_ref, q1_ref, q2_ref, k_ref, v_ref, sel_ref, o_ref, vt_ref, acc_ref, *, tq, kc, tiles):
    n = pl.program_id(2)
    t_first = n * (tq * tiles)
    cols = B_GROUP * tq

    @pl.when(n == 0)
    def _():
        for c in range(v_ref.shape[0] // kc):
            vt_ref[c] = v_ref[c * kc:(c + 1) * kc, :].astype(F32).T.astype(BF16)

    qs = [_group_q((q0_ref, q1_ref, q2_ref), u * tq, tq) for u in range(tiles)]
    qposs = [t_first + u * tq + lax.broadcasted_iota(jnp.int32, (1, tq), 1) for u in range(tiles)]
    key_in_chunk = lax.broadcasted_iota(jnp.int32, (kc, 1), 0)
    blocks = kc // SEL_LEN
    acc_ref[...] = jnp.zeros(acc_ref.shape, F32)

    def chunk(i, carry, diagonal):
        k0 = pl.multiple_of(i * kc, kc)
        kpos = k0 + key_in_chunk
        keys = k_ref[pl.ds(k0, kc), :]
        blk0 = pl.multiple_of(i * blocks, blocks)
        out = []
        for u in range(tiles):
            m_old, l_old = carry[u]
            rows = sel_ref[pl.ds(blk0, blocks), u * tq:(u + 1) * tq].astype(F32)
            bias = jnp.broadcast_to(rows[:, None, :], (blocks, SEL_LEN, tq)).reshape(kc, tq)
            if diagonal:
                bias = jnp.where(kpos - qposs[u] <= 0, bias, MASKED)
            s = _dot_nt(keys, qs[u]) + jnp.concatenate([bias] * B_GROUP, axis=1)
            m_new = jnp.maximum(m_old, jnp.max(s, axis=0, keepdims=True))
            p = jnp.exp(s - m_new)
            alpha = jnp.exp(m_old - m_new)
            acc_ref[u] = alpha * acc_ref[u] + _dot(vt_ref[i], p.astype(BF16))
            out.append((m_new, alpha * l_old + jnp.sum(p, axis=0, keepdims=True)))
        return tuple(out)

    init = tuple((jnp.full((1, cols), MASKED, F32), jnp.zeros((1, cols), F32)) for _ in range(tiles))
    last = t_first // kc
    carry = lax.fori_loop(0, last, lambda i, c: chunk(i, c, False), init)
    final = chunk(last, carry, True)
    for u in range(tiles):
        o_t = acc_ref[u] / final[u][1]
        for g in range(B_GROUP):
            o_ref[g, u * tq:(u + 1) * tq, :] = o_t[:, g * tq:(g + 1) * tq].T.astype(o_ref.dtype)


def _selected(q, q_head0, k, k_head0, v, v_head0, sel):
    _, nb, seq, _ = q.shape
    tq = min(SEL_TQ, seq)
    kc = min(SEL_CHUNK, seq)
    tiles = min(SEL_TILES, kc // tq)
    assert kc % (tq * tiles) == 0 and seq % kc == 0 and tq == HEAD_DIM
    return pl.pallas_call(
        functools.partial(_selected_kernel, tq=tq, kc=kc, tiles=tiles),
        grid=(nb, B_KV_HEADS, seq // (tq * tiles)),
        in_specs=_group_q_specs(tq * tiles, q_head0) + [_seq_spec(seq, k_head0), _seq_spec(seq, v_head0),
                                                        _sel_spec(tq * tiles)],
        out_specs=_group_o_spec(tq * tiles),
        out_shape=jax.ShapeDtypeStruct((B_HEADS, nb, seq, HEAD_DIM), BF16),
        scratch_shapes=[pltpu.VMEM((seq // kc, HEAD_DIM, kc), BF16),
                        pltpu.VMEM((tiles, HEAD_DIM, B_GROUP * tq), F32)],
        compiler_params=_params(("parallel", "parallel", "arbitrary")),
        name="nsa_selected",
    )(q, q, q, k, v, sel)


def _window_kernel(q0_ref, q1_ref, q2_ref, k_ref, v_ref, o_ref, *, tq, tiles, seq):
    rows = B_GROUP * tq
    span = min(WIN_LEN + tq, seq)
    for u in range(tiles):
        t0 = (pl.program_id(2) * tiles + u) * tq
        start = _window_start(t0, WIN_LEN, span, seq, tq)
        s = _dot_nt(_group_q((q0_ref, q1_ref, q2_ref), u * tq, tq), k_ref[pl.ds(start, span), :])
        qpos = t0 + (lax.broadcasted_iota(jnp.int32, (rows, 1), 0) & (tq - 1))
        kpos = start + lax.broadcasted_iota(jnp.int32, (1, span), 1)
        dist = qpos - kpos
        s = jnp.where((dist >= 0) & (dist <= WIN_LEN - 1), s, -jnp.inf)
        m = jnp.max(s, axis=1, keepdims=True)
        p = jnp.exp(s - m)
        l = jnp.sum(p, axis=1, keepdims=True)
        o = _dot(p.astype(BF16), v_ref[pl.ds(start, span), :]) / l
        o_ref[:, u * tq:(u + 1) * tq, :] = o.reshape(B_GROUP, tq, HEAD_DIM).astype(o_ref.dtype)


def _window(q, q_head0, k, k_head0, v, v_head0):
    _, nb, seq, _ = q.shape
    tq = min(ATT_TQ, seq)
    tiles = min(ATT_TILES, seq // tq)
    assert WIN_LEN % tq == 0
    return pl.pallas_call(
        functools.partial(_window_kernel, tq=tq, tiles=tiles, seq=seq),
        grid=(nb, B_KV_HEADS, seq // (tq * tiles)),
        in_specs=_group_q_specs(tq * tiles, q_head0) + [_seq_spec(seq, k_head0), _seq_spec(seq, v_head0)],
        out_specs=_group_o_spec(tq * tiles),
        out_shape=jax.ShapeDtypeStruct((B_HEADS, nb, seq, HEAD_DIM), BF16),
        compiler_params=_params(("parallel", "parallel", "arbitrary")),
        name="nsa_window",
    )(q, q, q, k, v)


def _nsa_gate_kernel(g_ref, oc_ref, os_ref, ow_ref, y_ref):
    g = g_ref[...]
    for h in range(B_HEADS):
        y = (g[:, 3 * h:3 * h + 1] * oc_ref[h].astype(F32)
             + g[:, 3 * h + 1:3 * h + 2] * os_ref[h].astype(F32)
             + g[:, 3 * h + 2:3 * h + 3] * ow_ref[h].astype(F32))
        y_ref[:, h * HEAD_DIM:(h + 1) * HEAD_DIM] = y.astype(y_ref.dtype)


def _nsa_gate(gates, o_cmp, o_sel, o_win):
    rows = gates.shape[0]
    tm = min(256, rows)
    o_spec = pl.BlockSpec((B_HEADS, tm, HEAD_DIM), lambda i: (0, i, 0))
    return pl.pallas_call(
        _nsa_gate_kernel,
        grid=(rows // tm,),
        in_specs=[pl.BlockSpec((tm, HEAD_DIM), lambda i: (i, 0)), o_spec, o_spec, o_spec],
        out_specs=pl.BlockSpec((tm, B_HEADS * HEAD_DIM), lambda i: (i, 0)),
        out_shape=jax.ShapeDtypeStruct((rows, B_HEADS * HEAD_DIM), BF16),
        compiler_params=_params(("parallel",)),
        name="nsa_gate",
    )(gates, o_cmp, o_sel, o_win)


def _stick_kernel(*refs, tq, heads):
    q_refs, k_refs, v_refs = refs[:heads], refs[heads:2 * heads], refs[2 * heads:3 * heads]
    o_ref, acc_ref, car_ref = refs[3 * heads:]
    n = pl.program_id(2)
    sub = min(CUM_TILE, tq)
    n_sub = tq // sub
    r_idx = lax.broadcasted_iota(jnp.int32, (2 * sub, 2 * sub), 0) & (sub - 1)
    c_idx = lax.broadcasted_iota(jnp.int32, (2 * sub, 2 * sub), 1)
    suffix_and_total = jnp.where((c_idx >= sub) | (r_idx > c_idx), 1.0, 0.0).astype(BF16)
    q_in_tile = lax.broadcasted_iota(jnp.int32, (tq, 1), 0)
    k_in_sub = lax.broadcasted_iota(jnp.int32, (1, sub), 1)
    acc_ref[...] = jnp.zeros(acc_ref.shape, F32)
    car_ref[...] = jnp.zeros(car_ref.shape, F32)

    def tile(k0, diagonal):
        for h in range(heads):
            z = _dot_nt(q_refs[h][...], k_refs[h][pl.ds(k0, tq), :])
            soft = jnp.log(1.0 + jnp.exp(-jnp.abs(z)))
            log_beta = jnp.minimum(z, 0.0) - soft
            log_rest = log_beta - z
            carried = car_ref[h]
            parts = [None] * n_sub
            for u in reversed(range(n_sub)):
                cols = slice(u * sub, (u + 1) * sub)
                rest_u = log_rest[:, cols]
                if diagonal:
                    before = (u * sub + k_in_sub) - q_in_tile < 0
                    rest_u = jnp.where(before, rest_u, 0.0)
                sums = _dot(jnp.concatenate(_split_bf16(rest_u), axis=1), suffix_and_total)
                a = jnp.exp(log_beta[:, cols] + (sums[:, :sub] + carried))
                if diagonal:
                    a = jnp.where(before, a, 0.0)
                carried = carried + sums[:, sub:]
                parts[u] = a.astype(BF16)
            car_ref[h] = carried
            acc_ref[h] += _dot(jnp.concatenate(parts, axis=1), v_refs[h][pl.ds(k0, tq), :])

    tile(pl.multiple_of(n * tq, tq), True)

    def step(i, carry):
        tile(pl.multiple_of((n - 1 - i) * tq, tq), False)
        return carry

    lax.fori_loop(0, n, step, 0)
    for h in range(heads):
        o_ref[:, h * HEAD_DIM:(h + 1) * HEAD_DIM] = acc_ref[h].astype(o_ref.dtype)


def _stick_breaking(q, q_head0, k, k_head0, v, v_head0):
    _, nb, seq, _ = q.shape
    tq = min(STICK_TILE, seq)
    heads = STICK_HEADS
    sub = min(CUM_TILE, tq)

    def per_head(make, head0):
        return [make(head0 + h) for h in range(heads)]

    def q_spec(head):
        return pl.BlockSpec((None, None, tq, HEAD_DIM), lambda b, g, n: (head + g * heads, b, n, 0))

    def kv_spec(head):
        return pl.BlockSpec((None, None, seq, HEAD_DIM), lambda b, g, n: (head + g * heads, b, 0, 0))

    return pl.pallas_call(
        functools.partial(_stick_kernel, tq=tq, heads=heads),
        grid=(nb, C_HEADS // heads, seq // tq),
        in_specs=per_head(q_spec, q_head0) + per_head(kv_spec, k_head0) + per_head(kv_spec, v_head0),
        out_specs=pl.BlockSpec((None, tq, heads * HEAD_DIM), lambda b, g, n: (b, n, g)),
        out_shape=jax.ShapeDtypeStruct((nb, seq, C_HEADS * HEAD_DIM), BF16),
        scratch_shapes=[pltpu.VMEM((heads, tq, HEAD_DIM), F32), pltpu.VMEM((heads, tq, sub), F32)],
        compiler_params=_params(("parallel", "parallel", "arbitrary")),
        name="stick_breaking",
    )(*([q] * heads + [k] * heads + [v] * heads))


def _merge_kernel(ya_ref, yb_ref, yc_ref, wa_ref, wb_ref, wc_ref, ga_ref, gb_ref, gc_ref, o_ref):
    mixed = (ga_ref[...].astype(F32) * _dot(ya_ref[...], wa_ref[...])
             + gb_ref[...].astype(F32) * _dot(yb_ref[...], wb_ref[...])
             + gc_ref[...].astype(F32) * _dot(yc_ref[...], wc_ref[...]))
    o_ref[...] = mixed.astype(o_ref.dtype)


def _merge(ya, yb, yc, wa, wb, wc, gates, d_model):
    m = ya.shape[0]
    tm, tn = min(512, m), 1024
    col_tiles = d_model // tn

    def y_spec(y):
        return pl.BlockSpec((tm, y.shape[1]), lambda i, j: (i, 0))

    def w_spec(w):
        return pl.BlockSpec((w.shape[0], tn), lambda i, j: (0, j))

    def g_spec(branch):
        return pl.BlockSpec((tm, tn), lambda i, j: (i, branch * col_tiles + j))

    return pl.pallas_call(
        _merge_kernel,
        grid=(m // tm, col_tiles),
        in_specs=[y_spec(ya), y_spec(yb), y_spec(yc), w_spec(wa), w_spec(wb), w_spec(wc),
                  g_spec(0), g_spec(1), g_spec(2)],
        out_specs=pl.BlockSpec((tm, tn), lambda i, j: (i, j)),
        out_shape=jax.ShapeDtypeStruct((m, d_model), BF16),
        compiler_params=_params(("parallel", "arbitrary")),
        name="branch_merge",
    )(ya, yb, yc, wa, wb, wc, gates, gates, gates)


def _ln_kernel(x_ref, y_ref, g_ref, b_ref, o_ref, ob_ref):
    z = ALPHA * x_ref[...] + y_ref[...]
    mu = jnp.mean(z, axis=1, keepdims=True)
    zc = z - mu
    var = jnp.mean(zc * zc, axis=1, keepdims=True)
    out = zc * lax.rsqrt(var + LN_EPS) * g_ref[...] + b_ref[...]
    o_ref[...] = out
    ob_ref[...] = out.astype(BF16)


def _residual_ln(x, y, g, b):
    m, d = x.shape
    tm = min(128, m)
    row = pl.BlockSpec((tm, d), lambda i: (i, 0))
    vec = pl.BlockSpec((1, d), lambda i: (0, 0))
    return pl.pallas_call(
        _ln_kernel,
        grid=(m // tm,),
        in_specs=[row, row, vec, vec],
        out_specs=[row, row],
        out_shape=[jax.ShapeDtypeStruct((m, d), F32), jax.ShapeDtypeStruct((m, d), BF16)],
        compiler_params=_params(("parallel",)),
        name="residual_layer_norm",
    )(x, y, g.reshape(1, d), b.reshape(1, d))


def _rope_tables(seq):
    half = HEAD_DIM // 2
    inv = 1.0 / (ROPE_THETA ** (jnp.arange(half, dtype=F32) / half))
    ang = jnp.arange(seq).astype(F32)[:, None] * inv[None, :]
    cos, sin = jnp.cos(ang), jnp.sin(ang)
    return jnp.concatenate([cos, cos], axis=1), jnp.concatenate([-sin, sin], axis=1)


_A_KV, _B_KV = A_SLOTS, B_KV_HEADS
_ALIGNED_FIELDS = (('q_a', A_HEADS, 'rope_scale'), ('k_a', _A_KV, 'rope'), ('v_a', _A_KV, 'none'),
                   ('q_b', B_HEADS, 'rope_scale'), ('kc_b', _B_KV, 'rope'), ('vc_b', _B_KV, 'none'),
                   ('ks_b', _B_KV, 'rope'), ('vs_b', _B_KV, 'none'), ('kw_b', _B_KV, 'rope'),
                   ('vw_b', _B_KV, 'none'))
_STICK_FIELDS = (('q_c', C_HEADS, 'scale'), ('k_c', C_HEADS, 'none'), ('v_c', C_HEADS, 'none'))
N_NSA_GATES = 3 * B_HEADS


def _field_layout(fields, heads_per_tile):
    head0, modes, start = {}, [], 0
    for name, heads, mode in fields:
        assert heads % heads_per_tile == 0
        head0[name] = start
        modes += [mode] * (heads // heads_per_tile)
        start += heads
    return head0, tuple(modes), start


def _layer(layer, x, xb, nb, seq, tables, w_rows, cmp_pe_k, cmp_wk1, cmp_wk2, cmp_pe_v, cmp_wv1, cmp_wv2,
           w_br_a, w_br_b, w_br_c, w_out, ln1_g, ln1_b, w_up, w_down, ln2_g, ln2_b):
    m, d_model = x.shape

    tn_att = 4 * HEAD_DIM
    at, att_modes, att_heads = _field_layout(_ALIGNED_FIELDS, tn_att // HEAD_DIM)
    att = _matmul(xb, w_rows, layer=layer, b_rows=True, name="proj_attention", out_dtype=BF16,
                  tile_modes=att_modes, tm=1024, tn=tn_att, head_major=True, rope_tables=tables)
    tn_tail = 1024
    st, stick_modes, stick_heads = _field_layout(_STICK_FIELDS, tn_tail // HEAD_DIM)
    stick_col0 = att_heads * HEAD_DIM
    gates_col0 = stick_col0 + stick_heads * HEAD_DIM
    nsa_col0 = gates_col0 + 3 * d_model
    stick = _matmul(xb, w_rows, layer=layer, b_rows=True, col0=stick_col0, name="proj_stick",
                    out_dtype=BF16, tile_modes=stick_modes, tm=1024, tn=tn_tail, head_major=True)
    gates = _matmul(xb, w_rows, layer=layer, b_rows=True, col0=gates_col0, name="proj_branch_gates",
                    out_dtype=BF16, tile_modes=('sigmoid',) * (3 * d_model // tn_tail), tm=1024, tn=tn_tail)
    g_nsa = _matmul(xb, w_rows, layer=layer, b_rows=True, col0=nsa_col0, name="proj_nsa_gates",
                    out_dtype=F32, tile_modes=('sigmoid',), tm=1024, tn=HEAD_DIM)

    att = att.reshape(att_heads, nb, seq, HEAD_DIM)
    stick = stick.reshape(stick_heads, nb, seq, HEAD_DIM)

    y_a = _dilated_mixer(att, at['q_a'], at['k_a'], at['v_a'])

    kc = _compress(att[at['kc_b']:at['kc_b'] + B_KV_HEADS], cmp_pe_k, cmp_wk1, cmp_wk2, "nsa_compress_k")
    vc = _compress(att[at['vc_b']:at['vc_b'] + B_KV_HEADS], cmp_pe_v, cmp_wv1, cmp_wv2, "nsa_compress_v")
    o_cmp, sel = _cmp_select(att, at['q_b'], kc, vc)
    o_sel = _selected(att, at['q_b'], att, at['ks_b'], att, at['vs_b'], sel)
    o_win = _window(att, at['q_b'], att, at['kw_b'], att, at['vw_b'])
    y_b = _nsa_gate(g_nsa, o_cmp.reshape(B_HEADS, m, HEAD_DIM), o_sel.reshape(B_HEADS, m, HEAD_DIM),
                    o_win.reshape(B_HEADS, m, HEAD_DIM))

    y_c = _stick_breaking(stick, st['q_c'], stick, st['k_c'], stick, st['v_c'])

    merged = _merge(y_a.reshape(m, -1), y_b, y_c.reshape(m, -1), w_br_a.astype(BF16),
                    w_br_b.astype(BF16), w_br_c.astype(BF16), gates, d_model)
    mixed = _matmul(merged, w_out, layer=layer, name="out_proj", out_dtype=F32,
                    tile_modes=('none',) * (d_model // 512), tm=1024, tn=512)
    x1, x1b = _residual_ln(x, mixed, ln1_g, ln1_b)

    d_ff = w_up.shape[2]
    hidden = _matmul(x1b, w_up, layer=layer, name="mlp_up", out_dtype=BF16,
                     tile_modes=('relu2',) * (d_ff // 512), tm=1024, tn=512)
    down = _matmul(hidden, w_down, layer=layer, name="mlp_down", out_dtype=F32,
                   tile_modes=('none',) * (d_model // 1024), tm=2048, tn=1024, tk=1024)
    return _residual_ln(x1, down, ln2_g, ln2_b)


def kernel(x, w_in, cmp_pe_k, cmp_wk1, cmp_wk2, cmp_pe_v, cmp_wv1, cmp_wv2, w_br_a, w_br_b, w_br_c,
           w_out, ln1_g, ln1_b, w_up, w_down, ln2_g, ln2_b):
    nb, seq, d_model = x.shape
    tables = _rope_tables(seq)
    xf = x.reshape(nb * seq, d_model)
    xb = xf.astype(BF16)
    aligned_cols = sum(heads for _, heads, _ in _ALIGNED_FIELDS) * HEAD_DIM
    w_rows = _input_weight_rows(w_in, aligned_cols, N_NSA_GATES)
    for l in range(w_in.shape[0]):
        xf, xb = _layer(l, xf, xb, nb, seq, tables, w_rows, cmp_pe_k[l], cmp_wk1[l], cmp_wk2[l],
                        cmp_pe_v[l], cmp_wv1[l], cmp_wv2[l], w_br_a[l], w_br_b[l], w_br_c[l],
                        w_out, ln1_g[l], ln1_b[l], w_up, w_down, ln2_g[l], ln2_b[l])
    return xf.reshape(nb, seq, d_model)
```

```python
import functools
import math

import numpy as np
import jax
import jax.numpy as jnp
from jax import lax
from jax.experimental import pallas as pl
from jax.experimental.pallas import tpu as pltpu

F32 = jnp.float32
BF16 = jnp.bfloat16

HEAD_DIM = 128
ROPE_THETA = 10000.0
LN_EPS = 1e-5
DEPTH = 2

DIL_GROUPS = ((128, 1), (512, 4), (2048, 16))
A_SLOTS = 4
A_HEADS = A_SLOTS * len(DIL_GROUPS)

B_HEADS = 12
B_KV_HEADS = 4
B_GROUP = B_HEADS // B_KV_HEADS
CMP_LEN = 32
CMP_STRIDE = 16
SEL_LEN = 64
SEL_SHIFT = 6
SEL_TOPK = 16
WIN_LEN = 512

C_HEADS = 8

MASKED = -3e38
ALPHA = (2.0 * DEPTH) ** 0.25
Q_SCALE = HEAD_DIM ** -0.5

VMEM_LIMIT = 48 * 1024 * 1024
MXU_COLS = 256
ATT_TQ = 128
ATT_TILES = 8
DIL_ROWS = 128
DIL_ROWS_DENSE = 256
SEL_TQ = 128
SEL_TILES = 4
SEL_CHUNK = 1024
STICK_TILE = 512
STICK_HEADS = 2
CUM_TILE = 128


def _params(semantics):
    return pltpu.CompilerParams(dimension_semantics=semantics, vmem_limit_bytes=VMEM_LIMIT)


def _dot(a, b):
    return jnp.dot(a, b, preferred_element_type=F32)


def _dot_nt(a, b):
    return lax.dot_general(a, b, (((1,), (1,)), ((), ())), preferred_element_type=F32)


def _split_bf16(x):
    hi = x.astype(BF16)
    lo = (x - hi.astype(F32)).astype(BF16)
    return hi, lo


def _tile_flag(tile_modes, wanted):
    hits = [mode in wanted for mode in tile_modes]
    if all(hits) or not any(hits):
        return hits[0]
    j = pl.program_id(1)
    flag = None
    for c, hit in enumerate(hits):
        if hit:
            flag = (j == c) if flag is None else (flag | (j == c))
    return flag


def _write_columns(acc, o_ref, col0, tile_modes, head_major, cos_ref, sin_ref):
    is_rope = _tile_flag(tile_modes, ('rope', 'rope_scale'))
    is_scaled = _tile_flag(tile_modes, ('rope_scale', 'scale'))
    uniform = tile_modes[0] if len(set(tile_modes)) == 1 else None
    assert uniform is not None or not (set(tile_modes) & {'sigmoid', 'relu2'})

    def epilogue(y):
        if is_rope is not False:
            roped = y * cos_ref[...] + pltpu.roll(y, HEAD_DIM // 2, 1) * sin_ref[...]
            y = roped if is_rope is True else jnp.where(is_rope, roped, y)
        if is_scaled is not False:
            y = y * (Q_SCALE if is_scaled is True else jnp.where(is_scaled, Q_SCALE, 1.0))
        if uniform == 'sigmoid':
            y = jax.nn.sigmoid(y)
        if uniform == 'relu2':
            y = jnp.square(jnp.maximum(y, 0.0))
        return y

    width = acc.shape[1]
    if head_major:
        for c in range(width // HEAD_DIM):
            y = epilogue(acc[:, c * HEAD_DIM:(c + 1) * HEAD_DIM])
            o_ref[col0 // HEAD_DIM + c] = y.astype(o_ref.dtype)
    else:
        o_ref[:, col0:col0 + width] = epilogue(acc).astype(o_ref.dtype)


def _matmul_kernel(*refs, tile_modes, head_major, n_k, use_rope, acc_in_out, b_rows):
    refs = list(refs)
    a_ref, b_ref = refs[0], refs[1]
    cos_ref, sin_ref = (refs[2], refs[3]) if use_rope else (None, None)
    o_ref = refs[4] if use_rope else refs[2]
    acc_ref = o_ref if acc_in_out else (refs[-1] if n_k > 1 else None)
    tn = b_ref.shape[0] if b_rows else b_ref.shape[1]

    def product(col0, width):
        if b_rows:
            return _dot_nt(a_ref[...], b_ref[col0:col0 + width, :].astype(BF16))
        return _dot(a_ref[...], b_ref[:, col0:col0 + width].astype(BF16))

    if n_k == 1:
        width = min(MXU_COLS, tn)
        for col0 in range(0, tn, width):
            _write_columns(product(col0, width), o_ref, col0, tile_modes, head_major, cos_ref, sin_ref)
        return

    @pl.when(pl.program_id(2) == 0)
    def _():
        acc_ref[...] = jnp.zeros(acc_ref.shape, F32)

    acc_ref[...] += product(0, tn)
    if not acc_in_out:
        pl.when(pl.program_id(2) == n_k - 1)(
            lambda: _write_columns(acc_ref[...], o_ref, 0, tile_modes, head_major, cos_ref, sin_ref))


def _matmul(a, b, *, name, out_dtype, tile_modes, tm, tn, tk=None, layer=None, col0=0,
            head_major=False, rope_tables=None, b_rows=False):
    m, kdim = a.shape
    n = tn * len(tile_modes)
    tm = min(tm, m)
    tk = kdim if tk is None else min(tk, kdim)
    assert m % tm == 0 and kdim % tk == 0 and col0 % tn == 0
    n_k = kdim // tk
    col_tile0 = col0 // tn
    use_rope = any(mode.startswith('rope') for mode in tile_modes)
    acc_in_out = n_k > 1 and out_dtype == F32 and set(tile_modes) == {'none'} and not head_major
    if b_rows:
        assert layer is not None
        b_spec = pl.BlockSpec((None, tn, tk), lambda i, j, k: (layer, col_tile0 + j, k))
    elif layer is None:
        b_spec = pl.BlockSpec((tk, tn), lambda i, j, k: (k, col_tile0 + j))
    else:
        b_spec = pl.BlockSpec((None, tk, tn), lambda i, j, k: (layer, k, col_tile0 + j))
    in_specs = [pl.BlockSpec((tm, tk), lambda i, j, k: (i, k)), b_spec]
    operands = [a, b]
    if use_rope:
        assert head_major
        cos, sin = rope_tables
        seq_tiles = cos.shape[0] // tm
        assert cos.shape[0] % tm == 0
        spec = pl.BlockSpec((tm, HEAD_DIM), lambda i, j, k: (i % seq_tiles, 0))
        in_specs += [spec, spec]
        operands += [cos, sin]
    if head_major:
        out_shape = jax.ShapeDtypeStruct((n // HEAD_DIM, m, HEAD_DIM), out_dtype)
        out_spec = pl.BlockSpec((tn // HEAD_DIM, tm, HEAD_DIM), lambda i, j, k: (j, i, 0))
    else:
        out_shape = jax.ShapeDtypeStruct((m, n), out_dtype)
        out_spec = pl.BlockSpec((tm, tn), lambda i, j, k: (i, j))
    scratch = [pltpu.VMEM((tm, tn), F32)] if (n_k > 1 and not acc_in_out) else []
    return pl.pallas_call(
        functools.partial(_matmul_kernel, tile_modes=tuple(tile_modes), head_major=head_major, n_k=n_k,
                          use_rope=use_rope, acc_in_out=acc_in_out, b_rows=b_rows),
        grid=(m // tm, len(tile_modes), n_k),
        in_specs=in_specs,
        out_specs=out_spec,
        out_shape=out_shape,
        scratch_shapes=scratch,
        compiler_params=_params(("parallel", "parallel", "arbitrary")),
        name=name,
    )(*operands)


def _gather_rows_kernel(w_ref, o_ref, *, tn, k_chunks, n_layers):
    stride = k_chunks * n_layers
    by_chunk = pltpu.einshape("nck->cnk", w_ref[...].reshape(tn, stride, HEAD_DIM))
    for layer in range(n_layers):
        pieces = [by_chunk[c * n_layers + layer] for c in range(k_chunks)]
        o_ref[layer] = jnp.concatenate(pieces, axis=1).astype(o_ref.dtype)


def _input_weight_rows(w_in, aligned_cols, n_gates, tn=HEAD_DIM):
    n_layers, kdim, n = w_in.shape
    k_chunks = kdim // HEAD_DIM
    tail = n - aligned_cols - n_gates
    assert aligned_cols % tn == 0 and tail % tn == 0 and kdim % HEAD_DIM == 0
    aligned_tiles, tail_tiles = aligned_cols // tn, tail // tn
    rows_per_col = k_chunks * n_layers
    flat = w_in.reshape(n_layers, k_chunks, HEAD_DIM, n).transpose(3, 1, 0, 2).reshape(n * rows_per_col, HEAD_DIM)

    def source_row(j):
        col = jnp.where(j < aligned_tiles, tn * j,
                        jnp.where(j < aligned_tiles + tail_tiles,
                                  aligned_cols + n_gates + tn * (j - aligned_tiles), aligned_cols))
        return (col * rows_per_col, 0)

    tiles = aligned_tiles + tail_tiles + 1
    return pl.pallas_call(
        functools.partial(_gather_rows_kernel, tn=tn, k_chunks=k_chunks, n_layers=n_layers),
        grid=(tiles,),
        in_specs=[pl.BlockSpec((pl.Element(tn * rows_per_col), pl.Element(HEAD_DIM)), source_row)],
        out_specs=pl.BlockSpec((n_layers, tn, kdim), lambda j: (0, j, 0)),
        out_shape=jax.ShapeDtypeStruct((n_layers, tiles * tn, kdim), BF16),
        compiler_params=_params(("parallel",)),
        name="input_weight_rows",
    )(flat)


def _window_start(t0, back, span, seq, align):
    start = jnp.minimum(jnp.maximum(t0 - back, 0), seq - span)
    return pl.multiple_of(start, align)


def _by_residue(x, dil):
    rows = x.shape[0] // dil
    return pltpu.einshape("ldk->dlk", x.reshape(rows, dil, HEAD_DIM))


def _dilated_group_kernel(q_ref, k_ref, v_ref, o_ref, lse_ref, *scratch, dil, rows, band, n_rows):
    j = pl.program_id(2)
    if dil > 1:
        kr_ref, vr_ref = scratch

        @pl.when(j == 0)
        def _():
            kr_ref[...] = _by_residue(k_ref[...], dil)
            vr_ref[...] = _by_residue(v_ref[...], dil)

        q_by = _by_residue(q_ref[...], dil)
    l0 = j * rows
    span = min(band + rows, n_rows)
    start = _window_start(l0, band, span, n_rows, math.gcd(rows, band))
    dist = (l0 + lax.broadcasted_iota(jnp.int32, (rows, 1), 0)) - (start + lax.broadcasted_iota(jnp.int32, (1, span), 1))
    keep = (dist >= 0) & (dist <= band)
    for r in range(dil):
        if dil == 1:
            q, keys, vals = q_ref[...], k_ref[pl.ds(start, span), :], v_ref[pl.ds(start, span), :]
        else:
            q, keys, vals = q_by[r], kr_ref[r, pl.ds(start, span), :], vr_ref[r, pl.ds(start, span), :]
        s = jnp.where(keep, _dot_nt(q, keys), -jnp.inf)
        m = jnp.max(s, axis=1, keepdims=True)
        p = jnp.exp(s - m)
        l = jnp.sum(p, axis=1, keepdims=True)
        o = _dot(p.astype(BF16), vals) / l
        lse = jnp.broadcast_to(m + jnp.log(l), (rows, HEAD_DIM))
        if dil == 1:
            o_ref[...] = o
            lse_ref[...] = lse
        else:
            o_ref[pl.ds(r, rows, stride=dil), :] = o
            lse_ref[pl.ds(r, rows, stride=dil), :] = lse


def _head_spec(rows, head0):
    return pl.BlockSpec((None, None, rows, HEAD_DIM), lambda b, h, n: (head0 + h, b, n, 0))


def _seq_spec(seq, head0):
    return pl.BlockSpec((None, None, seq, HEAD_DIM), lambda b, h, n: (head0 + h, b, 0, 0))


def _dilated_group(att, q_head0, k_head0, v_head0, window, dil):
    _, nb, seq, _ = att.shape
    n_rows = seq // dil
    band = window // dil
    rows = min(DIL_ROWS if dil > 1 else DIL_ROWS_DENSE, n_rows)
    assert seq % dil == 0 and n_rows % rows == 0
    out_spec = _head_spec(rows * dil, 0)
    out_shape = jax.ShapeDtypeStruct((A_SLOTS, nb, seq, HEAD_DIM), F32)
    by_residue = pltpu.VMEM((dil, n_rows, HEAD_DIM), BF16)
    return pl.pallas_call(
        functools.partial(_dilated_group_kernel, dil=dil, rows=rows, band=band, n_rows=n_rows),
        grid=(nb, A_SLOTS, n_rows // rows),
        in_specs=[_head_spec(rows * dil, q_head0), _seq_spec(seq, k_head0), _seq_spec(seq, v_head0)],
        out_specs=[out_spec, out_spec],
        out_shape=[out_shape, out_shape],
        scratch_shapes=[by_residue, by_residue] if dil > 1 else [],
        compiler_params=_params(("parallel", "parallel", "arbitrary")),
        name="dilated_group_%d" % dil,
    )(att, att, att)


def _dilated_merge_kernel(o0_ref, o1_ref, o2_ref, l0_ref, l1_ref, l2_ref, y_ref):
    for s in range(A_SLOTS):
        lses = (l0_ref[s], l1_ref[s], l2_ref[s])
        top = jnp.maximum(jnp.maximum(lses[0], lses[1]), lses[2])
        ws = [jnp.exp(lse - top) for lse in lses]
        y = (ws[0] * o0_ref[s] + ws[1] * o1_ref[s] + ws[2] * o2_ref[s]) / (ws[0] + ws[1] + ws[2])
        y_ref[:, s * HEAD_DIM:(s + 1) * HEAD_DIM] = y.astype(y_ref.dtype)


def _dilated_mixer(att, q_head0, k_head0, v_head0):
    _, nb, seq, _ = att.shape
    outs, lses = [], []
    for g, (window, dil) in enumerate(DIL_GROUPS):
        o, lse = _dilated_group(att, q_head0 + g * A_SLOTS, k_head0, v_head0, window, dil)
        outs.append(o)
        lses.append(lse)
    tq = min(256, seq)
    spec = pl.BlockSpec((A_SLOTS, None, tq, HEAD_DIM), lambda b, n: (0, b, n, 0))
    return pl.pallas_call(
        _dilated_merge_kernel,
        grid=(nb, seq // tq),
        in_specs=[spec] * 6,
        out_specs=pl.BlockSpec((None, tq, A_SLOTS * HEAD_DIM), lambda b, n: (b, n, 0)),
        out_shape=jax.ShapeDtypeStruct((nb, seq, A_SLOTS * HEAD_DIM), BF16),
        compiler_params=_params(("parallel", "parallel")),
        name="dilated_merge",
    )(*outs, *lses)


def _gelu_tanh(x):
    return 0.5 * x * (1.0 + jnp.tanh(math.sqrt(2.0 / math.pi) * (x + 0.044715 * (x * x * x))))


def _compress_kernel(x_ref, pe_ref, w1_ref, w2_ref, o_ref):
    by_token = _by_residue(x_ref[...], CMP_STRIDE)
    x = jnp.concatenate([by_token[r] for r in range(CMP_STRIDE)], axis=1).astype(F32)
    first = _dot((x + pe_ref[0:1, :]).astype(BF16), w1_ref[0])
    second = _dot((x + pe_ref[1:2, :]).astype(BF16), w1_ref[1])
    chunks = x.shape[0]
    hidden = first + pltpu.roll(second, chunks - 1, 0)
    o_ref[...] = _dot(_gelu_tanh(hidden).astype(BF16), w2_ref[...]).astype(o_ref.dtype)


def _compress(t, head0, pe, w1, w2, name):
    _, nb, seq, _ = t.shape
    nh = B_KV_HEADS
    chunks = seq // CMP_STRIDE
    width = CMP_STRIDE * HEAD_DIM
    ratio = CMP_LEN // CMP_STRIDE
    return pl.pallas_call(
        _compress_kernel,
        grid=(nh, nb),
        in_specs=[pl.BlockSpec((None, None, seq, HEAD_DIM), lambda h, b: (head0 + h, b, 0, 0)),
                  pl.BlockSpec((ratio, width), lambda h, b: (0, 0)),
                  pl.BlockSpec((ratio, width, HEAD_DIM), lambda h, b: (0, 0, 0)),
                  pl.BlockSpec((HEAD_DIM, HEAD_DIM), lambda h, b: (0, 0))],
        out_specs=pl.BlockSpec((None, None, chunks, HEAD_DIM), lambda h, b: (h, b, 0, 0)),
        out_shape=jax.ShapeDtypeStruct((nh, nb, chunks, HEAD_DIM), BF16),
        compiler_params=_params(("parallel", "parallel")),
        name=name,
    )(t, pe.reshape(ratio, width), w1.reshape(ratio, width, HEAD_DIM).astype(BF16), w2.astype(BF16))


def _group_q(q_refs, row0=0, rows=None):
    rows = q_refs[0].shape[0] if rows is None else rows
    return jnp.concatenate([r[row0:row0 + rows, :] for r in q_refs], axis=0)


def _group_q_specs(tq, head0):
    return [pl.BlockSpec((None, None, tq, HEAD_DIM),
                         functools.partial(lambda b, h, n, g: (head0 + h * B_GROUP + g, b, n, 0), g=g))
            for g in range(B_GROUP)]


def _cmp_select_kernel(q0_ref, q1_ref, q2_ref, kc_ref, vc_ref, ov_ref, o_ref, sel_ref, *, tq, tiles):
    for u in range(tiles):
        _cmp_select_tile((q0_ref, q1_ref, q2_ref), kc_ref, vc_ref, ov_ref, o_ref, sel_ref,
                         (pl.program_id(2) * tiles + u) * tq, u * tq, tq)


def _cmp_select_tile(q_refs, kc_ref, vc_ref, ov_ref, o_ref, sel_ref, t0, row0, tq):
    n_cmp = kc_ref.shape[0]
    rows = B_GROUP * tq
    s = _dot_nt(_group_q(q_refs, row0, tq), kc_ref[...])
    tpos3 = t0 + (lax.broadcasted_iota(jnp.int32, (rows, 1), 0) & (tq - 1))
    c_end = lax.broadcasted_iota(jnp.int32, (1, n_cmp), 1) * CMP_STRIDE + (CMP_LEN - 1)
    s = jnp.where(c_end - tpos3 <= 0, s, -jnp.inf)
    m = jnp.max(s, axis=1, keepdims=True)
    m = jnp.where(jnp.abs(m) < jnp.inf, m, 0.0)
    e = jnp.exp(s - m)
    den = jnp.sum(e, axis=1, keepdims=True)
    p = e / jnp.where(den > 0, den, 1.0)
    o = _dot(p.astype(BF16), vc_ref[...]).reshape(B_GROUP, tq, HEAD_DIM)
    o_ref[:, row0:row0 + tq, :] = o.astype(o_ref.dtype)

    p_sum = p[0:tq] + p[tq:2 * tq] + p[2 * tq:3 * tq]
    hi, lo = _split_bf16(p_sum)
    imp = _dot(hi, ov_ref[...]) + _dot(lo, ov_ref[...])
    tpos = t0 + lax.broadcasted_iota(jnp.int32, (tq, 1), 0)
    rel = lax.broadcasted_iota(jnp.int32, (1, HEAD_DIM), 1) - (tpos >> SEL_SHIFT)
    j_abs = jnp.broadcast_to(lax.broadcasted_iota(jnp.int32, (1, HEAD_DIM), 1), rel.shape)
    forced = (j_abs == 0) | (rel == 0) | (rel == -1)
    imp = jnp.where(forced, jnp.inf, jnp.where(rel <= 0, imp, -jnp.inf))

    n_sel = HEAD_DIM // 2
    imp_t = imp.T
    mine = imp_t[0:n_sel]
    j_idx = lax.broadcasted_iota(jnp.int32, (n_sel, tq), 0)
    beaten = jnp.zeros((n_sel, tq), F32)
    for kk in range(n_sel):
        other = imp_t[kk:kk + 1, :]
        wins = (other > mine) | ((other == mine) & (j_idx > kk))
        beaten = beaten + jnp.where(wins, 1.0, 0.0)
    chosen = jnp.where(beaten < SEL_TOPK, 0.0, MASKED)
    chosen = jnp.concatenate([chosen, jnp.full((HEAD_DIM - n_sel, tq), MASKED, F32)], axis=0)
    sel_ref[:, row0:row0 + tq] = chosen.astype(sel_ref.dtype)


def _overlap_matrix(n_cmp_rows, n_sel):
    c_start = np.arange(n_cmp_rows) * CMP_STRIDE
    c_end = c_start + CMP_LEN - 1
    s_start = np.arange(HEAD_DIM) * SEL_LEN
    ov = (c_start[:, None] <= s_start[None, :] + SEL_LEN - 1) & (c_end[:, None] >= s_start[None, :])
    ov &= (np.arange(HEAD_DIM) < n_sel)[None, :]
    return jnp.asarray(ov.astype(np.float32), dtype=BF16)


def _group_o_spec(tq):
    return pl.BlockSpec((B_GROUP, None, tq, HEAD_DIM), lambda b, h, n: (h, b, n, 0))


def _sel_spec(tq):
    return pl.BlockSpec((None, None, HEAD_DIM, tq), lambda b, h, n: (h, b, 0, n))


def _cmp_select(q, q_head0, kc, vc):
    _, nb, seq, _ = q.shape
    assert seq // SEL_LEN <= HEAD_DIM // 2
    tq = min(ATT_TQ, seq)
    tiles = min(ATT_TILES, seq // tq)
    n_cmp = kc.shape[2]
    kv_spec = pl.BlockSpec((None, None, n_cmp, HEAD_DIM), lambda b, h, n: (h, b, 0, 0))
    return pl.pallas_call(
        functools.partial(_cmp_select_kernel, tq=tq, tiles=tiles),
        grid=(nb, B_KV_HEADS, seq // (tq * tiles)),
        in_specs=_group_q_specs(tq * tiles, q_head0) + [kv_spec, kv_spec,
                                                        pl.BlockSpec((n_cmp, HEAD_DIM), lambda b, h, n: (0, 0))],
        out_specs=[_group_o_spec(tq * tiles), _sel_spec(tq * tiles)],
        out_shape=[jax.ShapeDtypeStruct((B_HEADS, nb, seq, HEAD_DIM), BF16),
                   jax.ShapeDtypeStruct((B_KV_HEADS, nb, HEAD_DIM, seq), BF16)],
        compiler_params=_params(("parallel", "parallel", "arbitrary")),
        name="nsa_compressed_select",
    )(q, q, q, kc, vc, _overlap_matrix(n_cmp, seq // SEL_LEN))


def _selected_kernel(q0_ref, q1_ref, q2_ref, k_ref, v_ref, sel_ref, o_ref, vt_ref, acc_ref, *, tq, kc, tiles):
    n = pl.program_id(2)
    t_first = n * (tq * tiles)
    cols = B_GROUP * tq

    @pl.when(n == 0)
    def _():
        for c in range(v_ref.shape[0] // kc):
            vt_ref[c] = v_ref[c * kc:(c + 1) * kc, :].astype(F32).T.astype(BF16)

    qs = [_group_q((q0_ref, q1_ref, q2_ref), u * tq, tq) for u in range(tiles)]
    qposs = [t_first + u * tq + lax.broadcasted_iota(jnp.int32, (1, tq), 1) for u in range(tiles)]
    key_in_chunk = lax.broadcasted_iota(jnp.int32, (kc, 1), 0)
    blocks = kc // SEL_LEN
    acc_ref[...] = jnp.zeros(acc_ref.shape, F32)

    def chunk(i, carry, diagonal):
        k0 = pl.multiple_of(i * kc, kc)
        kpos = k0 + key_in_chunk
        keys = k_ref[pl.ds(k0, kc), :]
        blk0 = pl.multiple_of(i * blocks, blocks)
        out = []
        for u in range(tiles):
            m_old, l_old = carry[u]
            rows = sel_ref[pl.ds(blk0, blocks), u * tq:(u + 1) * tq].astype(F32)
            bias = jnp.broadcast_to(rows[:, None, :], (blocks, SEL_LEN, tq)).reshape(kc, tq)
            if diagonal:
                bias = jnp.where(kpos - qposs[u] <= 0, bias, MASKED)
            s = _dot_nt(keys, qs[u]) + jnp.concatenate([bias] * B_GROUP, axis=1)
            m_new = jnp.maximum(m_old, jnp.max(s, axis=0, keepdims=True))
            p = jnp.exp(s - m_new)
            alpha = jnp.exp(m_old - m_new)
            acc_ref[u] = alpha * acc_ref[u] + _dot(vt_ref[i], p.astype(BF16))
            out.append((m_new, alpha * l_old + jnp.sum(p, axis=0, keepdims=True)))
        return tuple(out)

    init = tuple((jnp.full((1, cols), MASKED, F32), jnp.zeros((1, cols), F32)) for _ in range(tiles))
    last = t_first // kc
    carry = lax.fori_loop(0, last, lambda i, c: chunk(i, c, False), init)
    final = chunk(last, carry, True)
    for u in range(tiles):
        o_t = acc_ref[u] / final[u][1]
        for g in range(B_GROUP):
            o_ref[g, u * tq:(u + 1) * tq, :] = o_t[:, g * tq:(g + 1) * tq].T.astype(o_ref.dtype)


def _selected(q, q_head0, k, k_head0, v, v_head0, sel):
    _, nb, seq, _ = q.shape
    tq = min(SEL_TQ, seq)
    kc = min(SEL_CHUNK, seq)
    tiles = min(SEL_TILES, kc // tq)
    assert kc % (tq * tiles) == 0 and seq % kc == 0 and tq == HEAD_DIM
    return pl.pallas_call(
        functools.partial(_selected_kernel, tq=tq, kc=kc, tiles=tiles),
        grid=(nb, B_KV_HEADS, seq // (tq * tiles)),
        in_specs=_group_q_specs(tq * tiles, q_head0) + [_seq_spec(seq, k_head0), _seq_spec(seq, v_head0),
                                                        _sel_spec(tq * tiles)],
        out_specs=_group_o_spec(tq * tiles),
        out_shape=jax.ShapeDtypeStruct((B_HEADS, nb, seq, HEAD_DIM), BF16),
        scratch_shapes=[pltpu.VMEM((seq // kc, HEAD_DIM, kc), BF16),
                        pltpu.VMEM((tiles, HEAD_DIM, B_GROUP * tq), F32)],
        compiler_params=_params(("parallel", "parallel", "arbitrary")),
        name="nsa_selected",
    )(q, q, q, k, v, sel)


def _window_kernel(q0_ref, q1_ref, q2_ref, k_ref, v_ref, o_ref, *, tq, tiles, seq):
    rows = B_GROUP * tq
    span = min(WIN_LEN + tq, seq)
    for u in range(tiles):
        t0 = (pl.program_id(2) * tiles + u) * tq
        start = _window_start(t0, WIN_LEN, span, seq, tq)
        s = _dot_nt(_group_q((q0_ref, q1_ref, q2_ref), u * tq, tq), k_ref[pl.ds(start, span), :])
        qpos = t0 + (lax.broadcasted_iota(jnp.int32, (rows, 1), 0) & (tq - 1))
        kpos = start + lax.broadcasted_iota(jnp.int32, (1, span), 1)
        dist = qpos - kpos
        s = jnp.where((dist >= 0) & (dist <= WIN_LEN - 1), s, -jnp.inf)
        m = jnp.max(s, axis=1, keepdims=True)
        p = jnp.exp(s - m)
        l = jnp.sum(p, axis=1, keepdims=True)
        o = _dot(p.astype(BF16), v_ref[pl.ds(start, span), :]) / l
        o_ref[:, u * tq:(u + 1) * tq, :] = o.reshape(B_GROUP, tq, HEAD_DIM).astype(o_ref.dtype)


def _window(q, q_head0, k, k_head0, v, v_head0):
    _, nb, seq, _ = q.shape
    tq = min(ATT_TQ, seq)
    tiles = min(ATT_TILES, seq // tq)
    assert WIN_LEN % tq == 0
    return pl.pallas_call(
        functools.partial(_window_kernel, tq=tq, tiles=tiles, seq=seq),
        grid=(nb, B_KV_HEADS, seq // (tq * tiles)),
        in_specs=_group_q_specs(tq * tiles, q_head0) + [_seq_spec(seq, k_head0), _seq_spec(seq, v_head0)],
        out_specs=_group_o_spec(tq * tiles),
        out_shape=jax.ShapeDtypeStruct((B_HEADS, nb, seq, HEAD_DIM), BF16),
        compiler_params=_params(("parallel", "parallel", "arbitrary")),
        name="nsa_window",
    )(q, q, q, k, v)


def _nsa_gate_kernel(g_ref, oc_ref, os_ref, ow_ref, y_ref):
    g = g_ref[...]
    for h in range(B_HEADS):
        y = (g[:, 3 * h:3 * h + 1] * oc_ref[h].astype(F32)
             + g[:, 3 * h + 1:3 * h + 2] * os_ref[h].astype(F32)
             + g[:, 3 * h + 2:3 * h + 3] * ow_ref[h].astype(F32))
        y_ref[:, h * HEAD_DIM:(h + 1) * HEAD_DIM] = y.astype(y_ref.dtype)


def _nsa_gate(gates, o_cmp, o_sel, o_win):
    rows = gates.shape[0]
    tm = min(256, rows)
    o_spec = pl.BlockSpec((B_HEADS, tm, HEAD_DIM), lambda i: (0, i, 0))
    return pl.pallas_call(
        _nsa_gate_kernel,
        grid=(rows // tm,),
        in_specs=[pl.BlockSpec((tm, HEAD_DIM), lambda i: (i, 0)), o_spec, o_spec, o_spec],
        out_specs=pl.BlockSpec((tm, B_HEADS * HEAD_DIM), lambda i: (i, 0)),
        out_shape=jax.ShapeDtypeStruct((rows, B_HEADS * HEAD_DIM), BF16),
        compiler_params=_params(("parallel",)),
        name="nsa_gate",
    )(gates, o_cmp, o_sel, o_win)


def _stick_kernel(*refs, tq, heads):
    q_refs, k_refs, v_refs = refs[:heads], refs[heads:2 * heads], refs[2 * heads:3 * heads]
    o_ref, acc_ref, car_ref = refs[3 * heads:]
    n = pl.program_id(2)
    sub = min(CUM_TILE, tq)
    n_sub = tq // sub
    r_idx = lax.broadcasted_iota(jnp.int32, (2 * sub, 2 * sub), 0) & (sub - 1)
    c_idx = lax.broadcasted_iota(jnp.int32, (2 * sub, 2 * sub), 1)
    suffix_and_total = jnp.where((c_idx >= sub) | (r_idx > c_idx), 1.0, 0.0).astype(BF16)
    q_in_tile = lax.broadcasted_iota(jnp.int32, (tq, 1), 0)
    k_in_sub = lax.broadcasted_iota(jnp.int32, (1, sub), 1)
    acc_ref[...] = jnp.zeros(acc_ref.shape, F32)
    car_ref[...] = jnp.zeros(car_ref.shape, F32)

    def tile(k0, diagonal):
        for h in range(heads):
            z = _dot_nt(q_refs[h][...], k_refs[h][pl.ds(k0, tq), :])
            soft = jnp.log(1.0 + jnp.exp(-jnp.abs(z)))
            log_beta = jnp.minimum(z, 0.0) - soft
            log_rest = log_beta - z
            carried = car_ref[h]
            parts = [None] * n_sub
            for u in reversed(range(n_sub)):
                cols = slice(u * sub, (u + 1) * sub)
                rest_u = log_rest[:, cols]
                if diagonal:
                    before = (u * sub + k_in_sub) - q_in_tile < 0
                    rest_u = jnp.where(before, rest_u, 0.0)
                sums = _dot(jnp.concatenate(_split_bf16(rest_u), axis=1), suffix_and_total)
                a = jnp.exp(log_beta[:, cols] + (sums[:, :sub] + carried))
                if diagonal:
                    a = jnp.where(before, a, 0.0)
                carried = carried + sums[:, sub:]
                parts[u] = a.astype(BF16)
            car_ref[h] = carried
            acc_ref[h] += _dot(jnp.concatenate(parts, axis=1), v_refs[h][pl.ds(k0, tq), :])

    tile(pl.multiple_of(n * tq, tq), True)

    def step(i, carry):
        tile(pl.multiple_of((n - 1 - i) * tq, tq), False)
        return carry

    lax.fori_loop(0, n, step, 0)
    for h in range(heads):
        o_ref[:, h * HEAD_DIM:(h + 1) * HEAD_DIM] = acc_ref[h].astype(o_ref.dtype)


def _stick_breaking(q, q_head0, k, k_head0, v, v_head0):
    _, nb, seq, _ = q.shape
    tq = min(STICK_TILE, seq)
    heads = STICK_HEADS
    sub = min(CUM_TILE, tq)

    def per_head(make, head0):
        return [make(head0 + h) for h in range(heads)]

    def q_spec(head):
        return pl.BlockSpec((None, None, tq, HEAD_DIM), lambda b, g, n: (head + g * heads, b, n, 0))

    def kv_spec(head):
        return pl.BlockSpec((None, None, seq, HEAD_DIM), lambda b, g, n: (head + g * heads, b, 0, 0))

    return pl.pallas_call(
        functools.partial(_stick_kernel, tq=tq, heads=heads),
        grid=(nb, C_HEADS // heads, seq // tq),
        in_specs=per_head(q_spec, q_head0) + per_head(kv_spec, k_head0) + per_head(kv_spec, v_head0),
        out_specs=pl.BlockSpec((None, tq, heads * HEAD_DIM), lambda b, g, n: (b, n, g)),
        out_shape=jax.ShapeDtypeStruct((nb, seq, C_HEADS * HEAD_DIM), BF16),
        scratch_shapes=[pltpu.VMEM((heads, tq, HEAD_DIM), F32), pltpu.VMEM((heads, tq, sub), F32)],
        compiler_params=_params(("parallel", "parallel", "arbitrary")),
        name="stick_breaking",
    )(*([q] * heads + [k] * heads + [v] * heads))


def _merge_kernel(ya_ref, yb_ref, yc_ref, wa_ref, wb_ref, wc_ref, ga_ref, gb_ref, gc_ref, o_ref):
    mixed = (ga_ref[...].astype(F32) * _dot(ya_ref[...], wa_ref[...])
             + gb_ref[...].astype(F32) * _dot(yb_ref[...], wb_ref[...])
             + gc_ref[...].astype(F32) * _dot(yc_ref[...], wc_ref[...]))
    o_ref[...] = mixed.astype(o_ref.dtype)


def _merge(ya, yb, yc, wa, wb, wc, gates, d_model):
    m = ya.shape[0]
    tm, tn = min(512, m), 1024
    col_tiles = d_model // tn

    def y_spec(y):
        return pl.BlockSpec((tm, y.shape[1]), lambda i, j: (i, 0))

    def w_spec(w):
        return pl.BlockSpec((w.shape[0], tn), lambda i, j: (0, j))

    def g_spec(branch):
        return pl.BlockSpec((tm, tn), lambda i, j: (i, branch * col_tiles + j))

    return pl.pallas_call(
        _merge_kernel,
        grid=(m // tm, col_tiles),
        in_specs=[y_spec(ya), y_spec(yb), y_spec(yc), w_spec(wa), w_spec(wb), w_spec(wc),
                  g_spec(0), g_spec(1), g_spec(2)],
        out_specs=pl.BlockSpec((tm, tn), lambda i, j: (i, j)),
        out_shape=jax.ShapeDtypeStruct((m, d_model), BF16),
        compiler_params=_params(("parallel", "arbitrary")),
        name="branch_merge",
    )(ya, yb, yc, wa, wb, wc, gates, gates, gates)


def _ln_kernel(x_ref, y_ref, g_ref, b_ref, o_ref, ob_ref):
    z = ALPHA * x_ref[...] + y_ref[...]
    mu = jnp.mean(z, axis=1, keepdims=True)
    zc = z - mu
    var = jnp.mean(zc * zc, axis=1, keepdims=True)
    out = zc * lax.rsqrt(var + LN_EPS) * g_ref[...] + b_ref[...]
    o_ref[...] = out
    ob_ref[...] = out.astype(BF16)


def _residual_ln(x, y, g, b):
    m, d = x.shape
    tm = min(256, m)
    row = pl.BlockSpec((tm, d), lambda i: (i, 0))
    vec = pl.BlockSpec((1, d), lambda i: (0, 0))
    return pl.pallas_call(
        _ln_kernel,
        grid=(m // tm,),
        in_specs=[row, row, vec, vec],
        out_specs=[row, row],
        out_shape=[jax.ShapeDtypeStruct((m, d), F32), jax.ShapeDtypeStruct((m, d), BF16)],
        compiler_params=_params(("parallel",)),
        name="residual_layer_norm",
    )(x, y, g.reshape(1, d), b.reshape(1, d))


def _rope_tables(seq):
    half = HEAD_DIM // 2
    inv = 1.0 / (ROPE_THETA ** (jnp.arange(half, dtype=F32) / half))
    ang = jnp.arange(seq).astype(F32)[:, None] * inv[None, :]
    cos, sin = jnp.cos(ang), jnp.sin(ang)
    return jnp.concatenate([cos, cos], axis=1), jnp.concatenate([-sin, sin], axis=1)


_A_KV, _B_KV = A_SLOTS, B_KV_HEADS
_ALIGNED_FIELDS = (('q_a', A_HEADS, 'rope_scale'), ('k_a', _A_KV, 'rope'), ('v_a', _A_KV, 'none'),
                   ('q_b', B_HEADS, 'rope_scale'), ('kc_b', _B_KV, 'rope'), ('vc_b', _B_KV, 'none'),
                   ('ks_b', _B_KV, 'rope'), ('vs_b', _B_KV, 'none'), ('kw_b', _B_KV, 'rope'),
                   ('vw_b', _B_KV, 'none'))
_STICK_FIELDS = (('q_c', C_HEADS, 'scale'), ('k_c', C_HEADS, 'none'), ('v_c', C_HEADS, 'none'))
N_NSA_GATES = 3 * B_HEADS


def _field_layout(fields, heads_per_tile):
    head0, modes, start = {}, [], 0
    for name, heads, mode in fields:
        assert heads % heads_per_tile == 0
        head0[name] = start
        modes += [mode] * (heads // heads_per_tile)
        start += heads
    return head0, tuple(modes), start


def _layer(layer, x, xb, nb, seq, tables, w_rows, cmp_pe_k, cmp_wk1, cmp_wk2, cmp_pe_v, cmp_wv1, cmp_wv2,
           w_br_a, w_br_b, w_br_c, w_out, ln1_g, ln1_b, w_up, w_down, ln2_g, ln2_b):
    m, d_model = x.shape

    tn_att = 4 * HEAD_DIM
    at, att_modes, att_heads = _field_layout(_ALIGNED_FIELDS, tn_att // HEAD_DIM)
    att = _matmul(xb, w_rows, layer=layer, b_rows=True, name="proj_attention", out_dtype=BF16,
                  tile_modes=att_modes, tm=1024, tn=tn_att, head_major=True, rope_tables=tables)
    tn_tail = 1024
    st, stick_modes, stick_heads = _field_layout(_STICK_FIELDS, tn_tail // HEAD_DIM)
    stick_col0 = att_heads * HEAD_DIM
    gates_col0 = stick_col0 + stick_heads * HEAD_DIM
    nsa_col0 = gates_col0 + 3 * d_model
    stick = _matmul(xb, w_rows, layer=layer, b_rows=True, col0=stick_col0, name="proj_stick",
                    out_dtype=BF16, tile_modes=stick_modes, tm=1024, tn=tn_tail, head_major=True)
    gates = _matmul(xb, w_rows, layer=layer, b_rows=True, col0=gates_col0, name="proj_branch_gates",
                    out_dtype=BF16, tile_modes=('sigmoid',) * (3 * d_model // tn_tail), tm=1024, tn=tn_tail)
    g_nsa = _matmul(xb, w_rows, layer=layer, b_rows=True, col0=nsa_col0, name="proj_nsa_gates",
                    out_dtype=F32, tile_modes=('sigmoid',), tm=1024, tn=HEAD_DIM)

    att = att.reshape(att_heads, nb, seq, HEAD_DIM)
    stick = stick.reshape(stick_heads, nb, seq, HEAD_DIM)

    y_a = _dilated_mixer(att, at['q_a'], at['k_a'], at['v_a'])

    kc = _compress(att, at['kc_b'], cmp_pe_k, cmp_wk1, cmp_wk2, "nsa_compress_k")
    vc = _compress(att, at['vc_b'], cmp_pe_v, cmp_wv1, cmp_wv2, "nsa_compress_v")
    o_cmp, sel = _cmp_select(att, at['q_b'], kc, vc)
    o_sel = _selected(att, at['q_b'], att, at['ks_b'], att, at['vs_b'], sel)
    o_win = _window(att, at['q_b'], att, at['kw_b'], att, at['vw_b'])
    y_b = _nsa_gate(g_nsa, o_cmp.reshape(B_HEADS, m, HEAD_DIM), o_sel.reshape(B_HEADS, m, HEAD_DIM),
                    o_win.reshape(B_HEADS, m, HEAD_DIM))

    y_c = _stick_breaking(stick, st['q_c'], stick, st['k_c'], stick, st['v_c'])

    merged = _merge(y_a.reshape(m, -1), y_b, y_c.reshape(m, -1), w_br_a.astype(BF16),
                    w_br_b.astype(BF16), w_br_c.astype(BF16), gates, d_model)
    mixed = _matmul(merged, w_out, layer=layer, name="out_proj", out_dtype=F32,
                    tile_modes=('none',) * (d_model // 512), tm=1024, tn=512)
    x1, x1b = _residual_ln(x, mixed, ln1_g, ln1_b)

    d_ff = w_up.shape[2]
    hidden = _matmul(x1b, w_up, layer=layer, name="mlp_up", out_dtype=BF16,
                     tile_modes=('relu2',) * (d_ff // 512), tm=1024, tn=512)
    down = _matmul(hidden, w_down, layer=layer, name="mlp_down", out_dtype=F32,
                   tile_modes=('none',) * (d_model // 1024), tm=2048, tn=1024, tk=1024)
    return _residual_ln(x1, down, ln2_g, ln2_b)


def kernel(x, w_in, cmp_pe_k, cmp_wk1, cmp_wk2, cmp_pe_v, cmp_wv1, cmp_wv2, w_br_a, w_br_b, w_br_c,
           w_out, ln1_g, ln1_b, w_up, w_down, ln2_g, ln2_b):
    nb, seq, d_model = x.shape
    tables = _rope_tables(seq)
    xf = x.reshape(nb * seq, d_model)
    xb = xf.astype(BF16)
    aligned_cols = sum(heads for _, heads, _ in _ALIGNED_FIELDS) * HEAD_DIM
    w_rows = _input_weight_rows(w_in, aligned_cols, N_NSA_GATES)
    for l in range(w_in.shape[0]):
        xf, xb = _layer(l, xf, xb, nb, seq, tables, w_rows, cmp_pe_k[l], cmp_wk1[l], cmp_wk2[l],
                        cmp_pe_v[l], cmp_wv1[l], cmp_wv2[l], w_br_a[l], w_br_b[l], w_br_c[l],
                        w_out, ln1_g[l], ln1_b[l], w_up, w_down, ln2_g[l], ln2_b[l])
    return xf.reshape(nb, seq, d_model)
```

```python
import functools
import math

import numpy as np
import jax
import jax.numpy as jnp
from jax import lax
from jax.experimental import pallas as pl
from jax.experimental.pallas import tpu as pltpu

F32 = jnp.float32
BF16 = jnp.bfloat16

HEAD_DIM = 128
ROPE_THETA = 10000.0
LN_EPS = 1e-5
DEPTH = 2

DIL_GROUPS = ((128, 1), (512, 4), (2048, 16))
A_SLOTS = 4
A_HEADS = A_SLOTS * len(DIL_GROUPS)

B_HEADS = 12
B_KV_HEADS = 4
B_GROUP = B_HEADS // B_KV_HEADS
CMP_LEN = 32
CMP_STRIDE = 16
SEL_LEN = 64
SEL_SHIFT = 6
SEL_TOPK = 16
WIN_LEN = 512

C_HEADS = 8

MASKED = -3e38
ALPHA = (2.0 * DEPTH) ** 0.25
Q_SCALE = HEAD_DIM ** -0.5

VMEM_LIMIT = 48 * 1024 * 1024
MXU_COLS = 256
ATT_TQ = 128
ATT_TILES = 8
DIL_ROWS = 128
DIL_ROWS_DENSE = 512
SEL_TQ = 128
SEL_TILES = 8
SEL_CHUNK = 1024
STICK_TILE = 512
STICK_HEADS = 4
CUM_TILE = 128


def _params(semantics):
    return pltpu.CompilerParams(dimension_semantics=semantics, vmem_limit_bytes=VMEM_LIMIT)


def _dot(a, b):
    return jnp.dot(a, b, preferred_element_type=F32)


def _dot_nt(a, b):
    return lax.dot_general(a, b, (((1,), (1,)), ((), ())), preferred_element_type=F32)


def _split_bf16(x):
    hi = x.astype(BF16)
    lo = (x - hi.astype(F32)).astype(BF16)
    return hi, lo


def _tile_flag(tile_modes, wanted):
    hits = [mode in wanted for mode in tile_modes]
    if all(hits) or not any(hits):
        return hits[0]
    j = pl.program_id(1)
    flag = None
    for c, hit in enumerate(hits):
        if hit:
            flag = (j == c) if flag is None else (flag | (j == c))
    return flag


def _write_columns(acc, o_ref, col0, tile_modes, head_major, cos_ref, sin_ref):
    is_rope = _tile_flag(tile_modes, ('rope', 'rope_scale'))
    is_scaled = _tile_flag(tile_modes, ('rope_scale', 'scale'))
    uniform = tile_modes[0] if len(set(tile_modes)) == 1 else None
    assert uniform is not None or not (set(tile_modes) & {'sigmoid', 'relu2'})

    def epilogue(y):
        if is_rope is not False:
            roped = y * cos_ref[...] + pltpu.roll(y, HEAD_DIM // 2, 1) * sin_ref[...]
            y = roped if is_rope is True else jnp.where(is_rope, roped, y)
        if is_scaled is not False:
            y = y * (Q_SCALE if is_scaled is True else jnp.where(is_scaled, Q_SCALE, 1.0))
        if uniform == 'sigmoid':
            y = jax.nn.sigmoid(y)
        if uniform == 'relu2':
            y = jnp.square(jnp.maximum(y, 0.0))
        return y

    width = acc.shape[1]
    if head_major:
        for c in range(width // HEAD_DIM):
            y = epilogue(acc[:, c * HEAD_DIM:(c + 1) * HEAD_DIM])
            o_ref[col0 // HEAD_DIM + c] = y.astype(o_ref.dtype)
    else:
        o_ref[:, col0:col0 + width] = epilogue(acc).astype(o_ref.dtype)


def _matmul_kernel(*refs, tile_modes, head_major, n_k, use_rope, acc_in_out, b_rows):
    refs = list(refs)
    a_ref, b_ref = refs[0], refs[1]
    cos_ref, sin_ref = (refs[2], refs[3]) if use_rope else (None, None)
    o_ref = refs[4] if use_rope else refs[2]
    acc_ref = o_ref if acc_in_out else (refs[-1] if n_k > 1 else None)
    tn = b_ref.shape[0] if b_rows else b_ref.shape[1]

    def product(col0, width):
        if b_rows:
            return _dot_nt(a_ref[...], b_ref[col0:col0 + width, :].astype(BF16))
        return _dot(a_ref[...], b_ref[:, col0:col0 + width].astype(BF16))

    if n_k == 1:
        width = min(MXU_COLS, tn)
        for col0 in range(0, tn, width):
            _write_columns(product(col0, width), o_ref, col0, tile_modes, head_major, cos_ref, sin_ref)
        return

    @pl.when(pl.program_id(2) == 0)
    def _():
        acc_ref[...] = jnp.zeros(acc_ref.shape, F32)

    acc_ref[...] += product(0, tn)
    if not acc_in_out:
        pl.when(pl.program_id(2) == n_k - 1)(
            lambda: _write_columns(acc_ref[...], o_ref, 0, tile_modes, head_major, cos_ref, sin_ref))


def _matmul(a, b, *, name, out_dtype, tile_modes, tm, tn, tk=None, layer=None, col0=0,
            head_major=False, rope_tables=None, b_rows=False):
    m, kdim = a.shape
    n = tn * len(tile_modes)
    tm = min(tm, m)
    tk = kdim if tk is None else min(tk, kdim)
    assert m % tm == 0 and kdim % tk == 0 and col0 % tn == 0
    n_k = kdim // tk
    col_tile0 = col0 // tn
    use_rope = any(mode.startswith('rope') for mode in tile_modes)
    acc_in_out = n_k > 1 and out_dtype == F32 and set(tile_modes) == {'none'} and not head_major
    if b_rows:
        assert layer is not None
        b_spec = pl.BlockSpec((None, tn, tk), lambda i, j, k: (layer, col_tile0 + j, k))
    elif layer is None:
        b_spec = pl.BlockSpec((tk, tn), lambda i, j, k: (k, col_tile0 + j))
    else:
        b_spec = pl.BlockSpec((None, tk, tn), lambda i, j, k: (layer, k, col_tile0 + j))
    in_specs = [pl.BlockSpec((tm, tk), lambda i, j, k: (i, k)), b_spec]
    operands = [a, b]
    if use_rope:
        assert head_major
        cos, sin = rope_tables
        seq_tiles = cos.shape[0] // tm
        assert cos.shape[0] % tm == 0
        spec = pl.BlockSpec((tm, HEAD_DIM), lambda i, j, k: (i % seq_tiles, 0))
        in_specs += [spec, spec]
        operands += [cos, sin]
    if head_major:
        out_shape = jax.ShapeDtypeStruct((n // HEAD_DIM, m, HEAD_DIM), out_dtype)
        out_spec = pl.BlockSpec((tn // HEAD_DIM, tm, HEAD_DIM), lambda i, j, k: (j, i, 0))
    else:
        out_shape = jax.ShapeDtypeStruct((m, n), out_dtype)
        out_spec = pl.BlockSpec((tm, tn), lambda i, j, k: (i, j))
    scratch = [pltpu.VMEM((tm, tn), F32)] if (n_k > 1 and not acc_in_out) else []
    return pl.pallas_call(
        functools.partial(_matmul_kernel, tile_modes=tuple(tile_modes), head_major=head_major, n_k=n_k,
                          use_rope=use_rope, acc_in_out=acc_in_out, b_rows=b_rows),
        grid=(m // tm, len(tile_modes), n_k),
        in_specs=in_specs,
        out_specs=out_spec,
        out_shape=out_shape,
        scratch_shapes=scratch,
        compiler_params=_params(("parallel", "parallel", "arbitrary")),
        name=name,
    )(*operands)


def _gather_rows_kernel(w_ref, o_ref, *, tn, k_chunks, n_layers):
    stride = k_chunks * n_layers
    by_chunk = pltpu.einshape("nck->cnk", w_ref[...].reshape(tn, stride, HEAD_DIM))
    for layer in range(n_layers):
        pieces = [by_chunk[c * n_layers + layer] for c in range(k_chunks)]
        o_ref[layer] = jnp.concatenate(pieces, axis=1).astype(o_ref.dtype)


def _input_weight_rows(w_in, aligned_cols, n_gates, tn=HEAD_DIM):
    n_layers, kdim, n = w_in.shape
    k_chunks = kdim // HEAD_DIM
    tail = n - aligned_cols - n_gates
    assert aligned_cols % tn == 0 and tail % tn == 0 and kdim % HEAD_DIM == 0
    aligned_tiles, tail_tiles = aligned_cols // tn, tail // tn
    rows_per_col = k_chunks * n_layers
    flat = w_in.reshape(n_layers, k_chunks, HEAD_DIM, n).transpose(3, 1, 0, 2).reshape(n * rows_per_col, HEAD_DIM)

    def source_row(j):
        col = jnp.where(j < aligned_tiles, tn * j,
                        jnp.where(j < aligned_tiles + tail_tiles,
                                  aligned_cols + n_gates + tn * (j - aligned_tiles), aligned_cols))
        return (col * rows_per_col, 0)

    tiles = aligned_tiles + tail_tiles + 1
    return pl.pallas_call(
        functools.partial(_gather_rows_kernel, tn=tn, k_chunks=k_chunks, n_layers=n_layers),
        grid=(tiles,),
        in_specs=[pl.BlockSpec((pl.Element(tn * rows_per_col), pl.Element(HEAD_DIM)), source_row)],
        out_specs=pl.BlockSpec((n_layers, tn, kdim), lambda j: (0, j, 0)),
        out_shape=jax.ShapeDtypeStruct((n_layers, tiles * tn, kdim), BF16),
        compiler_params=_params(("parallel",)),
        name="input_weight_rows",
    )(flat)


def _window_start(t0, back, span, seq, align):
    start = jnp.minimum(jnp.maximum(t0 - back, 0), seq - span)
    return pl.multiple_of(start, align)


def _by_residue(x, dil):
    rows = x.shape[0] // dil
    return pltpu.einshape("ldk->dlk", x.reshape(rows, dil, HEAD_DIM))


def _dilated_group_kernel(q_ref, k_ref, v_ref, o_ref, lse_ref, *scratch, dil, rows, band, n_rows):
    j = pl.program_id(2)
    if dil > 1:
        kr_ref, vr_ref = scratch

        @pl.when(j == 0)
        def _():
            kr_ref[...] = _by_residue(k_ref[...], dil)
            vr_ref[...] = _by_residue(v_ref[...], dil)

        q_by = _by_residue(q_ref[...], dil)
    l0 = j * rows
    span = min(band + rows, n_rows)
    start = _window_start(l0, band, span, n_rows, math.gcd(rows, band))
    dist = (l0 + lax.broadcasted_iota(jnp.int32, (rows, 1), 0)) - (start + lax.broadcasted_iota(jnp.int32, (1, span), 1))
    keep = (dist >= 0) & (dist <= band)
    for r in range(dil):
        if dil == 1:
            q, keys, vals = q_ref[...], k_ref[pl.ds(start, span), :], v_ref[pl.ds(start, span), :]
        else:
            q, keys, vals = q_by[r], kr_ref[r, pl.ds(start, span), :], vr_ref[r, pl.ds(start, span), :]
        s = jnp.where(keep, _dot_nt(q, keys), -jnp.inf)
        m = jnp.max(s, axis=1, keepdims=True)
        p = jnp.exp(s - m)
        l = jnp.sum(p, axis=1, keepdims=True)
        o = _dot(p.astype(BF16), vals) / l
        lse = jnp.broadcast_to(m + jnp.log(l), (rows, HEAD_DIM))
        if dil == 1:
            o_ref[...] = o
            lse_ref[...] = lse
        else:
            o_ref[pl.ds(r, rows, stride=dil), :] = o
            lse_ref[pl.ds(r, rows, stride=dil), :] = lse


def _head_spec(rows, head0):
    return pl.BlockSpec((None, None, rows, HEAD_DIM), lambda b, h, n: (head0 + h, b, n, 0))


def _seq_spec(seq, head0):
    return pl.BlockSpec((None, None, seq, HEAD_DIM), lambda b, h, n: (head0 + h, b, 0, 0))


def _dilated_group(att, q_head0, k_head0, v_head0, window, dil):
    _, nb, seq, _ = att.shape
    n_rows = seq // dil
    band = window // dil
    rows = min(DIL_ROWS if dil > 1 else DIL_ROWS_DENSE, n_rows)
    assert seq % dil == 0 and n_rows % rows == 0
    out_spec = _head_spec(rows * dil, 0)
    out_shape = jax.ShapeDtypeStruct((A_SLOTS, nb, seq, HEAD_DIM), F32)
    by_residue = pltpu.VMEM((dil, n_rows, HEAD_DIM), BF16)
    return pl.pallas_call(
        functools.partial(_dilated_group_kernel, dil=dil, rows=rows, band=band, n_rows=n_rows),
        grid=(nb, A_SLOTS, n_rows // rows),
        in_specs=[_head_spec(rows * dil, q_head0), _seq_spec(seq, k_head0), _seq_spec(seq, v_head0)],
        out_specs=[out_spec, out_spec],
        out_shape=[out_shape, out_shape],
        scratch_shapes=[by_residue, by_residue] if dil > 1 else [],
        compiler_params=_params(("parallel", "parallel", "arbitrary")),
        name="dilated_group_%d" % dil,
    )(att, att, att)


def _dilated_merge_kernel(o0_ref, o1_ref, o2_ref, l0_ref, l1_ref, l2_ref, y_ref):
    for s in range(A_SLOTS):
        lses = (l0_ref[s], l1_ref[s], l2_ref[s])
        top = jnp.maximum(jnp.maximum(lses[0], lses[1]), lses[2])
        ws = [jnp.exp(lse - top) for lse in lses]
        y = (ws[0] * o0_ref[s] + ws[1] * o1_ref[s] + ws[2] * o2_ref[s]) / (ws[0] + ws[1] + ws[2])
        y_ref[:, s * HEAD_DIM:(s + 1) * HEAD_DIM] = y.astype(y_ref.dtype)


def _dilated_mixer(att, q_head0, k_head0, v_head0):
    _, nb, seq, _ = att.shape
    outs, lses = [], []
    for g, (window, dil) in enumerate(DIL_GROUPS):
        o, lse = _dilated_group(att, q_head0 + g * A_SLOTS, k_head0, v_head0, window, dil)
        outs.append(o)
        lses.append(lse)
    tq = min(256, seq)
    spec = pl.BlockSpec((A_SLOTS, None, tq, HEAD_DIM), lambda b, n: (0, b, n, 0))
    return pl.pallas_call(
        _dilated_merge_kernel,
        grid=(nb, seq // tq),
        in_specs=[spec] * 6,
        out_specs=pl.BlockSpec((None, tq, A_SLOTS * HEAD_DIM), lambda b, n: (b, n, 0)),
        out_shape=jax.ShapeDtypeStruct((nb, seq, A_SLOTS * HEAD_DIM), BF16),
        compiler_params=_params(("parallel", "parallel")),
        name="dilated_merge",
    )(*outs, *lses)


def _gelu_tanh(x):
    return 0.5 * x * (1.0 + jnp.tanh(math.sqrt(2.0 / math.pi) * (x + 0.044715 * (x * x * x))))


def _compress_kernel(x_ref, pe_ref, w1_ref, w2_ref, o_ref):
    by_token = _by_residue(x_ref[...], CMP_STRIDE)
    x = jnp.concatenate([by_token[r] for r in range(CMP_STRIDE)], axis=1).astype(F32)
    first = _dot((x + pe_ref[0:1, :]).astype(BF16), w1_ref[0])
    second = _dot((x + pe_ref[1:2, :]).astype(BF16), w1_ref[1])
    chunks = x.shape[0]
    hidden = first + pltpu.roll(second, chunks - 1, 0)
    o_ref[...] = _dot(_gelu_tanh(hidden).astype(BF16), w2_ref[...]).astype(o_ref.dtype)


def _compress(t, head0, pe, w1, w2, name):
    _, nb, seq, _ = t.shape
    nh = B_KV_HEADS
    chunks = seq // CMP_STRIDE
    width = CMP_STRIDE * HEAD_DIM
    ratio = CMP_LEN // CMP_STRIDE
    return pl.pallas_call(
        _compress_kernel,
        grid=(nh, nb),
        in_specs=[pl.BlockSpec((None, None, seq, HEAD_DIM), lambda h, b: (head0 + h, b, 0, 0)),
                  pl.BlockSpec((ratio, width), lambda h, b: (0, 0)),
                  pl.BlockSpec((ratio, width, HEAD_DIM), lambda h, b: (0, 0, 0)),
                  pl.BlockSpec((HEAD_DIM, HEAD_DIM), lambda h, b: (0, 0))],
        out_specs=pl.BlockSpec((None, None, chunks, HEAD_DIM), lambda h, b: (h, b, 0, 0)),
        out_shape=jax.ShapeDtypeStruct((nh, nb, chunks, HEAD_DIM), BF16),
        compiler_params=_params(("parallel", "parallel")),
        name=name,
    )(t, pe.reshape(ratio, width), w1.reshape(ratio, width, HEAD_DIM).astype(BF16), w2.astype(BF16))


def _group_q(q_refs, row0=0, rows=None):
    rows = q_refs[0].shape[0] if rows is None else rows
    return jnp.concatenate([r[row0:row0 + rows, :] for r in q_refs], axis=0)


def _group_q_specs(tq, head0):
    return [pl.BlockSpec((None, None, tq, HEAD_DIM),
                         functools.partial(lambda b, h, n, g: (head0 + h * B_GROUP + g, b, n, 0), g=g))
            for g in range(B_GROUP)]


def _cmp_select_kernel(q0_ref, q1_ref, q2_ref, kc_ref, vc_ref, ov_ref, o_ref, sel_ref, *, tq, tiles):
    for u in range(tiles):
        _cmp_select_tile((q0_ref, q1_ref, q2_ref), kc_ref, vc_ref, ov_ref, o_ref, sel_ref,
                         (pl.program_id(2) * tiles + u) * tq, u * tq, tq)


def _cmp_select_tile(q_refs, kc_ref, vc_ref, ov_ref, o_ref, sel_ref, t0, row0, tq):
    n_cmp = kc_ref.shape[0]
    rows = B_GROUP * tq
    s = _dot_nt(_group_q(q_refs, row0, tq), kc_ref[...])
    tpos3 = t0 + (lax.broadcasted_iota(jnp.int32, (rows, 1), 0) & (tq - 1))
    c_end = lax.broadcasted_iota(jnp.int32, (1, n_cmp), 1) * CMP_STRIDE + (CMP_LEN - 1)
    s = jnp.where(c_end - tpos3 <= 0, s, -jnp.inf)
    m = jnp.max(s, axis=1, keepdims=True)
    m = jnp.where(jnp.abs(m) < jnp.inf, m, 0.0)
    e = jnp.exp(s - m)
    den = jnp.sum(e, axis=1, keepdims=True)
    p = e / jnp.where(den > 0, den, 1.0)
    o = _dot(p.astype(BF16), vc_ref[...]).reshape(B_GROUP, tq, HEAD_DIM)
    o_ref[:, row0:row0 + tq, :] = o.astype(o_ref.dtype)

    p_sum = p[0:tq] + p[tq:2 * tq] + p[2 * tq:3 * tq]
    hi, lo = _split_bf16(p_sum)
    imp = _dot(hi, ov_ref[...]) + _dot(lo, ov_ref[...])
    tpos = t0 + lax.broadcasted_iota(jnp.int32, (tq, 1), 0)
    rel = lax.broadcasted_iota(jnp.int32, (1, HEAD_DIM), 1) - (tpos >> SEL_SHIFT)
    j_abs = jnp.broadcast_to(lax.broadcasted_iota(jnp.int32, (1, HEAD_DIM), 1), rel.shape)
    forced = (j_abs == 0) | (rel == 0) | (rel == -1)
    imp = jnp.where(forced, jnp.inf, jnp.where(rel <= 0, imp, -jnp.inf))

    n_sel = HEAD_DIM // 2
    imp_t = imp.T
    mine = imp_t[0:n_sel]
    j_idx = lax.broadcasted_iota(jnp.int32, (n_sel, tq), 0)
    beaten = jnp.zeros((n_sel, tq), F32)
    for kk in range(n_sel):
        other = imp_t[kk:kk + 1, :]
        wins = (other > mine) | ((other == mine) & (j_idx > kk))
        beaten = beaten + jnp.where(wins, 1.0, 0.0)
    chosen = jnp.where(beaten < SEL_TOPK, 0.0, MASKED)
    chosen = jnp.concatenate([chosen, jnp.full((HEAD_DIM - n_sel, tq), MASKED, F32)], axis=0)
    sel_ref[:, row0:row0 + tq] = chosen.astype(sel_ref.dtype)


def _overlap_matrix(n_cmp_rows, n_sel):
    c_start = np.arange(n_cmp_rows) * CMP_STRIDE
    c_end = c_start + CMP_LEN - 1
    s_start = np.arange(HEAD_DIM) * SEL_LEN
    ov = (c_start[:, None] <= s_start[None, :] + SEL_LEN - 1) & (c_end[:, None] >= s_start[None, :])
    ov &= (np.arange(HEAD_DIM) < n_sel)[None, :]
    return jnp.asarray(ov.astype(np.float32), dtype=BF16)


def _group_o_spec(tq):
    return pl.BlockSpec((B_GROUP, None, tq, HEAD_DIM), lambda b, h, n: (h, b, n, 0))


def _sel_spec(tq):
    return pl.BlockSpec((None, None, HEAD_DIM, tq), lambda b, h, n: (h, b, 0, n))


def _cmp_select(q, q_head0, kc, vc):
    _, nb, seq, _ = q.shape
    assert seq // SEL_LEN <= HEAD_DIM // 2
    tq = min(ATT_TQ, seq)
    tiles = min(ATT_TILES, seq // tq)
    n_cmp = kc.shape[2]
    kv_spec = pl.BlockSpec((None, None, n_cmp, HEAD_DIM), lambda b, h, n: (h, b, 0, 0))
    return pl.pallas_call(
        functools.partial(_cmp_select_kernel, tq=tq, tiles=tiles),
        grid=(nb, B_KV_HEADS, seq // (tq * tiles)),
        in_specs=_group_q_specs(tq * tiles, q_head0) + [kv_spec, kv_spec,
                                                        pl.BlockSpec((n_cmp, HEAD_DIM), lambda b, h, n: (0, 0))],
        out_specs=[_group_o_spec(tq * tiles), _sel_spec(tq * tiles)],
        out_shape=[jax.ShapeDtypeStruct((B_HEADS, nb, seq, HEAD_DIM), BF16),
                   jax.ShapeDtypeStruct((B_KV_HEADS, nb, HEAD_DIM, seq), BF16)],
        compiler_params=_params(("parallel", "parallel", "arbitrary")),
        name="nsa_compressed_select",
    )(q, q, q, kc, vc, _overlap_matrix(n_cmp, seq // SEL_LEN))


def _selected_kernel(q0_ref, q1_ref, q2_ref, k_ref, v_ref, sel_ref, o_ref, vt_ref, acc_ref, *, tq, kc, tiles):
    n = pl.program_id(2)
    t_first = n * (tq * tiles)
    cols = B_GROUP * tq

    @pl.when(n == 0)
    def _():
        for c in range(v_ref.shape[0] // kc):
            vt_ref[c] = v_ref[c * kc:(c + 1) * kc, :].astype(F32).T.astype(BF16)

    qs = [_group_q((q0_ref, q1_ref, q2_ref), u * tq, tq) for u in range(tiles)]
    qposs = [t_first + u * tq + lax.broadcasted_iota(jnp.int32, (1, tq), 1) for u in range(tiles)]
    key_in_chunk = lax.broadcasted_iota(jnp.int32, (kc, 1), 0)
    blocks = kc // SEL_LEN
    acc_ref[...] = jnp.zeros(acc_ref.shape, F32)

    def chunk(i, carry, diagonal):
        k0 = pl.multiple_of(i * kc, kc)
        kpos = k0 + key_in_chunk
        keys = k_ref[pl.ds(k0, kc), :]
        blk0 = pl.multiple_of(i * blocks, blocks)
        out = []
        for u in range(tiles):
            m_old, l_old = carry[u]
            rows = sel_ref[pl.ds(blk0, blocks), u * tq:(u + 1) * tq].astype(F32)
            bias = jnp.broadcast_to(rows[:, None, :], (blocks, SEL_LEN, tq)).reshape(kc, tq)
            if diagonal:
                bias = jnp.where(kpos - qposs[u] <= 0, bias, MASKED)
            s = _dot_nt(keys, qs[u]) + jnp.concatenate([bias] * B_GROUP, axis=1)
            m_new = jnp.maximum(m_old, jnp.max(s, axis=0, keepdims=True))
            p = jnp.exp(s - m_new)
            alpha = jnp.exp(m_old - m_new)
            acc_ref[u] = alpha * acc_ref[u] + _dot(vt_ref[i], p.astype(BF16))
            out.append((m_new, alpha * l_old + jnp.sum(p, axis=0, keepdims=True)))
        return tuple(out)

    init = tuple((jnp.full((1, cols), MASKED, F32), jnp.zeros((1, cols), F32)) for _ in range(tiles))
    last = t_first // kc
    carry = lax.fori_loop(0, last, lambda i, c: chunk(i, c, False), init)
    final = chunk(last, carry, True)
    for u in range(tiles):
        o_t = acc_ref[u] / final[u][1]
        for g in range(B_GROUP):
            o_ref[g, u * tq:(u + 1) * tq, :] = o_t[:, g * tq:(g + 1) * tq].T.astype(o_ref.dtype)


def _selected(q, q_head0, k, k_head0, v, v_head0, sel):
    _, nb, seq, _ = q.shape
    tq = min(SEL_TQ, seq)
    kc = min(SEL_CHUNK, seq)
    tiles = min(SEL_TILES, kc // tq)
    assert kc % (tq * tiles) == 0 and seq % kc == 0 and tq == HEAD_DIM
    return pl.pallas_call(
        functools.partial(_selected_kernel, tq=tq, kc=kc, tiles=tiles),
        grid=(nb, B_KV_HEADS, seq // (tq * tiles)),
        in_specs=_group_q_specs(tq * tiles, q_head0) + [_seq_spec(seq, k_head0), _seq_spec(seq, v_head0),
                                                        _sel_spec(tq * tiles)],
        out_specs=_group_o_spec(tq * tiles),
        out_shape=jax.ShapeDtypeStruct((B_HEADS, nb, seq, HEAD_DIM), BF16),
        scratch_shapes=[pltpu.VMEM((seq // kc, HEAD_DIM, kc), BF16),
                        pltpu.VMEM((tiles, HEAD_DIM, B_GROUP * tq), F32)],
        compiler_params=_params(("parallel", "parallel", "arbitrary")),
        name="nsa_selected",
    )(q, q, q, k, v, sel)


def _window_kernel(q0_ref, q1_ref, q2_ref, k_ref, v_ref, o_ref, *, tq, tiles, seq):
    rows = B_GROUP * tq
    span = min(WIN_LEN + tq, seq)
    for u in range(tiles):
        t0 = (pl.program_id(2) * tiles + u) * tq
        start = _window_start(t0, WIN_LEN, span, seq, tq)
        s = _dot_nt(_group_q((q0_ref, q1_ref, q2_ref), u * tq, tq), k_ref[pl.ds(start, span), :])
        qpos = t0 + (lax.broadcasted_iota(jnp.int32, (rows, 1), 0) & (tq - 1))
        kpos = start + lax.broadcasted_iota(jnp.int32, (1, span), 1)
        dist = qpos - kpos
        s = jnp.where((dist >= 0) & (dist <= WIN_LEN - 1), s, -jnp.inf)
        m = jnp.max(s, axis=1, keepdims=True)
        p = jnp.exp(s - m)
        l = jnp.sum(p, axis=1, keepdims=True)
        o = _dot(p.astype(BF16), v_ref[pl.ds(start, span), :]) / l
        o_ref[:, u * tq:(u + 1) * tq, :] = o.reshape(B_GROUP, tq, HEAD_DIM).astype(o_ref.dtype)


def _window(q, q_head0, k, k_head0, v, v_head0):
    _, nb, seq, _ = q.shape
    tq = min(ATT_TQ, seq)
    tiles = min(ATT_TILES, seq // tq)
    assert WIN_LEN % tq == 0
    return pl.pallas_call(
        functools.partial(_window_kernel, tq=tq, tiles=tiles, seq=seq),
        grid=(nb, B_KV_HEADS, seq // (tq * tiles)),
        in_specs=_group_q_specs(tq * tiles, q_head0) + [_seq_spec(seq, k_head0), _seq_spec(seq, v_head0)],
        out_specs=_group_o_spec(tq * tiles),
        out_shape=jax.ShapeDtypeStruct((B_HEADS, nb, seq, HEAD_DIM), BF16),
        compiler_params=_params(("parallel", "parallel", "arbitrary")),
        name="nsa_window",
    )(q, q, q, k, v)


def _nsa_gate_kernel(g_ref, oc_ref, os_ref, ow_ref, y_ref):
    g = g_ref[...]
    for h in range(B_HEADS):
        y = (g[:, 3 * h:3 * h + 1] * oc_ref[h].astype(F32)
             + g[:, 3 * h + 1:3 * h + 2] * os_ref[h].astype(F32)
             + g[:, 3 * h + 2:3 * h + 3] * ow_ref[h].astype(F32))
        y_ref[:, h * HEAD_DIM:(h + 1) * HEAD_DIM] = y.astype(y_ref.dtype)


def _nsa_gate(gates, o_cmp, o_sel, o_win):
    rows = gates.shape[0]
    tm = min(256, rows)
    o_spec = pl.BlockSpec((B_HEADS, tm, HEAD_DIM), lambda i: (0, i, 0))
    return pl.pallas_call(
        _nsa_gate_kernel,
        grid=(rows // tm,),
        in_specs=[pl.BlockSpec((tm, HEAD_DIM), lambda i: (i, 0)), o_spec, o_spec, o_spec],
        out_specs=pl.BlockSpec((tm, B_HEADS * HEAD_DIM), lambda i: (i, 0)),
        out_shape=jax.ShapeDtypeStruct((rows, B_HEADS * HEAD_DIM), BF16),
        compiler_params=_params(("parallel",)),
        name="nsa_gate",
    )(gates, o_cmp, o_sel, o_win)


def _stick_kernel(*refs, tq, heads):
    q_refs, k_refs, v_refs = refs[:heads], refs[heads:2 * heads], refs[2 * heads:3 * heads]
    o_ref, acc_ref, car_ref = refs[3 * heads:]
    n = pl.program_id(2)
    sub = min(CUM_TILE, tq)
    n_sub = tq // sub
    r_idx = lax.broadcasted_iota(jnp.int32, (2 * sub, 2 * sub), 0) & (sub - 1)
    c_idx = lax.broadcasted_iota(jnp.int32, (2 * sub, 2 * sub), 1)
    suffix_and_total = jnp.where((c_idx >= sub) | (r_idx > c_idx), 1.0, 0.0).astype(BF16)
    k_in_sub = lax.broadcasted_iota(jnp.int32, (1, sub), 1)
    acc_ref[...] = jnp.zeros(acc_ref.shape, F32)
    car_ref[...] = jnp.zeros(car_ref.shape, F32)

    q_in_tile = lax.broadcasted_iota(jnp.int32, (tq, 1), 0)

    def tile(k0, diagonal):
        for h in range(heads):
            z = _dot_nt(q_refs[h][...], k_refs[h][pl.ds(k0, tq), :])
            soft = jnp.log(1.0 + jnp.exp(-jnp.abs(z)))
            log_beta = jnp.minimum(z, 0.0) - soft
            log_rest = log_beta - z
            carried = car_ref[h]
            parts = [None] * n_sub
            for u in reversed(range(n_sub)):
                cols = slice(u * sub, (u + 1) * sub)
                rest_u = log_rest[:, cols]
                if diagonal:
                    before = (u * sub + k_in_sub) - q_in_tile < 0
                    rest_u = jnp.where(before, rest_u, 0.0)
                sums = _dot(jnp.concatenate(_split_bf16(rest_u), axis=1), suffix_and_total)
                a = jnp.exp(log_beta[:, cols] + (sums[:, :sub] + carried))
                if diagonal:
                    a = jnp.where(before, a, 0.0)
                carried = carried + sums[:, sub:]
                parts[u] = a.astype(BF16)
            car_ref[h] = carried
            acc_ref[h] += _dot(jnp.concatenate(parts, axis=1), v_refs[h][pl.ds(k0, tq), :])

    tile(pl.multiple_of(n * tq, tq), True)

    def step(i, carry):
        tile(pl.multiple_of((n - 1 - i) * tq, tq), False)
        return carry

    lax.fori_loop(0, n, step, 0)
    for h in range(heads):
        o_ref[:, h * HEAD_DIM:(h + 1) * HEAD_DIM] = acc_ref[h].astype(o_ref.dtype)


def _stick_breaking(q, q_head0, k, k_head0, v, v_head0):
    _, nb, seq, _ = q.shape
    tq = min(STICK_TILE, seq)
    heads = STICK_HEADS
    sub = min(CUM_TILE, tq)

    def per_head(make, head0):
        return [make(head0 + h) for h in range(heads)]

    def q_spec(head):
        return pl.BlockSpec((None, None, tq, HEAD_DIM), lambda b, g, n: (head + g * heads, b, n, 0))

    def kv_spec(head):
        return pl.BlockSpec((None, None, seq, HEAD_DIM), lambda b, g, n: (head + g * heads, b, 0, 0))

    return pl.pallas_call(
        functools.partial(_stick_kernel, tq=tq, heads=heads),
        grid=(nb, C_HEADS // heads, seq // tq),
        in_specs=per_head(q_spec, q_head0) + per_head(kv_spec, k_head0) + per_head(kv_spec, v_head0),
        out_specs=pl.BlockSpec((None, tq, heads * HEAD_DIM), lambda b, g, n: (b, n, g)),
        out_shape=jax.ShapeDtypeStruct((nb, seq, C_HEADS * HEAD_DIM), BF16),
        scratch_shapes=[pltpu.VMEM((heads, tq, HEAD_DIM), F32), pltpu.VMEM((heads, tq, sub), F32)],
        compiler_params=_params(("parallel", "parallel", "arbitrary")),
        name="stick_breaking",
    )(*([q] * heads + [k] * heads + [v] * heads))


def _merge_kernel(ya_ref, yb_ref, yc_ref, wa_ref, wb_ref, wc_ref, ga_ref, gb_ref, gc_ref, o_ref):
    mixed = (ga_ref[...].astype(F32) * _dot(ya_ref[...], wa_ref[...])
             + gb_ref[...].astype(F32) * _dot(yb_ref[...], wb_ref[...])
             + gc_ref[...].astype(F32) * _dot(yc_ref[...], wc_ref[...]))
    o_ref[...] = mixed.astype(o_ref.dtype)


def _merge(ya, yb, yc, wa, wb, wc, gates, d_model):
    m = ya.shape[0]
    tm, tn = min(512, m), 1024
    col_tiles = d_model // tn

    def y_spec(y):
        return pl.BlockSpec((tm, y.shape[1]), lambda i, j: (i, 0))

    def w_spec(w):
        return pl.BlockSpec((w.shape[0], tn), lambda i, j: (0, j))

    def g_spec(branch):
        return pl.BlockSpec((tm, tn), lambda i, j: (i, branch * col_tiles + j))

    return pl.pallas_call(
        _merge_kernel,
        grid=(m // tm, col_tiles),
        in_specs=[y_spec(ya), y_spec(yb), y_spec(yc), w_spec(wa), w_spec(wb), w_spec(wc),
                  g_spec(0), g_spec(1), g_spec(2)],
        out_specs=pl.BlockSpec((tm, tn), lambda i, j: (i, j)),
        out_shape=jax.ShapeDtypeStruct((m, d_model), BF16),
        compiler_params=_params(("parallel", "arbitrary")),
        name="branch_merge",
    )(ya, yb, yc, wa, wb, wc, gates, gates, gates)


def _ln_kernel(x_ref, y_ref, g_ref, b_ref, o_ref, ob_ref):
    z = ALPHA * x_ref[...] + y_ref[...]
    mu = jnp.mean(z, axis=1, keepdims=True)
    zc = z - mu
    var = jnp.mean(zc * zc, axis=1, keepdims=True)
    out = zc * lax.rsqrt(var + LN_EPS) * g_ref[...] + b_ref[...]
    o_ref[...] = out
    ob_ref[...] = out.astype(BF16)


def _residual_ln(x, y, g, b):
    m, d = x.shape
    tm = min(256, m)
    row = pl.BlockSpec((tm, d), lambda i: (i, 0))
    vec = pl.BlockSpec((1, d), lambda i: (0, 0))
    return pl.pallas_call(
        _ln_kernel,
        grid=(m // tm,),
        in_specs=[row, row, vec, vec],
        out_specs=[row, row],
        out_shape=[jax.ShapeDtypeStruct((m, d), F32), jax.ShapeDtypeStruct((m, d), BF16)],
        compiler_params=_params(("parallel",)),
        name="residual_layer_norm",
    )(x, y, g.reshape(1, d), b.reshape(1, d))


def _rope_tables(seq):
    half = HEAD_DIM // 2
    inv = 1.0 / (ROPE_THETA ** (jnp.arange(half, dtype=F32) / half))
    ang = jnp.arange(seq).astype(F32)[:, None] * inv[None, :]
    cos, sin = jnp.cos(ang), jnp.sin(ang)
    return jnp.concatenate([cos, cos], axis=1), jnp.concatenate([-sin, sin], axis=1)


_A_KV, _B_KV = A_SLOTS, B_KV_HEADS
_ALIGNED_FIELDS = (('q_a', A_HEADS, 'rope_scale'), ('k_a', _A_KV, 'rope'), ('v_a', _A_KV, 'none'),
                   ('q_b', B_HEADS, 'rope_scale'), ('kc_b', _B_KV, 'rope'), ('vc_b', _B_KV, 'none'),
                   ('ks_b', _B_KV, 'rope'), ('vs_b', _B_KV, 'none'), ('kw_b', _B_KV, 'rope'),
                   ('vw_b', _B_KV, 'none'))
_STICK_FIELDS = (('q_c', C_HEADS, 'scale'), ('k_c', C_HEADS, 'none'), ('v_c', C_HEADS, 'none'))
N_NSA_GATES = 3 * B_HEADS


def _field_layout(fields, heads_per_tile):
    head0, modes, start = {}, [], 0
    for name, heads, mode in fields:
        assert heads % heads_per_tile == 0
        head0[name] = start
        modes += [mode] * (heads // heads_per_tile)
        start += heads
    return head0, tuple(modes), start


def _layer(layer, x, xb, nb, seq, tables, w_rows, cmp_pe_k, cmp_wk1, cmp_wk2, cmp_pe_v, cmp_wv1, cmp_wv2,
           w_br_a, w_br_b, w_br_c, w_out, ln1_g, ln1_b, w_up, w_down, ln2_g, ln2_b):
    m, d_model = x.shape

    tn_att = 4 * HEAD_DIM
    at, att_modes, att_heads = _field_layout(_ALIGNED_FIELDS, tn_att // HEAD_DIM)
    att = _matmul(xb, w_rows, layer=layer, b_rows=True, name="proj_attention", out_dtype=BF16,
                  tile_modes=att_modes, tm=1024, tn=tn_att, head_major=True, rope_tables=tables)
    tn_tail = 1024
    st, stick_modes, stick_heads = _field_layout(_STICK_FIELDS, tn_tail // HEAD_DIM)
    stick_col0 = att_heads * HEAD_DIM
    gates_col0 = stick_col0 + stick_heads * HEAD_DIM
    nsa_col0 = gates_col0 + 3 * d_model
    stick = _matmul(xb, w_rows, layer=layer, b_rows=True, col0=stick_col0, name="proj_stick",
                    out_dtype=BF16, tile_modes=stick_modes, tm=1024, tn=tn_tail, head_major=True)
    gates = _matmul(xb, w_rows, layer=layer, b_rows=True, col0=gates_col0, name="proj_branch_gates",
                    out_dtype=BF16, tile_modes=('sigmoid',) * (3 * d_model // tn_tail), tm=1024, tn=tn_tail)
    g_nsa = _matmul(xb, w_rows, layer=layer, b_rows=True, col0=nsa_col0, name="proj_nsa_gates",
                    out_dtype=F32, tile_modes=('sigmoid',), tm=1024, tn=HEAD_DIM)

    att = att.reshape(att_heads, nb, seq, HEAD_DIM)
    stick = stick.reshape(stick_heads, nb, seq, HEAD_DIM)

    y_a = _dilated_mixer(att, at['q_a'], at['k_a'], at['v_a'])

    kc = _compress(att, at['kc_b'], cmp_pe_k, cmp_wk1, cmp_wk2, "nsa_compress_k")
    vc = _compress(att, at['vc_b'], cmp_pe_v, cmp_wv1, cmp_wv2, "nsa_compress_v")
    o_cmp, sel = _cmp_select(att, at['q_b'], kc, vc)
    o_sel = _selected(att, at['q_b'], att, at['ks_b'], att, at['vs_b'], sel)
    o_win = _window(att, at['q_b'], att, at['kw_b'], att, at['vw_b'])
    y_b = _nsa_gate(g_nsa, o_cmp.reshape(B_HEADS, m, HEAD_DIM), o_sel.reshape(B_HEADS, m, HEAD_DIM),
                    o_win.reshape(B_HEADS, m, HEAD_DIM))

    y_c = _stick_breaking(stick, st['q_c'], stick, st['k_c'], stick, st['v_c'])

    merged = _merge(y_a.reshape(m, -1), y_b, y_c.reshape(m, -1), w_br_a.astype(BF16),
                    w_br_b.astype(BF16), w_br_c.astype(BF16), gates, d_model)
    mixed = _matmul(merged, w_out, layer=layer, name="out_proj", out_dtype=F32,
                    tile_modes=('none',) * (d_model // 512), tm=1024, tn=512)
    x1, x1b = _residual_ln(x, mixed, ln1_g, ln1_b)

    d_ff = w_up.shape[2]
    hidden = _matmul(x1b, w_up, layer=layer, name="mlp_up", out_dtype=BF16,
                     tile_modes=('relu2',) * (d_ff // 512), tm=1024, tn=512)
    down = _matmul(hidden, w_down, layer=layer, name="mlp_down", out_dtype=F32,
                   tile_modes=('none',) * (d_model // 1024), tm=2048, tn=1024, tk=1024)
    return _residual_ln(x1, down, ln2_g, ln2_b)


def kernel(x, w_in, cmp_pe_k, cmp_wk1, cmp_wk2, cmp_pe_v, cmp_wv1, cmp_wv2, w_br_a, w_br_b, w_br_c,
           w_out, ln1_g, ln1_b, w_up, w_down, ln2_g, ln2_b):
    nb, seq, d_model = x.shape
    tables = _rope_tables(seq)
    xf = x.reshape(nb * seq, d_model)
    xb = xf.astype(BF16)
    aligned_cols = sum(heads for _, heads, _ in _ALIGNED_FIELDS) * HEAD_DIM
    w_rows = _input_weight_rows(w_in, aligned_cols, N_NSA_GATES)
    for l in range(w_in.shape[0]):
        xf, xb = _layer(l, xf, xb, nb, seq, tables, w_rows, cmp_pe_k[l], cmp_wk1[l], cmp_wk2[l],
                        cmp_pe_v[l], cmp_wv1[l], cmp_wv2[l], w_br_a[l], w_br_b[l], w_br_c[l],
                        w_out, ln1_g[l], ln1_b[l], w_up, w_down, ln2_g[l], ln2_b[l])
    return xf.reshape(nb, seq, d_model)
```

```python
import functools
import math

import numpy as np
import jax
import jax.numpy as jnp
from jax import lax
from jax.experimental import pallas as pl
from jax.experimental.pallas import tpu as pltpu

F32 = jnp.float32
BF16 = jnp.bfloat16

HEAD_DIM = 128
ROPE_THETA = 10000.0
LN_EPS = 1e-5
DEPTH = 2

DIL_GROUPS = ((128, 1), (512, 4), (2048, 16))
A_SLOTS = 4
A_HEADS = A_SLOTS * len(DIL_GROUPS)

B_HEADS = 12
B_KV_HEADS = 4
B_GROUP = B_HEADS // B_KV_HEADS
CMP_LEN = 32
CMP_STRIDE = 16
SEL_LEN = 64
SEL_SHIFT = 6
SEL_TOPK = 16
WIN_LEN = 512

C_HEADS = 8

MASKED = -3e38
ALPHA = (2.0 * DEPTH) ** 0.25
Q_SCALE = HEAD_DIM ** -0.5

VMEM_LIMIT = 48 * 1024 * 1024
MXU_COLS = 256
ATT_TQ = 128
ATT_TILES = 8
DIL_ROWS = 128
DIL_ROWS_DENSE = 512
SEL_TQ = 128
SEL_TILES = 8
SEL_CHUNK = 1024
STICK_TILE = 512
STICK_HEADS = 4
CUM_TILE = 128


def _params(semantics):
    return pltpu.CompilerParams(dimension_semantics=semantics, vmem_limit_bytes=VMEM_LIMIT)


def _dot(a, b):
    return jnp.dot(a, b, preferred_element_type=F32)


def _dot_nt(a, b):
    return lax.dot_general(a, b, (((1,), (1,)), ((), ())), preferred_element_type=F32)


def _split_bf16(x):
    hi = x.astype(BF16)
    lo = (x - hi.astype(F32)).astype(BF16)
    return hi, lo


def _tile_flag(tile_modes, wanted):
    hits = [mode in wanted for mode in tile_modes]
    if all(hits) or not any(hits):
        return hits[0]
    j = pl.program_id(1)
    flag = None
    for c, hit in enumerate(hits):
        if hit:
            flag = (j == c) if flag is None else (flag | (j == c))
    return flag


def _write_columns(acc, o_ref, col0, tile_modes, head_major, cos_ref, sin_ref):
    is_rope = _tile_flag(tile_modes, ('rope', 'rope_scale'))
    is_scaled = _tile_flag(tile_modes, ('rope_scale', 'scale'))
    uniform = tile_modes[0] if len(set(tile_modes)) == 1 else None
    assert uniform is not None or not (set(tile_modes) & {'sigmoid', 'relu2'})

    def epilogue(y):
        if is_rope is not False:
            roped = y * cos_ref[...] + pltpu.roll(y, HEAD_DIM // 2, 1) * sin_ref[...]
            y = roped if is_rope is True else jnp.where(is_rope, roped, y)
        if is_scaled is not False:
            y = y * (Q_SCALE if is_scaled is True else jnp.where(is_scaled, Q_SCALE, 1.0))
        if uniform == 'sigmoid':
            y = jax.nn.sigmoid(y)
        if uniform == 'relu2':
            y = jnp.square(jnp.maximum(y, 0.0))
        return y

    width = acc.shape[1]
    if head_major:
        for c in range(width // HEAD_DIM):
            y = epilogue(acc[:, c * HEAD_DIM:(c + 1) * HEAD_DIM])
            o_ref[col0 // HEAD_DIM + c] = y.astype(o_ref.dtype)
    else:
        o_ref[:, col0:col0 + width] = epilogue(acc).astype(o_ref.dtype)


def _matmul_kernel(*refs, tile_modes, head_major, n_k, use_rope, acc_in_out, b_rows):
    refs = list(refs)
    a_ref, b_ref = refs[0], refs[1]
    cos_ref, sin_ref = (refs[2], refs[3]) if use_rope else (None, None)
    o_ref = refs[4] if use_rope else refs[2]
    acc_ref = o_ref if acc_in_out else (refs[-1] if n_k > 1 else None)
    tn = b_ref.shape[0] if b_rows else b_ref.shape[1]

    def product(col0, width):
        if b_rows:
            return _dot_nt(a_ref[...], b_ref[col0:col0 + width, :].astype(BF16))
        return _dot(a_ref[...], b_ref[:, col0:col0 + width].astype(BF16))

    if n_k == 1:
        width = min(MXU_COLS, tn)
        for col0 in range(0, tn, width):
            _write_columns(product(col0, width), o_ref, col0, tile_modes, head_major, cos_ref, sin_ref)
        return

    @pl.when(pl.program_id(2) == 0)
    def _():
        acc_ref[...] = jnp.zeros(acc_ref.shape, F32)

    acc_ref[...] += product(0, tn)
    if not acc_in_out:
        pl.when(pl.program_id(2) == n_k - 1)(
            lambda: _write_columns(acc_ref[...], o_ref, 0, tile_modes, head_major, cos_ref, sin_ref))


def _matmul(a, b, *, name, out_dtype, tile_modes, tm, tn, tk=None, layer=None, col0=0,
            head_major=False, rope_tables=None, b_rows=False):
    m, kdim = a.shape
    n = tn * len(tile_modes)
    tm = min(tm, m)
    tk = kdim if tk is None else min(tk, kdim)
    assert m % tm == 0 and kdim % tk == 0 and col0 % tn == 0
    n_k = kdim // tk
    col_tile0 = col0 // tn
    use_rope = any(mode.startswith('rope') for mode in tile_modes)
    acc_in_out = n_k > 1 and out_dtype == F32 and set(tile_modes) == {'none'} and not head_major
    if b_rows:
        assert layer is not None
        b_spec = pl.BlockSpec((None, tn, tk), lambda i, j, k: (layer, col_tile0 + j, k))
    elif layer is None:
        b_spec = pl.BlockSpec((tk, tn), lambda i, j, k: (k, col_tile0 + j))
    else:
        b_spec = pl.BlockSpec((None, tk, tn), lambda i, j, k: (layer, k, col_tile0 + j))
    in_specs = [pl.BlockSpec((tm, tk), lambda i, j, k: (i, k)), b_spec]
    operands = [a, b]
    if use_rope:
        assert head_major
        cos, sin = rope_tables
        seq_tiles = cos.shape[0] // tm
        assert cos.shape[0] % tm == 0
        spec = pl.BlockSpec((tm, HEAD_DIM), lambda i, j, k: (i % seq_tiles, 0))
        in_specs += [spec, spec]
        operands += [cos, sin]
    if head_major:
        out_shape = jax.ShapeDtypeStruct((n // HEAD_DIM, m, HEAD_DIM), out_dtype)
        out_spec = pl.BlockSpec((tn // HEAD_DIM, tm, HEAD_DIM), lambda i, j, k: (j, i, 0))
    else:
        out_shape = jax.ShapeDtypeStruct((m, n), out_dtype)
        out_spec = pl.BlockSpec((tm, tn), lambda i, j, k: (i, j))
    scratch = [pltpu.VMEM((tm, tn), F32)] if (n_k > 1 and not acc_in_out) else []
    return pl.pallas_call(
        functools.partial(_matmul_kernel, tile_modes=tuple(tile_modes), head_major=head_major, n_k=n_k,
                          use_rope=use_rope, acc_in_out=acc_in_out, b_rows=b_rows),
        grid=(m // tm, len(tile_modes), n_k),
        in_specs=in_specs,
        out_specs=out_spec,
        out_shape=out_shape,
        scratch_shapes=scratch,
        compiler_params=_params(("parallel", "parallel", "arbitrary")),
        name=name,
    )(*operands)


def _gather_rows_kernel(w_ref, o_ref, *, tn, k_chunks, n_layers):
    stride = k_chunks * n_layers
    by_chunk = pltpu.einshape("nck->cnk", w_ref[...].reshape(tn, stride, HEAD_DIM))
    for layer in range(n_layers):
        pieces = [by_chunk[c * n_layers + layer] for c in range(k_chunks)]
        o_ref[layer] = jnp.concatenate(pieces, axis=1).astype(o_ref.dtype)


def _input_weight_rows(w_in, aligned_cols, n_gates, tn=HEAD_DIM):
    n_layers, kdim, n = w_in.shape
    k_chunks = kdim // HEAD_DIM
    tail = n - aligned_cols - n_gates
    assert aligned_cols % tn == 0 and tail % tn == 0 and kdim % HEAD_DIM == 0
    aligned_tiles, tail_tiles = aligned_cols // tn, tail // tn
    rows_per_col = k_chunks * n_layers
    flat = w_in.reshape(n_layers, k_chunks, HEAD_DIM, n).transpose(3, 1, 0, 2).reshape(n * rows_per_col, HEAD_DIM)

    def source_row(j):
        col = jnp.where(j < aligned_tiles, tn * j,
                        jnp.where(j < aligned_tiles + tail_tiles,
                                  aligned_cols + n_gates + tn * (j - aligned_tiles), aligned_cols))
        return (col * rows_per_col, 0)

    tiles = aligned_tiles + tail_tiles + 1
    return pl.pallas_call(
        functools.partial(_gather_rows_kernel, tn=tn, k_chunks=k_chunks, n_layers=n_layers),
        grid=(tiles,),
        in_specs=[pl.BlockSpec((pl.Element(tn * rows_per_col), pl.Element(HEAD_DIM)), source_row)],
        out_specs=pl.BlockSpec((n_layers, tn, kdim), lambda j: (0, j, 0)),
        out_shape=jax.ShapeDtypeStruct((n_layers, tiles * tn, kdim), BF16),
        compiler_params=_params(("parallel",)),
        name="input_weight_rows",
    )(flat)


def _window_start(t0, back, span, seq, align):
    start = jnp.minimum(jnp.maximum(t0 - back, 0), seq - span)
    return pl.multiple_of(start, align)


def _by_residue(x, dil):
    rows = x.shape[0] // dil
    return pltpu.einshape("ldk->dlk", x.reshape(rows, dil, HEAD_DIM))


def _dilated_group_kernel(*refs, dil, rows, band, n_rows, n_others):
    q_ref, k_ref, v_ref = refs[:3]
    others = refs[3:3 + 2 * n_others]
    if n_others:
        y_ref = refs[3 + 2 * n_others]
        kr_ref, vr_ref, o_ref, lse_ref = refs[4 + 2 * n_others:]
    else:
        o_ref, lse_ref = refs[3:5]
        kr_ref, vr_ref = refs[5:] if dil > 1 else (None, None)
    j = pl.program_id(2)
    if dil > 1:

        @pl.when(j == 0)
        def _():
            kr_ref[...] = _by_residue(k_ref[...], dil)
            vr_ref[...] = _by_residue(v_ref[...], dil)

        q_by = _by_residue(q_ref[...], dil)
    l0 = j * rows
    span = min(band + rows, n_rows)
    start = _window_start(l0, band, span, n_rows, math.gcd(rows, band))
    dist = (l0 + lax.broadcasted_iota(jnp.int32, (rows, 1), 0)) - (start + lax.broadcasted_iota(jnp.int32, (1, span), 1))
    keep = (dist >= 0) & (dist <= band)
    for r in range(dil):
        if dil == 1:
            q, keys, vals = q_ref[...], k_ref[pl.ds(start, span), :], v_ref[pl.ds(start, span), :]
        else:
            q, keys, vals = q_by[r], kr_ref[r, pl.ds(start, span), :], vr_ref[r, pl.ds(start, span), :]
        s = jnp.where(keep, _dot_nt(q, keys), -jnp.inf)
        m = jnp.max(s, axis=1, keepdims=True)
        p = jnp.exp(s - m)
        l = jnp.sum(p, axis=1, keepdims=True)
        o = _dot(p.astype(BF16), vals) / l
        lse = jnp.broadcast_to(m + jnp.log(l), (rows, HEAD_DIM))
        if dil == 1:
            o_ref[...] = o
            lse_ref[...] = lse
        else:
            o_ref[pl.ds(r, rows, stride=dil), :] = o
            lse_ref[pl.ds(r, rows, stride=dil), :] = lse
    if n_others:
        outs = [others[2 * g][...] for g in range(n_others)] + [o_ref[...]]
        lses = [others[2 * g + 1][...] for g in range(n_others)] + [lse_ref[...]]
        top = functools.reduce(jnp.maximum, lses)
        ws = [jnp.exp(lse - top) for lse in lses]
        total = functools.reduce(lambda a, b: a + b, ws)
        mixed = functools.reduce(lambda a, b: a + b, [w * o for w, o in zip(ws, outs)])
        y_ref[...] = (mixed / total).astype(y_ref.dtype)


def _head_spec(rows, head0):
    return pl.BlockSpec((None, None, rows, HEAD_DIM), lambda b, h, n: (head0 + h, b, n, 0))


def _seq_spec(seq, head0):
    return pl.BlockSpec((None, None, seq, HEAD_DIM), lambda b, h, n: (head0 + h, b, 0, 0))


def _dilated_group(att, q_head0, k_head0, v_head0, window, dil, others=()):
    _, nb, seq, _ = att.shape
    n_rows = seq // dil
    band = window // dil
    rows = min(DIL_ROWS if dil > 1 else DIL_ROWS_DENSE, n_rows)
    assert seq % dil == 0 and n_rows % rows == 0 and (dil > 1 or not others)
    pair_spec = _head_spec(rows * dil, 0)
    pair_shape = jax.ShapeDtypeStruct((A_SLOTS, nb, seq, HEAD_DIM), F32)
    scratch = [pltpu.VMEM((dil, n_rows, HEAD_DIM), BF16)] * 2 if dil > 1 else []
    if others:
        out_specs = pl.BlockSpec((None, rows * dil, HEAD_DIM), lambda b, s, j: (b, j, s))
        out_shape = jax.ShapeDtypeStruct((nb, seq, A_SLOTS * HEAD_DIM), BF16)
        scratch = scratch + [pltpu.VMEM((rows * dil, HEAD_DIM), F32)] * 2
    else:
        out_specs, out_shape = [pair_spec, pair_spec], [pair_shape, pair_shape]
    return pl.pallas_call(
        functools.partial(_dilated_group_kernel, dil=dil, rows=rows, band=band, n_rows=n_rows,
                          n_others=len(others) // 2),
        grid=(nb, A_SLOTS, n_rows // rows),
        in_specs=[_head_spec(rows * dil, q_head0), _seq_spec(seq, k_head0), _seq_spec(seq, v_head0)]
        + [pair_spec] * len(others),
        out_specs=out_specs,
        out_shape=out_shape,
        scratch_shapes=scratch,
        compiler_params=_params(("parallel", "parallel", "arbitrary")),
        name="dilated_group_%d" % dil,
    )(att, att, att, *others)


def _dilated_mixer(att, q_head0, k_head0, v_head0):
    pairs = []
    for g, (window, dil) in enumerate(DIL_GROUPS):
        last = g == len(DIL_GROUPS) - 1
        result = _dilated_group(att, q_head0 + g * A_SLOTS, k_head0, v_head0, window, dil,
                                others=tuple(pairs) if last else ())
        if last:
            return result
        pairs += list(result)


def _gelu_tanh(x):
    return 0.5 * x * (1.0 + jnp.tanh(math.sqrt(2.0 / math.pi) * (x + 0.044715 * (x * x * x))))


def _compress_kernel(x_ref, pe_ref, w1_ref, w2_ref, o_ref):
    by_token = _by_residue(x_ref[...], CMP_STRIDE)
    x = jnp.concatenate([by_token[r] for r in range(CMP_STRIDE)], axis=1).astype(F32)
    first = _dot((x + pe_ref[0:1, :]).astype(BF16), w1_ref[0])
    second = _dot((x + pe_ref[1:2, :]).astype(BF16), w1_ref[1])
    chunks = x.shape[0]
    hidden = first + pltpu.roll(second, chunks - 1, 0)
    o_ref[...] = _dot(_gelu_tanh(hidden).astype(BF16), w2_ref[...]).astype(o_ref.dtype)


def _compress(t, head0, pe, w1, w2, name):
    _, nb, seq, _ = t.shape
    nh = B_KV_HEADS
    chunks = seq // CMP_STRIDE
    width = CMP_STRIDE * HEAD_DIM
    ratio = CMP_LEN // CMP_STRIDE
    return pl.pallas_call(
        _compress_kernel,
        grid=(nh, nb),
        in_specs=[pl.BlockSpec((None, None, seq, HEAD_DIM), lambda h, b: (head0 + h, b, 0, 0)),
                  pl.BlockSpec((ratio, width), lambda h, b: (0, 0)),
                  pl.BlockSpec((ratio, width, HEAD_DIM), lambda h, b: (0, 0, 0)),
                  pl.BlockSpec((HEAD_DIM, HEAD_DIM), lambda h, b: (0, 0))],
        out_specs=pl.BlockSpec((None, None, chunks, HEAD_DIM), lambda h, b: (h, b, 0, 0)),
        out_shape=jax.ShapeDtypeStruct((nh, nb, chunks, HEAD_DIM), BF16),
        compiler_params=_params(("parallel", "parallel")),
        name=name,
    )(t, pe.reshape(ratio, width), w1.reshape(ratio, width, HEAD_DIM).astype(BF16), w2.astype(BF16))


def _group_q(q_refs, row0=0, rows=None):
    rows = q_refs[0].shape[0] if rows is None else rows
    return jnp.concatenate([r[row0:row0 + rows, :] for r in q_refs], axis=0)


def _group_q_specs(tq, head0):
    return [pl.BlockSpec((None, None, tq, HEAD_DIM),
                         functools.partial(lambda b, h, n, g: (head0 + h * B_GROUP + g, b, n, 0), g=g))
            for g in range(B_GROUP)]


def _cmp_select_kernel(q0_ref, q1_ref, q2_ref, kc_ref, vc_ref, ov_ref, o_ref, sel_ref, *, tq, tiles):
    for u in range(tiles):
        _cmp_select_tile((q0_ref, q1_ref, q2_ref), kc_ref, vc_ref, ov_ref, o_ref, sel_ref,
                         (pl.program_id(2) * tiles + u) * tq, u * tq, tq)


def _cmp_select_tile(q_refs, kc_ref, vc_ref, ov_ref, o_ref, sel_ref, t0, row0, tq):
    n_cmp = kc_ref.shape[0]
    rows = B_GROUP * tq
    s = _dot_nt(_group_q(q_refs, row0, tq), kc_ref[...])
    tpos3 = t0 + (lax.broadcasted_iota(jnp.int32, (rows, 1), 0) & (tq - 1))
    c_end = lax.broadcasted_iota(jnp.int32, (1, n_cmp), 1) * CMP_STRIDE + (CMP_LEN - 1)
    s = jnp.where(c_end - tpos3 <= 0, s, -jnp.inf)
    m = jnp.max(s, axis=1, keepdims=True)
    m = jnp.where(jnp.abs(m) < jnp.inf, m, 0.0)
    e = jnp.exp(s - m)
    den = jnp.sum(e, axis=1, keepdims=True)
    p = e / jnp.where(den > 0, den, 1.0)
    o = _dot(p.astype(BF16), vc_ref[...]).reshape(B_GROUP, tq, HEAD_DIM)
    o_ref[:, row0:row0 + tq, :] = o.astype(o_ref.dtype)

    p_sum = p[0:tq] + p[tq:2 * tq] + p[2 * tq:3 * tq]
    hi, lo = _split_bf16(p_sum)
    imp = _dot(hi, ov_ref[...]) + _dot(lo, ov_ref[...])
    tpos = t0 + lax.broadcasted_iota(jnp.int32, (tq, 1), 0)
    rel = lax.broadcasted_iota(jnp.int32, (1, HEAD_DIM), 1) - (tpos >> SEL_SHIFT)
    j_abs = jnp.broadcast_to(lax.broadcasted_iota(jnp.int32, (1, HEAD_DIM), 1), rel.shape)
    forced = (j_abs == 0) | (rel == 0) | (rel == -1)
    imp = jnp.where(forced, jnp.inf, jnp.where(rel <= 0, imp, -jnp.inf))

    n_sel = HEAD_DIM // 2
    imp_t = imp.T
    mine = imp_t[0:n_sel]
    j_idx = lax.broadcasted_iota(jnp.int32, (n_sel, tq), 0)
    beaten = jnp.zeros((n_sel, tq), F32)
    for kk in range(n_sel):
        other = imp_t[kk:kk + 1, :]
        wins = (other > mine) | ((other == mine) & (j_idx > kk))
        beaten = beaten + jnp.where(wins, 1.0, 0.0)
    chosen = jnp.where(beaten < SEL_TOPK, 0.0, MASKED)
    chosen = jnp.concatenate([chosen, jnp.full((HEAD_DIM - n_sel, tq), MASKED, F32)], axis=0)
    sel_ref[:, row0:row0 + tq] = chosen.astype(sel_ref.dtype)


def _overlap_matrix(n_cmp_rows, n_sel):
    c_start = np.arange(n_cmp_rows) * CMP_STRIDE
    c_end = c_start + CMP_LEN - 1
    s_start = np.arange(HEAD_DIM) * SEL_LEN
    ov = (c_start[:, None] <= s_start[None, :] + SEL_LEN - 1) & (c_end[:, None] >= s_start[None, :])
    ov &= (np.arange(HEAD_DIM) < n_sel)[None, :]
    return jnp.asarray(ov.astype(np.float32), dtype=BF16)


def _group_o_spec(tq):
    return pl.BlockSpec((B_GROUP, None, tq, HEAD_DIM), lambda b, h, n: (h, b, n, 0))


def _sel_spec(tq):
    return pl.BlockSpec((None, None, HEAD_DIM, tq), lambda b, h, n: (h, b, 0, n))


def _cmp_select(q, q_head0, kc, vc):
    _, nb, seq, _ = q.shape
    assert seq // SEL_LEN <= HEAD_DIM // 2
    tq = min(ATT_TQ, seq)
    tiles = min(ATT_TILES, seq // tq)
    n_cmp = kc.shape[2]
    kv_spec = pl.BlockSpec((None, None, n_cmp, HEAD_DIM), lambda b, h, n: (h, b, 0, 0))
    return pl.pallas_call(
        functools.partial(_cmp_select_kernel, tq=tq, tiles=tiles),
        grid=(nb, B_KV_HEADS, seq // (tq * tiles)),
        in_specs=_group_q_specs(tq * tiles, q_head0) + [kv_spec, kv_spec,
                                                        pl.BlockSpec((n_cmp, HEAD_DIM), lambda b, h, n: (0, 0))],
        out_specs=[_group_o_spec(tq * tiles), _sel_spec(tq * tiles)],
        out_shape=[jax.ShapeDtypeStruct((B_HEADS, nb, seq, HEAD_DIM), BF16),
                   jax.ShapeDtypeStruct((B_KV_HEADS, nb, HEAD_DIM, seq), BF16)],
        compiler_params=_params(("parallel", "parallel", "arbitrary")),
        name="nsa_compressed_select",
    )(q, q, q, kc, vc, _overlap_matrix(n_cmp, seq // SEL_LEN))


def _selected_kernel(q0_ref, q1_ref, q2_ref, k_ref, v_ref, sel_ref, o_ref, vt_ref, acc_ref, *, tq, kc, tiles):
    n = pl.program_id(2)
    t_first = n * (tq * tiles)
    cols = B_GROUP * tq

    @pl.when(n == 0)
    def _():
        for c in range(v_ref.shape[0] // kc):
            vt_ref[c] = v_ref[c * kc:(c + 1) * kc, :].astype(F32).T.astype(BF16)

    qs = [_group_q((q0_ref, q1_ref, q2_ref), u * tq, tq) for u in range(tiles)]
    qposs = [t_first + u * tq + lax.broadcasted_iota(jnp.int32, (1, tq), 1) for u in range(tiles)]
    key_in_chunk = lax.broadcasted_iota(jnp.int32, (kc, 1), 0)
    blocks = kc // SEL_LEN
    acc_ref[...] = jnp.zeros(acc_ref.shape, F32)

    def chunk(i, carry, diagonal):
        k0 = pl.multiple_of(i * kc, kc)
        kpos = k0 + key_in_chunk
        keys = k_ref[pl.ds(k0, kc), :]
        blk0 = pl.multiple_of(i * blocks, blocks)
        out = []
        for u in range(tiles):
            m_old, l_old = carry[u]
            rows = sel_ref[pl.ds(blk0, blocks), u * tq:(u + 1) * tq].astype(F32)
            bias = jnp.broadcast_to(rows[:, None, :], (blocks, SEL_LEN, tq)).reshape(kc, tq)
            if diagonal:
                bias = jnp.where(kpos - qposs[u] <= 0, bias, MASKED)
            s = _dot_nt(keys, qs[u]) + jnp.concatenate([bias] * B_GROUP, axis=1)
            m_new = jnp.maximum(m_old, jnp.max(s, axis=0, keepdims=True))
            p = jnp.exp(s - m_new)
            alpha = jnp.exp(m_old - m_new)
            acc_ref[u] = alpha * acc_ref[u] + _dot(vt_ref[i], p.astype(BF16))
            out.append((m_new, alpha * l_old + jnp.sum(p, axis=0, keepdims=True)))
        return tuple(out)

    init = tuple((jnp.full((1, cols), MASKED, F32), jnp.zeros((1, cols), F32)) for _ in range(tiles))
    last = t_first // kc
    carry = lax.fori_loop(0, last, lambda i, c: chunk(i, c, False), init)
    final = chunk(last, carry, True)
    for u in range(tiles):
        o_t = acc_ref[u] / final[u][1]
        for g in range(B_GROUP):
            o_ref[g, u * tq:(u + 1) * tq, :] = o_t[:, g * tq:(g + 1) * tq].T.astype(o_ref.dtype)


def _selected(q, q_head0, k, k_head0, v, v_head0, sel):
    _, nb, seq, _ = q.shape
    tq = min(SEL_TQ, seq)
    kc = min(SEL_CHUNK, seq)
    tiles = min(SEL_TILES, kc // tq)
    assert kc % (tq * tiles) == 0 and seq % kc == 0 and tq == HEAD_DIM
    return pl.pallas_call(
        functools.partial(_selected_kernel, tq=tq, kc=kc, tiles=tiles),
        grid=(nb, B_KV_HEADS, seq // (tq * tiles)),
        in_specs=_group_q_specs(tq * tiles, q_head0) + [_seq_spec(seq, k_head0), _seq_spec(seq, v_head0),
                                                        _sel_spec(tq * tiles)],
        out_specs=_group_o_spec(tq * tiles),
        out_shape=jax.ShapeDtypeStruct((B_HEADS, nb, seq, HEAD_DIM), BF16),
        scratch_shapes=[pltpu.VMEM((seq // kc, HEAD_DIM, kc), BF16),
                        pltpu.VMEM((tiles, HEAD_DIM, B_GROUP * tq), F32)],
        compiler_params=_params(("parallel", "parallel", "arbitrary")),
        name="nsa_selected",
    )(q, q, q, k, v, sel)


def _window_kernel(q0_ref, q1_ref, q2_ref, k_ref, v_ref, o_ref, *, tq, tiles, seq):
    rows = B_GROUP * tq
    span = min(WIN_LEN + tq, seq)
    for u in range(tiles):
        t0 = (pl.program_id(2) * tiles + u) * tq
        start = _window_start(t0, WIN_LEN, span, seq, tq)
        s = _dot_nt(_group_q((q0_ref, q1_ref, q2_ref), u * tq, tq), k_ref[pl.ds(start, span), :])
        qpos = t0 + (lax.broadcasted_iota(jnp.int32, (rows, 1), 0) & (tq - 1))
        kpos = start + lax.broadcasted_iota(jnp.int32, (1, span), 1)
        dist = qpos - kpos
        s = jnp.where((dist >= 0) & (dist <= WIN_LEN - 1), s, -jnp.inf)
        m = jnp.max(s, axis=1, keepdims=True)
        p = jnp.exp(s - m)
        l = jnp.sum(p, axis=1, keepdims=True)
        o = _dot(p.astype(BF16), v_ref[pl.ds(start, span), :]) / l
        o_ref[:, u * tq:(u + 1) * tq, :] = o.reshape(B_GROUP, tq, HEAD_DIM).astype(o_ref.dtype)


def _window(q, q_head0, k, k_head0, v, v_head0):
    _, nb, seq, _ = q.shape
    tq = min(ATT_TQ, seq)
    tiles = min(ATT_TILES, seq // tq)
    assert WIN_LEN % tq == 0
    return pl.pallas_call(
        functools.partial(_window_kernel, tq=tq, tiles=tiles, seq=seq),
        grid=(nb, B_KV_HEADS, seq // (tq * tiles)),
        in_specs=_group_q_specs(tq * tiles, q_head0) + [_seq_spec(seq, k_head0), _seq_spec(seq, v_head0)],
        out_specs=_group_o_spec(tq * tiles),
        out_shape=jax.ShapeDtypeStruct((B_HEADS, nb, seq, HEAD_DIM), BF16),
        compiler_params=_params(("parallel", "parallel", "arbitrary")),
        name="nsa_window",
    )(q, q, q, k, v)


def _nsa_gate_kernel(g_ref, oc_ref, os_ref, ow_ref, y_ref):
    g = g_ref[...]
    for h in range(B_HEADS):
        y = (g[:, 3 * h:3 * h + 1] * oc_ref[h].astype(F32)
             + g[:, 3 * h + 1:3 * h + 2] * os_ref[h].astype(F32)
             + g[:, 3 * h + 2:3 * h + 3] * ow_ref[h].astype(F32))
        y_ref[:, h * HEAD_DIM:(h + 1) * HEAD_DIM] = y.astype(y_ref.dtype)


def _nsa_gate(gates, o_cmp, o_sel, o_win):
    rows = gates.shape[0]
    tm = min(256, rows)
    o_spec = pl.BlockSpec((B_HEADS, tm, HEAD_DIM), lambda i: (0, i, 0))
    return pl.pallas_call(
        _nsa_gate_kernel,
        grid=(rows // tm,),
        in_specs=[pl.BlockSpec((tm, HEAD_DIM), lambda i: (i, 0)), o_spec, o_spec, o_spec],
        out_specs=pl.BlockSpec((tm, B_HEADS * HEAD_DIM), lambda i: (i, 0)),
        out_shape=jax.ShapeDtypeStruct((rows, B_HEADS * HEAD_DIM), BF16),
        compiler_params=_params(("parallel",)),
        name="nsa_gate",
    )(gates, o_cmp, o_sel, o_win)


def _stick_kernel(*refs, tq, heads):
    q_refs, k_refs, v_refs = refs[:heads], refs[heads:2 * heads], refs[2 * heads:3 * heads]
    o_ref, acc_ref, car_ref = refs[3 * heads:]
    n = pl.program_id(2)
    sub = min(CUM_TILE, tq)
    n_sub = tq // sub
    r_idx = lax.broadcasted_iota(jnp.int32, (2 * sub, 2 * sub), 0) & (sub - 1)
    c_idx = lax.broadcasted_iota(jnp.int32, (2 * sub, 2 * sub), 1)
    suffix_and_total = jnp.where((c_idx >= sub) | (r_idx > c_idx), 1.0, 0.0).astype(BF16)
    k_in_sub = lax.broadcasted_iota(jnp.int32, (1, sub), 1)
    acc_ref[...] = jnp.zeros(acc_ref.shape, F32)
    car_ref[...] = jnp.zeros(car_ref.shape, F32)

    q_in_tile = lax.broadcasted_iota(jnp.int32, (tq, 1), 0)

    def tile(k0, diagonal):
        for h in range(heads):
            z = _dot_nt(q_refs[h][...], k_refs[h][pl.ds(k0, tq), :])
            soft = jnp.log(1.0 + jnp.exp(-jnp.abs(z)))
            log_beta = jnp.minimum(z, 0.0) - soft
            log_rest = log_beta - z
            carried = car_ref[h]
            parts = [None] * n_sub
            for u in reversed(range(n_sub)):
                cols = slice(u * sub, (u + 1) * sub)
                rest_u = log_rest[:, cols]
                if diagonal:
                    before = (u * sub + k_in_sub) - q_in_tile < 0
                    rest_u = jnp.where(before, rest_u, 0.0)
                sums = _dot(jnp.concatenate(_split_bf16(rest_u), axis=1), suffix_and_total)
                a = jnp.exp(log_beta[:, cols] + (sums[:, :sub] + carried))
                if diagonal:
                    a = jnp.where(before, a, 0.0)
                carried = carried + sums[:, sub:]
                parts[u] = a.astype(BF16)
            car_ref[h] = carried
            acc_ref[h] += _dot(jnp.concatenate(parts, axis=1), v_refs[h][pl.ds(k0, tq), :])

    tile(pl.multiple_of(n * tq, tq), True)

    def step(i, carry):
        tile(pl.multiple_of((n - 1 - i) * tq, tq), False)
        return carry

    lax.fori_loop(0, n, step, 0)
    for h in range(heads):
        o_ref[:, h * HEAD_DIM:(h + 1) * HEAD_DIM] = acc_ref[h].astype(o_ref.dtype)


def _stick_breaking(q, q_head0, k, k_head0, v, v_head0):
    _, nb, seq, _ = q.shape
    tq = min(STICK_TILE, seq)
    heads = STICK_HEADS
    sub = min(CUM_TILE, tq)

    def per_head(make, head0):
        return [make(head0 + h) for h in range(heads)]

    def q_spec(head):
        return pl.BlockSpec((None, None, tq, HEAD_DIM), lambda b, g, n: (head + g * heads, b, n, 0))

    def kv_spec(head):
        return pl.BlockSpec((None, None, seq, HEAD_DIM), lambda b, g, n: (head + g * heads, b, 0, 0))

    return pl.pallas_call(
        functools.partial(_stick_kernel, tq=tq, heads=heads),
        grid=(nb, C_HEADS // heads, seq // tq),
        in_specs=per_head(q_spec, q_head0) + per_head(kv_spec, k_head0) + per_head(kv_spec, v_head0),
        out_specs=pl.BlockSpec((None, tq, heads * HEAD_DIM), lambda b, g, n: (b, n, g)),
        out_shape=jax.ShapeDtypeStruct((nb, seq, C_HEADS * HEAD_DIM), BF16),
        scratch_shapes=[pltpu.VMEM((heads, tq, HEAD_DIM), F32), pltpu.VMEM((heads, tq, sub), F32)],
        compiler_params=_params(("parallel", "parallel", "arbitrary")),
        name="stick_breaking",
    )(*([q] * heads + [k] * heads + [v] * heads))


def _merge_kernel(ya_ref, yb_ref, yc_ref, wa_ref, wb_ref, wc_ref, ga_ref, gb_ref, gc_ref, o_ref):
    mixed = (ga_ref[...].astype(F32) * _dot(ya_ref[...], wa_ref[...])
             + gb_ref[...].astype(F32) * _dot(yb_ref[...], wb_ref[...])
             + gc_ref[...].astype(F32) * _dot(yc_ref[...], wc_ref[...]))
    o_ref[...] = mixed.astype(o_ref.dtype)


def _merge(ya, yb, yc, wa, wb, wc, gates, d_model):
    m = ya.shape[0]
    tm, tn = min(512, m), 1024
    col_tiles = d_model // tn

    def y_spec(y):
        return pl.BlockSpec((tm, y.shape[1]), lambda i, j: (i, 0))

    def w_spec(w):
        return pl.BlockSpec((w.shape[0], tn), lambda i, j: (0, j))

    def g_spec(branch):
        return pl.BlockSpec((tm, tn), lambda i, j: (i, branch * col_tiles + j))

    return pl.pallas_call(
        _merge_kernel,
        grid=(m // tm, col_tiles),
        in_specs=[y_spec(ya), y_spec(yb), y_spec(yc), w_spec(wa), w_spec(wb), w_spec(wc),
                  g_spec(0), g_spec(1), g_spec(2)],
        out_specs=pl.BlockSpec((tm, tn), lambda i, j: (i, j)),
        out_shape=jax.ShapeDtypeStruct((m, d_model), BF16),
        compiler_params=_params(("parallel", "arbitrary")),
        name="branch_merge",
    )(ya, yb, yc, wa, wb, wc, gates, gates, gates)


def _ln_kernel(x_ref, y_ref, g_ref, b_ref, o_ref, ob_ref):
    z = ALPHA * x_ref[...] + y_ref[...]
    mu = jnp.mean(z, axis=1, keepdims=True)
    zc = z - mu
    var = jnp.mean(zc * zc, axis=1, keepdims=True)
    out = zc * lax.rsqrt(var + LN_EPS) * g_ref[...] + b_ref[...]
    o_ref[...] = out
    ob_ref[...] = out.astype(BF16)


def _residual_ln(x, y, g, b):
    m, d = x.shape
    tm = min(256, m)
    row = pl.BlockSpec((tm, d), lambda i: (i, 0))
    vec = pl.BlockSpec((1, d), lambda i: (0, 0))
    return pl.pallas_call(
        _ln_kernel,
        grid=(m // tm,),
        in_specs=[row, row, vec, vec],
        out_specs=[row, row],
        out_shape=[jax.ShapeDtypeStruct((m, d), F32), jax.ShapeDtypeStruct((m, d), BF16)],
        compiler_params=_params(("parallel",)),
        name="residual_layer_norm",
    )(x, y, g.reshape(1, d), b.reshape(1, d))


def _rope_tables(seq):
    half = HEAD_DIM // 2
    inv = 1.0 / (ROPE_THETA ** (jnp.arange(half, dtype=F32) / half))
    ang = jnp.arange(seq).astype(F32)[:, None] * inv[None, :]
    cos, sin = jnp.cos(ang), jnp.sin(ang)
    return jnp.concatenate([cos, cos], axis=1), jnp.concatenate([-sin, sin], axis=1)


_A_KV, _B_KV = A_SLOTS, B_KV_HEADS
_ALIGNED_FIELDS = (('q_a', A_HEADS, 'rope_scale'), ('k_a', _A_KV, 'rope'), ('v_a', _A_KV, 'none'),
                   ('q_b', B_HEADS, 'rope_scale'), ('kc_b', _B_KV, 'rope'), ('vc_b', _B_KV, 'none'),
                   ('ks_b', _B_KV, 'rope'), ('vs_b', _B_KV, 'none'), ('kw_b', _B_KV, 'rope'),
                   ('vw_b', _B_KV, 'none'))
_STICK_FIELDS = (('q_c', C_HEADS, 'scale'), ('k_c', C_HEADS, 'none'), ('v_c', C_HEADS, 'none'))
N_NSA_GATES = 3 * B_HEADS


def _field_layout(fields, heads_per_tile):
    head0, modes, start = {}, [], 0
    for name, heads, mode in fields:
        assert heads % heads_per_tile == 0
        head0[name] = start
        modes += [mode] * (heads // heads_per_tile)
        start += heads
    return head0, tuple(modes), start


def _layer(layer, x, xb, nb, seq, tables, w_rows, cmp_pe_k, cmp_wk1, cmp_wk2, cmp_pe_v, cmp_wv1, cmp_wv2,
           w_br_a, w_br_b, w_br_c, w_out, ln1_g, ln1_b, w_up, w_down, ln2_g, ln2_b):
    m, d_model = x.shape

    tn_att = 4 * HEAD_DIM
    at, att_modes, att_heads = _field_layout(_ALIGNED_FIELDS, tn_att // HEAD_DIM)
    att = _matmul(xb, w_rows, layer=layer, b_rows=True, name="proj_attention", out_dtype=BF16,
                  tile_modes=att_modes, tm=1024, tn=tn_att, head_major=True, rope_tables=tables)
    tn_tail = 1024
    st, stick_modes, stick_heads = _field_layout(_STICK_FIELDS, tn_tail // HEAD_DIM)
    stick_col0 = att_heads * HEAD_DIM
    gates_col0 = stick_col0 + stick_heads * HEAD_DIM
    nsa_col0 = gates_col0 + 3 * d_model
    stick = _matmul(xb, w_rows, layer=layer, b_rows=True, col0=stick_col0, name="proj_stick",
                    out_dtype=BF16, tile_modes=stick_modes, tm=1024, tn=tn_tail, head_major=True)
    gates = _matmul(xb, w_rows, layer=layer, b_rows=True, col0=gates_col0, name="proj_branch_gates",
                    out_dtype=BF16, tile_modes=('sigmoid',) * (3 * d_model // tn_tail), tm=1024, tn=tn_tail)
    g_nsa = _matmul(xb, w_rows, layer=layer, b_rows=True, col0=nsa_col0, name="proj_nsa_gates",
                    out_dtype=F32, tile_modes=('sigmoid',), tm=1024, tn=HEAD_DIM)

    att = att.reshape(att_heads, nb, seq, HEAD_DIM)
    stick = stick.reshape(stick_heads, nb, seq, HEAD_DIM)

    y_a = _dilated_mixer(att, at['q_a'], at['k_a'], at['v_a'])

    kc = _compress(att, at['kc_b'], cmp_pe_k, cmp_wk1, cmp_wk2, "nsa_compress_k")
    vc = _compress(att, at['vc_b'], cmp_pe_v, cmp_wv1, cmp_wv2, "nsa_compress_v")
    o_cmp, sel = _cmp_select(att, at['q_b'], kc, vc)
    o_sel = _selected(att, at['q_b'], att, at['ks_b'], att, at['vs_b'], sel)
    o_win = _window(att, at['q_b'], att, at['kw_b'], att, at['vw_b'])
    y_b = _nsa_gate(g_nsa, o_cmp.reshape(B_HEADS, m, HEAD_DIM), o_sel.reshape(B_HEADS, m, HEAD_DIM),
                    o_win.reshape(B_HEADS, m, HEAD_DIM))

    y_c = _stick_breaking(stick, st['q_c'], stick, st['k_c'], stick, st['v_c'])

    merged = _merge(y_a.reshape(m, -1), y_b, y_c.reshape(m, -1), w_br_a.astype(BF16),
                    w_br_b.astype(BF16), w_br_c.astype(BF16), gates, d_model)
    mixed = _matmul(merged, w_out, layer=layer, name="out_proj", out_dtype=F32,
                    tile_modes=('none',) * (d_model // 512), tm=1024, tn=512)
    x1, x1b = _residual_ln(x, mixed, ln1_g, ln1_b)

    d_ff = w_up.shape[2]
    hidden = _matmul(x1b, w_up, layer=layer, name="mlp_up", out_dtype=BF16,
                     tile_modes=('relu2',) * (d_ff // 512), tm=1024, tn=512)
    down = _matmul(hidden, w_down, layer=layer, name="mlp_down", out_dtype=F32,
                   tile_modes=('none',) * (d_model // 1024), tm=2048, tn=1024, tk=1024)
    return _residual_ln(x1, down, ln2_g, ln2_b)


def kernel(x, w_in, cmp_pe_k, cmp_wk1, cmp_wk2, cmp_pe_v, cmp_wv1, cmp_wv2, w_br_a, w_br_b, w_br_c,
           w_out, ln1_g, ln1_b, w_up, w_down, ln2_g, ln2_b):
    nb, seq, d_model = x.shape
    tables = _rope_tables(seq)
    xf = x.reshape(nb * seq, d_model)
    xb = xf.astype(BF16)
    aligned_cols = sum(heads for _, heads, _ in _ALIGNED_FIELDS) * HEAD_DIM
    w_rows = _input_weight_rows(w_in, aligned_cols, N_NSA_GATES)
    for l in range(w_in.shape[0]):
        xf, xb = _layer(l, xf, xb, nb, seq, tables, w_rows, cmp_pe_k[l], cmp_wk1[l], cmp_wk2[l],
                        cmp_pe_v[l], cmp_wv1[l], cmp_wv2[l], w_br_a[l], w_br_b[l], w_br_c[l],
                        w_out, ln1_g[l], ln1_b[l], w_up, w_down, ln2_g[l], ln2_b[l])
    return xf.reshape(nb, seq, d_model)
```

```python
import functools
import math

import numpy as np
import jax
import jax.numpy as jnp
from jax import lax
from jax.experimental import pallas as pl
from jax.experimental.pallas import tpu as pltpu

F32 = jnp.float32
BF16 = jnp.bfloat16

HEAD_DIM = 128
ROPE_THETA = 10000.0
LN_EPS = 1e-5
DEPTH = 2

DIL_GROUPS = ((128, 1), (512, 4), (2048, 16))
A_SLOTS = 4
A_HEADS = A_SLOTS * len(DIL_GROUPS)

B_HEADS = 12
B_KV_HEADS = 4
B_GROUP = B_HEADS // B_KV_HEADS
CMP_LEN = 32
CMP_STRIDE = 16
SEL_LEN = 64
SEL_SHIFT = 6
SEL_TOPK = 16
WIN_LEN = 512

C_HEADS = 8

MASKED = -3e38
ALPHA = (2.0 * DEPTH) ** 0.25
Q_SCALE = HEAD_DIM ** -0.5

VMEM_LIMIT = 48 * 1024 * 1024
MXU_COLS = 256
ATT_TQ = 128
ATT_TILES = 8
DIL_ROWS = 128
DIL_ROWS_DENSE = 512
SEL_TQ = 128
SEL_TILES = 8
SEL_CHUNK = 1024
STICK_TILE = 512
STICK_HEADS = 4
CUM_TILE = 128


def _params(semantics):
    return pltpu.CompilerParams(dimension_semantics=semantics, vmem_limit_bytes=VMEM_LIMIT)


def _dot(a, b):
    return jnp.dot(a, b, preferred_element_type=F32)


def _dot_nt(a, b):
    return lax.dot_general(a, b, (((1,), (1,)), ((), ())), preferred_element_type=F32)


def _split_bf16(x):
    hi = x.astype(BF16)
    lo = (x - hi.astype(F32)).astype(BF16)
    return hi, lo


def _tile_flag(tile_modes, wanted):
    hits = [mode in wanted for mode in tile_modes]
    if all(hits) or not any(hits):
        return hits[0]
    j = pl.program_id(1)
    flag = None
    for c, hit in enumerate(hits):
        if hit:
            flag = (j == c) if flag is None else (flag | (j == c))
    return flag


def _write_columns(acc, o_ref, col0, tile_modes, head_major, cos_ref, sin_ref):
    is_rope = _tile_flag(tile_modes, ('rope', 'rope_scale'))
    is_scaled = _tile_flag(tile_modes, ('rope_scale', 'scale'))
    uniform = tile_modes[0] if len(set(tile_modes)) == 1 else None
    assert uniform is not None or not (set(tile_modes) & {'sigmoid', 'relu2'})

    def epilogue(y):
        if is_rope is not False:
            roped = y * cos_ref[...] + pltpu.roll(y, HEAD_DIM // 2, 1) * sin_ref[...]
            y = roped if is_rope is True else jnp.where(is_rope, roped, y)
        if is_scaled is not False:
            y = y * (Q_SCALE if is_scaled is True else jnp.where(is_scaled, Q_SCALE, 1.0))
        if uniform == 'sigmoid':
            y = jax.nn.sigmoid(y)
        if uniform == 'relu2':
            y = jnp.square(jnp.maximum(y, 0.0))
        return y

    width = acc.shape[1]
    if head_major:
        for c in range(width // HEAD_DIM):
            y = epilogue(acc[:, c * HEAD_DIM:(c + 1) * HEAD_DIM])
            o_ref[col0 // HEAD_DIM + c] = y.astype(o_ref.dtype)
    else:
        o_ref[:, col0:col0 + width] = epilogue(acc).astype(o_ref.dtype)


def _matmul_kernel(*refs, tile_modes, head_major, n_k, use_rope, acc_in_out, b_rows):
    refs = list(refs)
    a_ref, b_ref = refs[0], refs[1]
    cos_ref, sin_ref = (refs[2], refs[3]) if use_rope else (None, None)
    o_ref = refs[4] if use_rope else refs[2]
    acc_ref = o_ref if acc_in_out else (refs[-1] if n_k > 1 else None)
    tn = b_ref.shape[0] if b_rows else b_ref.shape[1]

    def product(col0, width):
        if b_rows:
            return _dot_nt(a_ref[...], b_ref[col0:col0 + width, :].astype(BF16))
        return _dot(a_ref[...], b_ref[:, col0:col0 + width].astype(BF16))

    if n_k == 1:
        width = min(MXU_COLS, tn)
        for col0 in range(0, tn, width):
            _write_columns(product(col0, width), o_ref, col0, tile_modes, head_major, cos_ref, sin_ref)
        return

    @pl.when(pl.program_id(2) == 0)
    def _():
        acc_ref[...] = jnp.zeros(acc_ref.shape, F32)

    acc_ref[...] += product(0, tn)
    if not acc_in_out:
        pl.when(pl.program_id(2) == n_k - 1)(
            lambda: _write_columns(acc_ref[...], o_ref, 0, tile_modes, head_major, cos_ref, sin_ref))


def _matmul(a, b, *, name, out_dtype, tile_modes, tm, tn, tk=None, layer=None, col0=0,
            head_major=False, rope_tables=None, b_rows=False):
    m, kdim = a.shape
    n = tn * len(tile_modes)
    tm = min(tm, m)
    tk = kdim if tk is None else min(tk, kdim)
    assert m % tm == 0 and kdim % tk == 0 and col0 % tn == 0
    n_k = kdim // tk
    col_tile0 = col0 // tn
    use_rope = any(mode.startswith('rope') for mode in tile_modes)
    acc_in_out = n_k > 1 and out_dtype == F32 and set(tile_modes) == {'none'} and not head_major
    if b_rows:
        assert layer is not None
        b_spec = pl.BlockSpec((None, tn, tk), lambda i, j, k: (layer, col_tile0 + j, k))
    elif layer is None:
        b_spec = pl.BlockSpec((tk, tn), lambda i, j, k: (k, col_tile0 + j))
    else:
        b_spec = pl.BlockSpec((None, tk, tn), lambda i, j, k: (layer, k, col_tile0 + j))
    in_specs = [pl.BlockSpec((tm, tk), lambda i, j, k: (i, k)), b_spec]
    operands = [a, b]
    if use_rope:
        assert head_major
        cos, sin = rope_tables
        seq_tiles = cos.shape[0] // tm
        assert cos.shape[0] % tm == 0
        spec = pl.BlockSpec((tm, HEAD_DIM), lambda i, j, k: (i % seq_tiles, 0))
        in_specs += [spec, spec]
        operands += [cos, sin]
    if head_major:
        out_shape = jax.ShapeDtypeStruct((n // HEAD_DIM, m, HEAD_DIM), out_dtype)
        out_spec = pl.BlockSpec((tn // HEAD_DIM, tm, HEAD_DIM), lambda i, j, k: (j, i, 0))
    else:
        out_shape = jax.ShapeDtypeStruct((m, n), out_dtype)
        out_spec = pl.BlockSpec((tm, tn), lambda i, j, k: (i, j))
    scratch = [pltpu.VMEM((tm, tn), F32)] if (n_k > 1 and not acc_in_out) else []
    return pl.pallas_call(
        functools.partial(_matmul_kernel, tile_modes=tuple(tile_modes), head_major=head_major, n_k=n_k,
                          use_rope=use_rope, acc_in_out=acc_in_out, b_rows=b_rows),
        grid=(m // tm, len(tile_modes), n_k),
        in_specs=in_specs,
        out_specs=out_spec,
        out_shape=out_shape,
        scratch_shapes=scratch,
        compiler_params=_params(("parallel", "parallel", "arbitrary")),
        name=name,
    )(*operands)


def _gather_rows_kernel(w_ref, o_ref, *, tn, k_chunks, n_layers):
    stride = k_chunks * n_layers
    by_chunk = pltpu.einshape("nck->cnk", w_ref[...].reshape(tn, stride, HEAD_DIM))
    for layer in range(n_layers):
        pieces = [by_chunk[c * n_layers + layer] for c in range(k_chunks)]
        o_ref[layer] = jnp.concatenate(pieces, axis=1).astype(o_ref.dtype)


def _input_weight_rows(w_in, aligned_cols, n_gates, tn=HEAD_DIM):
    n_layers, kdim, n = w_in.shape
    k_chunks = kdim // HEAD_DIM
    tail = n - aligned_cols - n_gates
    assert aligned_cols % tn == 0 and tail % tn == 0 and kdim % HEAD_DIM == 0
    aligned_tiles, tail_tiles = aligned_cols // tn, tail // tn
    rows_per_col = k_chunks * n_layers
    flat = w_in.reshape(n_layers, k_chunks, HEAD_DIM, n).transpose(3, 1, 0, 2).reshape(n * rows_per_col, HEAD_DIM)

    def source_row(j):
        col = jnp.where(j < aligned_tiles, tn * j,
                        jnp.where(j < aligned_tiles + tail_tiles,
                                  aligned_cols + n_gates + tn * (j - aligned_tiles), aligned_cols))
        return (col * rows_per_col, 0)

    tiles = aligned_tiles + tail_tiles + 1
    return pl.pallas_call(
        functools.partial(_gather_rows_kernel, tn=tn, k_chunks=k_chunks, n_layers=n_layers),
        grid=(tiles,),
        in_specs=[pl.BlockSpec((pl.Element(tn * rows_per_col), pl.Element(HEAD_DIM)), source_row)],
        out_specs=pl.BlockSpec((n_layers, tn, kdim), lambda j: (0, j, 0)),
        out_shape=jax.ShapeDtypeStruct((n_layers, tiles * tn, kdim), BF16),
        compiler_params=_params(("parallel",)),
        name="input_weight_rows",
    )(flat)


def _window_start(t0, back, span, seq, align):
    start = jnp.minimum(jnp.maximum(t0 - back, 0), seq - span)
    return pl.multiple_of(start, align)


def _by_residue(x, dil):
    rows = x.shape[0] // dil
    return pltpu.einshape("ldk->dlk", x.reshape(rows, dil, HEAD_DIM))


def _dilated_group_kernel(*refs, dil, rows, band, n_rows, n_others):
    q_ref, k_ref, v_ref = refs[:3]
    others = refs[3:3 + 2 * n_others]
    if n_others:
        y_ref = refs[3 + 2 * n_others]
        kr_ref, vr_ref, o_ref, lse_ref = refs[4 + 2 * n_others:]
    else:
        o_ref, lse_ref = refs[3:5]
        kr_ref, vr_ref = refs[5:] if dil > 1 else (None, None)
    j = pl.program_id(2)
    if dil > 1:

        @pl.when(j == 0)
        def _():
            kr_ref[...] = _by_residue(k_ref[...], dil)
            vr_ref[...] = _by_residue(v_ref[...], dil)

        q_by = _by_residue(q_ref[...], dil)
    l0 = j * rows
    span = min(band + rows, n_rows)
    start = _window_start(l0, band, span, n_rows, math.gcd(rows, band))
    dist = (l0 + lax.broadcasted_iota(jnp.int32, (rows, 1), 0)) - (start + lax.broadcasted_iota(jnp.int32, (1, span), 1))
    keep = (dist >= 0) & (dist <= band)
    for r in range(dil):
        if dil == 1:
            q, keys, vals = q_ref[...], k_ref[pl.ds(start, span), :], v_ref[pl.ds(start, span), :]
        else:
            q, keys, vals = q_by[r], kr_ref[r, pl.ds(start, span), :], vr_ref[r, pl.ds(start, span), :]
        s = jnp.where(keep, _dot_nt(q, keys), -jnp.inf)
        m = jnp.max(s, axis=1, keepdims=True)
        p = jnp.exp(s - m)
        l = jnp.sum(p, axis=1, keepdims=True)
        o = _dot(p.astype(BF16), vals) / l
        lse = jnp.broadcast_to(m + jnp.log(l), (rows, HEAD_DIM))
        if dil == 1:
            o_ref[...] = o
            lse_ref[...] = lse
        else:
            o_ref[pl.ds(r, rows, stride=dil), :] = o
            lse_ref[pl.ds(r, rows, stride=dil), :] = lse
    if n_others:
        outs = [others[2 * g][...] for g in range(n_others)] + [o_ref[...]]
        lses = [others[2 * g + 1][...] for g in range(n_others)] + [lse_ref[...]]
        top = functools.reduce(jnp.maximum, lses)
        ws = [jnp.exp(lse - top) for lse in lses]
        total = functools.reduce(lambda a, b: a + b, ws)
        mixed = functools.reduce(lambda a, b: a + b, [w * o for w, o in zip(ws, outs)])
        y_ref[...] = (mixed / total).astype(y_ref.dtype)


def _head_spec(rows, head0):
    return pl.BlockSpec((None, None, rows, HEAD_DIM), lambda b, h, n: (head0 + h, b, n, 0))


def _seq_spec(seq, head0):
    return pl.BlockSpec((None, None, seq, HEAD_DIM), lambda b, h, n: (head0 + h, b, 0, 0))


def _dilated_group(att, q_head0, k_head0, v_head0, window, dil, others=()):
    _, nb, seq, _ = att.shape
    n_rows = seq // dil
    band = window // dil
    rows = min(DIL_ROWS if dil > 1 else DIL_ROWS_DENSE, n_rows)
    assert seq % dil == 0 and n_rows % rows == 0 and (dil > 1 or not others)
    pair_spec = _head_spec(rows * dil, 0)
    pair_shape = jax.ShapeDtypeStruct((A_SLOTS, nb, seq, HEAD_DIM), F32)
    scratch = [pltpu.VMEM((dil, n_rows, HEAD_DIM), BF16)] * 2 if dil > 1 else []
    if others:
        out_specs = pl.BlockSpec((None, rows * dil, HEAD_DIM), lambda b, s, j: (b, j, s))
        out_shape = jax.ShapeDtypeStruct((nb, seq, A_SLOTS * HEAD_DIM), BF16)
        scratch = scratch + [pltpu.VMEM((rows * dil, HEAD_DIM), F32)] * 2
    else:
        out_specs, out_shape = [pair_spec, pair_spec], [pair_shape, pair_shape]
    return pl.pallas_call(
        functools.partial(_dilated_group_kernel, dil=dil, rows=rows, band=band, n_rows=n_rows,
                          n_others=len(others) // 2),
        grid=(nb, A_SLOTS, n_rows // rows),
        in_specs=[_head_spec(rows * dil, q_head0), _seq_spec(seq, k_head0), _seq_spec(seq, v_head0)]
        + [pair_spec] * len(others),
        out_specs=out_specs,
        out_shape=out_shape,
        scratch_shapes=scratch,
        compiler_params=_params(("parallel", "parallel", "arbitrary")),
        name="dilated_group_%d" % dil,
    )(att, att, att, *others)


def _dilated_mixer(att, q_head0, k_head0, v_head0):
    pairs = []
    for g, (window, dil) in enumerate(DIL_GROUPS):
        last = g == len(DIL_GROUPS) - 1
        result = _dilated_group(att, q_head0 + g * A_SLOTS, k_head0, v_head0, window, dil,
                                others=tuple(pairs) if last else ())
        if last:
            return result
        pairs += list(result)


def _gelu_tanh(x):
    return 0.5 * x * (1.0 + jnp.tanh(math.sqrt(2.0 / math.pi) * (x + 0.044715 * (x * x * x))))


def _compress_kernel(x_ref, pe_ref, w1_ref, w2_ref, o_ref):
    by_token = _by_residue(x_ref[...], CMP_STRIDE)
    x = jnp.concatenate([by_token[r] for r in range(CMP_STRIDE)], axis=1).astype(F32)
    first = _dot((x + pe_ref[0:1, :]).astype(BF16), w1_ref[0])
    second = _dot((x + pe_ref[1:2, :]).astype(BF16), w1_ref[1])
    chunks = x.shape[0]
    hidden = first + pltpu.roll(second, chunks - 1, 0)
    o_ref[...] = _dot(_gelu_tanh(hidden).astype(BF16), w2_ref[...]).astype(o_ref.dtype)


def _compress(t, head0, pe, w1, w2, name):
    _, nb, seq, _ = t.shape
    nh = B_KV_HEADS
    chunks = seq // CMP_STRIDE
    width = CMP_STRIDE * HEAD_DIM
    ratio = CMP_LEN // CMP_STRIDE
    return pl.pallas_call(
        _compress_kernel,
        grid=(nh, nb),
        in_specs=[pl.BlockSpec((None, None, seq, HEAD_DIM), lambda h, b: (head0 + h, b, 0, 0)),
                  pl.BlockSpec((ratio, width), lambda h, b: (0, 0)),
                  pl.BlockSpec((ratio, width, HEAD_DIM), lambda h, b: (0, 0, 0)),
                  pl.BlockSpec((HEAD_DIM, HEAD_DIM), lambda h, b: (0, 0))],
        out_specs=pl.BlockSpec((None, None, chunks, HEAD_DIM), lambda h, b: (h, b, 0, 0)),
        out_shape=jax.ShapeDtypeStruct((nh, nb, chunks, HEAD_DIM), BF16),
        compiler_params=_params(("parallel", "parallel")),
        name=name,
    )(t, pe.reshape(ratio, width), w1.reshape(ratio, width, HEAD_DIM).astype(BF16), w2.astype(BF16))


def _group_q(q_refs, row0=0, rows=None):
    rows = q_refs[0].shape[0] if rows is None else rows
    return jnp.concatenate([r[row0:row0 + rows, :] for r in q_refs], axis=0)


def _group_q_specs(tq, head0):
    return [pl.BlockSpec((None, None, tq, HEAD_DIM),
                         functools.partial(lambda b, h, n, g: (head0 + h * B_GROUP + g, b, n, 0), g=g))
            for g in range(B_GROUP)]


def _cmp_select_kernel(q0_ref, q1_ref, q2_ref, kc_ref, vc_ref, ov_ref, o_ref, sel_ref, *, tq, tiles):
    for u in range(tiles):
        _cmp_select_tile((q0_ref, q1_ref, q2_ref), kc_ref, vc_ref, ov_ref, o_ref, sel_ref,
                         (pl.program_id(2) * tiles + u) * tq, u * tq, tq)


def _cmp_select_tile(q_refs, kc_ref, vc_ref, ov_ref, o_ref, sel_ref, t0, row0, tq):
    n_cmp = kc_ref.shape[0]
    rows = B_GROUP * tq
    s = _dot_nt(_group_q(q_refs, row0, tq), kc_ref[...])
    tpos3 = t0 + (lax.broadcasted_iota(jnp.int32, (rows, 1), 0) & (tq - 1))
    c_end = lax.broadcasted_iota(jnp.int32, (1, n_cmp), 1) * CMP_STRIDE + (CMP_LEN - 1)
    s = jnp.where(c_end - tpos3 <= 0, s, -jnp.inf)
    m = jnp.max(s, axis=1, keepdims=True)
    m = jnp.where(jnp.abs(m) < jnp.inf, m, 0.0)
    e = jnp.exp(s - m)
    den = jnp.sum(e, axis=1, keepdims=True)
    p = e / jnp.where(den > 0, den, 1.0)
    o = _dot(p.astype(BF16), vc_ref[...]).reshape(B_GROUP, tq, HEAD_DIM)
    o_ref[:, row0:row0 + tq, :] = o.astype(o_ref.dtype)

    p_sum = p[0:tq] + p[tq:2 * tq] + p[2 * tq:3 * tq]
    hi, lo = _split_bf16(p_sum)
    imp = _dot(hi, ov_ref[...]) + _dot(lo, ov_ref[...])
    tpos = t0 + lax.broadcasted_iota(jnp.int32, (tq, 1), 0)
    rel = lax.broadcasted_iota(jnp.int32, (1, HEAD_DIM), 1) - (tpos >> SEL_SHIFT)
    j_abs = jnp.broadcast_to(lax.broadcasted_iota(jnp.int32, (1, HEAD_DIM), 1), rel.shape)
    forced = (j_abs == 0) | (rel == 0) | (rel == -1)
    imp = jnp.where(forced, jnp.inf, jnp.where(rel <= 0, imp, -jnp.inf))

    n_sel = HEAD_DIM // 2
    imp_t = imp.T
    mine = imp_t[0:n_sel]
    j_idx = lax.broadcasted_iota(jnp.int32, (n_sel, tq), 0)
    beaten = jnp.zeros((n_sel, tq), F32)
    for kk in range(n_sel):
        other = imp_t[kk:kk + 1, :]
        wins = (other > mine) | ((other == mine) & (j_idx > kk))
        beaten = beaten + jnp.where(wins, 1.0, 0.0)
    chosen = jnp.where(beaten < SEL_TOPK, 0.0, MASKED)
    chosen = jnp.concatenate([chosen, jnp.full((HEAD_DIM - n_sel, tq), MASKED, F32)], axis=0)
    sel_ref[:, row0:row0 + tq] = chosen.astype(sel_ref.dtype)


def _overlap_matrix(n_cmp_rows, n_sel):
    c_start = np.arange(n_cmp_rows) * CMP_STRIDE
    c_end = c_start + CMP_LEN - 1
    s_start = np.arange(HEAD_DIM) * SEL_LEN
    ov = (c_start[:, None] <= s_start[None, :] + SEL_LEN - 1) & (c_end[:, None] >= s_start[None, :])
    ov &= (np.arange(HEAD_DIM) < n_sel)[None, :]
    return jnp.asarray(ov.astype(np.float32), dtype=BF16)


def _group_o_spec(tq):
    return pl.BlockSpec((B_GROUP, None, tq, HEAD_DIM), lambda b, h, n: (h, b, n, 0))


def _sel_spec(tq):
    return pl.BlockSpec((None, None, HEAD_DIM, tq), lambda b, h, n: (h, b, 0, n))


def _cmp_select(q, q_head0, kc, vc):
    _, nb, seq, _ = q.shape
    assert seq // SEL_LEN <= HEAD_DIM // 2
    tq = min(ATT_TQ, seq)
    tiles = min(ATT_TILES, seq // tq)
    n_cmp = kc.shape[2]
    kv_spec = pl.BlockSpec((None, None, n_cmp, HEAD_DIM), lambda b, h, n: (h, b, 0, 0))
    return pl.pallas_call(
        functools.partial(_cmp_select_kernel, tq=tq, tiles=tiles),
        grid=(nb, B_KV_HEADS, seq // (tq * tiles)),
        in_specs=_group_q_specs(tq * tiles, q_head0) + [kv_spec, kv_spec,
                                                        pl.BlockSpec((n_cmp, HEAD_DIM), lambda b, h, n: (0, 0))],
        out_specs=[_group_o_spec(tq * tiles), _sel_spec(tq * tiles)],
        out_shape=[jax.ShapeDtypeStruct((B_HEADS, nb, seq, HEAD_DIM), BF16),
                   jax.ShapeDtypeStruct((B_KV_HEADS, nb, HEAD_DIM, seq), BF16)],
        compiler_params=_params(("parallel", "parallel", "arbitrary")),
        name="nsa_compressed_select",
    )(q, q, q, kc, vc, _overlap_matrix(n_cmp, seq // SEL_LEN))


def _selected_kernel(q0_ref, q1_ref, q2_ref, k_ref, v_ref, sel_ref, o_ref, vt_ref, acc_ref, *, tq, kc, tiles):
    n = pl.program_id(2)
    t_first = n * (tq * tiles)
    cols = B_GROUP * tq

    @pl.when(n == 0)
    def _():
        for c in range(v_ref.shape[0] // kc):
            vt_ref[c] = v_ref[c * kc:(c + 1) * kc, :].astype(F32).T.astype(BF16)

    qs = [_group_q((q0_ref, q1_ref, q2_ref), u * tq, tq) for u in range(tiles)]
    qposs = [t_first + u * tq + lax.broadcasted_iota(jnp.int32, (1, tq), 1) for u in range(tiles)]
    key_in_chunk = lax.broadcasted_iota(jnp.int32, (kc, 1), 0)
    blocks = kc // SEL_LEN
    acc_ref[...] = jnp.zeros(acc_ref.shape, F32)

    def chunk(i, carry, diagonal):
        k0 = pl.multiple_of(i * kc, kc)
        kpos = k0 + key_in_chunk
        keys = k_ref[pl.ds(k0, kc), :]
        blk0 = pl.multiple_of(i * blocks, blocks)
        out = []
        for u in range(tiles):
            m_old, l_old = carry[u]
            n_keys = tq * (u + 1) if (diagonal and tq * tiles == kc) else kc
            rows = sel_ref[pl.ds(blk0, blocks), u * tq:(u + 1) * tq].astype(F32)
            bias = jnp.broadcast_to(rows[:, None, :], (blocks, SEL_LEN, tq)).reshape(kc, tq)[:n_keys]
            if diagonal:
                bias = jnp.where(kpos[:n_keys] - qposs[u] <= 0, bias, MASKED)
            s = _dot_nt(keys[:n_keys], qs[u]) + jnp.concatenate([bias] * B_GROUP, axis=1)
            m_new = jnp.maximum(m_old, jnp.max(s, axis=0, keepdims=True))
            p = jnp.exp(s - m_new)
            alpha = jnp.exp(m_old - m_new)
            acc_ref[u] = alpha * acc_ref[u] + _dot(vt_ref[i, :, :n_keys], p.astype(BF16))
            out.append((m_new, alpha * l_old + jnp.sum(p, axis=0, keepdims=True)))
        return tuple(out)

    init = tuple((jnp.full((1, cols), MASKED, F32), jnp.zeros((1, cols), F32)) for _ in range(tiles))
    last = t_first // kc
    carry = lax.fori_loop(0, last, lambda i, c: chunk(i, c, False), init)
    final = chunk(last, carry, True)
    for u in range(tiles):
        o_t = acc_ref[u] / final[u][1]
        for g in range(B_GROUP):
            o_ref[g, u * tq:(u + 1) * tq, :] = o_t[:, g * tq:(g + 1) * tq].T.astype(o_ref.dtype)


def _selected(q, q_head0, k, k_head0, v, v_head0, sel):
    _, nb, seq, _ = q.shape
    tq = min(SEL_TQ, seq)
    kc = min(SEL_CHUNK, seq)
    tiles = min(SEL_TILES, kc // tq)
    assert kc % (tq * tiles) == 0 and seq % kc == 0 and tq == HEAD_DIM
    return pl.pallas_call(
        functools.partial(_selected_kernel, tq=tq, kc=kc, tiles=tiles),
        grid=(nb, B_KV_HEADS, seq // (tq * tiles)),
        in_specs=_group_q_specs(tq * tiles, q_head0) + [_seq_spec(seq, k_head0), _seq_spec(seq, v_head0),
                                                        _sel_spec(tq * tiles)],
        out_specs=_group_o_spec(tq * tiles),
        out_shape=jax.ShapeDtypeStruct((B_HEADS, nb, seq, HEAD_DIM), BF16),
        scratch_shapes=[pltpu.VMEM((seq // kc, HEAD_DIM, kc), BF16),
                        pltpu.VMEM((tiles, HEAD_DIM, B_GROUP * tq), F32)],
        compiler_params=_params(("parallel", "parallel", "arbitrary")),
        name="nsa_selected",
    )(q, q, q, k, v, sel)


def _window_kernel(q0_ref, q1_ref, q2_ref, k_ref, v_ref, o_ref, *, tq, tiles, seq):
    rows = B_GROUP * tq
    span = min(WIN_LEN + tq, seq)
    for u in range(tiles):
        t0 = (pl.program_id(2) * tiles + u) * tq
        start = _window_start(t0, WIN_LEN, span, seq, tq)
        s = _dot_nt(_group_q((q0_ref, q1_ref, q2_ref), u * tq, tq), k_ref[pl.ds(start, span), :])
        qpos = t0 + (lax.broadcasted_iota(jnp.int32, (rows, 1), 0) & (tq - 1))
        kpos = start + lax.broadcasted_iota(jnp.int32, (1, span), 1)
        dist = qpos - kpos
        s = jnp.where((dist >= 0) & (dist <= WIN_LEN - 1), s, -jnp.inf)
        m = jnp.max(s, axis=1, keepdims=True)
        p = jnp.exp(s - m)
        l = jnp.sum(p, axis=1, keepdims=True)
        o = _dot(p.astype(BF16), v_ref[pl.ds(start, span), :]) / l
        o_ref[:, u * tq:(u + 1) * tq, :] = o.reshape(B_GROUP, tq, HEAD_DIM).astype(o_ref.dtype)


def _window(q, q_head0, k, k_head0, v, v_head0):
    _, nb, seq, _ = q.shape
    tq = min(ATT_TQ, seq)
    tiles = min(ATT_TILES, seq // tq)
    assert WIN_LEN % tq == 0
    return pl.pallas_call(
        functools.partial(_window_kernel, tq=tq, tiles=tiles, seq=seq),
        grid=(nb, B_KV_HEADS, seq // (tq * tiles)),
        in_specs=_group_q_specs(tq * tiles, q_head0) + [_seq_spec(seq, k_head0), _seq_spec(seq, v_head0)],
        out_specs=_group_o_spec(tq * tiles),
        out_shape=jax.ShapeDtypeStruct((B_HEADS, nb, seq, HEAD_DIM), BF16),
        compiler_params=_params(("parallel", "parallel", "arbitrary")),
        name="nsa_window",
    )(q, q, q, k, v)


def _nsa_gate_kernel(g_ref, oc_ref, os_ref, ow_ref, y_ref):
    g = g_ref[...]
    for h in range(B_HEADS):
        y = (g[:, 3 * h:3 * h + 1] * oc_ref[h].astype(F32)
             + g[:, 3 * h + 1:3 * h + 2] * os_ref[h].astype(F32)
             + g[:, 3 * h + 2:3 * h + 3] * ow_ref[h].astype(F32))
        y_ref[:, h * HEAD_DIM:(h + 1) * HEAD_DIM] = y.astype(y_ref.dtype)


def _nsa_gate(gates, o_cmp, o_sel, o_win):
    rows = gates.shape[0]
    tm = min(256, rows)
    o_spec = pl.BlockSpec((B_HEADS, tm, HEAD_DIM), lambda i: (0, i, 0))
    return pl.pallas_call(
        _nsa_gate_kernel,
        grid=(rows // tm,),
        in_specs=[pl.BlockSpec((tm, HEAD_DIM), lambda i: (i, 0)), o_spec, o_spec, o_spec],
        out_specs=pl.BlockSpec((tm, B_HEADS * HEAD_DIM), lambda i: (i, 0)),
        out_shape=jax.ShapeDtypeStruct((rows, B_HEADS * HEAD_DIM), BF16),
        compiler_params=_params(("parallel",)),
        name="nsa_gate",
    )(gates, o_cmp, o_sel, o_win)


def _stick_kernel(*refs, tq, heads):
    q_refs, k_refs, v_refs = refs[:heads], refs[heads:2 * heads], refs[2 * heads:3 * heads]
    o_ref, acc_ref, car_ref = refs[3 * heads:]
    n = pl.program_id(2)
    sub = min(CUM_TILE, tq)
    n_sub = tq // sub
    r_idx = lax.broadcasted_iota(jnp.int32, (2 * sub, 2 * sub), 0) & (sub - 1)
    c_idx = lax.broadcasted_iota(jnp.int32, (2 * sub, 2 * sub), 1)
    suffix_and_total = jnp.where((c_idx >= sub) | (r_idx > c_idx), 1.0, 0.0).astype(BF16)
    k_in_sub = lax.broadcasted_iota(jnp.int32, (1, sub), 1)
    acc_ref[...] = jnp.zeros(acc_ref.shape, F32)
    car_ref[...] = jnp.zeros(car_ref.shape, F32)

    q_in_tile = lax.broadcasted_iota(jnp.int32, (tq, 1), 0)

    def tile(k0, diagonal):
        for h in range(heads):
            z = _dot_nt(q_refs[h][...], k_refs[h][pl.ds(k0, tq), :])
            soft = jnp.log(1.0 + jnp.exp(-jnp.abs(z)))
            log_beta = jnp.minimum(z, 0.0) - soft
            log_rest = log_beta - z
            carried = car_ref[h]
            parts = [None] * n_sub
            for u in reversed(range(n_sub)):
                cols = slice(u * sub, (u + 1) * sub)
                rest_u = log_rest[:, cols]
                if diagonal:
                    before = (u * sub + k_in_sub) - q_in_tile < 0
                    rest_u = jnp.where(before, rest_u, 0.0)
                sums = _dot(jnp.concatenate(_split_bf16(rest_u), axis=1), suffix_and_total)
                a = jnp.exp(log_beta[:, cols] + (sums[:, :sub] + carried))
                if diagonal:
                    a = jnp.where(before, a, 0.0)
                carried = carried + sums[:, sub:]
                parts[u] = a.astype(BF16)
            car_ref[h] = carried
            acc_ref[h] += _dot(jnp.concatenate(parts, axis=1), v_refs[h][pl.ds(k0, tq), :])

    tile(pl.multiple_of(n * tq, tq), True)

    def step(i, carry):
        tile(pl.multiple_of((n - 1 - i) * tq, tq), False)
        return carry

    lax.fori_loop(0, n, step, 0)
    for h in range(heads):
        o_ref[:, h * HEAD_DIM:(h + 1) * HEAD_DIM] = acc_ref[h].astype(o_ref.dtype)


def _stick_breaking(q, q_head0, k, k_head0, v, v_head0):
    _, nb, seq, _ = q.shape
    tq = min(STICK_TILE, seq)
    heads = STICK_HEADS
    sub = min(CUM_TILE, tq)

    def per_head(make, head0):
        return [make(head0 + h) for h in range(heads)]

    def q_spec(head):
        return pl.BlockSpec((None, None, tq, HEAD_DIM), lambda b, g, n: (head + g * heads, b, n, 0))

    def kv_spec(head):
        return pl.BlockSpec((None, None, seq, HEAD_DIM), lambda b, g, n: (head + g * heads, b, 0, 0))

    return pl.pallas_call(
        functools.partial(_stick_kernel, tq=tq, heads=heads),
        grid=(nb, C_HEADS // heads, seq // tq),
        in_specs=per_head(q_spec, q_head0) + per_head(kv_spec, k_head0) + per_head(kv_spec, v_head0),
        out_specs=pl.BlockSpec((None, tq, heads * HEAD_DIM), lambda b, g, n: (b, n, g)),
        out_shape=jax.ShapeDtypeStruct((nb, seq, C_HEADS * HEAD_DIM), BF16),
        scratch_shapes=[pltpu.VMEM((heads, tq, HEAD_DIM), F32), pltpu.VMEM((heads, tq, sub), F32)],
        compiler_params=_params(("parallel", "parallel", "arbitrary")),
        name="stick_breaking",
    )(*([q] * heads + [k] * heads + [v] * heads))


def _merge_kernel(ya_ref, yb_ref, yc_ref, wa_ref, wb_ref, wc_ref, ga_ref, gb_ref, gc_ref, o_ref):
    mixed = (ga_ref[...].astype(F32) * _dot(ya_ref[...], wa_ref[...])
             + gb_ref[...].astype(F32) * _dot(yb_ref[...], wb_ref[...])
             + gc_ref[...].astype(F32) * _dot(yc_ref[...], wc_ref[...]))
    o_ref[...] = mixed.astype(o_ref.dtype)


def _merge(ya, yb, yc, wa, wb, wc, gates, d_model):
    m = ya.shape[0]
    tm, tn = min(512, m), 1024
    col_tiles = d_model // tn

    def y_spec(y):
        return pl.BlockSpec((tm, y.shape[1]), lambda i, j: (i, 0))

    def w_spec(w):
        return pl.BlockSpec((w.shape[0], tn), lambda i, j: (0, j))

    def g_spec(branch):
        return pl.BlockSpec((tm, tn), lambda i, j: (i, branch * col_tiles + j))

    return pl.pallas_call(
        _merge_kernel,
        grid=(m // tm, col_tiles),
        in_specs=[y_spec(ya), y_spec(yb), y_spec(yc), w_spec(wa), w_spec(wb), w_spec(wc),
                  g_spec(0), g_spec(1), g_spec(2)],
        out_specs=pl.BlockSpec((tm, tn), lambda i, j: (i, j)),
        out_shape=jax.ShapeDtypeStruct((m, d_model), BF16),
        compiler_params=_params(("parallel", "arbitrary")),
        name="branch_merge",
    )(ya, yb, yc, wa, wb, wc, gates, gates, gates)


def _ln_kernel(x_ref, y_ref, g_ref, b_ref, o_ref, ob_ref):
    z = ALPHA * x_ref[...] + y_ref[...]
    mu = jnp.mean(z, axis=1, keepdims=True)
    zc = z - mu
    var = jnp.mean(zc * zc, axis=1, keepdims=True)
    out = zc * lax.rsqrt(var + LN_EPS) * g_ref[...] + b_ref[...]
    o_ref[...] = out
    ob_ref[...] = out.astype(BF16)


def _residual_ln(x, y, g, b):
    m, d = x.shape
    tm = min(256, m)
    row = pl.BlockSpec((tm, d), lambda i: (i, 0))
    vec = pl.BlockSpec((1, d), lambda i: (0, 0))
    return pl.pallas_call(
        _ln_kernel,
        grid=(m // tm,),
        in_specs=[row, row, vec, vec],
        out_specs=[row, row],
        out_shape=[jax.ShapeDtypeStruct((m, d), F32), jax.ShapeDtypeStruct((m, d), BF16)],
        compiler_params=_params(("parallel",)),
        name="residual_layer_norm",
    )(x, y, g.reshape(1, d), b.reshape(1, d))


def _rope_tables(seq):
    half = HEAD_DIM // 2
    inv = 1.0 / (ROPE_THETA ** (jnp.arange(half, dtype=F32) / half))
    ang = jnp.arange(seq).astype(F32)[:, None] * inv[None, :]
    cos, sin = jnp.cos(ang), jnp.sin(ang)
    return jnp.concatenate([cos, cos], axis=1), jnp.concatenate([-sin, sin], axis=1)


_A_KV, _B_KV = A_SLOTS, B_KV_HEADS
_ALIGNED_FIELDS = (('q_a', A_HEADS, 'rope_scale'), ('k_a', _A_KV, 'rope'), ('v_a', _A_KV, 'none'),
                   ('q_b', B_HEADS, 'rope_scale'), ('kc_b', _B_KV, 'rope'), ('vc_b', _B_KV, 'none'),
                   ('ks_b', _B_KV, 'rope'), ('vs_b', _B_KV, 'none'), ('kw_b', _B_KV, 'rope'),
                   ('vw_b', _B_KV, 'none'))
_STICK_FIELDS = (('q_c', C_HEADS, 'scale'), ('k_c', C_HEADS, 'none'), ('v_c', C_HEADS, 'none'))
N_NSA_GATES = 3 * B_HEADS


def _field_layout(fields, heads_per_tile):
    head0, modes, start = {}, [], 0
    for name, heads, mode in fields:
        assert heads % heads_per_tile == 0
        head0[name] = start
        modes += [mode] * (heads // heads_per_tile)
        start += heads
    return head0, tuple(modes), start


def _layer(layer, x, xb, nb, seq, tables, w_rows, cmp_pe_k, cmp_wk1, cmp_wk2, cmp_pe_v, cmp_wv1, cmp_wv2,
           w_br_a, w_br_b, w_br_c, w_out, ln1_g, ln1_b, w_up, w_down, ln2_g, ln2_b):
    m, d_model = x.shape

    tn_att = 4 * HEAD_DIM
    at, att_modes, att_heads = _field_layout(_ALIGNED_FIELDS, tn_att // HEAD_DIM)
    att = _matmul(xb, w_rows, layer=layer, b_rows=True, name="proj_attention", out_dtype=BF16,
                  tile_modes=att_modes, tm=1024, tn=tn_att, head_major=True, rope_tables=tables)
    tn_tail = 1024
    st, stick_modes, stick_heads = _field_layout(_STICK_FIELDS, tn_tail // HEAD_DIM)
    stick_col0 = att_heads * HEAD_DIM
    gates_col0 = stick_col0 + stick_heads * HEAD_DIM
    nsa_col0 = gates_col0 + 3 * d_model
    stick = _matmul(xb, w_rows, layer=layer, b_rows=True, col0=stick_col0, name="proj_stick",
                    out_dtype=BF16, tile_modes=stick_modes, tm=1024, tn=tn_tail, head_major=True)
    gates = _matmul(xb, w_rows, layer=layer, b_rows=True, col0=gates_col0, name="proj_branch_gates",
                    out_dtype=BF16, tile_modes=('sigmoid',) * (3 * d_model // tn_tail), tm=1024, tn=tn_tail)
    g_nsa = _matmul(xb, w_rows, layer=layer, b_rows=True, col0=nsa_col0, name="proj_nsa_gates",
                    out_dtype=F32, tile_modes=('sigmoid',), tm=1024, tn=HEAD_DIM)

    att = att.reshape(att_heads, nb, seq, HEAD_DIM)
    stick = stick.reshape(stick_heads, nb, seq, HEAD_DIM)

    y_a = _dilated_mixer(att, at['q_a'], at['k_a'], at['v_a'])

    kc = _compress(att, at['kc_b'], cmp_pe_k, cmp_wk1, cmp_wk2, "nsa_compress_k")
    vc = _compress(att, at['vc_b'], cmp_pe_v, cmp_wv1, cmp_wv2, "nsa_compress_v")
    o_cmp, sel = _cmp_select(att, at['q_b'], kc, vc)
    o_sel = _selected(att, at['q_b'], att, at['ks_b'], att, at['vs_b'], sel)
    o_win = _window(att, at['q_b'], att, at['kw_b'], att, at['vw_b'])
    y_b = _nsa_gate(g_nsa, o_cmp.reshape(B_HEADS, m, HEAD_DIM), o_sel.reshape(B_HEADS, m, HEAD_DIM),
                    o_win.reshape(B_HEADS, m, HEAD_DIM))

    y_c = _stick_breaking(stick, st['q_c'], stick, st['k_c'], stick, st['v_c'])

    merged = _merge(y_a.reshape(m, -1), y_b, y_c.reshape(m, -1), w_br_a.astype(BF16),
                    w_br_b.astype(BF16), w_br_c.astype(BF16), gates, d_model)
    mixed = _matmul(merged, w_out, layer=layer, name="out_proj", out_dtype=F32,
                    tile_modes=('none',) * (d_model // 512), tm=1024, tn=512)
    x1, x1b = _residual_ln(x, mixed, ln1_g, ln1_b)

    d_ff = w_up.shape[2]
    hidden = _matmul(x1b, w_up, layer=layer, name="mlp_up", out_dtype=BF16,
                     tile_modes=('relu2',) * (d_ff // 512), tm=1024, tn=512)
    down = _matmul(hidden, w_down, layer=layer, name="mlp_down", out_dtype=F32,
                   tile_modes=('none',) * (d_model // 1024), tm=2048, tn=1024, tk=1024)
    return _residual_ln(x1, down, ln2_g, ln2_b)


def kernel(x, w_in, cmp_pe_k, cmp_wk1, cmp_wk2, cmp_pe_v, cmp_wv1, cmp_wv2, w_br_a, w_br_b, w_br_c,
           w_out, ln1_g, ln1_b, w_up, w_down, ln2_g, ln2_b):
    nb, seq, d_model = x.shape
    tables = _rope_tables(seq)
    xf = x.reshape(nb * seq, d_model)
    xb = xf.astype(BF16)
    aligned_cols = sum(heads for _, heads, _ in _ALIGNED_FIELDS) * HEAD_DIM
    w_rows = _input_weight_rows(w_in, aligned_cols, N_NSA_GATES)
    for l in range(w_in.shape[0]):
        xf, xb = _layer(l, xf, xb, nb, seq, tables, w_rows, cmp_pe_k[l], cmp_wk1[l], cmp_wk2[l],
                        cmp_pe_v[l], cmp_wv1[l], cmp_wv2[l], w_br_a[l], w_br_b[l], w_br_c[l],
                        w_out, ln1_g[l], ln1_b[l], w_up, w_down, ln2_g[l], ln2_b[l])
    return xf.reshape(nb, seq, d_model)
```

```python
import functools
import math

import numpy as np
import jax
import jax.numpy as jnp
from jax import lax
from jax.experimental import pallas as pl
from jax.experimental.pallas import tpu as pltpu

F32 = jnp.float32
BF16 = jnp.bfloat16

HEAD_DIM = 128
ROPE_THETA = 10000.0
LN_EPS = 1e-5
DEPTH = 2

DIL_GROUPS = ((128, 1), (512, 4), (2048, 16))
A_SLOTS = 4
A_HEADS = A_SLOTS * len(DIL_GROUPS)

B_HEADS = 12
B_KV_HEADS = 4
B_GROUP = B_HEADS // B_KV_HEADS
CMP_LEN = 32
CMP_STRIDE = 16
SEL_LEN = 64
SEL_SHIFT = 6
SEL_TOPK = 16
WIN_LEN = 512

C_HEADS = 8

MASKED = -3e38
ALPHA = (2.0 * DEPTH) ** 0.25
Q_SCALE = HEAD_DIM ** -0.5

VMEM_LIMIT = 48 * 1024 * 1024
MXU_COLS = 256
ATT_TQ = 128
ATT_TILES = 8
DIL_ROWS = 128
DIL_ROWS_DENSE = 512
SEL_TQ = 128
SEL_TILES = 8
SEL_CHUNK = 1024
STICK_TILE = 512
STICK_HEADS = 4
CUM_TILE = 128


def _params(semantics):
    return pltpu.CompilerParams(dimension_semantics=semantics, vmem_limit_bytes=VMEM_LIMIT)


def _dot(a, b):
    return jnp.dot(a, b, preferred_element_type=F32)


def _dot_nt(a, b):
    return lax.dot_general(a, b, (((1,), (1,)), ((), ())), preferred_element_type=F32)


def _split_bf16(x):
    hi = x.astype(BF16)
    lo = (x - hi.astype(F32)).astype(BF16)
    return hi, lo


def _tile_flag(tile_modes, wanted):
    hits = [mode in wanted for mode in tile_modes]
    if all(hits) or not any(hits):
        return hits[0]
    j = pl.program_id(1)
    flag = None
    for c, hit in enumerate(hits):
        if hit:
            flag = (j == c) if flag is None else (flag | (j == c))
    return flag


def _write_columns(acc, o_ref, col0, tile_modes, head_major, cos_ref, sin_ref):
    is_rope = _tile_flag(tile_modes, ('rope', 'rope_scale'))
    is_scaled = _tile_flag(tile_modes, ('rope_scale', 'scale'))
    uniform = tile_modes[0] if len(set(tile_modes)) == 1 else None
    assert uniform is not None or not (set(tile_modes) & {'sigmoid', 'relu2'})

    def epilogue(y):
        if is_rope is not False:
            roped = y * cos_ref[...] + pltpu.roll(y, HEAD_DIM // 2, 1) * sin_ref[...]
            y = roped if is_rope is True else jnp.where(is_rope, roped, y)
        if is_scaled is not False:
            y = y * (Q_SCALE if is_scaled is True else jnp.where(is_scaled, Q_SCALE, 1.0))
        if uniform == 'sigmoid':
            y = jax.nn.sigmoid(y)
        if uniform == 'relu2':
            y = jnp.square(jnp.maximum(y, 0.0))
        return y

    width = acc.shape[1]
    if head_major:
        for c in range(width // HEAD_DIM):
            y = epilogue(acc[:, c * HEAD_DIM:(c + 1) * HEAD_DIM])
            o_ref[col0 // HEAD_DIM + c] = y.astype(o_ref.dtype)
    else:
        o_ref[:, col0:col0 + width] = epilogue(acc).astype(o_ref.dtype)


def _matmul_kernel(*refs, tile_modes, units, head_major, n_k, use_rope, acc_in_out, b_rows):
    refs = list(refs)
    a_ref, b_ref = refs[0], refs[1]
    cos_ref, sin_ref = (refs[2], refs[3]) if use_rope else (None, None)
    o_ref = refs[4] if use_rope else refs[2]
    acc_ref = o_ref if acc_in_out else (refs[-1] if n_k > 1 else None)
    tn = b_ref.shape[0] if b_rows else b_ref.shape[1]

    def product(col0, width):
        if b_rows:
            return _dot_nt(a_ref[...], b_ref[col0:col0 + width, :].astype(BF16))
        return _dot(a_ref[...], b_ref[:, col0:col0 + width].astype(BF16))

    if n_k == 1:
        unit_cols = tn // units
        width = min(MXU_COLS, unit_cols)
        for col0 in range(0, tn, width):
            per_tile = tile_modes[col0 // unit_cols::units]
            _write_columns(product(col0, width), o_ref, col0, per_tile, head_major, cos_ref, sin_ref)
        return
    assert units == 1

    @pl.when(pl.program_id(2) == 0)
    def _():
        acc_ref[...] = jnp.zeros(acc_ref.shape, F32)

    acc_ref[...] += product(0, tn)
    if not acc_in_out:
        pl.when(pl.program_id(2) == n_k - 1)(
            lambda: _write_columns(acc_ref[...], o_ref, 0, tile_modes, head_major, cos_ref, sin_ref))


def _matmul(a, b, *, name, out_dtype, tile_modes, tm, tn, tk=None, layer=None, col0=0,
            head_major=False, rope_tables=None, b_rows=False, mode_cols=None):
    m, kdim = a.shape
    mode_cols = tn if mode_cols is None else mode_cols
    n = mode_cols * len(tile_modes)
    assert tn % mode_cols == 0 and n % tn == 0
    tm = min(tm, m)
    tk = kdim if tk is None else min(tk, kdim)
    assert m % tm == 0 and kdim % tk == 0 and col0 % tn == 0
    n_k = kdim // tk
    col_tile0 = col0 // tn
    use_rope = any(mode.startswith('rope') for mode in tile_modes)
    acc_in_out = n_k > 1 and out_dtype == F32 and set(tile_modes) == {'none'} and not head_major
    if b_rows:
        assert layer is not None
        b_spec = pl.BlockSpec((None, tn, tk), lambda i, j, k: (layer, col_tile0 + j, k))
    elif layer is None:
        b_spec = pl.BlockSpec((tk, tn), lambda i, j, k: (k, col_tile0 + j))
    else:
        b_spec = pl.BlockSpec((None, tk, tn), lambda i, j, k: (layer, k, col_tile0 + j))
    in_specs = [pl.BlockSpec((tm, tk), lambda i, j, k: (i, k)), b_spec]
    operands = [a, b]
    if use_rope:
        assert head_major
        cos, sin = rope_tables
        seq_tiles = cos.shape[0] // tm
        assert cos.shape[0] % tm == 0
        spec = pl.BlockSpec((tm, HEAD_DIM), lambda i, j, k: (i % seq_tiles, 0))
        in_specs += [spec, spec]
        operands += [cos, sin]
    if head_major:
        out_shape = jax.ShapeDtypeStruct((n // HEAD_DIM, m, HEAD_DIM), out_dtype)
        out_spec = pl.BlockSpec((tn // HEAD_DIM, tm, HEAD_DIM), lambda i, j, k: (j, i, 0))
    else:
        out_shape = jax.ShapeDtypeStruct((m, n), out_dtype)
        out_spec = pl.BlockSpec((tm, tn), lambda i, j, k: (i, j))
    scratch = [pltpu.VMEM((tm, tn), F32)] if (n_k > 1 and not acc_in_out) else []
    return pl.pallas_call(
        functools.partial(_matmul_kernel, tile_modes=tuple(tile_modes), units=tn // mode_cols,
                          head_major=head_major, n_k=n_k, use_rope=use_rope, acc_in_out=acc_in_out,
                          b_rows=b_rows),
        grid=(m // tm, n // tn, n_k),
        in_specs=in_specs,
        out_specs=out_spec,
        out_shape=out_shape,
        scratch_shapes=scratch,
        compiler_params=_params(("parallel", "parallel", "arbitrary")),
        name=name,
    )(*operands)


def _gather_rows_kernel(w_ref, o_ref, *, tn, k_chunks, n_layers):
    stride = k_chunks * n_layers
    by_chunk = pltpu.einshape("nck->cnk", w_ref[...].reshape(tn, stride, HEAD_DIM))
    for layer in range(n_layers):
        pieces = [by_chunk[c * n_layers + layer] for c in range(k_chunks)]
        o_ref[layer] = jnp.concatenate(pieces, axis=1).astype(o_ref.dtype)


def _input_weight_rows(w_in, aligned_cols, n_gates, tn=HEAD_DIM):
    n_layers, kdim, n = w_in.shape
    k_chunks = kdim // HEAD_DIM
    tail = n - aligned_cols - n_gates
    assert aligned_cols % tn == 0 and tail % tn == 0 and kdim % HEAD_DIM == 0
    aligned_tiles, tail_tiles = aligned_cols // tn, tail // tn
    rows_per_col = k_chunks * n_layers
    flat = w_in.reshape(n_layers, k_chunks, HEAD_DIM, n).transpose(3, 1, 0, 2).reshape(n * rows_per_col, HEAD_DIM)

    def source_row(j):
        col = jnp.where(j < aligned_tiles, tn * j,
                        jnp.where(j < aligned_tiles + tail_tiles,
                                  aligned_cols + n_gates + tn * (j - aligned_tiles), aligned_cols))
        return (col * rows_per_col, 0)

    tiles = aligned_tiles + tail_tiles + 1
    return pl.pallas_call(
        functools.partial(_gather_rows_kernel, tn=tn, k_chunks=k_chunks, n_layers=n_layers),
        grid=(tiles,),
        in_specs=[pl.BlockSpec((pl.Element(tn * rows_per_col), pl.Element(HEAD_DIM)), source_row)],
        out_specs=pl.BlockSpec((n_layers, tn, kdim), lambda j: (0, j, 0)),
        out_shape=jax.ShapeDtypeStruct((n_layers, tiles * tn, kdim), BF16),
        compiler_params=_params(("parallel",)),
        name="input_weight_rows",
    )(flat)


def _window_start(t0, back, span, seq, align):
    start = jnp.minimum(jnp.maximum(t0 - back, 0), seq - span)
    return pl.multiple_of(start, align)


def _by_residue(x, dil):
    rows = x.shape[0] // dil
    return pltpu.einshape("ldk->dlk", x.reshape(rows, dil, HEAD_DIM))


def _dilated_group_kernel(*refs, dil, rows, band, n_rows, n_others):
    q_ref, k_ref, v_ref = refs[:3]
    others = refs[3:3 + 2 * n_others]
    if n_others:
        y_ref = refs[3 + 2 * n_others]
        kr_ref, vr_ref, o_ref, lse_ref = refs[4 + 2 * n_others:]
    else:
        o_ref, lse_ref = refs[3:5]
        kr_ref, vr_ref = refs[5:] if dil > 1 else (None, None)
    j = pl.program_id(2)
    if dil > 1:

        @pl.when(j == 0)
        def _():
            kr_ref[...] = _by_residue(k_ref[...], dil)
            vr_ref[...] = _by_residue(v_ref[...], dil)

        q_by = _by_residue(q_ref[...], dil)
    l0 = j * rows
    span = min(band + rows, n_rows)
    start = _window_start(l0, band, span, n_rows, math.gcd(rows, band))
    dist = (l0 + lax.broadcasted_iota(jnp.int32, (rows, 1), 0)) - (start + lax.broadcasted_iota(jnp.int32, (1, span), 1))
    keep = (dist >= 0) & (dist <= band)
    for r in range(dil):
        if dil == 1:
            q, keys, vals = q_ref[...], k_ref[pl.ds(start, span), :], v_ref[pl.ds(start, span), :]
        else:
            q, keys, vals = q_by[r], kr_ref[r, pl.ds(start, span), :], vr_ref[r, pl.ds(start, span), :]
        s = jnp.where(keep, _dot_nt(q, keys), -jnp.inf)
        m = jnp.max(s, axis=1, keepdims=True)
        p = jnp.exp(s - m)
        l = jnp.sum(p, axis=1, keepdims=True)
        o = _dot(p.astype(BF16), vals) / l
        lse = jnp.broadcast_to(m + jnp.log(l), (rows, HEAD_DIM))
        if dil == 1:
            o_ref[...] = o
            lse_ref[...] = lse
        else:
            o_ref[pl.ds(r, rows, stride=dil), :] = o
            lse_ref[pl.ds(r, rows, stride=dil), :] = lse
    if n_others:
        outs = [others[2 * g][...] for g in range(n_others)] + [o_ref[...]]
        lses = [others[2 * g + 1][...] for g in range(n_others)] + [lse_ref[...]]
        top = functools.reduce(jnp.maximum, lses)
        ws = [jnp.exp(lse - top) for lse in lses]
        total = functools.reduce(lambda a, b: a + b, ws)
        mixed = functools.reduce(lambda a, b: a + b, [w * o for w, o in zip(ws, outs)])
        y_ref[...] = (mixed / total).astype(y_ref.dtype)


def _head_spec(rows, head0):
    return pl.BlockSpec((None, None, rows, HEAD_DIM), lambda b, h, n: (head0 + h, b, n, 0))


def _seq_spec(seq, head0):
    return pl.BlockSpec((None, None, seq, HEAD_DIM), lambda b, h, n: (head0 + h, b, 0, 0))


def _dilated_group(att, q_head0, k_head0, v_head0, window, dil, others=()):
    _, nb, seq, _ = att.shape
    n_rows = seq // dil
    band = window // dil
    rows = min(DIL_ROWS if dil > 1 else DIL_ROWS_DENSE, n_rows)
    assert seq % dil == 0 and n_rows % rows == 0 and (dil > 1 or not others)
    pair_spec = _head_spec(rows * dil, 0)
    pair_shape = jax.ShapeDtypeStruct((A_SLOTS, nb, seq, HEAD_DIM), F32)
    scratch = [pltpu.VMEM((dil, n_rows, HEAD_DIM), BF16)] * 2 if dil > 1 else []
    if others:
        out_specs = pl.BlockSpec((None, rows * dil, HEAD_DIM), lambda b, s, j: (b, j, s))
        out_shape = jax.ShapeDtypeStruct((nb, seq, A_SLOTS * HEAD_DIM), BF16)
        scratch = scratch + [pltpu.VMEM((rows * dil, HEAD_DIM), F32)] * 2
    else:
        out_specs, out_shape = [pair_spec, pair_spec], [pair_shape, pair_shape]
    return pl.pallas_call(
        functools.partial(_dilated_group_kernel, dil=dil, rows=rows, band=band, n_rows=n_rows,
                          n_others=len(others) // 2),
        grid=(nb, A_SLOTS, n_rows // rows),
        in_specs=[_head_spec(rows * dil, q_head0), _seq_spec(seq, k_head0), _seq_spec(seq, v_head0)]
        + [pair_spec] * len(others),
        out_specs=out_specs,
        out_shape=out_shape,
        scratch_shapes=scratch,
        compiler_params=_params(("parallel", "parallel", "arbitrary")),
        name="dilated_group_%d" % dil,
    )(att, att, att, *others)


def _dilated_mixer(att, q_head0, k_head0, v_head0):
    pairs = []
    for g, (window, dil) in enumerate(DIL_GROUPS):
        last = g == len(DIL_GROUPS) - 1
        result = _dilated_group(att, q_head0 + g * A_SLOTS, k_head0, v_head0, window, dil,
                                others=tuple(pairs) if last else ())
        if last:
            return result
        pairs += list(result)


def _gelu_tanh(x):
    return 0.5 * x * (1.0 + jnp.tanh(math.sqrt(2.0 / math.pi) * (x + 0.044715 * (x * x * x))))


def _compress_kernel(x_ref, pe_ref, w1_ref, w2_ref, o_ref):
    by_token = _by_residue(x_ref[...], CMP_STRIDE)
    x = jnp.concatenate([by_token[r] for r in range(CMP_STRIDE)], axis=1).astype(F32)
    first = _dot((x + pe_ref[0:1, :]).astype(BF16), w1_ref[0])
    second = _dot((x + pe_ref[1:2, :]).astype(BF16), w1_ref[1])
    chunks = x.shape[0]
    hidden = first + pltpu.roll(second, chunks - 1, 0)
    o_ref[...] = _dot(_gelu_tanh(hidden).astype(BF16), w2_ref[...]).astype(o_ref.dtype)


def _compress(t, head0, pe, w1, w2, name):
    _, nb, seq, _ = t.shape
    nh = B_KV_HEADS
    chunks = seq // CMP_STRIDE
    width = CMP_STRIDE * HEAD_DIM
    ratio = CMP_LEN // CMP_STRIDE
    return pl.pallas_call(
        _compress_kernel,
        grid=(nh, nb),
        in_specs=[pl.BlockSpec((None, None, seq, HEAD_DIM), lambda h, b: (head0 + h, b, 0, 0)),
                  pl.BlockSpec((ratio, width), lambda h, b: (0, 0)),
                  pl.BlockSpec((ratio, width, HEAD_DIM), lambda h, b: (0, 0, 0)),
                  pl.BlockSpec((HEAD_DIM, HEAD_DIM), lambda h, b: (0, 0))],
        out_specs=pl.BlockSpec((None, None, chunks, HEAD_DIM), lambda h, b: (h, b, 0, 0)),
        out_shape=jax.ShapeDtypeStruct((nh, nb, chunks, HEAD_DIM), BF16),
        compiler_params=_params(("parallel", "parallel")),
        name=name,
    )(t, pe.reshape(ratio, width), w1.reshape(ratio, width, HEAD_DIM).astype(BF16), w2.astype(BF16))


def _group_q(q_refs, row0=0, rows=None):
    rows = q_refs[0].shape[0] if rows is None else rows
    return jnp.concatenate([r[row0:row0 + rows, :] for r in q_refs], axis=0)


def _group_q_specs(tq, head0):
    return [pl.BlockSpec((None, None, tq, HEAD_DIM),
                         functools.partial(lambda b, h, n, g: (head0 + h * B_GROUP + g, b, n, 0), g=g))
            for g in range(B_GROUP)]


def _cmp_select_kernel(q0_ref, q1_ref, q2_ref, kc_ref, vc_ref, ov_ref, o_ref, sel_ref, *, tq, tiles):
    for u in range(tiles):
        _cmp_select_tile((q0_ref, q1_ref, q2_ref), kc_ref, vc_ref, ov_ref, o_ref, sel_ref,
                         (pl.program_id(2) * tiles + u) * tq, u * tq, tq)


def _cmp_select_tile(q_refs, kc_ref, vc_ref, ov_ref, o_ref, sel_ref, t0, row0, tq):
    n_cmp = kc_ref.shape[0]
    rows = B_GROUP * tq
    s = _dot_nt(_group_q(q_refs, row0, tq), kc_ref[...])
    tpos3 = t0 + (lax.broadcasted_iota(jnp.int32, (rows, 1), 0) & (tq - 1))
    c_end = lax.broadcasted_iota(jnp.int32, (1, n_cmp), 1) * CMP_STRIDE + (CMP_LEN - 1)
    s = jnp.where(c_end - tpos3 <= 0, s, -jnp.inf)
    m = jnp.max(s, axis=1, keepdims=True)
    m = jnp.where(jnp.abs(m) < jnp.inf, m, 0.0)
    e = jnp.exp(s - m)
    den = jnp.sum(e, axis=1, keepdims=True)
    p = e / jnp.where(den > 0, den, 1.0)
    o = _dot(p.astype(BF16), vc_ref[...]).reshape(B_GROUP, tq, HEAD_DIM)
    o_ref[:, row0:row0 + tq, :] = o.astype(o_ref.dtype)

    p_sum = p[0:tq] + p[tq:2 * tq] + p[2 * tq:3 * tq]
    hi, lo = _split_bf16(p_sum)
    imp = _dot(hi, ov_ref[...]) + _dot(lo, ov_ref[...])
    tpos = t0 + lax.broadcasted_iota(jnp.int32, (tq, 1), 0)
    rel = lax.broadcasted_iota(jnp.int32, (1, HEAD_DIM), 1) - (tpos >> SEL_SHIFT)
    j_abs = jnp.broadcast_to(lax.broadcasted_iota(jnp.int32, (1, HEAD_DIM), 1), rel.shape)
    forced = (j_abs == 0) | (rel == 0) | (rel == -1)
    imp = jnp.where(forced, jnp.inf, jnp.where(rel <= 0, imp, -jnp.inf))

    n_sel = HEAD_DIM // 2
    imp_t = imp.T
    mine = imp_t[0:n_sel]
    j_idx = lax.broadcasted_iota(jnp.int32, (n_sel, tq), 0)
    beaten = jnp.zeros((n_sel, tq), F32)
    for kk in range(n_sel):
        other = imp_t[kk:kk + 1, :]
        wins = (other > mine) | ((other == mine) & (j_idx > kk))
        beaten = beaten + jnp.where(wins, 1.0, 0.0)
    chosen = jnp.where(beaten < SEL_TOPK, 0.0, MASKED)
    chosen = jnp.concatenate([chosen, jnp.full((HEAD_DIM - n_sel, tq), MASKED, F32)], axis=0)
    sel_ref[:, row0:row0 + tq] = chosen.astype(sel_ref.dtype)


def _overlap_matrix(n_cmp_rows, n_sel):
    c_start = np.arange(n_cmp_rows) * CMP_STRIDE
    c_end = c_start + CMP_LEN - 1
    s_start = np.arange(HEAD_DIM) * SEL_LEN
    ov = (c_start[:, None] <= s_start[None, :] + SEL_LEN - 1) & (c_end[:, None] >= s_start[None, :])
    ov &= (np.arange(HEAD_DIM) < n_sel)[None, :]
    return jnp.asarray(ov.astype(np.float32), dtype=BF16)


def _group_o_spec(tq):
    return pl.BlockSpec((B_GROUP, None, tq, HEAD_DIM), lambda b, h, n: (h, b, n, 0))


def _sel_spec(tq):
    return pl.BlockSpec((None, None, HEAD_DIM, tq), lambda b, h, n: (h, b, 0, n))


def _cmp_select(q, q_head0, kc, vc):
    _, nb, seq, _ = q.shape
    assert seq // SEL_LEN <= HEAD_DIM // 2
    tq = min(ATT_TQ, seq)
    tiles = min(ATT_TILES, seq // tq)
    n_cmp = kc.shape[2]
    kv_spec = pl.BlockSpec((None, None, n_cmp, HEAD_DIM), lambda b, h, n: (h, b, 0, 0))
    return pl.pallas_call(
        functools.partial(_cmp_select_kernel, tq=tq, tiles=tiles),
        grid=(nb, B_KV_HEADS, seq // (tq * tiles)),
        in_specs=_group_q_specs(tq * tiles, q_head0) + [kv_spec, kv_spec,
                                                        pl.BlockSpec((n_cmp, HEAD_DIM), lambda b, h, n: (0, 0))],
        out_specs=[_group_o_spec(tq * tiles), _sel_spec(tq * tiles)],
        out_shape=[jax.ShapeDtypeStruct((B_HEADS, nb, seq, HEAD_DIM), BF16),
                   jax.ShapeDtypeStruct((B_KV_HEADS, nb, HEAD_DIM, seq), BF16)],
        compiler_params=_params(("parallel", "parallel", "arbitrary")),
        name="nsa_compressed_select",
    )(q, q, q, kc, vc, _overlap_matrix(n_cmp, seq // SEL_LEN))


def _selected_kernel(q0_ref, q1_ref, q2_ref, k_ref, v_ref, sel_ref, o_ref, vt_ref, acc_ref, *, tq, kc, tiles):
    n = pl.program_id(2)
    t_first = n * (tq * tiles)
    cols = B_GROUP * tq

    @pl.when(n == 0)
    def _():
        for c in range(v_ref.shape[0] // kc):
            vt_ref[c] = v_ref[c * kc:(c + 1) * kc, :].astype(F32).T.astype(BF16)

    qs = [_group_q((q0_ref, q1_ref, q2_ref), u * tq, tq) for u in range(tiles)]
    qposs = [t_first + u * tq + lax.broadcasted_iota(jnp.int32, (1, tq), 1) for u in range(tiles)]
    key_in_chunk = lax.broadcasted_iota(jnp.int32, (kc, 1), 0)
    blocks = kc // SEL_LEN
    acc_ref[...] = jnp.zeros(acc_ref.shape, F32)

    def chunk(i, carry, diagonal):
        k0 = pl.multiple_of(i * kc, kc)
        kpos = k0 + key_in_chunk
        keys = k_ref[pl.ds(k0, kc), :]
        blk0 = pl.multiple_of(i * blocks, blocks)
        out = []
        for u in range(tiles):
            m_old, l_old = carry[u]
            n_keys = tq * (u + 1) if (diagonal and tq * tiles == kc) else kc
            rows = sel_ref[pl.ds(blk0, blocks), u * tq:(u + 1) * tq].astype(F32)
            bias = jnp.broadcast_to(rows[:, None, :], (blocks, SEL_LEN, tq)).reshape(kc, tq)[:n_keys]
            if diagonal:
                bias = jnp.where(kpos[:n_keys] - qposs[u] <= 0, bias, MASKED)
            s = _dot_nt(keys[:n_keys], qs[u]) + jnp.concatenate([bias] * B_GROUP, axis=1)
            m_new = jnp.maximum(m_old, jnp.max(s, axis=0, keepdims=True))
            p = jnp.exp(s - m_new)
            alpha = jnp.exp(m_old - m_new)
            acc_ref[u] = alpha * acc_ref[u] + _dot(vt_ref[i, :, :n_keys], p.astype(BF16))
            out.append((m_new, alpha * l_old + jnp.sum(p, axis=0, keepdims=True)))
        return tuple(out)

    init = tuple((jnp.full((1, cols), MASKED, F32), jnp.zeros((1, cols), F32)) for _ in range(tiles))
    last = t_first // kc
    carry = lax.fori_loop(0, last, lambda i, c: chunk(i, c, False), init)
    final = chunk(last, carry, True)
    for u in range(tiles):
        o_t = acc_ref[u] / final[u][1]
        for g in range(B_GROUP):
            o_ref[g, u * tq:(u + 1) * tq, :] = o_t[:, g * tq:(g + 1) * tq].T.astype(o_ref.dtype)


def _selected(q, q_head0, k, k_head0, v, v_head0, sel):
    _, nb, seq, _ = q.shape
    tq = min(SEL_TQ, seq)
    kc = min(SEL_CHUNK, seq)
    tiles = min(SEL_TILES, kc // tq)
    assert kc % (tq * tiles) == 0 and seq % kc == 0 and tq == HEAD_DIM
    return pl.pallas_call(
        functools.partial(_selected_kernel, tq=tq, kc=kc, tiles=tiles),
        grid=(nb, B_KV_HEADS, seq // (tq * tiles)),
        in_specs=_group_q_specs(tq * tiles, q_head0) + [_seq_spec(seq, k_head0), _seq_spec(seq, v_head0),
                                                        _sel_spec(tq * tiles)],
        out_specs=_group_o_spec(tq * tiles),
        out_shape=jax.ShapeDtypeStruct((B_HEADS, nb, seq, HEAD_DIM), BF16),
        scratch_shapes=[pltpu.VMEM((seq // kc, HEAD_DIM, kc), BF16),
                        pltpu.VMEM((tiles, HEAD_DIM, B_GROUP * tq), F32)],
        compiler_params=_params(("parallel", "parallel", "arbitrary")),
        name="nsa_selected",
    )(q, q, q, k, v, sel)


def _window_kernel(q0_ref, q1_ref, q2_ref, k_ref, v_ref, o_ref, *, tq, tiles, seq):
    rows = B_GROUP * tq
    span = min(WIN_LEN + tq, seq)
    for u in range(tiles):
        t0 = (pl.program_id(2) * tiles + u) * tq
        start = _window_start(t0, WIN_LEN, span, seq, tq)
        s = _dot_nt(_group_q((q0_ref, q1_ref, q2_ref), u * tq, tq), k_ref[pl.ds(start, span), :])
        qpos = t0 + (lax.broadcasted_iota(jnp.int32, (rows, 1), 0) & (tq - 1))
        kpos = start + lax.broadcasted_iota(jnp.int32, (1, span), 1)
        dist = qpos - kpos
        s = jnp.where((dist >= 0) & (dist <= WIN_LEN - 1), s, -jnp.inf)
        m = jnp.max(s, axis=1, keepdims=True)
        p = jnp.exp(s - m)
        l = jnp.sum(p, axis=1, keepdims=True)
        o = _dot(p.astype(BF16), v_ref[pl.ds(start, span), :]) / l
        o_ref[:, u * tq:(u + 1) * tq, :] = o.reshape(B_GROUP, tq, HEAD_DIM).astype(o_ref.dtype)


def _window(q, q_head0, k, k_head0, v, v_head0):
    _, nb, seq, _ = q.shape
    tq = min(ATT_TQ, seq)
    tiles = min(ATT_TILES, seq // tq)
    assert WIN_LEN % tq == 0
    return pl.pallas_call(
        functools.partial(_window_kernel, tq=tq, tiles=tiles, seq=seq),
        grid=(nb, B_KV_HEADS, seq // (tq * tiles)),
        in_specs=_group_q_specs(tq * tiles, q_head0) + [_seq_spec(seq, k_head0), _seq_spec(seq, v_head0)],
        out_specs=_group_o_spec(tq * tiles),
        out_shape=jax.ShapeDtypeStruct((B_HEADS, nb, seq, HEAD_DIM), BF16),
        compiler_params=_params(("parallel", "parallel", "arbitrary")),
        name="nsa_window",
    )(q, q, q, k, v)


def _nsa_gate_kernel(g_ref, oc_ref, os_ref, ow_ref, y_ref):
    g = g_ref[...]
    for h in range(B_HEADS):
        y = (g[:, 3 * h:3 * h + 1] * oc_ref[h].astype(F32)
             + g[:, 3 * h + 1:3 * h + 2] * os_ref[h].astype(F32)
             + g[:, 3 * h + 2:3 * h + 3] * ow_ref[h].astype(F32))
        y_ref[:, h * HEAD_DIM:(h + 1) * HEAD_DIM] = y.astype(y_ref.dtype)


def _nsa_gate(gates, o_cmp, o_sel, o_win):
    rows = gates.shape[0]
    tm = min(256, rows)
    o_spec = pl.BlockSpec((B_HEADS, tm, HEAD_DIM), lambda i: (0, i, 0))
    return pl.pallas_call(
        _nsa_gate_kernel,
        grid=(rows // tm,),
        in_specs=[pl.BlockSpec((tm, HEAD_DIM), lambda i: (i, 0)), o_spec, o_spec, o_spec],
        out_specs=pl.BlockSpec((tm, B_HEADS * HEAD_DIM), lambda i: (i, 0)),
        out_shape=jax.ShapeDtypeStruct((rows, B_HEADS * HEAD_DIM), BF16),
        compiler_params=_params(("parallel",)),
        name="nsa_gate",
    )(gates, o_cmp, o_sel, o_win)


def _stick_kernel(*refs, tq, heads):
    q_refs, k_refs, v_refs = refs[:heads], refs[heads:2 * heads], refs[2 * heads:3 * heads]
    o_ref, acc_ref, car_ref = refs[3 * heads:]
    n = pl.program_id(2)
    sub = min(CUM_TILE, tq)
    n_sub = tq // sub
    r_idx = lax.broadcasted_iota(jnp.int32, (2 * sub, 2 * sub), 0) & (sub - 1)
    c_idx = lax.broadcasted_iota(jnp.int32, (2 * sub, 2 * sub), 1)
    suffix_and_total = jnp.where((c_idx >= sub) | (r_idx > c_idx), 1.0, 0.0).astype(BF16)
    k_in_sub = lax.broadcasted_iota(jnp.int32, (1, sub), 1)
    acc_ref[...] = jnp.zeros(acc_ref.shape, F32)
    car_ref[...] = jnp.zeros(car_ref.shape, F32)

    q_in_tile = lax.broadcasted_iota(jnp.int32, (tq, 1), 0)

    def tile(k0, diagonal):
        for h in range(heads):
            z = _dot_nt(q_refs[h][...], k_refs[h][pl.ds(k0, tq), :])
            soft = jnp.log(1.0 + jnp.exp(-jnp.abs(z)))
            log_beta = jnp.minimum(z, 0.0) - soft
            log_rest = log_beta - z
            carried = car_ref[h]
            parts = [None] * n_sub
            for u in reversed(range(n_sub)):
                cols = slice(u * sub, (u + 1) * sub)
                rest_u = log_rest[:, cols]
                if diagonal:
                    before = (u * sub + k_in_sub) - q_in_tile < 0
                    rest_u = jnp.where(before, rest_u, 0.0)
                sums = _dot(jnp.concatenate(_split_bf16(rest_u), axis=1), suffix_and_total)
                a = jnp.exp(log_beta[:, cols] + (sums[:, :sub] + carried))
                if diagonal:
                    a = jnp.where(before, a, 0.0)
                carried = carried + sums[:, sub:]
                parts[u] = a.astype(BF16)
            car_ref[h] = carried
            acc_ref[h] += _dot(jnp.concatenate(parts, axis=1), v_refs[h][pl.ds(k0, tq), :])

    tile(pl.multiple_of(n * tq, tq), True)

    def step(i, carry):
        tile(pl.multiple_of((n - 1 - i) * tq, tq), False)
        return carry

    lax.fori_loop(0, n, step, 0)
    for h in range(heads):
        o_ref[:, h * HEAD_DIM:(h + 1) * HEAD_DIM] = acc_ref[h].astype(o_ref.dtype)


def _stick_breaking(q, q_head0, k, k_head0, v, v_head0):
    _, nb, seq, _ = q.shape
    tq = min(STICK_TILE, seq)
    heads = STICK_HEADS
    sub = min(CUM_TILE, tq)

    def per_head(make, head0):
        return [make(head0 + h) for h in range(heads)]

    def q_spec(head):
        return pl.BlockSpec((None, None, tq, HEAD_DIM), lambda b, g, n: (head + g * heads, b, n, 0))

    def kv_spec(head):
        return pl.BlockSpec((None, None, seq, HEAD_DIM), lambda b, g, n: (head + g * heads, b, 0, 0))

    return pl.pallas_call(
        functools.partial(_stick_kernel, tq=tq, heads=heads),
        grid=(nb, C_HEADS // heads, seq // tq),
        in_specs=per_head(q_spec, q_head0) + per_head(kv_spec, k_head0) + per_head(kv_spec, v_head0),
        out_specs=pl.BlockSpec((None, tq, heads * HEAD_DIM), lambda b, g, n: (b, n, g)),
        out_shape=jax.ShapeDtypeStruct((nb, seq, C_HEADS * HEAD_DIM), BF16),
        scratch_shapes=[pltpu.VMEM((heads, tq, HEAD_DIM), F32), pltpu.VMEM((heads, tq, sub), F32)],
        compiler_params=_params(("parallel", "parallel", "arbitrary")),
        name="stick_breaking",
    )(*([q] * heads + [k] * heads + [v] * heads))


def _merge_kernel(ya_ref, yb_ref, yc_ref, wa_ref, wb_ref, wc_ref, ga_ref, gb_ref, gc_ref, o_ref):
    mixed = (ga_ref[...].astype(F32) * _dot(ya_ref[...], wa_ref[...])
             + gb_ref[...].astype(F32) * _dot(yb_ref[...], wb_ref[...])
             + gc_ref[...].astype(F32) * _dot(yc_ref[...], wc_ref[...]))
    o_ref[...] = mixed.astype(o_ref.dtype)


def _merge(ya, yb, yc, wa, wb, wc, gates, d_model):
    m = ya.shape[0]
    tm, tn = min(512, m), 1024
    col_tiles = d_model // tn

    def y_spec(y):
        return pl.BlockSpec((tm, y.shape[1]), lambda i, j: (i, 0))

    def w_spec(w):
        return pl.BlockSpec((w.shape[0], tn), lambda i, j: (0, j))

    def g_spec(branch):
        return pl.BlockSpec((tm, tn), lambda i, j: (i, branch * col_tiles + j))

    return pl.pallas_call(
        _merge_kernel,
        grid=(m // tm, col_tiles),
        in_specs=[y_spec(ya), y_spec(yb), y_spec(yc), w_spec(wa), w_spec(wb), w_spec(wc),
                  g_spec(0), g_spec(1), g_spec(2)],
        out_specs=pl.BlockSpec((tm, tn), lambda i, j: (i, j)),
        out_shape=jax.ShapeDtypeStruct((m, d_model), BF16),
        compiler_params=_params(("parallel", "arbitrary")),
        name="branch_merge",
    )(ya, yb, yc, wa, wb, wc, gates, gates, gates)


def _ln_kernel(x_ref, y_ref, g_ref, b_ref, o_ref, ob_ref):
    z = ALPHA * x_ref[...] + y_ref[...]
    mu = jnp.mean(z, axis=1, keepdims=True)
    zc = z - mu
    var = jnp.mean(zc * zc, axis=1, keepdims=True)
    out = zc * lax.rsqrt(var + LN_EPS) * g_ref[...] + b_ref[...]
    o_ref[...] = out
    ob_ref[...] = out.astype(BF16)


def _residual_ln(x, y, g, b):
    m, d = x.shape
    tm = min(256, m)
    row = pl.BlockSpec((tm, d), lambda i: (i, 0))
    vec = pl.BlockSpec((1, d), lambda i: (0, 0))
    return pl.pallas_call(
        _ln_kernel,
        grid=(m // tm,),
        in_specs=[row, row, vec, vec],
        out_specs=[row, row],
        out_shape=[jax.ShapeDtypeStruct((m, d), F32), jax.ShapeDtypeStruct((m, d), BF16)],
        compiler_params=_params(("parallel",)),
        name="residual_layer_norm",
    )(x, y, g.reshape(1, d), b.reshape(1, d))


def _rope_tables(seq):
    half = HEAD_DIM // 2
    inv = 1.0 / (ROPE_THETA ** (jnp.arange(half, dtype=F32) / half))
    ang = jnp.arange(seq).astype(F32)[:, None] * inv[None, :]
    cos, sin = jnp.cos(ang), jnp.sin(ang)
    return jnp.concatenate([cos, cos], axis=1), jnp.concatenate([-sin, sin], axis=1)


_A_KV, _B_KV = A_SLOTS, B_KV_HEADS
_ALIGNED_FIELDS = (('q_a', A_HEADS, 'rope_scale'), ('k_a', _A_KV, 'rope'), ('v_a', _A_KV, 'none'),
                   ('q_b', B_HEADS, 'rope_scale'), ('kc_b', _B_KV, 'rope'), ('vc_b', _B_KV, 'none'),
                   ('ks_b', _B_KV, 'rope'), ('vs_b', _B_KV, 'none'), ('kw_b', _B_KV, 'rope'),
                   ('vw_b', _B_KV, 'none'))
_STICK_FIELDS = (('q_c', C_HEADS, 'scale'), ('k_c', C_HEADS, 'none'), ('v_c', C_HEADS, 'none'))
N_NSA_GATES = 3 * B_HEADS


def _field_layout(fields, heads_per_tile):
    head0, modes, start = {}, [], 0
    for name, heads, mode in fields:
        assert heads % heads_per_tile == 0
        head0[name] = start
        modes += [mode] * (heads // heads_per_tile)
        start += heads
    return head0, tuple(modes), start


def _layer(layer, x, xb, nb, seq, tables, w_rows, cmp_pe_k, cmp_wk1, cmp_wk2, cmp_pe_v, cmp_wv1, cmp_wv2,
           w_br_a, w_br_b, w_br_c, w_out, ln1_g, ln1_b, w_up, w_down, ln2_g, ln2_b):
    m, d_model = x.shape

    att_mode_cols = 4 * HEAD_DIM
    at, att_modes, att_heads = _field_layout(_ALIGNED_FIELDS, att_mode_cols // HEAD_DIM)
    att = _matmul(xb, w_rows, layer=layer, b_rows=True, name="proj_attention", out_dtype=BF16,
                  tile_modes=att_modes, mode_cols=att_mode_cols, tm=1024, tn=1024, head_major=True,
                  rope_tables=tables)
    tn_tail = 1024
    st, stick_modes, stick_heads = _field_layout(_STICK_FIELDS, tn_tail // HEAD_DIM)
    stick_col0 = att_heads * HEAD_DIM
    gates_col0 = stick_col0 + stick_heads * HEAD_DIM
    nsa_col0 = gates_col0 + 3 * d_model
    stick = _matmul(xb, w_rows, layer=layer, b_rows=True, col0=stick_col0, name="proj_stick",
                    out_dtype=BF16, tile_modes=stick_modes, tm=1024, tn=tn_tail, head_major=True)
    gates = _matmul(xb, w_rows, layer=layer, b_rows=True, col0=gates_col0, name="proj_branch_gates",
                    out_dtype=BF16, tile_modes=('sigmoid',) * (3 * d_model // tn_tail), tm=1024, tn=tn_tail)
    g_nsa = _matmul(xb, w_rows, layer=layer, b_rows=True, col0=nsa_col0, name="proj_nsa_gates",
                    out_dtype=F32, tile_modes=('sigmoid',), tm=1024, tn=HEAD_DIM)

    att = att.reshape(att_heads, nb, seq, HEAD_DIM)
    stick = stick.reshape(stick_heads, nb, seq, HEAD_DIM)

    y_a = _dilated_mixer(att, at['q_a'], at['k_a'], at['v_a'])

    kc = _compress(att, at['kc_b'], cmp_pe_k, cmp_wk1, cmp_wk2, "nsa_compress_k")
    vc = _compress(att, at['vc_b'], cmp_pe_v, cmp_wv1, cmp_wv2, "nsa_compress_v")
    o_cmp, sel = _cmp_select(att, at['q_b'], kc, vc)
    o_sel = _selected(att, at['q_b'], att, at['ks_b'], att, at['vs_b'], sel)
    o_win = _window(att, at['q_b'], att, at['kw_b'], att, at['vw_b'])
    y_b = _nsa_gate(g_nsa, o_cmp.reshape(B_HEADS, m, HEAD_DIM), o_sel.reshape(B_HEADS, m, HEAD_DIM),
                    o_win.reshape(B_HEADS, m, HEAD_DIM))

    y_c = _stick_breaking(stick, st['q_c'], stick, st['k_c'], stick, st['v_c'])

    merged = _merge(y_a.reshape(m, -1), y_b, y_c.reshape(m, -1), w_br_a.astype(BF16),
                    w_br_b.astype(BF16), w_br_c.astype(BF16), gates, d_model)
    mixed = _matmul(merged, w_out, layer=layer, name="out_proj", out_dtype=F32,
                    tile_modes=('none',) * (d_model // 512), tm=1024, tn=512)
    x1, x1b = _residual_ln(x, mixed, ln1_g, ln1_b)

    d_ff = w_up.shape[2]
    hidden = _matmul(x1b, w_up, layer=layer, name="mlp_up", out_dtype=BF16,
                     tile_modes=('relu2',) * (d_ff // 512), tm=1024, tn=512)
    down = _matmul(hidden, w_down, layer=layer, name="mlp_down", out_dtype=F32,
                   tile_modes=('none',) * (d_model // 1024), tm=2048, tn=1024, tk=1024)
    return _residual_ln(x1, down, ln2_g, ln2_b)


def kernel(x, w_in, cmp_pe_k, cmp_wk1, cmp_wk2, cmp_pe_v, cmp_wv1, cmp_wv2, w_br_a, w_br_b, w_br_c,
           w_out, ln1_g, ln1_b, w_up, w_down, ln2_g, ln2_b):
    nb, seq, d_model = x.shape
    tables = _rope_tables(seq)
    xf = x.reshape(nb * seq, d_model)
    xb = xf.astype(BF16)
    aligned_cols = sum(heads for _, heads, _ in _ALIGNED_FIELDS) * HEAD_DIM
    w_rows = _input_weight_rows(w_in, aligned_cols, N_NSA_GATES)
    for l in range(w_in.shape[0]):
        xf, xb = _layer(l, xf, xb, nb, seq, tables, w_rows, cmp_pe_k[l], cmp_wk1[l], cmp_wk2[l],
                        cmp_pe_v[l], cmp_wv1[l], cmp_wv2[l], w_br_a[l], w_br_b[l], w_br_c[l],
                        w_out, ln1_g[l], ln1_b[l], w_up, w_down, ln2_g[l], ln2_b[l])
    return xf.reshape(nb, seq, d_model)
```

```python
import functools
import math

import numpy as np
import jax
import jax.numpy as jnp
from jax import lax
from jax.experimental import pallas as pl
from jax.experimental.pallas import tpu as pltpu

F32 = jnp.float32
BF16 = jnp.bfloat16

HEAD_DIM = 128
ROPE_THETA = 10000.0
LN_EPS = 1e-5
DEPTH = 2

DIL_GROUPS = ((128, 1), (512, 4), (2048, 16))
A_SLOTS = 4
A_HEADS = A_SLOTS * len(DIL_GROUPS)

B_HEADS = 12
B_KV_HEADS = 4
B_GROUP = B_HEADS // B_KV_HEADS
CMP_LEN = 32
CMP_STRIDE = 16
SEL_LEN = 64
SEL_SHIFT = 6
SEL_TOPK = 16
WIN_LEN = 512

C_HEADS = 8

MASKED = -3e38
ALPHA = (2.0 * DEPTH) ** 0.25
Q_SCALE = HEAD_DIM ** -0.5

VMEM_LIMIT = 48 * 1024 * 1024
MXU_COLS = 256
ATT_TQ = 128
ATT_TILES = 16
DIL_ROWS = 128
DIL_ROWS_DENSE = 512
SEL_TQ = 128
SEL_TILES = 8
SEL_CHUNK = 1024
STICK_TILE = 512
STICK_HEADS = 8
CUM_TILE = 128


def _params(semantics):
    return pltpu.CompilerParams(dimension_semantics=semantics, vmem_limit_bytes=VMEM_LIMIT)


def _dot(a, b):
    return jnp.dot(a, b, preferred_element_type=F32)


def _dot_nt(a, b):
    return lax.dot_general(a, b, (((1,), (1,)), ((), ())), preferred_element_type=F32)


def _split_bf16(x):
    hi = x.astype(BF16)
    lo = (x - hi.astype(F32)).astype(BF16)
    return hi, lo


def _tile_flag(tile_modes, wanted):
    hits = [mode in wanted for mode in tile_modes]
    if all(hits) or not any(hits):
        return hits[0]
    j = pl.program_id(1)
    flag = None
    for c, hit in enumerate(hits):
        if hit:
            flag = (j == c) if flag is None else (flag | (j == c))
    return flag


def _write_columns(acc, o_ref, col0, tile_modes, head_major, cos_ref, sin_ref):
    is_rope = _tile_flag(tile_modes, ('rope', 'rope_scale'))
    is_scaled = _tile_flag(tile_modes, ('rope_scale', 'scale'))
    uniform = tile_modes[0] if len(set(tile_modes)) == 1 else None
    assert uniform is not None or not (set(tile_modes) & {'sigmoid', 'relu2'})

    def epilogue(y):
        if is_rope is not False:
            roped = y * cos_ref[...] + pltpu.roll(y, HEAD_DIM // 2, 1) * sin_ref[...]
            y = roped if is_rope is True else jnp.where(is_rope, roped, y)
        if is_scaled is not False:
            y = y * (Q_SCALE if is_scaled is True else jnp.where(is_scaled, Q_SCALE, 1.0))
        if uniform == 'sigmoid':
            y = jax.nn.sigmoid(y)
        if uniform == 'relu2':
            y = jnp.square(jnp.maximum(y, 0.0))
        return y

    width = acc.shape[1]
    if head_major:
        for c in range(width // HEAD_DIM):
            y = epilogue(acc[:, c * HEAD_DIM:(c + 1) * HEAD_DIM])
            o_ref[col0 // HEAD_DIM + c] = y.astype(o_ref.dtype)
    else:
        o_ref[:, col0:col0 + width] = epilogue(acc).astype(o_ref.dtype)


def _matmul_kernel(*refs, tile_modes, units, head_major, n_k, use_rope, acc_in_out, b_rows):
    refs = list(refs)
    a_ref, b_ref = refs[0], refs[1]
    cos_ref, sin_ref = (refs[2], refs[3]) if use_rope else (None, None)
    o_ref = refs[4] if use_rope else refs[2]
    acc_ref = o_ref if acc_in_out else (refs[-1] if n_k > 1 else None)
    tn = b_ref.shape[0] if b_rows else b_ref.shape[1]

    def product(col0, width):
        if b_rows:
            return _dot_nt(a_ref[...], b_ref[col0:col0 + width, :].astype(BF16))
        return _dot(a_ref[...], b_ref[:, col0:col0 + width].astype(BF16))

    if n_k == 1:
        unit_cols = tn // units
        width = min(MXU_COLS, unit_cols)
        for col0 in range(0, tn, width):
            per_tile = tile_modes[col0 // unit_cols::units]
            _write_columns(product(col0, width), o_ref, col0, per_tile, head_major, cos_ref, sin_ref)
        return
    assert units == 1

    @pl.when(pl.program_id(2) == 0)
    def _():
        acc_ref[...] = jnp.zeros(acc_ref.shape, F32)

    acc_ref[...] += product(0, tn)
    if not acc_in_out:
        pl.when(pl.program_id(2) == n_k - 1)(
            lambda: _write_columns(acc_ref[...], o_ref, 0, tile_modes, head_major, cos_ref, sin_ref))


def _matmul(a, b, *, name, out_dtype, tile_modes, tm, tn, tk=None, layer=None, col0=0,
            head_major=False, rope_tables=None, b_rows=False, mode_cols=None):
    m, kdim = a.shape
    mode_cols = tn if mode_cols is None else mode_cols
    n = mode_cols * len(tile_modes)
    assert tn % mode_cols == 0 and n % tn == 0
    tm = min(tm, m)
    tk = kdim if tk is None else min(tk, kdim)
    assert m % tm == 0 and kdim % tk == 0 and col0 % tn == 0
    n_k = kdim // tk
    col_tile0 = col0 // tn
    use_rope = any(mode.startswith('rope') for mode in tile_modes)
    acc_in_out = n_k > 1 and out_dtype == F32 and set(tile_modes) == {'none'} and not head_major
    if b_rows:
        assert layer is not None
        b_spec = pl.BlockSpec((None, tn, tk), lambda i, j, k: (layer, col_tile0 + j, k))
    elif layer is None:
        b_spec = pl.BlockSpec((tk, tn), lambda i, j, k: (k, col_tile0 + j))
    else:
        b_spec = pl.BlockSpec((None, tk, tn), lambda i, j, k: (layer, k, col_tile0 + j))
    in_specs = [pl.BlockSpec((tm, tk), lambda i, j, k: (i, k)), b_spec]
    operands = [a, b]
    if use_rope:
        assert head_major
        cos, sin = rope_tables
        seq_tiles = cos.shape[0] // tm
        assert cos.shape[0] % tm == 0
        spec = pl.BlockSpec((tm, HEAD_DIM), lambda i, j, k: (i % seq_tiles, 0))
        in_specs += [spec, spec]
        operands += [cos, sin]
    if head_major:
        out_shape = jax.ShapeDtypeStruct((n // HEAD_DIM, m, HEAD_DIM), out_dtype)
        out_spec = pl.BlockSpec((tn // HEAD_DIM, tm, HEAD_DIM), lambda i, j, k: (j, i, 0))
    else:
        out_shape = jax.ShapeDtypeStruct((m, n), out_dtype)
        out_spec = pl.BlockSpec((tm, tn), lambda i, j, k: (i, j))
    scratch = [pltpu.VMEM((tm, tn), F32)] if (n_k > 1 and not acc_in_out) else []
    return pl.pallas_call(
        functools.partial(_matmul_kernel, tile_modes=tuple(tile_modes), units=tn // mode_cols,
                          head_major=head_major, n_k=n_k, use_rope=use_rope, acc_in_out=acc_in_out,
                          b_rows=b_rows),
        grid=(m // tm, n // tn, n_k),
        in_specs=in_specs,
        out_specs=out_spec,
        out_shape=out_shape,
        scratch_shapes=scratch,
        compiler_params=_params(("parallel", "parallel", "arbitrary")),
        name=name,
    )(*operands)


def _gather_rows_kernel(w_ref, o_ref, *, tn, k_chunks, n_layers):
    stride = k_chunks * n_layers
    by_chunk = pltpu.einshape("nck->cnk", w_ref[...].reshape(tn, stride, HEAD_DIM))
    for layer in range(n_layers):
        pieces = [by_chunk[c * n_layers + layer] for c in range(k_chunks)]
        o_ref[layer] = jnp.concatenate(pieces, axis=1).astype(o_ref.dtype)


def _input_weight_rows(w_in, aligned_cols, n_gates, tn=HEAD_DIM):
    n_layers, kdim, n = w_in.shape
    k_chunks = kdim // HEAD_DIM
    tail = n - aligned_cols - n_gates
    assert aligned_cols % tn == 0 and tail % tn == 0 and kdim % HEAD_DIM == 0
    aligned_tiles, tail_tiles = aligned_cols // tn, tail // tn
    rows_per_col = k_chunks * n_layers
    flat = w_in.reshape(n_layers, k_chunks, HEAD_DIM, n).transpose(3, 1, 0, 2).reshape(n * rows_per_col, HEAD_DIM)

    def source_row(j):
        col = jnp.where(j < aligned_tiles, tn * j,
                        jnp.where(j < aligned_tiles + tail_tiles,
                                  aligned_cols + n_gates + tn * (j - aligned_tiles), aligned_cols))
        return (col * rows_per_col, 0)

    tiles = aligned_tiles + tail_tiles + 1
    return pl.pallas_call(
        functools.partial(_gather_rows_kernel, tn=tn, k_chunks=k_chunks, n_layers=n_layers),
        grid=(tiles,),
        in_specs=[pl.BlockSpec((pl.Element(tn * rows_per_col), pl.Element(HEAD_DIM)), source_row)],
        out_specs=pl.BlockSpec((n_layers, tn, kdim), lambda j: (0, j, 0)),
        out_shape=jax.ShapeDtypeStruct((n_layers, tiles * tn, kdim), BF16),
        compiler_params=_params(("parallel",)),
        name="input_weight_rows",
    )(flat)


def _window_start(t0, back, span, seq, align):
    start = jnp.minimum(jnp.maximum(t0 - back, 0), seq - span)
    return pl.multiple_of(start, align)


def _by_residue(x, dil):
    rows = x.shape[0] // dil
    return pltpu.einshape("ldk->dlk", x.reshape(rows, dil, HEAD_DIM))


def _dilated_group_kernel(*refs, dil, rows, band, n_rows, n_others):
    q_ref, k_ref, v_ref = refs[:3]
    others = refs[3:3 + 2 * n_others]
    if n_others:
        y_ref = refs[3 + 2 * n_others]
        kr_ref, vr_ref, o_ref, lse_ref = refs[4 + 2 * n_others:]
    else:
        o_ref, lse_ref = refs[3:5]
        kr_ref, vr_ref = refs[5:] if dil > 1 else (None, None)
    j = pl.program_id(2)
    if dil > 1:

        @pl.when(j == 0)
        def _():
            kr_ref[...] = _by_residue(k_ref[...], dil)
            vr_ref[...] = _by_residue(v_ref[...], dil)

        q_by = _by_residue(q_ref[...], dil)
    l0 = j * rows
    span = min(band + rows, n_rows)
    start = _window_start(l0, band, span, n_rows, math.gcd(rows, band))
    dist = (l0 + lax.broadcasted_iota(jnp.int32, (rows, 1), 0)) - (start + lax.broadcasted_iota(jnp.int32, (1, span), 1))
    keep = (dist >= 0) & (dist <= band)
    for r in range(dil):
        if dil == 1:
            q, keys, vals = q_ref[...], k_ref[pl.ds(start, span), :], v_ref[pl.ds(start, span), :]
        else:
            q, keys, vals = q_by[r], kr_ref[r, pl.ds(start, span), :], vr_ref[r, pl.ds(start, span), :]
        s = jnp.where(keep, _dot_nt(q, keys), -jnp.inf)
        m = jnp.max(s, axis=1, keepdims=True)
        p = jnp.exp(s - m)
        l = jnp.sum(p, axis=1, keepdims=True)
        o = _dot(p.astype(BF16), vals) / l
        lse = jnp.broadcast_to(m + jnp.log(l), (rows, HEAD_DIM))
        if dil == 1:
            o_ref[...] = o
            lse_ref[...] = lse
        else:
            o_ref[pl.ds(r, rows, stride=dil), :] = o
            lse_ref[pl.ds(r, rows, stride=dil), :] = lse
    if n_others:
        outs = [others[2 * g][...] for g in range(n_others)] + [o_ref[...]]
        lses = [others[2 * g + 1][...] for g in range(n_others)] + [lse_ref[...]]
        top = functools.reduce(jnp.maximum, lses)
        ws = [jnp.exp(lse - top) for lse in lses]
        total = functools.reduce(lambda a, b: a + b, ws)
        mixed = functools.reduce(lambda a, b: a + b, [w * o for w, o in zip(ws, outs)])
        y_ref[...] = (mixed / total).astype(y_ref.dtype)


def _head_spec(rows, head0):
    return pl.BlockSpec((None, None, rows, HEAD_DIM), lambda b, h, n: (head0 + h, b, n, 0))


def _seq_spec(seq, head0):
    return pl.BlockSpec((None, None, seq, HEAD_DIM), lambda b, h, n: (head0 + h, b, 0, 0))


def _dilated_group(att, q_head0, k_head0, v_head0, window, dil, others=()):
    _, nb, seq, _ = att.shape
    n_rows = seq // dil
    band = window // dil
    rows = min(DIL_ROWS if dil > 1 else DIL_ROWS_DENSE, n_rows)
    assert seq % dil == 0 and n_rows % rows == 0 and (dil > 1 or not others)
    pair_spec = _head_spec(rows * dil, 0)
    pair_shape = jax.ShapeDtypeStruct((A_SLOTS, nb, seq, HEAD_DIM), F32)
    scratch = [pltpu.VMEM((dil, n_rows, HEAD_DIM), BF16)] * 2 if dil > 1 else []
    if others:
        out_specs = pl.BlockSpec((None, rows * dil, HEAD_DIM), lambda b, s, j: (b, j, s))
        out_shape = jax.ShapeDtypeStruct((nb, seq, A_SLOTS * HEAD_DIM), BF16)
        scratch = scratch + [pltpu.VMEM((rows * dil, HEAD_DIM), F32)] * 2
    else:
        out_specs, out_shape = [pair_spec, pair_spec], [pair_shape, pair_shape]
    return pl.pallas_call(
        functools.partial(_dilated_group_kernel, dil=dil, rows=rows, band=band, n_rows=n_rows,
                          n_others=len(others) // 2),
        grid=(nb, A_SLOTS, n_rows // rows),
        in_specs=[_head_spec(rows * dil, q_head0), _seq_spec(seq, k_head0), _seq_spec(seq, v_head0)]
        + [pair_spec] * len(others),
        out_specs=out_specs,
        out_shape=out_shape,
        scratch_shapes=scratch,
        compiler_params=_params(("parallel", "parallel", "arbitrary")),
        name="dilated_group_%d" % dil,
    )(att, att, att, *others)


def _dilated_mixer(att, q_head0, k_head0, v_head0):
    pairs = []
    for g, (window, dil) in enumerate(DIL_GROUPS):
        last = g == len(DIL_GROUPS) - 1
        result = _dilated_group(att, q_head0 + g * A_SLOTS, k_head0, v_head0, window, dil,
                                others=tuple(pairs) if last else ())
        if last:
            return result
        pairs += list(result)


def _gelu_tanh(x):
    return 0.5 * x * (1.0 + jnp.tanh(math.sqrt(2.0 / math.pi) * (x + 0.044715 * (x * x * x))))


def _compress_kernel(x_ref, pe_ref, w1_ref, w2_ref, o_ref):
    by_token = _by_residue(x_ref[...], CMP_STRIDE)
    x = jnp.concatenate([by_token[r] for r in range(CMP_STRIDE)], axis=1).astype(F32)
    first = _dot((x + pe_ref[0:1, :]).astype(BF16), w1_ref[0])
    second = _dot((x + pe_ref[1:2, :]).astype(BF16), w1_ref[1])
    chunks = x.shape[0]
    hidden = first + pltpu.roll(second, chunks - 1, 0)
    o_ref[...] = _dot(_gelu_tanh(hidden).astype(BF16), w2_ref[...]).astype(o_ref.dtype)


def _compress(t, head0, pe, w1, w2, name):
    _, nb, seq, _ = t.shape
    nh = B_KV_HEADS
    chunks = seq // CMP_STRIDE
    width = CMP_STRIDE * HEAD_DIM
    ratio = CMP_LEN // CMP_STRIDE
    return pl.pallas_call(
        _compress_kernel,
        grid=(nh, nb),
        in_specs=[pl.BlockSpec((None, None, seq, HEAD_DIM), lambda h, b: (head0 + h, b, 0, 0)),
                  pl.BlockSpec((ratio, width), lambda h, b: (0, 0)),
                  pl.BlockSpec((ratio, width, HEAD_DIM), lambda h, b: (0, 0, 0)),
                  pl.BlockSpec((HEAD_DIM, HEAD_DIM), lambda h, b: (0, 0))],
        out_specs=pl.BlockSpec((None, None, chunks, HEAD_DIM), lambda h, b: (h, b, 0, 0)),
        out_shape=jax.ShapeDtypeStruct((nh, nb, chunks, HEAD_DIM), BF16),
        compiler_params=_params(("parallel", "parallel")),
        name=name,
    )(t, pe.reshape(ratio, width), w1.reshape(ratio, width, HEAD_DIM).astype(BF16), w2.astype(BF16))


def _group_q(q_refs, row0=0, rows=None):
    rows = q_refs[0].shape[0] if rows is None else rows
    return jnp.concatenate([r[row0:row0 + rows, :] for r in q_refs], axis=0)


def _group_q_specs(tq, head0):
    return [pl.BlockSpec((None, None, tq, HEAD_DIM),
                         functools.partial(lambda b, h, n, g: (head0 + h * B_GROUP + g, b, n, 0), g=g))
            for g in range(B_GROUP)]


def _cmp_select_kernel(q0_ref, q1_ref, q2_ref, kc_ref, vc_ref, ov_ref, o_ref, sel_ref, *, tq, tiles):
    for u in range(tiles):
        _cmp_select_tile((q0_ref, q1_ref, q2_ref), kc_ref, vc_ref, ov_ref, o_ref, sel_ref,
                         (pl.program_id(2) * tiles + u) * tq, u * tq, tq)


def _cmp_select_tile(q_refs, kc_ref, vc_ref, ov_ref, o_ref, sel_ref, t0, row0, tq):
    n_cmp = kc_ref.shape[0]
    rows = B_GROUP * tq
    s = _dot_nt(_group_q(q_refs, row0, tq), kc_ref[...])
    tpos3 = t0 + (lax.broadcasted_iota(jnp.int32, (rows, 1), 0) & (tq - 1))
    c_end = lax.broadcasted_iota(jnp.int32, (1, n_cmp), 1) * CMP_STRIDE + (CMP_LEN - 1)
    s = jnp.where(c_end - tpos3 <= 0, s, -jnp.inf)
    m = jnp.max(s, axis=1, keepdims=True)
    m = jnp.where(jnp.abs(m) < jnp.inf, m, 0.0)
    e = jnp.exp(s - m)
    den = jnp.sum(e, axis=1, keepdims=True)
    p = e / jnp.where(den > 0, den, 1.0)
    o = _dot(p.astype(BF16), vc_ref[...]).reshape(B_GROUP, tq, HEAD_DIM)
    o_ref[:, row0:row0 + tq, :] = o.astype(o_ref.dtype)

    p_sum = p[0:tq] + p[tq:2 * tq] + p[2 * tq:3 * tq]
    hi, lo = _split_bf16(p_sum)
    imp = _dot(hi, ov_ref[...]) + _dot(lo, ov_ref[...])
    tpos = t0 + lax.broadcasted_iota(jnp.int32, (tq, 1), 0)
    rel = lax.broadcasted_iota(jnp.int32, (1, HEAD_DIM), 1) - (tpos >> SEL_SHIFT)
    j_abs = jnp.broadcast_to(lax.broadcasted_iota(jnp.int32, (1, HEAD_DIM), 1), rel.shape)
    forced = (j_abs == 0) | (rel == 0) | (rel == -1)
    imp = jnp.where(forced, jnp.inf, jnp.where(rel <= 0, imp, -jnp.inf))

    n_sel = HEAD_DIM // 2
    imp_t = imp.T
    mine = imp_t[0:n_sel]
    j_idx = lax.broadcasted_iota(jnp.int32, (n_sel, tq), 0)
    beaten = jnp.zeros((n_sel, tq), F32)
    for kk in range(n_sel):
        other = imp_t[kk:kk + 1, :]
        wins = (other > mine) | ((other == mine) & (j_idx > kk))
        beaten = beaten + jnp.where(wins, 1.0, 0.0)
    chosen = jnp.where(beaten < SEL_TOPK, 0.0, MASKED)
    chosen = jnp.concatenate([chosen, jnp.full((HEAD_DIM - n_sel, tq), MASKED, F32)], axis=0)
    sel_ref[:, row0:row0 + tq] = chosen.astype(sel_ref.dtype)


def _overlap_matrix(n_cmp_rows, n_sel):
    c_start = np.arange(n_cmp_rows) * CMP_STRIDE
    c_end = c_start + CMP_LEN - 1
    s_start = np.arange(HEAD_DIM) * SEL_LEN
    ov = (c_start[:, None] <= s_start[None, :] + SEL_LEN - 1) & (c_end[:, None] >= s_start[None, :])
    ov &= (np.arange(HEAD_DIM) < n_sel)[None, :]
    return jnp.asarray(ov.astype(np.float32), dtype=BF16)


def _group_o_spec(tq):
    return pl.BlockSpec((B_GROUP, None, tq, HEAD_DIM), lambda b, h, n: (h, b, n, 0))


def _sel_spec(tq):
    return pl.BlockSpec((None, None, HEAD_DIM, tq), lambda b, h, n: (h, b, 0, n))


def _cmp_select(q, q_head0, kc, vc):
    _, nb, seq, _ = q.shape
    assert seq // SEL_LEN <= HEAD_DIM // 2
    tq = min(ATT_TQ, seq)
    tiles = min(ATT_TILES, seq // tq)
    n_cmp = kc.shape[2]
    kv_spec = pl.BlockSpec((None, None, n_cmp, HEAD_DIM), lambda b, h, n: (h, b, 0, 0))
    return pl.pallas_call(
        functools.partial(_cmp_select_kernel, tq=tq, tiles=tiles),
        grid=(nb, B_KV_HEADS, seq // (tq * tiles)),
        in_specs=_group_q_specs(tq * tiles, q_head0) + [kv_spec, kv_spec,
                                                        pl.BlockSpec((n_cmp, HEAD_DIM), lambda b, h, n: (0, 0))],
        out_specs=[_group_o_spec(tq * tiles), _sel_spec(tq * tiles)],
        out_shape=[jax.ShapeDtypeStruct((B_HEADS, nb, seq, HEAD_DIM), BF16),
                   jax.ShapeDtypeStruct((B_KV_HEADS, nb, HEAD_DIM, seq), BF16)],
        compiler_params=_params(("parallel", "parallel", "arbitrary")),
        name="nsa_compressed_select",
    )(q, q, q, kc, vc, _overlap_matrix(n_cmp, seq // SEL_LEN))


def _selected_kernel(q0_ref, q1_ref, q2_ref, k_ref, v_ref, sel_ref, o_ref, vt_ref, acc_ref, *, tq, kc, tiles):
    n = pl.program_id(2)
    t_first = n * (tq * tiles)
    cols = B_GROUP * tq

    @pl.when(n == 0)
    def _():
        for c in range(v_ref.shape[0] // kc):
            vt_ref[c] = v_ref[c * kc:(c + 1) * kc, :].astype(F32).T.astype(BF16)

    qs = [_group_q((q0_ref, q1_ref, q2_ref), u * tq, tq) for u in range(tiles)]
    qposs = [t_first + u * tq + lax.broadcasted_iota(jnp.int32, (1, tq), 1) for u in range(tiles)]
    key_in_chunk = lax.broadcasted_iota(jnp.int32, (kc, 1), 0)
    blocks = kc // SEL_LEN
    acc_ref[...] = jnp.zeros(acc_ref.shape, F32)

    def chunk(i, carry, diagonal):
        k0 = pl.multiple_of(i * kc, kc)
        kpos = k0 + key_in_chunk
        keys = k_ref[pl.ds(k0, kc), :]
        blk0 = pl.multiple_of(i * blocks, blocks)
        out = []
        for u in range(tiles):
            m_old, l_old = carry[u]
            n_keys = tq * (u + 1) if (diagonal and tq * tiles == kc) else kc
            rows = sel_ref[pl.ds(blk0, blocks), u * tq:(u + 1) * tq].astype(F32)
            bias = jnp.broadcast_to(rows[:, None, :], (blocks, SEL_LEN, tq)).reshape(kc, tq)[:n_keys]
            if diagonal:
                bias = jnp.where(kpos[:n_keys] - qposs[u] <= 0, bias, MASKED)
            s = _dot_nt(keys[:n_keys], qs[u]) + jnp.concatenate([bias] * B_GROUP, axis=1)
            m_new = jnp.maximum(m_old, jnp.max(s, axis=0, keepdims=True))
            p = jnp.exp(s - m_new)
            alpha = jnp.exp(m_old - m_new)
            acc_ref[u] = alpha * acc_ref[u] + _dot(vt_ref[i, :, :n_keys], p.astype(BF16))
            out.append((m_new, alpha * l_old + jnp.sum(p, axis=0, keepdims=True)))
        return tuple(out)

    init = tuple((jnp.full((1, cols), MASKED, F32), jnp.zeros((1, cols), F32)) for _ in range(tiles))
    last = t_first // kc
    carry = lax.fori_loop(0, last, lambda i, c: chunk(i, c, False), init)
    final = chunk(last, carry, True)
    for u in range(tiles):
        o_t = acc_ref[u] / final[u][1]
        for g in range(B_GROUP):
            o_ref[g, u * tq:(u + 1) * tq, :] = o_t[:, g * tq:(g + 1) * tq].T.astype(o_ref.dtype)


def _selected(q, q_head0, k, k_head0, v, v_head0, sel):
    _, nb, seq, _ = q.shape
    tq = min(SEL_TQ, seq)
    kc = min(SEL_CHUNK, seq)
    tiles = min(SEL_TILES, kc // tq)
    assert kc % (tq * tiles) == 0 and seq % kc == 0 and tq == HEAD_DIM
    return pl.pallas_call(
        functools.partial(_selected_kernel, tq=tq, kc=kc, tiles=tiles),
        grid=(nb, B_KV_HEADS, seq // (tq * tiles)),
        in_specs=_group_q_specs(tq * tiles, q_head0) + [_seq_spec(seq, k_head0), _seq_spec(seq, v_head0),
                                                        _sel_spec(tq * tiles)],
        out_specs=_group_o_spec(tq * tiles),
        out_shape=jax.ShapeDtypeStruct((B_HEADS, nb, seq, HEAD_DIM), BF16),
        scratch_shapes=[pltpu.VMEM((seq // kc, HEAD_DIM, kc), BF16),
                        pltpu.VMEM((tiles, HEAD_DIM, B_GROUP * tq), F32)],
        compiler_params=_params(("parallel", "parallel", "arbitrary")),
        name="nsa_selected",
    )(q, q, q, k, v, sel)


def _window_kernel(q0_ref, q1_ref, q2_ref, k_ref, v_ref, o_ref, *, tq, tiles, seq):
    rows = B_GROUP * tq
    span = min(WIN_LEN + tq, seq)
    for u in range(tiles):
        t0 = (pl.program_id(2) * tiles + u) * tq
        start = _window_start(t0, WIN_LEN, span, seq, tq)
        s = _dot_nt(_group_q((q0_ref, q1_ref, q2_ref), u * tq, tq), k_ref[pl.ds(start, span), :])
        qpos = t0 + (lax.broadcasted_iota(jnp.int32, (rows, 1), 0) & (tq - 1))
        kpos = start + lax.broadcasted_iota(jnp.int32, (1, span), 1)
        dist = qpos - kpos
        s = jnp.where((dist >= 0) & (dist <= WIN_LEN - 1), s, -jnp.inf)
        m = jnp.max(s, axis=1, keepdims=True)
        p = jnp.exp(s - m)
        l = jnp.sum(p, axis=1, keepdims=True)
        o = _dot(p.astype(BF16), v_ref[pl.ds(start, span), :]) / l
        o_ref[:, u * tq:(u + 1) * tq, :] = o.reshape(B_GROUP, tq, HEAD_DIM).astype(o_ref.dtype)


def _window(q, q_head0, k, k_head0, v, v_head0):
    _, nb, seq, _ = q.shape
    tq = min(ATT_TQ, seq)
    tiles = min(ATT_TILES, seq // tq)
    assert WIN_LEN % tq == 0
    return pl.pallas_call(
        functools.partial(_window_kernel, tq=tq, tiles=tiles, seq=seq),
        grid=(nb, B_KV_HEADS, seq // (tq * tiles)),
        in_specs=_group_q_specs(tq * tiles, q_head0) + [_seq_spec(seq, k_head0), _seq_spec(seq, v_head0)],
        out_specs=_group_o_spec(tq * tiles),
        out_shape=jax.ShapeDtypeStruct((B_HEADS, nb, seq, HEAD_DIM), BF16),
        compiler_params=_params(("parallel", "parallel", "arbitrary")),
        name="nsa_window",
    )(q, q, q, k, v)


def _nsa_gate_kernel(g_ref, oc_ref, os_ref, ow_ref, y_ref):
    g = g_ref[...]
    for h in range(B_HEADS):
        y = (g[:, 3 * h:3 * h + 1] * oc_ref[h].astype(F32)
             + g[:, 3 * h + 1:3 * h + 2] * os_ref[h].astype(F32)
             + g[:, 3 * h + 2:3 * h + 3] * ow_ref[h].astype(F32))
        y_ref[:, h * HEAD_DIM:(h + 1) * HEAD_DIM] = y.astype(y_ref.dtype)


def _nsa_gate(gates, o_cmp, o_sel, o_win):
    rows = gates.shape[0]
    tm = min(256, rows)
    o_spec = pl.BlockSpec((B_HEADS, tm, HEAD_DIM), lambda i: (0, i, 0))
    return pl.pallas_call(
        _nsa_gate_kernel,
        grid=(rows // tm,),
        in_specs=[pl.BlockSpec((tm, HEAD_DIM), lambda i: (i, 0)), o_spec, o_spec, o_spec],
        out_specs=pl.BlockSpec((tm, B_HEADS * HEAD_DIM), lambda i: (i, 0)),
        out_shape=jax.ShapeDtypeStruct((rows, B_HEADS * HEAD_DIM), BF16),
        compiler_params=_params(("parallel",)),
        name="nsa_gate",
    )(gates, o_cmp, o_sel, o_win)


def _stick_kernel(*refs, tq, heads):
    q_refs, k_refs, v_refs = refs[:heads], refs[heads:2 * heads], refs[2 * heads:3 * heads]
    o_ref, acc_ref, car_ref = refs[3 * heads:]
    n = pl.program_id(2)
    sub = min(CUM_TILE, tq)
    n_sub = tq // sub
    r_idx = lax.broadcasted_iota(jnp.int32, (2 * sub, 2 * sub), 0) & (sub - 1)
    c_idx = lax.broadcasted_iota(jnp.int32, (2 * sub, 2 * sub), 1)
    suffix_and_total = jnp.where((c_idx >= sub) | (r_idx > c_idx), 1.0, 0.0).astype(BF16)
    k_in_sub = lax.broadcasted_iota(jnp.int32, (1, sub), 1)
    acc_ref[...] = jnp.zeros(acc_ref.shape, F32)
    car_ref[...] = jnp.zeros(car_ref.shape, F32)

    q_in_tile = lax.broadcasted_iota(jnp.int32, (tq, 1), 0)

    def tile(k0, diagonal):
        for h in range(heads):
            z = _dot_nt(q_refs[h][...], k_refs[h][pl.ds(k0, tq), :])
            soft = jnp.log(1.0 + jnp.exp(-jnp.abs(z)))
            log_beta = jnp.minimum(z, 0.0) - soft
            log_rest = log_beta - z
            carried = car_ref[h]
            parts = [None] * n_sub
            for u in reversed(range(n_sub)):
                cols = slice(u * sub, (u + 1) * sub)
                rest_u = log_rest[:, cols]
                if diagonal:
                    before = (u * sub + k_in_sub) - q_in_tile < 0
                    rest_u = jnp.where(before, rest_u, 0.0)
                sums = _dot(jnp.concatenate(_split_bf16(rest_u), axis=1), suffix_and_total)
                a = jnp.exp(log_beta[:, cols] + (sums[:, :sub] + carried))
                if diagonal:
                    a = jnp.where(before, a, 0.0)
                carried = carried + sums[:, sub:]
                parts[u] = a.astype(BF16)
            car_ref[h] = carried
            acc_ref[h] += _dot(jnp.concatenate(parts, axis=1), v_refs[h][pl.ds(k0, tq), :])

    tile(pl.multiple_of(n * tq, tq), True)

    def step(i, carry):
        tile(pl.multiple_of((n - 1 - i) * tq, tq), False)
        return carry

    lax.fori_loop(0, n, step, 0)
    for h in range(heads):
        o_ref[:, h * HEAD_DIM:(h + 1) * HEAD_DIM] = acc_ref[h].astype(o_ref.dtype)


def _stick_breaking(q, q_head0, k, k_head0, v, v_head0):
    _, nb, seq, _ = q.shape
    tq = min(STICK_TILE, seq)
    heads = STICK_HEADS
    sub = min(CUM_TILE, tq)

    def per_head(make, head0):
        return [make(head0 + h) for h in range(heads)]

    def q_spec(head):
        return pl.BlockSpec((None, None, tq, HEAD_DIM), lambda b, g, n: (head + g * heads, b, n, 0))

    def kv_spec(head):
        return pl.BlockSpec((None, None, seq, HEAD_DIM), lambda b, g, n: (head + g * heads, b, 0, 0))

    return pl.pallas_call(
        functools.partial(_stick_kernel, tq=tq, heads=heads),
        grid=(nb, C_HEADS // heads, seq // tq),
        in_specs=per_head(q_spec, q_head0) + per_head(kv_spec, k_head0) + per_head(kv_spec, v_head0),
        out_specs=pl.BlockSpec((None, tq, heads * HEAD_DIM), lambda b, g, n: (b, n, g)),
        out_shape=jax.ShapeDtypeStruct((nb, seq, C_HEADS * HEAD_DIM), BF16),
        scratch_shapes=[pltpu.VMEM((heads, tq, HEAD_DIM), F32), pltpu.VMEM((heads, tq, sub), F32)],
        compiler_params=_params(("parallel", "parallel", "arbitrary")),
        name="stick_breaking",
    )(*([q] * heads + [k] * heads + [v] * heads))


def _merge_kernel(ya_ref, yb_ref, yc_ref, wa_ref, wb_ref, wc_ref, ga_ref, gb_ref, gc_ref, o_ref):
    mixed = (ga_ref[...].astype(F32) * _dot(ya_ref[...], wa_ref[...])
             + gb_ref[...].astype(F32) * _dot(yb_ref[...], wb_ref[...])
             + gc_ref[...].astype(F32) * _dot(yc_ref[...], wc_ref[...]))
    o_ref[...] = mixed.astype(o_ref.dtype)


def _merge(ya, yb, yc, wa, wb, wc, gates, d_model):
    m = ya.shape[0]
    tm, tn = min(512, m), 1024
    col_tiles = d_model // tn

    def y_spec(y):
        return pl.BlockSpec((tm, y.shape[1]), lambda i, j: (i, 0))

    def w_spec(w):
        return pl.BlockSpec((w.shape[0], tn), lambda i, j: (0, j))

    def g_spec(branch):
        return pl.BlockSpec((tm, tn), lambda i, j: (i, branch * col_tiles + j))

    return pl.pallas_call(
        _merge_kernel,
        grid=(m // tm, col_tiles),
        in_specs=[y_spec(ya), y_spec(yb), y_spec(yc), w_spec(wa), w_spec(wb), w_spec(wc),
                  g_spec(0), g_spec(1), g_spec(2)],
        out_specs=pl.BlockSpec((tm, tn), lambda i, j: (i, j)),
        out_shape=jax.ShapeDtypeStruct((m, d_model), BF16),
        compiler_params=_params(("parallel", "arbitrary")),
        name="branch_merge",
    )(ya, yb, yc, wa, wb, wc, gates, gates, gates)


def _ln_kernel(x_ref, y_ref, g_ref, b_ref, o_ref, ob_ref):
    z = ALPHA * x_ref[...] + y_ref[...]
    mu = jnp.mean(z, axis=1, keepdims=True)
    zc = z - mu
    var = jnp.mean(zc * zc, axis=1, keepdims=True)
    out = zc * lax.rsqrt(var + LN_EPS) * g_ref[...] + b_ref[...]
    o_ref[...] = out
    ob_ref[...] = out.astype(BF16)


def _residual_ln(x, y, g, b):
    m, d = x.shape
    tm = min(256, m)
    row = pl.BlockSpec((tm, d), lambda i: (i, 0))
    vec = pl.BlockSpec((1, d), lambda i: (0, 0))
    return pl.pallas_call(
        _ln_kernel,
        grid=(m // tm,),
        in_specs=[row, row, vec, vec],
        out_specs=[row, row],
        out_shape=[jax.ShapeDtypeStruct((m, d), F32), jax.ShapeDtypeStruct((m, d), BF16)],
        compiler_params=_params(("parallel",)),
        name="residual_layer_norm",
    )(x, y, g.reshape(1, d), b.reshape(1, d))


def _rope_tables(seq):
    half = HEAD_DIM // 2
    inv = 1.0 / (ROPE_THETA ** (jnp.arange(half, dtype=F32) / half))
    ang = jnp.arange(seq).astype(F32)[:, None] * inv[None, :]
    cos, sin = jnp.cos(ang), jnp.sin(ang)
    return jnp.concatenate([cos, cos], axis=1), jnp.concatenate([-sin, sin], axis=1)


_A_KV, _B_KV = A_SLOTS, B_KV_HEADS
_ALIGNED_FIELDS = (('q_a', A_HEADS, 'rope_scale'), ('k_a', _A_KV, 'rope'), ('v_a', _A_KV, 'none'),
                   ('q_b', B_HEADS, 'rope_scale'), ('kc_b', _B_KV, 'rope'), ('vc_b', _B_KV, 'none'),
                   ('ks_b', _B_KV, 'rope'), ('vs_b', _B_KV, 'none'), ('kw_b', _B_KV, 'rope'),
                   ('vw_b', _B_KV, 'none'))
_STICK_FIELDS = (('q_c', C_HEADS, 'scale'), ('k_c', C_HEADS, 'none'), ('v_c', C_HEADS, 'none'))
N_NSA_GATES = 3 * B_HEADS


def _field_layout(fields, heads_per_tile):
    head0, modes, start = {}, [], 0
    for name, heads, mode in fields:
        assert heads % heads_per_tile == 0
        head0[name] = start
        modes += [mode] * (heads // heads_per_tile)
        start += heads
    return head0, tuple(modes), start


def _layer(layer, x, xb, nb, seq, tables, w_rows, cmp_pe_k, cmp_wk1, cmp_wk2, cmp_pe_v, cmp_wv1, cmp_wv2,
           w_br_a, w_br_b, w_br_c, w_out, ln1_g, ln1_b, w_up, w_down, ln2_g, ln2_b):
    m, d_model = x.shape

    att_mode_cols = 4 * HEAD_DIM
    at, att_modes, att_heads = _field_layout(_ALIGNED_FIELDS, att_mode_cols // HEAD_DIM)
    att = _matmul(xb, w_rows, layer=layer, b_rows=True, name="proj_attention", out_dtype=BF16,
                  tile_modes=att_modes, mode_cols=att_mode_cols, tm=1024, tn=1024, head_major=True,
                  rope_tables=tables)
    tn_tail = 1024
    st, stick_modes, stick_heads = _field_layout(_STICK_FIELDS, tn_tail // HEAD_DIM)
    stick_col0 = att_heads * HEAD_DIM
    gates_col0 = stick_col0 + stick_heads * HEAD_DIM
    nsa_col0 = gates_col0 + 3 * d_model
    stick = _matmul(xb, w_rows, layer=layer, b_rows=True, col0=stick_col0, name="proj_stick",
                    out_dtype=BF16, tile_modes=stick_modes, tm=1024, tn=tn_tail, head_major=True)
    gates = _matmul(xb, w_rows, layer=layer, b_rows=True, col0=gates_col0, name="proj_branch_gates",
                    out_dtype=BF16, tile_modes=('sigmoid',) * (3 * d_model // tn_tail), tm=1024, tn=tn_tail)
    g_nsa = _matmul(xb, w_rows, layer=layer, b_rows=True, col0=nsa_col0, name="proj_nsa_gates",
                    out_dtype=F32, tile_modes=('sigmoid',), tm=1024, tn=HEAD_DIM)

    att = att.reshape(att_heads, nb, seq, HEAD_DIM)
    stick = stick.reshape(stick_heads, nb, seq, HEAD_DIM)

    y_a = _dilated_mixer(att, at['q_a'], at['k_a'], at['v_a'])

    kc = _compress(att, at['kc_b'], cmp_pe_k, cmp_wk1, cmp_wk2, "nsa_compress_k")
    vc = _compress(att, at['vc_b'], cmp_pe_v, cmp_wv1, cmp_wv2, "nsa_compress_v")
    o_cmp, sel = _cmp_select(att, at['q_b'], kc, vc)
    o_sel = _selected(att, at['q_b'], att, at['ks_b'], att, at['vs_b'], sel)
    o_win = _window(att, at['q_b'], att, at['kw_b'], att, at['vw_b'])
    y_b = _nsa_gate(g_nsa, o_cmp.reshape(B_HEADS, m, HEAD_DIM), o_sel.reshape(B_HEADS, m, HEAD_DIM),
                    o_win.reshape(B_HEADS, m, HEAD_DIM))

    y_c = _stick_breaking(stick, st['q_c'], stick, st['k_c'], stick, st['v_c'])

    merged = _merge(y_a.reshape(m, -1), y_b, y_c.reshape(m, -1), w_br_a.astype(BF16),
                    w_br_b.astype(BF16), w_br_c.astype(BF16), gates, d_model)
    mixed = _matmul(merged, w_out, layer=layer, name="out_proj", out_dtype=F32,
                    tile_modes=('none',) * (d_model // 512), tm=1024, tn=512)
    x1, x1b = _residual_ln(x, mixed, ln1_g, ln1_b)

    d_ff = w_up.shape[2]
    hidden = _matmul(x1b, w_up, layer=layer, name="mlp_up", out_dtype=BF16,
                     tile_modes=('relu2',) * (d_ff // 512), tm=1024, tn=512)
    down = _matmul(hidden, w_down, layer=layer, name="mlp_down", out_dtype=F32,
                   tile_modes=('none',) * (d_model // 1024), tm=2048, tn=1024, tk=1024)
    return _residual_ln(x1, down, ln2_g, ln2_b)


def kernel(x, w_in, cmp_pe_k, cmp_wk1, cmp_wk2, cmp_pe_v, cmp_wv1, cmp_wv2, w_br_a, w_br_b, w_br_c,
           w_out, ln1_g, ln1_b, w_up, w_down, ln2_g, ln2_b):
    nb, seq, d_model = x.shape
    tables = _rope_tables(seq)
    xf = x.reshape(nb * seq, d_model)
    xb = xf.astype(BF16)
    aligned_cols = sum(heads for _, heads, _ in _ALIGNED_FIELDS) * HEAD_DIM
    w_rows = _input_weight_rows(w_in, aligned_cols, N_NSA_GATES)
    for l in range(w_in.shape[0]):
        xf, xb = _layer(l, xf, xb, nb, seq, tables, w_rows, cmp_pe_k[l], cmp_wk1[l], cmp_wk2[l],
                        cmp_pe_v[l], cmp_wv1[l], cmp_wv2[l], w_br_a[l], w_br_b[l], w_br_c[l],
                        w_out, ln1_g[l], ln1_b[l], w_up, w_down, ln2_g[l], ln2_b[l])
    return xf.reshape(nb, seq, d_model)
```

```python
import functools
import math

import numpy as np
import jax
import jax.numpy as jnp
from jax import lax
from jax.experimental import pallas as pl
from jax.experimental.pallas import tpu as pltpu

F32 = jnp.float32
BF16 = jnp.bfloat16

HEAD_DIM = 128
ROPE_THETA = 10000.0
LN_EPS = 1e-5
DEPTH = 2

DIL_GROUPS = ((128, 1), (512, 4), (2048, 16))
A_SLOTS = 4
A_HEADS = A_SLOTS * len(DIL_GROUPS)

B_HEADS = 12
B_KV_HEADS = 4
B_GROUP = B_HEADS // B_KV_HEADS
CMP_LEN = 32
CMP_STRIDE = 16
SEL_LEN = 64
SEL_SHIFT = 6
SEL_TOPK = 16
WIN_LEN = 512

C_HEADS = 8

MASKED = -3e38
ALPHA = (2.0 * DEPTH) ** 0.25
Q_SCALE = HEAD_DIM ** -0.5

VMEM_LIMIT = 48 * 1024 * 1024
MXU_COLS = 256
ATT_TQ = 128
ATT_TILES = 16
DIL_ROWS = 128
DIL_ROWS_DENSE = 512
SEL_TQ = 128
SEL_TILES = 8
SEL_CHUNK = 1024
STICK_TILE = 512
STICK_HEADS = 8
CUM_TILE = 128


def _params(semantics):
    return pltpu.CompilerParams(dimension_semantics=semantics, vmem_limit_bytes=VMEM_LIMIT)


def _dot(a, b):
    return jnp.dot(a, b, preferred_element_type=F32)


def _dot_nt(a, b):
    return lax.dot_general(a, b, (((1,), (1,)), ((), ())), preferred_element_type=F32)


def _split_bf16(x):
    hi = x.astype(BF16)
    lo = (x - hi.astype(F32)).astype(BF16)
    return hi, lo


def _tile_flag(tile_modes, wanted):
    hits = [mode in wanted for mode in tile_modes]
    if all(hits) or not any(hits):
        return hits[0]
    j = pl.program_id(1)
    flag = None
    for c, hit in enumerate(hits):
        if hit:
            flag = (j == c) if flag is None else (flag | (j == c))
    return flag


def _write_columns(acc, o_ref, col0, tile_modes, head_major, cos_ref, sin_ref):
    is_rope = _tile_flag(tile_modes, ('rope', 'rope_scale'))
    is_scaled = _tile_flag(tile_modes, ('rope_scale', 'scale'))
    uniform = tile_modes[0] if len(set(tile_modes)) == 1 else None
    assert uniform is not None or not (set(tile_modes) & {'sigmoid', 'relu2'})

    def epilogue(y):
        if is_rope is not False:
            roped = y * cos_ref[...] + pltpu.roll(y, HEAD_DIM // 2, 1) * sin_ref[...]
            y = roped if is_rope is True else jnp.where(is_rope, roped, y)
        if is_scaled is not False:
            y = y * (Q_SCALE if is_scaled is True else jnp.where(is_scaled, Q_SCALE, 1.0))
        if uniform == 'sigmoid':
            y = jax.nn.sigmoid(y)
        if uniform == 'relu2':
            y = jnp.square(jnp.maximum(y, 0.0))
        return y

    width = acc.shape[1]
    if head_major:
        for c in range(width // HEAD_DIM):
            y = epilogue(acc[:, c * HEAD_DIM:(c + 1) * HEAD_DIM])
            o_ref[col0 // HEAD_DIM + c] = y.astype(o_ref.dtype)
    else:
        o_ref[:, col0:col0 + width] = epilogue(acc).astype(o_ref.dtype)


def _matmul_kernel(*refs, tile_modes, units, head_major, n_k, use_rope, acc_in_out, b_rows):
    refs = list(refs)
    a_ref, b_ref = refs[0], refs[1]
    cos_ref, sin_ref = (refs[2], refs[3]) if use_rope else (None, None)
    o_ref = refs[4] if use_rope else refs[2]
    acc_ref = o_ref if acc_in_out else (refs[-1] if n_k > 1 else None)
    tn = b_ref.shape[0] if b_rows else b_ref.shape[1]

    def product(col0, width):
        if b_rows:
            return _dot_nt(a_ref[...], b_ref[col0:col0 + width, :].astype(BF16))
        return _dot(a_ref[...], b_ref[:, col0:col0 + width].astype(BF16))

    if n_k == 1:
        unit_cols = tn // units
        width = min(MXU_COLS, unit_cols)
        for col0 in range(0, tn, width):
            per_tile = tile_modes[col0 // unit_cols::units]
            _write_columns(product(col0, width), o_ref, col0, per_tile, head_major, cos_ref, sin_ref)
        return
    assert units == 1

    @pl.when(pl.program_id(2) == 0)
    def _():
        acc_ref[...] = jnp.zeros(acc_ref.shape, F32)

    acc_ref[...] += product(0, tn)
    if not acc_in_out:
        pl.when(pl.program_id(2) == n_k - 1)(
            lambda: _write_columns(acc_ref[...], o_ref, 0, tile_modes, head_major, cos_ref, sin_ref))


def _matmul(a, b, *, name, out_dtype, tile_modes, tm, tn, tk=None, layer=None, col0=0,
            head_major=False, rope_tables=None, b_rows=False, mode_cols=None):
    m, kdim = a.shape
    mode_cols = tn if mode_cols is None else mode_cols
    n = mode_cols * len(tile_modes)
    assert tn % mode_cols == 0 and n % tn == 0
    tm = min(tm, m)
    tk = kdim if tk is None else min(tk, kdim)
    assert m % tm == 0 and kdim % tk == 0 and col0 % tn == 0
    n_k = kdim // tk
    col_tile0 = col0 // tn
    use_rope = any(mode.startswith('rope') for mode in tile_modes)
    acc_in_out = n_k > 1 and out_dtype == F32 and set(tile_modes) == {'none'} and not head_major
    if b_rows:
        assert layer is not None
        b_spec = pl.BlockSpec((None, tn, tk), lambda i, j, k: (layer, col_tile0 + j, k))
    elif layer is None:
        b_spec = pl.BlockSpec((tk, tn), lambda i, j, k: (k, col_tile0 + j))
    else:
        b_spec = pl.BlockSpec((None, tk, tn), lambda i, j, k: (layer, k, col_tile0 + j))
    in_specs = [pl.BlockSpec((tm, tk), lambda i, j, k: (i, k)), b_spec]
    operands = [a, b]
    if use_rope:
        assert head_major
        cos, sin = rope_tables
        seq_tiles = cos.shape[0] // tm
        assert cos.shape[0] % tm == 0
        spec = pl.BlockSpec((tm, HEAD_DIM), lambda i, j, k: (i % seq_tiles, 0))
        in_specs += [spec, spec]
        operands += [cos, sin]
    if head_major:
        out_shape = jax.ShapeDtypeStruct((n // HEAD_DIM, m, HEAD_DIM), out_dtype)
        out_spec = pl.BlockSpec((tn // HEAD_DIM, tm, HEAD_DIM), lambda i, j, k: (j, i, 0))
    else:
        out_shape = jax.ShapeDtypeStruct((m, n), out_dtype)
        out_spec = pl.BlockSpec((tm, tn), lambda i, j, k: (i, j))
    scratch = [pltpu.VMEM((tm, tn), F32)] if (n_k > 1 and not acc_in_out) else []
    return pl.pallas_call(
        functools.partial(_matmul_kernel, tile_modes=tuple(tile_modes), units=tn // mode_cols,
                          head_major=head_major, n_k=n_k, use_rope=use_rope, acc_in_out=acc_in_out,
                          b_rows=b_rows),
        grid=(m // tm, n // tn, n_k),
        in_specs=in_specs,
        out_specs=out_spec,
        out_shape=out_shape,
        scratch_shapes=scratch,
        compiler_params=_params(("parallel", "parallel", "arbitrary")),
        name=name,
    )(*operands)


def _gather_rows_kernel(w_ref, o_ref, *, tn, k_chunks, n_layers):
    stride = k_chunks * n_layers
    by_chunk = pltpu.einshape("nck->cnk", w_ref[...].reshape(tn, stride, HEAD_DIM))
    for layer in range(n_layers):
        pieces = [by_chunk[c * n_layers + layer] for c in range(k_chunks)]
        o_ref[layer] = jnp.concatenate(pieces, axis=1).astype(o_ref.dtype)


def _input_weight_rows(w_in, aligned_cols, n_gates, tn=HEAD_DIM):
    n_layers, kdim, n = w_in.shape
    k_chunks = kdim // HEAD_DIM
    tail = n - aligned_cols - n_gates
    assert aligned_cols % tn == 0 and tail % tn == 0 and kdim % HEAD_DIM == 0
    aligned_tiles, tail_tiles = aligned_cols // tn, tail // tn
    rows_per_col = k_chunks * n_layers
    flat = w_in.reshape(n_layers, k_chunks, HEAD_DIM, n).transpose(3, 1, 0, 2).reshape(n * rows_per_col, HEAD_DIM)

    def source_row(j):
        col = jnp.where(j < aligned_tiles, tn * j,
                        jnp.where(j < aligned_tiles + tail_tiles,
                                  aligned_cols + n_gates + tn * (j - aligned_tiles), aligned_cols))
        return (col * rows_per_col, 0)

    tiles = aligned_tiles + tail_tiles + 1
    return pl.pallas_call(
        functools.partial(_gather_rows_kernel, tn=tn, k_chunks=k_chunks, n_layers=n_layers),
        grid=(tiles,),
        in_specs=[pl.BlockSpec((pl.Element(tn * rows_per_col), pl.Element(HEAD_DIM)), source_row)],
        out_specs=pl.BlockSpec((n_layers, tn, kdim), lambda j: (0, j, 0)),
        out_shape=jax.ShapeDtypeStruct((n_layers, tiles * tn, kdim), BF16),
        compiler_params=_params(("parallel",)),
        name="input_weight_rows",
    )(flat)


def _window_start(t0, back, span, seq, align):
    start = jnp.minimum(jnp.maximum(t0 - back, 0), seq - span)
    return pl.multiple_of(start, align)


def _by_residue(x, dil):
    rows = x.shape[0] // dil
    return pltpu.einshape("ldk->dlk", x.reshape(rows, dil, HEAD_DIM))


def _dilated_group_kernel(*refs, dil, rows, band, n_rows, n_others):
    q_ref, k_ref, v_ref = refs[:3]
    others = refs[3:3 + 2 * n_others]
    if n_others:
        y_ref = refs[3 + 2 * n_others]
        kr_ref, vr_ref, o_ref, lse_ref = refs[4 + 2 * n_others:]
    else:
        o_ref, lse_ref = refs[3:5]
        kr_ref, vr_ref = refs[5:] if dil > 1 else (None, None)
    j = pl.program_id(2)
    if dil > 1:

        @pl.when(j == 0)
        def _():
            kr_ref[...] = _by_residue(k_ref[...], dil)
            vr_ref[...] = _by_residue(v_ref[...], dil)

        q_by = _by_residue(q_ref[...], dil)
    sub = min(band, rows) if dil == 1 else rows
    span = min(band + sub, n_rows)
    for t in range(rows // sub):
        l0 = j * rows + t * sub
        start = _window_start(l0, band, span, n_rows, math.gcd(sub, band))
        dist = (l0 + lax.broadcasted_iota(jnp.int32, (sub, 1), 0)) - (start + lax.broadcasted_iota(jnp.int32, (1, span), 1))
        keep = (dist >= 0) & (dist <= band)
        for r in range(dil):
            if dil == 1:
                q = q_ref[t * sub:(t + 1) * sub, :]
                keys, vals = k_ref[pl.ds(start, span), :], v_ref[pl.ds(start, span), :]
            else:
                q, keys, vals = q_by[r], kr_ref[r, pl.ds(start, span), :], vr_ref[r, pl.ds(start, span), :]
            s = jnp.where(keep, _dot_nt(q, keys), -jnp.inf)
            m = jnp.max(s, axis=1, keepdims=True)
            p = jnp.exp(s - m)
            l = jnp.sum(p, axis=1, keepdims=True)
            o = _dot(p.astype(BF16), vals) / l
            lse = jnp.broadcast_to(m + jnp.log(l), (sub, HEAD_DIM))
            if dil == 1:
                o_ref[t * sub:(t + 1) * sub, :] = o
                lse_ref[t * sub:(t + 1) * sub, :] = lse
            else:
                o_ref[pl.ds(r, rows, stride=dil), :] = o
                lse_ref[pl.ds(r, rows, stride=dil), :] = lse
    if n_others:
        outs = [others[2 * g][...] for g in range(n_others)] + [o_ref[...]]
        lses = [others[2 * g + 1][...] for g in range(n_others)] + [lse_ref[...]]
        top = functools.reduce(jnp.maximum, lses)
        ws = [jnp.exp(lse - top) for lse in lses]
        total = functools.reduce(lambda a, b: a + b, ws)
        mixed = functools.reduce(lambda a, b: a + b, [w * o for w, o in zip(ws, outs)])
        y_ref[...] = (mixed / total).astype(y_ref.dtype)


def _head_spec(rows, head0):
    return pl.BlockSpec((None, None, rows, HEAD_DIM), lambda b, h, n: (head0 + h, b, n, 0))


def _seq_spec(seq, head0):
    return pl.BlockSpec((None, None, seq, HEAD_DIM), lambda b, h, n: (head0 + h, b, 0, 0))


def _dilated_group(att, q_head0, k_head0, v_head0, window, dil, others=()):
    _, nb, seq, _ = att.shape
    n_rows = seq // dil
    band = window // dil
    rows = min(DIL_ROWS if dil > 1 else DIL_ROWS_DENSE, n_rows)
    assert seq % dil == 0 and n_rows % rows == 0 and (dil > 1 or not others)
    pair_spec = _head_spec(rows * dil, 0)
    pair_shape = jax.ShapeDtypeStruct((A_SLOTS, nb, seq, HEAD_DIM), F32)
    scratch = [pltpu.VMEM((dil, n_rows, HEAD_DIM), BF16)] * 2 if dil > 1 else []
    if others:
        out_specs = pl.BlockSpec((None, rows * dil, HEAD_DIM), lambda b, s, j: (b, j, s))
        out_shape = jax.ShapeDtypeStruct((nb, seq, A_SLOTS * HEAD_DIM), BF16)
        scratch = scratch + [pltpu.VMEM((rows * dil, HEAD_DIM), F32)] * 2
    else:
        out_specs, out_shape = [pair_spec, pair_spec], [pair_shape, pair_shape]
    return pl.pallas_call(
        functools.partial(_dilated_group_kernel, dil=dil, rows=rows, band=band, n_rows=n_rows,
                          n_others=len(others) // 2),
        grid=(nb, A_SLOTS, n_rows // rows),
        in_specs=[_head_spec(rows * dil, q_head0), _seq_spec(seq, k_head0), _seq_spec(seq, v_head0)]
        + [pair_spec] * len(others),
        out_specs=out_specs,
        out_shape=out_shape,
        scratch_shapes=scratch,
        compiler_params=_params(("parallel", "parallel", "arbitrary")),
        name="dilated_group_%d" % dil,
    )(att, att, att, *others)


def _dilated_mixer(att, q_head0, k_head0, v_head0):
    pairs = []
    for g, (window, dil) in enumerate(DIL_GROUPS):
        last = g == len(DIL_GROUPS) - 1
        result = _dilated_group(att, q_head0 + g * A_SLOTS, k_head0, v_head0, window, dil,
                                others=tuple(pairs) if last else ())
        if last:
            return result
        pairs += list(result)


def _gelu_tanh(x):
    return 0.5 * x * (1.0 + jnp.tanh(math.sqrt(2.0 / math.pi) * (x + 0.044715 * (x * x * x))))


def _compress_kernel(x_ref, pe_ref, w1_ref, w2_ref, o_ref):
    by_token = _by_residue(x_ref[...], CMP_STRIDE)
    x = jnp.concatenate([by_token[r] for r in range(CMP_STRIDE)], axis=1).astype(F32)
    first = _dot((x + pe_ref[0:1, :]).astype(BF16), w1_ref[0])
    second = _dot((x + pe_ref[1:2, :]).astype(BF16), w1_ref[1])
    chunks = x.shape[0]
    hidden = first + pltpu.roll(second, chunks - 1, 0)
    o_ref[...] = _dot(_gelu_tanh(hidden).astype(BF16), w2_ref[...]).astype(o_ref.dtype)


def _compress(t, head0, pe, w1, w2, name):
    _, nb, seq, _ = t.shape
    nh = B_KV_HEADS
    chunks = seq // CMP_STRIDE
    width = CMP_STRIDE * HEAD_DIM
    ratio = CMP_LEN // CMP_STRIDE
    return pl.pallas_call(
        _compress_kernel,
        grid=(nh, nb),
        in_specs=[pl.BlockSpec((None, None, seq, HEAD_DIM), lambda h, b: (head0 + h, b, 0, 0)),
                  pl.BlockSpec((ratio, width), lambda h, b: (0, 0)),
                  pl.BlockSpec((ratio, width, HEAD_DIM), lambda h, b: (0, 0, 0)),
                  pl.BlockSpec((HEAD_DIM, HEAD_DIM), lambda h, b: (0, 0))],
        out_specs=pl.BlockSpec((None, None, chunks, HEAD_DIM), lambda h, b: (h, b, 0, 0)),
        out_shape=jax.ShapeDtypeStruct((nh, nb, chunks, HEAD_DIM), BF16),
        compiler_params=_params(("parallel", "parallel")),
        name=name,
    )(t, pe.reshape(ratio, width), w1.reshape(ratio, width, HEAD_DIM).astype(BF16), w2.astype(BF16))


def _group_q(q_refs, row0=0, rows=None):
    rows = q_refs[0].shape[0] if rows is None else rows
    return jnp.concatenate([r[row0:row0 + rows, :] for r in q_refs], axis=0)


def _group_q_specs(tq, head0):
    return [pl.BlockSpec((None, None, tq, HEAD_DIM),
                         functools.partial(lambda b, h, n, g: (head0 + h * B_GROUP + g, b, n, 0), g=g))
            for g in range(B_GROUP)]


def _cmp_select_kernel(q0_ref, q1_ref, q2_ref, kc_ref, vc_ref, ov_ref, o_ref, sel_ref, *, tq, tiles):
    for u in range(tiles):
        _cmp_select_tile((q0_ref, q1_ref, q2_ref), kc_ref, vc_ref, ov_ref, o_ref, sel_ref,
                         (pl.program_id(2) * tiles + u) * tq, u * tq, tq)


def _cmp_select_tile(q_refs, kc_ref, vc_ref, ov_ref, o_ref, sel_ref, t0, row0, tq):
    n_cmp = kc_ref.shape[0]
    rows = B_GROUP * tq
    s = _dot_nt(_group_q(q_refs, row0, tq), kc_ref[...])
    tpos3 = t0 + (lax.broadcasted_iota(jnp.int32, (rows, 1), 0) & (tq - 1))
    c_end = lax.broadcasted_iota(jnp.int32, (1, n_cmp), 1) * CMP_STRIDE + (CMP_LEN - 1)
    s = jnp.where(c_end - tpos3 <= 0, s, -jnp.inf)
    m = jnp.max(s, axis=1, keepdims=True)
    m = jnp.where(jnp.abs(m) < jnp.inf, m, 0.0)
    e = jnp.exp(s - m)
    den = jnp.sum(e, axis=1, keepdims=True)
    p = e / jnp.where(den > 0, den, 1.0)
    o = _dot(p.astype(BF16), vc_ref[...]).reshape(B_GROUP, tq, HEAD_DIM)
    o_ref[:, row0:row0 + tq, :] = o.astype(o_ref.dtype)

    p_sum = p[0:tq] + p[tq:2 * tq] + p[2 * tq:3 * tq]
    hi, lo = _split_bf16(p_sum)
    imp = _dot(hi, ov_ref[...]) + _dot(lo, ov_ref[...])
    tpos = t0 + lax.broadcasted_iota(jnp.int32, (tq, 1), 0)
    rel = lax.broadcasted_iota(jnp.int32, (1, HEAD_DIM), 1) - (tpos >> SEL_SHIFT)
    j_abs = jnp.broadcast_to(lax.broadcasted_iota(jnp.int32, (1, HEAD_DIM), 1), rel.shape)
    forced = (j_abs == 0) | (rel == 0) | (rel == -1)
    imp = jnp.where(forced, jnp.inf, jnp.where(rel <= 0, imp, -jnp.inf))

    n_sel = HEAD_DIM // 2
    imp_t = imp.T
    mine = imp_t[0:n_sel]
    j_idx = lax.broadcasted_iota(jnp.int32, (n_sel, tq), 0)
    beaten = jnp.zeros((n_sel, tq), F32)
    for kk in range(n_sel):
        other = imp_t[kk:kk + 1, :]
        wins = (other > mine) | ((other == mine) & (j_idx > kk))
        beaten = beaten + jnp.where(wins, 1.0, 0.0)
    chosen = jnp.where(beaten < SEL_TOPK, 0.0, MASKED)
    chosen = jnp.concatenate([chosen, jnp.full((HEAD_DIM - n_sel, tq), MASKED, F32)], axis=0)
    sel_ref[:, row0:row0 + tq] = chosen.astype(sel_ref.dtype)


def _overlap_matrix(n_cmp_rows, n_sel):
    c_start = np.arange(n_cmp_rows) * CMP_STRIDE
    c_end = c_start + CMP_LEN - 1
    s_start = np.arange(HEAD_DIM) * SEL_LEN
    ov = (c_start[:, None] <= s_start[None, :] + SEL_LEN - 1) & (c_end[:, None] >= s_start[None, :])
    ov &= (np.arange(HEAD_DIM) < n_sel)[None, :]
    return jnp.asarray(ov.astype(np.float32), dtype=BF16)


def _group_o_spec(tq):
    return pl.BlockSpec((B_GROUP, None, tq, HEAD_DIM), lambda b, h, n: (h, b, n, 0))


def _sel_spec(tq):
    return pl.BlockSpec((None, None, HEAD_DIM, tq), lambda b, h, n: (h, b, 0, n))


def _cmp_select(q, q_head0, kc, vc):
    _, nb, seq, _ = q.shape
    assert seq // SEL_LEN <= HEAD_DIM // 2
    tq = min(ATT_TQ, seq)
    tiles = min(ATT_TILES, seq // tq)
    n_cmp = kc.shape[2]
    kv_spec = pl.BlockSpec((None, None, n_cmp, HEAD_DIM), lambda b, h, n: (h, b, 0, 0))
    return pl.pallas_call(
        functools.partial(_cmp_select_kernel, tq=tq, tiles=tiles),
        grid=(nb, B_KV_HEADS, seq // (tq * tiles)),
        in_specs=_group_q_specs(tq * tiles, q_head0) + [kv_spec, kv_spec,
                                                        pl.BlockSpec((n_cmp, HEAD_DIM), lambda b, h, n: (0, 0))],
        out_specs=[_group_o_spec(tq * tiles), _sel_spec(tq * tiles)],
        out_shape=[jax.ShapeDtypeStruct((B_HEADS, nb, seq, HEAD_DIM), BF16),
                   jax.ShapeDtypeStruct((B_KV_HEADS, nb, HEAD_DIM, seq), BF16)],
        compiler_params=_params(("parallel", "parallel", "arbitrary")),
        name="nsa_compressed_select",
    )(q, q, q, kc, vc, _overlap_matrix(n_cmp, seq // SEL_LEN))


def _selected_kernel(q0_ref, q1_ref, q2_ref, k_ref, v_ref, sel_ref, o_ref, vt_ref, acc_ref, *, tq, kc, tiles):
    n = pl.program_id(2)
    t_first = n * (tq * tiles)
    cols = B_GROUP * tq

    @pl.when(n == 0)
    def _():
        for c in range(v_ref.shape[0] // kc):
            vt_ref[c] = v_ref[c * kc:(c + 1) * kc, :].astype(F32).T.astype(BF16)

    qs = [_group_q((q0_ref, q1_ref, q2_ref), u * tq, tq) for u in range(tiles)]
    qposs = [t_first + u * tq + lax.broadcasted_iota(jnp.int32, (1, tq), 1) for u in range(tiles)]
    key_in_chunk = lax.broadcasted_iota(jnp.int32, (kc, 1), 0)
    blocks = kc // SEL_LEN
    acc_ref[...] = jnp.zeros(acc_ref.shape, F32)

    def chunk(i, carry, diagonal):
        k0 = pl.multiple_of(i * kc, kc)
        kpos = k0 + key_in_chunk
        keys = k_ref[pl.ds(k0, kc), :]
        blk0 = pl.multiple_of(i * blocks, blocks)
        out = []
        for u in range(tiles):
            m_old, l_old = carry[u]
            n_keys = tq * (u + 1) if (diagonal and tq * tiles == kc) else kc
            rows = sel_ref[pl.ds(blk0, blocks), u * tq:(u + 1) * tq].astype(F32)
            bias = jnp.broadcast_to(rows[:, None, :], (blocks, SEL_LEN, tq)).reshape(kc, tq)[:n_keys]
            if diagonal:
                bias = jnp.where(kpos[:n_keys] - qposs[u] <= 0, bias, MASKED)
            s = _dot_nt(keys[:n_keys], qs[u]) + jnp.concatenate([bias] * B_GROUP, axis=1)
            m_new = jnp.maximum(m_old, jnp.max(s, axis=0, keepdims=True))
            p = jnp.exp(s - m_new)
            alpha = jnp.exp(m_old - m_new)
            acc_ref[u] = alpha * acc_ref[u] + _dot(vt_ref[i, :, :n_keys], p.astype(BF16))
            out.append((m_new, alpha * l_old + jnp.sum(p, axis=0, keepdims=True)))
        return tuple(out)

    init = tuple((jnp.full((1, cols), MASKED, F32), jnp.zeros((1, cols), F32)) for _ in range(tiles))
    last = t_first // kc
    carry = lax.fori_loop(0, last, lambda i, c: chunk(i, c, False), init)
    final = chunk(last, carry, True)
    for u in range(tiles):
        o_t = acc_ref[u] / final[u][1]
        for g in range(B_GROUP):
            o_ref[g, u * tq:(u + 1) * tq, :] = o_t[:, g * tq:(g + 1) * tq].T.astype(o_ref.dtype)


def _selected(q, q_head0, k, k_head0, v, v_head0, sel):
    _, nb, seq, _ = q.shape
    tq = min(SEL_TQ, seq)
    kc = min(SEL_CHUNK, seq)
    tiles = min(SEL_TILES, kc // tq)
    assert kc % (tq * tiles) == 0 and seq % kc == 0 and tq == HEAD_DIM
    return pl.pallas_call(
        functools.partial(_selected_kernel, tq=tq, kc=kc, tiles=tiles),
        grid=(nb, B_KV_HEADS, seq // (tq * tiles)),
        in_specs=_group_q_specs(tq * tiles, q_head0) + [_seq_spec(seq, k_head0), _seq_spec(seq, v_head0),
                                                        _sel_spec(tq * tiles)],
        out_specs=_group_o_spec(tq * tiles),
        out_shape=jax.ShapeDtypeStruct((B_HEADS, nb, seq, HEAD_DIM), BF16),
        scratch_shapes=[pltpu.VMEM((seq // kc, HEAD_DIM, kc), BF16),
                        pltpu.VMEM((tiles, HEAD_DIM, B_GROUP * tq), F32)],
        compiler_params=_params(("parallel", "parallel", "arbitrary")),
        name="nsa_selected",
    )(q, q, q, k, v, sel)


def _window_kernel(q0_ref, q1_ref, q2_ref, k_ref, v_ref, o_ref, *, tq, tiles, seq):
    rows = B_GROUP * tq
    span = min(WIN_LEN + tq, seq)
    for u in range(tiles):
        t0 = (pl.program_id(2) * tiles + u) * tq
        start = _window_start(t0, WIN_LEN, span, seq, tq)
        s = _dot_nt(_group_q((q0_ref, q1_ref, q2_ref), u * tq, tq), k_ref[pl.ds(start, span), :])
        qpos = t0 + (lax.broadcasted_iota(jnp.int32, (rows, 1), 0) & (tq - 1))
        kpos = start + lax.broadcasted_iota(jnp.int32, (1, span), 1)
        dist = qpos - kpos
        s = jnp.where((dist >= 0) & (dist <= WIN_LEN - 1), s, -jnp.inf)
        m = jnp.max(s, axis=1, keepdims=True)
        p = jnp.exp(s - m)
        l = jnp.sum(p, axis=1, keepdims=True)
        o = _dot(p.astype(BF16), v_ref[pl.ds(start, span), :]) / l
        o_ref[:, u * tq:(u + 1) * tq, :] = o.reshape(B_GROUP, tq, HEAD_DIM).astype(o_ref.dtype)


def _window(q, q_head0, k, k_head0, v, v_head0):
    _, nb, seq, _ = q.shape
    tq = min(ATT_TQ, seq)
    tiles = min(ATT_TILES, seq // tq)
    assert WIN_LEN % tq == 0
    return pl.pallas_call(
        functools.partial(_window_kernel, tq=tq, tiles=tiles, seq=seq),
        grid=(nb, B_KV_HEADS, seq // (tq * tiles)),
        in_specs=_group_q_specs(tq * tiles, q_head0) + [_seq_spec(seq, k_head0), _seq_spec(seq, v_head0)],
        out_specs=_group_o_spec(tq * tiles),
        out_shape=jax.ShapeDtypeStruct((B_HEADS, nb, seq, HEAD_DIM), BF16),
        compiler_params=_params(("parallel", "parallel", "arbitrary")),
        name="nsa_window",
    )(q, q, q, k, v)


def _nsa_gate_kernel(g_ref, oc_ref, os_ref, ow_ref, y_ref):
    g = g_ref[...]
    for h in range(B_HEADS):
        y = (g[:, 3 * h:3 * h + 1] * oc_ref[h].astype(F32)
             + g[:, 3 * h + 1:3 * h + 2] * os_ref[h].astype(F32)
             + g[:, 3 * h + 2:3 * h + 3] * ow_ref[h].astype(F32))
        y_ref[:, h * HEAD_DIM:(h + 1) * HEAD_DIM] = y.astype(y_ref.dtype)


def _nsa_gate(gates, o_cmp, o_sel, o_win):
    rows = gates.shape[0]
    tm = min(256, rows)
    o_spec = pl.BlockSpec((B_HEADS, tm, HEAD_DIM), lambda i: (0, i, 0))
    return pl.pallas_call(
        _nsa_gate_kernel,
        grid=(rows // tm,),
        in_specs=[pl.BlockSpec((tm, HEAD_DIM), lambda i: (i, 0)), o_spec, o_spec, o_spec],
        out_specs=pl.BlockSpec((tm, B_HEADS * HEAD_DIM), lambda i: (i, 0)),
        out_shape=jax.ShapeDtypeStruct((rows, B_HEADS * HEAD_DIM), BF16),
        compiler_params=_params(("parallel",)),
        name="nsa_gate",
    )(gates, o_cmp, o_sel, o_win)


def _stick_kernel(*refs, tq, heads):
    q_refs, k_refs, v_refs = refs[:heads], refs[heads:2 * heads], refs[2 * heads:3 * heads]
    o_ref, acc_ref, car_ref = refs[3 * heads:]
    n = pl.program_id(2)
    sub = min(CUM_TILE, tq)
    n_sub = tq // sub
    r_idx = lax.broadcasted_iota(jnp.int32, (2 * sub, 2 * sub), 0) & (sub - 1)
    c_idx = lax.broadcasted_iota(jnp.int32, (2 * sub, 2 * sub), 1)
    suffix_and_total = jnp.where((c_idx >= sub) | (r_idx > c_idx), 1.0, 0.0).astype(BF16)
    k_in_sub = lax.broadcasted_iota(jnp.int32, (1, sub), 1)
    acc_ref[...] = jnp.zeros(acc_ref.shape, F32)
    car_ref[...] = jnp.zeros(car_ref.shape, F32)

    q_in_tile = lax.broadcasted_iota(jnp.int32, (tq, 1), 0)

    def tile(k0, diagonal):
        for h in range(heads):
            z = _dot_nt(q_refs[h][...], k_refs[h][pl.ds(k0, tq), :])
            soft = jnp.log(1.0 + jnp.exp(-jnp.abs(z)))
            log_beta = jnp.minimum(z, 0.0) - soft
            log_rest = log_beta - z
            carried = car_ref[h]
            parts = [None] * n_sub
            for u in reversed(range(n_sub)):
                cols = slice(u * sub, (u + 1) * sub)
                rest_u = log_rest[:, cols]
                if diagonal:
                    before = (u * sub + k_in_sub) - q_in_tile < 0
                    rest_u = jnp.where(before, rest_u, 0.0)
                sums = _dot(jnp.concatenate(_split_bf16(rest_u), axis=1), suffix_and_total)
                a = jnp.exp(log_beta[:, cols] + (sums[:, :sub] + carried))
                if diagonal:
                    a = jnp.where(before, a, 0.0)
                carried = carried + sums[:, sub:]
                parts[u] = a.astype(BF16)
            car_ref[h] = carried
            acc_ref[h] += _dot(jnp.concatenate(parts, axis=1), v_refs[h][pl.ds(k0, tq), :])

    tile(pl.multiple_of(n * tq, tq), True)

    def step(i, carry):
        tile(pl.multiple_of((n - 1 - i) * tq, tq), False)
        return carry

    lax.fori_loop(0, n, step, 0)
    for h in range(heads):
        o_ref[:, h * HEAD_DIM:(h + 1) * HEAD_DIM] = acc_ref[h].astype(o_ref.dtype)


def _stick_breaking(q, q_head0, k, k_head0, v, v_head0):
    _, nb, seq, _ = q.shape
    tq = min(STICK_TILE, seq)
    heads = STICK_HEADS
    sub = min(CUM_TILE, tq)

    def per_head(make, head0):
        return [make(head0 + h) for h in range(heads)]

    def q_spec(head):
        return pl.BlockSpec((None, None, tq, HEAD_DIM), lambda b, g, n: (head + g * heads, b, n, 0))

    def kv_spec(head):
        return pl.BlockSpec((None, None, seq, HEAD_DIM), lambda b, g, n: (head + g * heads, b, 0, 0))

    return pl.pallas_call(
        functools.partial(_stick_kernel, tq=tq, heads=heads),
        grid=(nb, C_HEADS // heads, seq // tq),
        in_specs=per_head(q_spec, q_head0) + per_head(kv_spec, k_head0) + per_head(kv_spec, v_head0),
        out_specs=pl.BlockSpec((None, tq, heads * HEAD_DIM), lambda b, g, n: (b, n, g)),
        out_shape=jax.ShapeDtypeStruct((nb, seq, C_HEADS * HEAD_DIM), BF16),
        scratch_shapes=[pltpu.VMEM((heads, tq, HEAD_DIM), F32), pltpu.VMEM((heads, tq, sub), F32)],
        compiler_params=_params(("parallel", "parallel", "arbitrary")),
        name="stick_breaking",
    )(*([q] * heads + [k] * heads + [v] * heads))


def _merge_kernel(ya_ref, yb_ref, yc_ref, wa_ref, wb_ref, wc_ref, ga_ref, gb_ref, gc_ref, o_ref):
    mixed = (ga_ref[...].astype(F32) * _dot(ya_ref[...], wa_ref[...])
             + gb_ref[...].astype(F32) * _dot(yb_ref[...], wb_ref[...])
             + gc_ref[...].astype(F32) * _dot(yc_ref[...], wc_ref[...]))
    o_ref[...] = mixed.astype(o_ref.dtype)


def _merge(ya, yb, yc, wa, wb, wc, gates, d_model):
    m = ya.shape[0]
    tm, tn = min(512, m), 1024
    col_tiles = d_model // tn

    def y_spec(y):
        return pl.BlockSpec((tm, y.shape[1]), lambda i, j: (i, 0))

    def w_spec(w):
        return pl.BlockSpec((w.shape[0], tn), lambda i, j: (0, j))

    def g_spec(branch):
        return pl.BlockSpec((tm, tn), lambda i, j: (i, branch * col_tiles + j))

    return pl.pallas_call(
        _merge_kernel,
        grid=(m // tm, col_tiles),
        in_specs=[y_spec(ya), y_spec(yb), y_spec(yc), w_spec(wa), w_spec(wb), w_spec(wc),
                  g_spec(0), g_spec(1), g_spec(2)],
        out_specs=pl.BlockSpec((tm, tn), lambda i, j: (i, j)),
        out_shape=jax.ShapeDtypeStruct((m, d_model), BF16),
        compiler_params=_params(("parallel", "arbitrary")),
        name="branch_merge",
    )(ya, yb, yc, wa, wb, wc, gates, gates, gates)


def _ln_kernel(x_ref, y_ref, g_ref, b_ref, o_ref, ob_ref):
    z = ALPHA * x_ref[...] + y_ref[...]
    mu = jnp.mean(z, axis=1, keepdims=True)
    zc = z - mu
    var = jnp.mean(zc * zc, axis=1, keepdims=True)
    out = zc * lax.rsqrt(var + LN_EPS) * g_ref[...] + b_ref[...]
    o_ref[...] = out
    ob_ref[...] = out.astype(BF16)


def _residual_ln(x, y, g, b):
    m, d = x.shape
    tm = min(256, m)
    row = pl.BlockSpec((tm, d), lambda i: (i, 0))
    vec = pl.BlockSpec((1, d), lambda i: (0, 0))
    return pl.pallas_call(
        _ln_kernel,
        grid=(m // tm,),
        in_specs=[row, row, vec, vec],
        out_specs=[row, row],
        out_shape=[jax.ShapeDtypeStruct((m, d), F32), jax.ShapeDtypeStruct((m, d), BF16)],
        compiler_params=_params(("parallel",)),
        name="residual_layer_norm",
    )(x, y, g.reshape(1, d), b.reshape(1, d))


def _rope_tables(seq):
    half = HEAD_DIM // 2
    inv = 1.0 / (ROPE_THETA ** (jnp.arange(half, dtype=F32) / half))
    ang = jnp.arange(seq).astype(F32)[:, None] * inv[None, :]
    cos, sin = jnp.cos(ang), jnp.sin(ang)
    return jnp.concatenate([cos, cos], axis=1), jnp.concatenate([-sin, sin], axis=1)


_A_KV, _B_KV = A_SLOTS, B_KV_HEADS
_ALIGNED_FIELDS = (('q_a', A_HEADS, 'rope_scale'), ('k_a', _A_KV, 'rope'), ('v_a', _A_KV, 'none'),
                   ('q_b', B_HEADS, 'rope_scale'), ('kc_b', _B_KV, 'rope'), ('vc_b', _B_KV, 'none'),
                   ('ks_b', _B_KV, 'rope'), ('vs_b', _B_KV, 'none'), ('kw_b', _B_KV, 'rope'),
                   ('vw_b', _B_KV, 'none'))
_STICK_FIELDS = (('q_c', C_HEADS, 'scale'), ('k_c', C_HEADS, 'none'), ('v_c', C_HEADS, 'none'))
N_NSA_GATES = 3 * B_HEADS


def _field_layout(fields, heads_per_tile):
    head0, modes, start = {}, [], 0
    for name, heads, mode in fields:
        assert heads % heads_per_tile == 0
        head0[name] = start
        modes += [mode] * (heads // heads_per_tile)
        start += heads
    return head0, tuple(modes), start


def _layer(layer, x, xb, nb, seq, tables, w_rows, cmp_pe_k, cmp_wk1, cmp_wk2, cmp_pe_v, cmp_wv1, cmp_wv2,
           w_br_a, w_br_b, w_br_c, w_out, ln1_g, ln1_b, w_up, w_down, ln2_g, ln2_b):
    m, d_model = x.shape

    att_mode_cols = 4 * HEAD_DIM
    at, att_modes, att_heads = _field_layout(_ALIGNED_FIELDS, att_mode_cols // HEAD_DIM)
    att = _matmul(xb, w_rows, layer=layer, b_rows=True, name="proj_attention", out_dtype=BF16,
                  tile_modes=att_modes, mode_cols=att_mode_cols, tm=1024, tn=1024, head_major=True,
                  rope_tables=tables)
    tn_tail = 1024
    st, stick_modes, stick_heads = _field_layout(_STICK_FIELDS, tn_tail // HEAD_DIM)
    stick_col0 = att_heads * HEAD_DIM
    gates_col0 = stick_col0 + stick_heads * HEAD_DIM
    nsa_col0 = gates_col0 + 3 * d_model
    stick = _matmul(xb, w_rows, layer=layer, b_rows=True, col0=stick_col0, name="proj_stick",
                    out_dtype=BF16, tile_modes=stick_modes, tm=1024, tn=tn_tail, head_major=True)
    gates = _matmul(xb, w_rows, layer=layer, b_rows=True, col0=gates_col0, name="proj_branch_gates",
                    out_dtype=BF16, tile_modes=('sigmoid',) * (3 * d_model // tn_tail), tm=1024, tn=tn_tail)
    g_nsa = _matmul(xb, w_rows, layer=layer, b_rows=True, col0=nsa_col0, name="proj_nsa_gates",
                    out_dtype=F32, tile_modes=('sigmoid',), tm=1024, tn=HEAD_DIM)

    att = att.reshape(att_heads, nb, seq, HEAD_DIM)
    stick = stick.reshape(stick_heads, nb, seq, HEAD_DIM)

    y_a = _dilated_mixer(att, at['q_a'], at['k_a'], at['v_a'])

    kc = _compress(att, at['kc_b'], cmp_pe_k, cmp_wk1, cmp_wk2, "nsa_compress_k")
    vc = _compress(att, at['vc_b'], cmp_pe_v, cmp_wv1, cmp_wv2, "nsa_compress_v")
    o_cmp, sel = _cmp_select(att, at['q_b'], kc, vc)
    o_sel = _selected(att, at['q_b'], att, at['ks_b'], att, at['vs_b'], sel)
    o_win = _window(att, at['q_b'], att, at['kw_b'], att, at['vw_b'])
    y_b = _nsa_gate(g_nsa, o_cmp.reshape(B_HEADS, m, HEAD_DIM), o_sel.reshape(B_HEADS, m, HEAD_DIM),
                    o_win.reshape(B_HEADS, m, HEAD_DIM))

    y_c = _stick_breaking(stick, st['q_c'], stick, st['k_c'], stick, st['v_c'])

    merged = _merge(y_a.reshape(m, -1), y_b, y_c.reshape(m, -1), w_br_a.astype(BF16),
                    w_br_b.astype(BF16), w_br_c.astype(BF16), gates, d_model)
    mixed = _matmul(merged, w_out, layer=layer, name="out_proj", out_dtype=F32,
                    tile_modes=('none',) * (d_model // 512), tm=1024, tn=512)
    x1, x1b = _residual_ln(x, mixed, ln1_g, ln1_b)

    d_ff = w_up.shape[2]
    hidden = _matmul(x1b, w_up, layer=layer, name="mlp_up", out_dtype=BF16,
                     tile_modes=('relu2',) * (d_ff // 512), tm=1024, tn=512)
    down = _matmul(hidden, w_down, layer=layer, name="mlp_down", out_dtype=F32,
                   tile_modes=('none',) * (d_model // 1024), tm=2048, tn=1024, tk=1024)
    return _residual_ln(x1, down, ln2_g, ln2_b)


def kernel(x, w_in, cmp_pe_k, cmp_wk1, cmp_wk2, cmp_pe_v, cmp_wv1, cmp_wv2, w_br_a, w_br_b, w_br_c,
           w_out, ln1_g, ln1_b, w_up, w_down, ln2_g, ln2_b):
    nb, seq, d_model = x.shape
    tables = _rope_tables(seq)
    xf = x.reshape(nb * seq, d_model)
    xb = xf.astype(BF16)
    aligned_cols = sum(heads for _, heads, _ in _ALIGNED_FIELDS) * HEAD_DIM
    w_rows = _input_weight_rows(w_in, aligned_cols, N_NSA_GATES)
    for l in range(w_in.shape[0]):
        xf, xb = _layer(l, xf, xb, nb, seq, tables, w_rows, cmp_pe_k[l], cmp_wk1[l], cmp_wk2[l],
                        cmp_pe_v[l], cmp_wv1[l], cmp_wv2[l], w_br_a[l], w_br_b[l], w_br_c[l],
                        w_out, ln1_g[l], ln1_b[l], w_up, w_down, ln2_g[l], ln2_b[l])
    return xf.reshape(nb, seq, d_model)
```

```python
import functools
import math

import numpy as np
import jax
import jax.numpy as jnp
from jax import lax
from jax.experimental import pallas as pl
from jax.experimental.pallas import tpu as pltpu

F32 = jnp.float32
BF16 = jnp.bfloat16

HEAD_DIM = 128
ROPE_THETA = 10000.0
LN_EPS = 1e-5
DEPTH = 2

DIL_GROUPS = ((128, 1), (512, 4), (2048, 16))
A_SLOTS = 4
A_HEADS = A_SLOTS * len(DIL_GROUPS)

B_HEADS = 12
B_KV_HEADS = 4
B_GROUP = B_HEADS // B_KV_HEADS
CMP_LEN = 32
CMP_STRIDE = 16
SEL_LEN = 64
SEL_SHIFT = 6
SEL_TOPK = 16
WIN_LEN = 512

C_HEADS = 8

MASKED = -3e38
ALPHA = (2.0 * DEPTH) ** 0.25
Q_SCALE = HEAD_DIM ** -0.5

VMEM_LIMIT = 48 * 1024 * 1024
MXU_COLS = 256
ATT_TQ = 128
ATT_TILES = 16
DIL_ROWS = 128
DIL_ROWS_DENSE = 1024
SEL_TQ = 128
SEL_TILES = 8
SEL_CHUNK = 1024
STICK_TILE = 512
STICK_HEADS = 8
CUM_TILE = 128


def _params(semantics):
    return pltpu.CompilerParams(dimension_semantics=semantics, vmem_limit_bytes=VMEM_LIMIT)


def _dot(a, b):
    return jnp.dot(a, b, preferred_element_type=F32)


def _dot_nt(a, b):
    return lax.dot_general(a, b, (((1,), (1,)), ((), ())), preferred_element_type=F32)


def _split_bf16(x):
    hi = x.astype(BF16)
    lo = (x - hi.astype(F32)).astype(BF16)
    return hi, lo


def _tile_flag(tile_modes, wanted):
    hits = [mode in wanted for mode in tile_modes]
    if all(hits) or not any(hits):
        return hits[0]
    j = pl.program_id(1)
    flag = None
    for c, hit in enumerate(hits):
        if hit:
            flag = (j == c) if flag is None else (flag | (j == c))
    return flag


def _write_columns(acc, o_ref, col0, tile_modes, head_major, cos_ref, sin_ref):
    is_rope = _tile_flag(tile_modes, ('rope', 'rope_scale'))
    is_scaled = _tile_flag(tile_modes, ('rope_scale', 'scale'))
    uniform = tile_modes[0] if len(set(tile_modes)) == 1 else None
    assert uniform is not None or not (set(tile_modes) & {'sigmoid', 'relu2'})

    def epilogue(y):
        if is_rope is not False:
            roped = y * cos_ref[...] + pltpu.roll(y, HEAD_DIM // 2, 1) * sin_ref[...]
            y = roped if is_rope is True else jnp.where(is_rope, roped, y)
        if is_scaled is not False:
            y = y * (Q_SCALE if is_scaled is True else jnp.where(is_scaled, Q_SCALE, 1.0))
        if uniform == 'sigmoid':
            y = jax.nn.sigmoid(y)
        if uniform == 'relu2':
            y = jnp.square(jnp.maximum(y, 0.0))
        return y

    width = acc.shape[1]
    if head_major:
        for c in range(width // HEAD_DIM):
            y = epilogue(acc[:, c * HEAD_DIM:(c + 1) * HEAD_DIM])
            o_ref[col0 // HEAD_DIM + c] = y.astype(o_ref.dtype)
    else:
        o_ref[:, col0:col0 + width] = epilogue(acc).astype(o_ref.dtype)


def _matmul_kernel(*refs, tile_modes, units, head_major, n_k, use_rope, acc_in_out, b_rows):
    refs = list(refs)
    a_ref, b_ref = refs[0], refs[1]
    cos_ref, sin_ref = (refs[2], refs[3]) if use_rope else (None, None)
    o_ref = refs[4] if use_rope else refs[2]
    acc_ref = o_ref if acc_in_out else (refs[-1] if n_k > 1 else None)
    tn = b_ref.shape[0] if b_rows else b_ref.shape[1]

    def product(col0, width):
        if b_rows:
            return _dot_nt(a_ref[...], b_ref[col0:col0 + width, :].astype(BF16))
        return _dot(a_ref[...], b_ref[:, col0:col0 + width].astype(BF16))

    if n_k == 1:
        unit_cols = tn // units
        width = min(MXU_COLS, unit_cols)
        for col0 in range(0, tn, width):
            per_tile = tile_modes[col0 // unit_cols::units]
            _write_columns(product(col0, width), o_ref, col0, per_tile, head_major, cos_ref, sin_ref)
        return
    assert units == 1

    @pl.when(pl.program_id(2) == 0)
    def _():
        acc_ref[...] = jnp.zeros(acc_ref.shape, F32)

    acc_ref[...] += product(0, tn)
    if not acc_in_out:
        pl.when(pl.program_id(2) == n_k - 1)(
            lambda: _write_columns(acc_ref[...], o_ref, 0, tile_modes, head_major, cos_ref, sin_ref))


def _matmul(a, b, *, name, out_dtype, tile_modes, tm, tn, tk=None, layer=None, col0=0,
            head_major=False, rope_tables=None, b_rows=False, mode_cols=None):
    m, kdim = a.shape
    mode_cols = tn if mode_cols is None else mode_cols
    n = mode_cols * len(tile_modes)
    assert tn % mode_cols == 0 and n % tn == 0
    tm = min(tm, m)
    tk = kdim if tk is None else min(tk, kdim)
    assert m % tm == 0 and kdim % tk == 0 and col0 % tn == 0
    n_k = kdim // tk
    col_tile0 = col0 // tn
    use_rope = any(mode.startswith('rope') for mode in tile_modes)
    acc_in_out = n_k > 1 and out_dtype == F32 and set(tile_modes) == {'none'} and not head_major
    if b_rows:
        assert layer is not None
        b_spec = pl.BlockSpec((None, tn, tk), lambda i, j, k: (layer, col_tile0 + j, k))
    elif layer is None:
        b_spec = pl.BlockSpec((tk, tn), lambda i, j, k: (k, col_tile0 + j))
    else:
        b_spec = pl.BlockSpec((None, tk, tn), lambda i, j, k: (layer, k, col_tile0 + j))
    in_specs = [pl.BlockSpec((tm, tk), lambda i, j, k: (i, k)), b_spec]
    operands = [a, b]
    if use_rope:
        assert head_major
        cos, sin = rope_tables
        seq_tiles = cos.shape[0] // tm
        assert cos.shape[0] % tm == 0
        spec = pl.BlockSpec((tm, HEAD_DIM), lambda i, j, k: (i % seq_tiles, 0))
        in_specs += [spec, spec]
        operands += [cos, sin]
    if head_major:
        out_shape = jax.ShapeDtypeStruct((n // HEAD_DIM, m, HEAD_DIM), out_dtype)
        out_spec = pl.BlockSpec((tn // HEAD_DIM, tm, HEAD_DIM), lambda i, j, k: (j, i, 0))
    else:
        out_shape = jax.ShapeDtypeStruct((m, n), out_dtype)
        out_spec = pl.BlockSpec((tm, tn), lambda i, j, k: (i, j))
    scratch = [pltpu.VMEM((tm, tn), F32)] if (n_k > 1 and not acc_in_out) else []
    return pl.pallas_call(
        functools.partial(_matmul_kernel, tile_modes=tuple(tile_modes), units=tn // mode_cols,
                          head_major=head_major, n_k=n_k, use_rope=use_rope, acc_in_out=acc_in_out,
                          b_rows=b_rows),
        grid=(m // tm, n // tn, n_k),
        in_specs=in_specs,
        out_specs=out_spec,
        out_shape=out_shape,
        scratch_shapes=scratch,
        compiler_params=_params(("parallel", "parallel", "arbitrary")),
        name=name,
    )(*operands)


def _gather_rows_kernel(w_ref, o_ref, *, tn, k_chunks, n_layers):
    stride = k_chunks * n_layers
    by_chunk = pltpu.einshape("nck->cnk", w_ref[...].reshape(tn, stride, HEAD_DIM))
    for layer in range(n_layers):
        pieces = [by_chunk[c * n_layers + layer] for c in range(k_chunks)]
        o_ref[layer] = jnp.concatenate(pieces, axis=1).astype(o_ref.dtype)


def _input_weight_rows(w_in, aligned_cols, n_gates, tn=HEAD_DIM):
    n_layers, kdim, n = w_in.shape
    k_chunks = kdim // HEAD_DIM
    tail = n - aligned_cols - n_gates
    assert aligned_cols % tn == 0 and tail % tn == 0 and kdim % HEAD_DIM == 0
    aligned_tiles, tail_tiles = aligned_cols // tn, tail // tn
    rows_per_col = k_chunks * n_layers
    flat = w_in.reshape(n_layers, k_chunks, HEAD_DIM, n).transpose(3, 1, 0, 2).reshape(n * rows_per_col, HEAD_DIM)

    def source_row(j):
        col = jnp.where(j < aligned_tiles, tn * j,
                        jnp.where(j < aligned_tiles + tail_tiles,
                                  aligned_cols + n_gates + tn * (j - aligned_tiles), aligned_cols))
        return (col * rows_per_col, 0)

    tiles = aligned_tiles + tail_tiles + 1
    return pl.pallas_call(
        functools.partial(_gather_rows_kernel, tn=tn, k_chunks=k_chunks, n_layers=n_layers),
        grid=(tiles,),
        in_specs=[pl.BlockSpec((pl.Element(tn * rows_per_col), pl.Element(HEAD_DIM)), source_row)],
        out_specs=pl.BlockSpec((n_layers, tn, kdim), lambda j: (0, j, 0)),
        out_shape=jax.ShapeDtypeStruct((n_layers, tiles * tn, kdim), BF16),
        compiler_params=_params(("parallel",)),
        name="input_weight_rows",
    )(flat)


def _window_start(t0, back, span, seq, align):
    start = jnp.minimum(jnp.maximum(t0 - back, 0), seq - span)
    return pl.multiple_of(start, align)


def _by_residue(x, dil):
    rows = x.shape[0] // dil
    return pltpu.einshape("ldk->dlk", x.reshape(rows, dil, HEAD_DIM))


def _dilated_group_kernel(*refs, dil, rows, band, n_rows, n_others):
    q_ref, k_ref, v_ref = refs[:3]
    others = refs[3:3 + 2 * n_others]
    if n_others:
        y_ref = refs[3 + 2 * n_others]
        kr_ref, vr_ref, o_ref, lse_ref = refs[4 + 2 * n_others:]
    else:
        o_ref, lse_ref = refs[3:5]
        kr_ref, vr_ref = refs[5:] if dil > 1 else (None, None)
    j = pl.program_id(2)
    if dil > 1:

        @pl.when(j == 0)
        def _():
            kr_ref[...] = _by_residue(k_ref[...], dil)
            vr_ref[...] = _by_residue(v_ref[...], dil)

        q_by = _by_residue(q_ref[...], dil)
    sub = min(band, rows) if dil == 1 else rows
    span = min(band + sub, n_rows)
    for t in range(rows // sub):
        l0 = j * rows + t * sub
        start = _window_start(l0, band, span, n_rows, math.gcd(sub, band))
        dist = (l0 + lax.broadcasted_iota(jnp.int32, (sub, 1), 0)) - (start + lax.broadcasted_iota(jnp.int32, (1, span), 1))
        keep = (dist >= 0) & (dist <= band)
        for r in range(dil):
            if dil == 1:
                q = q_ref[t * sub:(t + 1) * sub, :]
                keys, vals = k_ref[pl.ds(start, span), :], v_ref[pl.ds(start, span), :]
            else:
                q, keys, vals = q_by[r], kr_ref[r, pl.ds(start, span), :], vr_ref[r, pl.ds(start, span), :]
            s = jnp.where(keep, _dot_nt(q, keys), -jnp.inf)
            m = jnp.max(s, axis=1, keepdims=True)
            p = jnp.exp(s - m)
            l = jnp.sum(p, axis=1, keepdims=True)
            o = _dot(p.astype(BF16), vals) / l
            lse = jnp.broadcast_to(m + jnp.log(l), (sub, HEAD_DIM))
            if dil == 1:
                o_ref[t * sub:(t + 1) * sub, :] = o
                lse_ref[t * sub:(t + 1) * sub, :] = lse
            else:
                o_ref[pl.ds(r, rows, stride=dil), :] = o
                lse_ref[pl.ds(r, rows, stride=dil), :] = lse
    if n_others:
        outs = [others[2 * g][...] for g in range(n_others)] + [o_ref[...]]
        lses = [others[2 * g + 1][...] for g in range(n_others)] + [lse_ref[...]]
        top = functools.reduce(jnp.maximum, lses)
        ws = [jnp.exp(lse - top) for lse in lses]
        total = functools.reduce(lambda a, b: a + b, ws)
        mixed = functools.reduce(lambda a, b: a + b, [w * o for w, o in zip(ws, outs)])
        y_ref[...] = (mixed / total).astype(y_ref.dtype)


def _head_spec(rows, head0):
    return pl.BlockSpec((None, None, rows, HEAD_DIM), lambda b, h, n: (head0 + h, b, n, 0))


def _seq_spec(seq, head0):
    return pl.BlockSpec((None, None, seq, HEAD_DIM), lambda b, h, n: (head0 + h, b, 0, 0))


def _dilated_group(att, q_head0, k_head0, v_head0, window, dil, others=()):
    _, nb, seq, _ = att.shape
    n_rows = seq // dil
    band = window // dil
    rows = min(DIL_ROWS if dil > 1 else DIL_ROWS_DENSE, n_rows)
    assert seq % dil == 0 and n_rows % rows == 0 and (dil > 1 or not others)
    pair_spec = _head_spec(rows * dil, 0)
    pair_shape = jax.ShapeDtypeStruct((A_SLOTS, nb, seq, HEAD_DIM), F32)
    scratch = [pltpu.VMEM((dil, n_rows, HEAD_DIM), BF16)] * 2 if dil > 1 else []
    if others:
        out_specs = pl.BlockSpec((None, rows * dil, HEAD_DIM), lambda b, s, j: (b, j, s))
        out_shape = jax.ShapeDtypeStruct((nb, seq, A_SLOTS * HEAD_DIM), BF16)
        scratch = scratch + [pltpu.VMEM((rows * dil, HEAD_DIM), F32)] * 2
    else:
        out_specs, out_shape = [pair_spec, pair_spec], [pair_shape, pair_shape]
    return pl.pallas_call(
        functools.partial(_dilated_group_kernel, dil=dil, rows=rows, band=band, n_rows=n_rows,
                          n_others=len(others) // 2),
        grid=(nb, A_SLOTS, n_rows // rows),
        in_specs=[_head_spec(rows * dil, q_head0), _seq_spec(seq, k_head0), _seq_spec(seq, v_head0)]
        + [pair_spec] * len(others),
        out_specs=out_specs,
        out_shape=out_shape,
        scratch_shapes=scratch,
        compiler_params=_params(("parallel", "parallel", "arbitrary")),
        name="dilated_group_%d" % dil,
    )(att, att, att, *others)


def _dilated_mixer(att, q_head0, k_head0, v_head0):
    pairs = []
    for g, (window, dil) in enumerate(DIL_GROUPS):
        last = g == len(DIL_GROUPS) - 1
        result = _dilated_group(att, q_head0 + g * A_SLOTS, k_head0, v_head0, window, dil,
                                others=tuple(pairs) if last else ())
        if last:
            return result
        pairs += list(result)


def _gelu_tanh(x):
    return 0.5 * x * (1.0 + jnp.tanh(math.sqrt(2.0 / math.pi) * (x + 0.044715 * (x * x * x))))


def _compress_kernel(x_ref, pe_ref, w1_ref, w2_ref, o_ref):
    by_token = _by_residue(x_ref[...], CMP_STRIDE)
    x = jnp.concatenate([by_token[r] for r in range(CMP_STRIDE)], axis=1).astype(F32)
    first = _dot((x + pe_ref[0:1, :]).astype(BF16), w1_ref[0])
    second = _dot((x + pe_ref[1:2, :]).astype(BF16), w1_ref[1])
    chunks = x.shape[0]
    hidden = first + pltpu.roll(second, chunks - 1, 0)
    o_ref[...] = _dot(_gelu_tanh(hidden).astype(BF16), w2_ref[...]).astype(o_ref.dtype)


def _compress(t, head0, pe, w1, w2, name):
    _, nb, seq, _ = t.shape
    nh = B_KV_HEADS
    chunks = seq // CMP_STRIDE
    width = CMP_STRIDE * HEAD_DIM
    ratio = CMP_LEN // CMP_STRIDE
    return pl.pallas_call(
        _compress_kernel,
        grid=(nh, nb),
        in_specs=[pl.BlockSpec((None, None, seq, HEAD_DIM), lambda h, b: (head0 + h, b, 0, 0)),
                  pl.BlockSpec((ratio, width), lambda h, b: (0, 0)),
                  pl.BlockSpec((ratio, width, HEAD_DIM), lambda h, b: (0, 0, 0)),
                  pl.BlockSpec((HEAD_DIM, HEAD_DIM), lambda h, b: (0, 0))],
        out_specs=pl.BlockSpec((None, None, chunks, HEAD_DIM), lambda h, b: (h, b, 0, 0)),
        out_shape=jax.ShapeDtypeStruct((nh, nb, chunks, HEAD_DIM), BF16),
        compiler_params=_params(("parallel", "parallel")),
        name=name,
    )(t, pe.reshape(ratio, width), w1.reshape(ratio, width, HEAD_DIM).astype(BF16), w2.astype(BF16))


def _group_q(q_refs, row0=0, rows=None):
    rows = q_refs[0].shape[0] if rows is None else rows
    return jnp.concatenate([r[row0:row0 + rows, :] for r in q_refs], axis=0)


def _group_q_specs(tq, head0):
    return [pl.BlockSpec((None, None, tq, HEAD_DIM),
                         functools.partial(lambda b, h, n, g: (head0 + h * B_GROUP + g, b, n, 0), g=g))
            for g in range(B_GROUP)]


def _cmp_select_kernel(q0_ref, q1_ref, q2_ref, kc_ref, vc_ref, ov_ref, o_ref, sel_ref, *, tq, tiles):
    for u in range(tiles):
        _cmp_select_tile((q0_ref, q1_ref, q2_ref), kc_ref, vc_ref, ov_ref, o_ref, sel_ref,
                         (pl.program_id(2) * tiles + u) * tq, u * tq, tq)


def _cmp_select_tile(q_refs, kc_ref, vc_ref, ov_ref, o_ref, sel_ref, t0, row0, tq):
    n_cmp = kc_ref.shape[0]
    rows = B_GROUP * tq
    s = _dot_nt(_group_q(q_refs, row0, tq), kc_ref[...])
    tpos3 = t0 + (lax.broadcasted_iota(jnp.int32, (rows, 1), 0) & (tq - 1))
    c_end = lax.broadcasted_iota(jnp.int32, (1, n_cmp), 1) * CMP_STRIDE + (CMP_LEN - 1)
    s = jnp.where(c_end - tpos3 <= 0, s, -jnp.inf)
    m = jnp.max(s, axis=1, keepdims=True)
    m = jnp.where(jnp.abs(m) < jnp.inf, m, 0.0)
    e = jnp.exp(s - m)
    den = jnp.sum(e, axis=1, keepdims=True)
    p = e / jnp.where(den > 0, den, 1.0)
    o = _dot(p.astype(BF16), vc_ref[...]).reshape(B_GROUP, tq, HEAD_DIM)
    o_ref[:, row0:row0 + tq, :] = o.astype(o_ref.dtype)

    p_sum = p[0:tq] + p[tq:2 * tq] + p[2 * tq:3 * tq]
    hi, lo = _split_bf16(p_sum)
    imp = _dot(hi, ov_ref[...]) + _dot(lo, ov_ref[...])
    tpos = t0 + lax.broadcasted_iota(jnp.int32, (tq, 1), 0)
    rel = lax.broadcasted_iota(jnp.int32, (1, HEAD_DIM), 1) - (tpos >> SEL_SHIFT)
    j_abs = jnp.broadcast_to(lax.broadcasted_iota(jnp.int32, (1, HEAD_DIM), 1), rel.shape)
    forced = (j_abs == 0) | (rel == 0) | (rel == -1)
    imp = jnp.where(forced, jnp.inf, jnp.where(rel <= 0, imp, -jnp.inf))

    n_sel = HEAD_DIM // 2
    imp_t = imp.T
    mine = imp_t[0:n_sel]
    j_idx = lax.broadcasted_iota(jnp.int32, (n_sel, tq), 0)
    beaten = jnp.zeros((n_sel, tq), F32)
    for kk in range(n_sel):
        other = imp_t[kk:kk + 1, :]
        wins = (other > mine) | ((other == mine) & (j_idx > kk))
        beaten = beaten + jnp.where(wins, 1.0, 0.0)
    chosen = jnp.where(beaten < SEL_TOPK, 0.0, MASKED)
    chosen = jnp.concatenate([chosen, jnp.full((HEAD_DIM - n_sel, tq), MASKED, F32)], axis=0)
    sel_ref[:, row0:row0 + tq] = chosen.astype(sel_ref.dtype)


def _overlap_matrix(n_cmp_rows, n_sel):
    c_start = np.arange(n_cmp_rows) * CMP_STRIDE
    c_end = c_start + CMP_LEN - 1
    s_start = np.arange(HEAD_DIM) * SEL_LEN
    ov = (c_start[:, None] <= s_start[None, :] + SEL_LEN - 1) & (c_end[:, None] >= s_start[None, :])
    ov &= (np.arange(HEAD_DIM) < n_sel)[None, :]
    return jnp.asarray(ov.astype(np.float32), dtype=BF16)


def _group_o_spec(tq):
    return pl.BlockSpec((B_GROUP, None, tq, HEAD_DIM), lambda b, h, n: (h, b, n, 0))


def _sel_spec(tq):
    return pl.BlockSpec((None, None, HEAD_DIM, tq), lambda b, h, n: (h, b, 0, n))


def _cmp_select(q, q_head0, kc, vc):
    _, nb, seq, _ = q.shape
    assert seq // SEL_LEN <= HEAD_DIM // 2
    tq = min(ATT_TQ, seq)
    tiles = min(ATT_TILES, seq // tq)
    n_cmp = kc.shape[2]
    kv_spec = pl.BlockSpec((None, None, n_cmp, HEAD_DIM), lambda b, h, n: (h, b, 0, 0))
    return pl.pallas_call(
        functools.partial(_cmp_select_kernel, tq=tq, tiles=tiles),
        grid=(nb, B_KV_HEADS, seq // (tq * tiles)),
        in_specs=_group_q_specs(tq * tiles, q_head0) + [kv_spec, kv_spec,
                                                        pl.BlockSpec((n_cmp, HEAD_DIM), lambda b, h, n: (0, 0))],
        out_specs=[_group_o_spec(tq * tiles), _sel_spec(tq * tiles)],
        out_shape=[jax.ShapeDtypeStruct((B_HEADS, nb, seq, HEAD_DIM), BF16),
                   jax.ShapeDtypeStruct((B_KV_HEADS, nb, HEAD_DIM, seq), BF16)],
        compiler_params=_params(("parallel", "parallel", "arbitrary")),
        name="nsa_compressed_select",
    )(q, q, q, kc, vc, _overlap_matrix(n_cmp, seq // SEL_LEN))


def _selected_kernel(q0_ref, q1_ref, q2_ref, k_ref, v_ref, sel_ref, o_ref, vt_ref, acc_ref, *, tq, kc, tiles):
    n = pl.program_id(2)
    t_first = n * (tq * tiles)
    cols = B_GROUP * tq

    @pl.when(n == 0)
    def _():
        for c in range(v_ref.shape[0] // kc):
            vt_ref[c] = v_ref[c * kc:(c + 1) * kc, :].astype(F32).T.astype(BF16)

    qs = [_group_q((q0_ref, q1_ref, q2_ref), u * tq, tq) for u in range(tiles)]
    qposs = [t_first + u * tq + lax.broadcasted_iota(jnp.int32, (1, tq), 1) for u in range(tiles)]
    key_in_chunk = lax.broadcasted_iota(jnp.int32, (kc, 1), 0)
    blocks = kc // SEL_LEN
    acc_ref[...] = jnp.zeros(acc_ref.shape, F32)

    def chunk(i, carry, diagonal):
        k0 = pl.multiple_of(i * kc, kc)
        kpos = k0 + key_in_chunk
        keys = k_ref[pl.ds(k0, kc), :]
        blk0 = pl.multiple_of(i * blocks, blocks)
        out = []
        for u in range(tiles):
            m_old, l_old = carry[u]
            n_keys = tq * (u + 1) if (diagonal and tq * tiles == kc) else kc
            rows = sel_ref[pl.ds(blk0, blocks), u * tq:(u + 1) * tq].astype(F32)
            bias = jnp.broadcast_to(rows[:, None, :], (blocks, SEL_LEN, tq)).reshape(kc, tq)[:n_keys]
            if diagonal:
                bias = jnp.where(kpos[:n_keys] - qposs[u] <= 0, bias, MASKED)
            s = _dot_nt(keys[:n_keys], qs[u]) + jnp.concatenate([bias] * B_GROUP, axis=1)
            m_new = jnp.maximum(m_old, jnp.max(s, axis=0, keepdims=True))
            p = jnp.exp(s - m_new)
            alpha = jnp.exp(m_old - m_new)
            acc_ref[u] = alpha * acc_ref[u] + _dot(vt_ref[i, :, :n_keys], p.astype(BF16))
            out.append((m_new, alpha * l_old + jnp.sum(p, axis=0, keepdims=True)))
        return tuple(out)

    init = tuple((jnp.full((1, cols), MASKED, F32), jnp.zeros((1, cols), F32)) for _ in range(tiles))
    last = t_first // kc
    carry = lax.fori_loop(0, last, lambda i, c: chunk(i, c, False), init)
    final = chunk(last, carry, True)
    for u in range(tiles):
        o_t = acc_ref[u] / final[u][1]
        for g in range(B_GROUP):
            o_ref[g, u * tq:(u + 1) * tq, :] = o_t[:, g * tq:(g + 1) * tq].T.astype(o_ref.dtype)


def _selected(q, q_head0, k, k_head0, v, v_head0, sel):
    _, nb, seq, _ = q.shape
    tq = min(SEL_TQ, seq)
    kc = min(SEL_CHUNK, seq)
    tiles = min(SEL_TILES, kc // tq)
    assert kc % (tq * tiles) == 0 and seq % kc == 0 and tq == HEAD_DIM
    return pl.pallas_call(
        functools.partial(_selected_kernel, tq=tq, kc=kc, tiles=tiles),
        grid=(nb, B_KV_HEADS, seq // (tq * tiles)),
        in_specs=_group_q_specs(tq * tiles, q_head0) + [_seq_spec(seq, k_head0), _seq_spec(seq, v_head0),
                                                        _sel_spec(tq * tiles)],
        out_specs=_group_o_spec(tq * tiles),
        out_shape=jax.ShapeDtypeStruct((B_HEADS, nb, seq, HEAD_DIM), BF16),
        scratch_shapes=[pltpu.VMEM((seq // kc, HEAD_DIM, kc), BF16),
                        pltpu.VMEM((tiles, HEAD_DIM, B_GROUP * tq), F32)],
        compiler_params=_params(("parallel", "parallel", "arbitrary")),
        name="nsa_selected",
    )(q, q, q, k, v, sel)


def _window_kernel(q0_ref, q1_ref, q2_ref, k_ref, v_ref, o_ref, *, tq, tiles, seq):
    rows = B_GROUP * tq
    span = min(WIN_LEN + tq, seq)
    for u in range(tiles):
        t0 = (pl.program_id(2) * tiles + u) * tq
        start = _window_start(t0, WIN_LEN, span, seq, tq)
        s = _dot_nt(_group_q((q0_ref, q1_ref, q2_ref), u * tq, tq), k_ref[pl.ds(start, span), :])
        qpos = t0 + (lax.broadcasted_iota(jnp.int32, (rows, 1), 0) & (tq - 1))
        kpos = start + lax.broadcasted_iota(jnp.int32, (1, span), 1)
        dist = qpos - kpos
        s = jnp.where((dist >= 0) & (dist <= WIN_LEN - 1), s, -jnp.inf)
        m = jnp.max(s, axis=1, keepdims=True)
        p = jnp.exp(s - m)
        l = jnp.sum(p, axis=1, keepdims=True)
        o = _dot(p.astype(BF16), v_ref[pl.ds(start, span), :]) / l
        o_ref[:, u * tq:(u + 1) * tq, :] = o.reshape(B_GROUP, tq, HEAD_DIM).astype(o_ref.dtype)


def _window(q, q_head0, k, k_head0, v, v_head0):
    _, nb, seq, _ = q.shape
    tq = min(ATT_TQ, seq)
    tiles = min(ATT_TILES, seq // tq)
    assert WIN_LEN % tq == 0
    return pl.pallas_call(
        functools.partial(_window_kernel, tq=tq, tiles=tiles, seq=seq),
        grid=(nb, B_KV_HEADS, seq // (tq * tiles)),
        in_specs=_group_q_specs(tq * tiles, q_head0) + [_seq_spec(seq, k_head0), _seq_spec(seq, v_head0)],
        out_specs=_group_o_spec(tq * tiles),
        out_shape=jax.ShapeDtypeStruct((B_HEADS, nb, seq, HEAD_DIM), BF16),
        compiler_params=_params(("parallel", "parallel", "arbitrary")),
        name="nsa_window",
    )(q, q, q, k, v)


def _nsa_gate_kernel(g_ref, oc_ref, os_ref, ow_ref, y_ref):
    g = g_ref[...]
    for h in range(B_HEADS):
        y = (g[:, 3 * h:3 * h + 1] * oc_ref[h].astype(F32)
             + g[:, 3 * h + 1:3 * h + 2] * os_ref[h].astype(F32)
             + g[:, 3 * h + 2:3 * h + 3] * ow_ref[h].astype(F32))
        y_ref[:, h * HEAD_DIM:(h + 1) * HEAD_DIM] = y.astype(y_ref.dtype)


def _nsa_gate(gates, o_cmp, o_sel, o_win):
    rows = gates.shape[0]
    tm = min(256, rows)
    o_spec = pl.BlockSpec((B_HEADS, tm, HEAD_DIM), lambda i: (0, i, 0))
    return pl.pallas_call(
        _nsa_gate_kernel,
        grid=(rows // tm,),
        in_specs=[pl.BlockSpec((tm, HEAD_DIM), lambda i: (i, 0)), o_spec, o_spec, o_spec],
        out_specs=pl.BlockSpec((tm, B_HEADS * HEAD_DIM), lambda i: (i, 0)),
        out_shape=jax.ShapeDtypeStruct((rows, B_HEADS * HEAD_DIM), BF16),
        compiler_params=_params(("parallel",)),
        name="nsa_gate",
    )(gates, o_cmp, o_sel, o_win)


def _stick_kernel(*refs, tq, heads):
    q_refs, k_refs, v_refs = refs[:heads], refs[heads:2 * heads], refs[2 * heads:3 * heads]
    o_ref, acc_ref, car_ref = refs[3 * heads:]
    n = pl.program_id(2)
    sub = min(CUM_TILE, tq)
    n_sub = tq // sub
    r_idx = lax.broadcasted_iota(jnp.int32, (2 * sub, 2 * sub), 0) & (sub - 1)
    c_idx = lax.broadcasted_iota(jnp.int32, (2 * sub, 2 * sub), 1)
    suffix_and_total = jnp.where((c_idx >= sub) | (r_idx > c_idx), 1.0, 0.0).astype(BF16)
    k_in_sub = lax.broadcasted_iota(jnp.int32, (1, sub), 1)
    acc_ref[...] = jnp.zeros(acc_ref.shape, F32)
    car_ref[...] = jnp.zeros(car_ref.shape, F32)

    q_in_tile = lax.broadcasted_iota(jnp.int32, (tq, 1), 0)

    def tile(k0, diagonal):
        for h in range(heads):
            z = _dot_nt(q_refs[h][...], k_refs[h][pl.ds(k0, tq), :])
            soft = jnp.log(1.0 + jnp.exp(-jnp.abs(z)))
            log_beta = jnp.minimum(z, 0.0) - soft
            log_rest = log_beta - z
            carried = car_ref[h]
            parts = [None] * n_sub
            for u in reversed(range(n_sub)):
                cols = slice(u * sub, (u + 1) * sub)
                rest_u = log_rest[:, cols]
                if diagonal:
                    before = (u * sub + k_in_sub) - q_in_tile < 0
                    rest_u = jnp.where(before, rest_u, 0.0)
                sums = _dot(jnp.concatenate(_split_bf16(rest_u), axis=1), suffix_and_total)
                a = jnp.exp(log_beta[:, cols] + (sums[:, :sub] + carried))
                if diagonal:
                    a = jnp.where(before, a, 0.0)
                carried = carried + sums[:, sub:]
                parts[u] = a.astype(BF16)
            car_ref[h] = carried
            acc_ref[h] += _dot(jnp.concatenate(parts, axis=1), v_refs[h][pl.ds(k0, tq), :])

    tile(pl.multiple_of(n * tq, tq), True)

    def step(i, carry):
        tile(pl.multiple_of((n - 1 - i) * tq, tq), False)
        return carry

    lax.fori_loop(0, n, step, 0)
    for h in range(heads):
        o_ref[:, h * HEAD_DIM:(h + 1) * HEAD_DIM] = acc_ref[h].astype(o_ref.dtype)


def _stick_breaking(q, q_head0, k, k_head0, v, v_head0):
    _, nb, seq, _ = q.shape
    tq = min(STICK_TILE, seq)
    heads = STICK_HEADS
    sub = min(CUM_TILE, tq)

    def per_head(make, head0):
        return [make(head0 + h) for h in range(heads)]

    def q_spec(head):
        return pl.BlockSpec((None, None, tq, HEAD_DIM), lambda b, g, n: (head + g * heads, b, n, 0))

    def kv_spec(head):
        return pl.BlockSpec((None, None, seq, HEAD_DIM), lambda b, g, n: (head + g * heads, b, 0, 0))

    return pl.pallas_call(
        functools.partial(_stick_kernel, tq=tq, heads=heads),
        grid=(nb, C_HEADS // heads, seq // tq),
        in_specs=per_head(q_spec, q_head0) + per_head(kv_spec, k_head0) + per_head(kv_spec, v_head0),
        out_specs=pl.BlockSpec((None, tq, heads * HEAD_DIM), lambda b, g, n: (b, n, g)),
        out_shape=jax.ShapeDtypeStruct((nb, seq, C_HEADS * HEAD_DIM), BF16),
        scratch_shapes=[pltpu.VMEM((heads, tq, HEAD_DIM), F32), pltpu.VMEM((heads, tq, sub), F32)],
        compiler_params=_params(("parallel", "parallel", "arbitrary")),
        name="stick_breaking",
    )(*([q] * heads + [k] * heads + [v] * heads))


def _merge_kernel(ya_ref, yb_ref, yc_ref, wa_ref, wb_ref, wc_ref, ga_ref, gb_ref, gc_ref, o_ref):
    mixed = (ga_ref[...].astype(F32) * _dot(ya_ref[...], wa_ref[...])
             + gb_ref[...].astype(F32) * _dot(yb_ref[...], wb_ref[...])
             + gc_ref[...].astype(F32) * _dot(yc_ref[...], wc_ref[...]))
    o_ref[...] = mixed.astype(o_ref.dtype)


def _merge(ya, yb, yc, wa, wb, wc, gates, d_model):
    m = ya.shape[0]
    tm, tn = min(512, m), 1024
    col_tiles = d_model // tn

    def y_spec(y):
        return pl.BlockSpec((tm, y.shape[1]), lambda i, j: (i, 0))

    def w_spec(w):
        return pl.BlockSpec((w.shape[0], tn), lambda i, j: (0, j))

    def g_spec(branch):
        return pl.BlockSpec((tm, tn), lambda i, j: (i, branch * col_tiles + j))

    return pl.pallas_call(
        _merge_kernel,
        grid=(m // tm, col_tiles),
        in_specs=[y_spec(ya), y_spec(yb), y_spec(yc), w_spec(wa), w_spec(wb), w_spec(wc),
                  g_spec(0), g_spec(1), g_spec(2)],
        out_specs=pl.BlockSpec((tm, tn), lambda i, j: (i, j)),
        out_shape=jax.ShapeDtypeStruct((m, d_model), BF16),
        compiler_params=_params(("parallel", "arbitrary")),
        name="branch_merge",
    )(ya, yb, yc, wa, wb, wc, gates, gates, gates)


def _ln_kernel(x_ref, y_ref, g_ref, b_ref, o_ref, ob_ref):
    z = ALPHA * x_ref[...] + y_ref[...]
    mu = jnp.mean(z, axis=1, keepdims=True)
    zc = z - mu
    var = jnp.mean(zc * zc, axis=1, keepdims=True)
    out = zc * lax.rsqrt(var + LN_EPS) * g_ref[...] + b_ref[...]
    o_ref[...] = out
    ob_ref[...] = out.astype(BF16)


def _residual_ln(x, y, g, b):
    m, d = x.shape
    tm = min(256, m)
    row = pl.BlockSpec((tm, d), lambda i: (i, 0))
    vec = pl.BlockSpec((1, d), lambda i: (0, 0))
    return pl.pallas_call(
        _ln_kernel,
        grid=(m // tm,),
        in_specs=[row, row, vec, vec],
        out_specs=[row, row],
        out_shape=[jax.ShapeDtypeStruct((m, d), F32), jax.ShapeDtypeStruct((m, d), BF16)],
        compiler_params=_params(("parallel",)),
        name="residual_layer_norm",
    )(x, y, g.reshape(1, d), b.reshape(1, d))


def _rope_tables(seq):
    half = HEAD_DIM // 2
    inv = 1.0 / (ROPE_THETA ** (jnp.arange(half, dtype=F32) / half))
    ang = jnp.arange(seq).astype(F32)[:, None] * inv[None, :]
    cos, sin = jnp.cos(ang), jnp.sin(ang)
    return jnp.concatenate([cos, cos], axis=1), jnp.concatenate([-sin, sin], axis=1)


_A_KV, _B_KV = A_SLOTS, B_KV_HEADS
_ALIGNED_FIELDS = (('q_a', A_HEADS, 'rope_scale'), ('k_a', _A_KV, 'rope'), ('v_a', _A_KV, 'none'),
                   ('q_b', B_HEADS, 'rope_scale'), ('kc_b', _B_KV, 'rope'), ('vc_b', _B_KV, 'none'),
                   ('ks_b', _B_KV, 'rope'), ('vs_b', _B_KV, 'none'), ('kw_b', _B_KV, 'rope'),
                   ('vw_b', _B_KV, 'none'))
_STICK_FIELDS = (('q_c', C_HEADS, 'scale'), ('k_c', C_HEADS, 'none'), ('v_c', C_HEADS, 'none'))
N_NSA_GATES = 3 * B_HEADS


def _field_layout(fields, heads_per_tile):
    head0, modes, start = {}, [], 0
    for name, heads, mode in fields:
        assert heads % heads_per_tile == 0
        head0[name] = start
        modes += [mode] * (heads // heads_per_tile)
        start += heads
    return head0, tuple(modes), start


def _layer(layer, x, xb, nb, seq, tables, w_rows, cmp_pe_k, cmp_wk1, cmp_wk2, cmp_pe_v, cmp_wv1, cmp_wv2,
           w_br_a, w_br_b, w_br_c, w_out, ln1_g, ln1_b, w_up, w_down, ln2_g, ln2_b):
    m, d_model = x.shape

    att_mode_cols = 4 * HEAD_DIM
    at, att_modes, att_heads = _field_layout(_ALIGNED_FIELDS, att_mode_cols // HEAD_DIM)
    att = _matmul(xb, w_rows, layer=layer, b_rows=True, name="proj_attention", out_dtype=BF16,
                  tile_modes=att_modes, mode_cols=att_mode_cols, tm=1024, tn=1024, head_major=True,
                  rope_tables=tables)
    tn_tail = 1024
    st, stick_modes, stick_heads = _field_layout(_STICK_FIELDS, tn_tail // HEAD_DIM)
    stick_col0 = att_heads * HEAD_DIM
    gates_col0 = stick_col0 + stick_heads * HEAD_DIM
    nsa_col0 = gates_col0 + 3 * d_model
    stick = _matmul(xb, w_rows, layer=layer, b_rows=True, col0=stick_col0, name="proj_stick",
                    out_dtype=BF16, tile_modes=stick_modes, tm=1024, tn=tn_tail, head_major=True)
    gates = _matmul(xb, w_rows, layer=layer, b_rows=True, col0=gates_col0, name="proj_branch_gates",
                    out_dtype=BF16, tile_modes=('sigmoid',) * (3 * d_model // tn_tail), tm=1024, tn=tn_tail)
    g_nsa = _matmul(xb, w_rows, layer=layer, b_rows=True, col0=nsa_col0, name="proj_nsa_gates",
                    out_dtype=F32, tile_modes=('sigmoid',), tm=1024, tn=HEAD_DIM)

    att = att.reshape(att_heads, nb, seq, HEAD_DIM)
    stick = stick.reshape(stick_heads, nb, seq, HEAD_DIM)

    y_a = _dilated_mixer(att, at['q_a'], at['k_a'], at['v_a'])

    kc = _compress(att, at['kc_b'], cmp_pe_k, cmp_wk1, cmp_wk2, "nsa_compress_k")
    vc = _compress(att, at['vc_b'], cmp_pe_v, cmp_wv1, cmp_wv2, "nsa_compress_v")
    o_cmp, sel = _cmp_select(att, at['q_b'], kc, vc)
    o_sel = _selected(att, at['q_b'], att, at['ks_b'], att, at['vs_b'], sel)
    o_win = _window(att, at['q_b'], att, at['kw_b'], att, at['vw_b'])
    y_b = _nsa_gate(g_nsa, o_cmp.reshape(B_HEADS, m, HEAD_DIM), o_sel.reshape(B_HEADS, m, HEAD_DIM),
                    o_win.reshape(B_HEADS, m, HEAD_DIM))

    y_c = _stick_breaking(stick, st['q_c'], stick, st['k_c'], stick, st['v_c'])

    merged = _merge(y_a.reshape(m, -1), y_b, y_c.reshape(m, -1), w_br_a.astype(BF16),
                    w_br_b.astype(BF16), w_br_c.astype(BF16), gates, d_model)
    mixed = _matmul(merged, w_out, layer=layer, name="out_proj", out_dtype=F32,
                    tile_modes=('none',) * (d_model // 512), tm=1024, tn=512)
    x1, x1b = _residual_ln(x, mixed, ln1_g, ln1_b)

    d_ff = w_up.shape[2]
    hidden = _matmul(x1b, w_up, layer=layer, name="mlp_up", out_dtype=BF16,
                     tile_modes=('relu2',) * (d_ff // 512), tm=1024, tn=512)
    down = _matmul(hidden, w_down, layer=layer, name="mlp_down", out_dtype=F32,
                   tile_modes=('none',) * (d_model // 1024), tm=2048, tn=1024, tk=1024)
    return _residual_ln(x1, down, ln2_g, ln2_b)


def kernel(x, w_in, cmp_pe_k, cmp_wk1, cmp_wk2, cmp_pe_v, cmp_wv1, cmp_wv2, w_br_a, w_br_b, w_br_c,
           w_out, ln1_g, ln1_b, w_up, w_down, ln2_g, ln2_b):
    nb, seq, d_model = x.shape
    tables = _rope_tables(seq)
    xf = x.reshape(nb * seq, d_model)
    xb = xf.astype(BF16)
    aligned_cols = sum(heads for _, heads, _ in _ALIGNED_FIELDS) * HEAD_DIM
    w_rows = _input_weight_rows(w_in, aligned_cols, N_NSA_GATES)
    for l in range(w_in.shape[0]):
        xf, xb = _layer(l, xf, xb, nb, seq, tables, w_rows, cmp_pe_k[l], cmp_wk1[l], cmp_wk2[l],
                        cmp_pe_v[l], cmp_wv1[l], cmp_wv2[l], w_br_a[l], w_br_b[l], w_br_c[l],
                        w_out, ln1_g[l], ln1_b[l], w_up, w_down, ln2_g[l], ln2_b[l])
    return xf.reshape(nb, seq, d_model)
```

```python
import functools
import math

import numpy as np
import jax
import jax.numpy as jnp
from jax import lax
from jax.experimental import pallas as pl
from jax.experimental.pallas import tpu as pltpu

F32 = jnp.float32
BF16 = jnp.bfloat16

HEAD_DIM = 128
ROPE_THETA = 10000.0
LN_EPS = 1e-5
DEPTH = 2

DIL_GROUPS = ((128, 1), (512, 4), (2048, 16))
A_SLOTS = 4
A_HEADS = A_SLOTS * len(DIL_GROUPS)

B_HEADS = 12
B_KV_HEADS = 4
B_GROUP = B_HEADS // B_KV_HEADS
CMP_LEN = 32
CMP_STRIDE = 16
SEL_LEN = 64
SEL_SHIFT = 6
SEL_TOPK = 16
WIN_LEN = 512

C_HEADS = 8

MASKED = -3e38
ALPHA = (2.0 * DEPTH) ** 0.25
Q_SCALE = HEAD_DIM ** -0.5

VMEM_LIMIT = 48 * 1024 * 1024
MXU_COLS = 256
ATT_TQ = 128
ATT_TILES = 16
DIL_ROWS = 128
DIL_ROWS_DENSE = 1024
SEL_TQ = 128
SEL_TILES = 8
SEL_CHUNK = 1024
STICK_TILE = 512
STICK_HEADS = 8
CUM_TILE = 128


def _params(semantics):
    return pltpu.CompilerParams(dimension_semantics=semantics, vmem_limit_bytes=VMEM_LIMIT)


def _dot(a, b):
    return jnp.dot(a, b, preferred_element_type=F32)


def _dot_nt(a, b):
    return lax.dot_general(a, b, (((1,), (1,)), ((), ())), preferred_element_type=F32)


def _split_bf16(x):
    hi = x.astype(BF16)
    lo = (x - hi.astype(F32)).astype(BF16)
    return hi, lo


def _tile_flag(tile_modes, wanted):
    hits = [mode in wanted for mode in tile_modes]
    if all(hits) or not any(hits):
        return hits[0]
    j = pl.program_id(1)
    flag = None
    for c, hit in enumerate(hits):
        if hit:
            flag = (j == c) if flag is None else (flag | (j == c))
    return flag


def _write_columns(acc, o_ref, col0, tile_modes, head_major, cos_ref, sin_ref):
    is_rope = _tile_flag(tile_modes, ('rope', 'rope_scale'))
    is_scaled = _tile_flag(tile_modes, ('rope_scale', 'scale'))
    uniform = tile_modes[0] if len(set(tile_modes)) == 1 else None
    assert uniform is not None or not (set(tile_modes) & {'sigmoid', 'relu2'})

    def epilogue(y):
        if is_rope is not False:
            roped = y * cos_ref[...] + pltpu.roll(y, HEAD_DIM // 2, 1) * sin_ref[...]
            y = roped if is_rope is True else jnp.where(is_rope, roped, y)
        if is_scaled is not False:
            y = y * (Q_SCALE if is_scaled is True else jnp.where(is_scaled, Q_SCALE, 1.0))
        if uniform == 'sigmoid':
            y = jax.nn.sigmoid(y)
        if uniform == 'relu2':
            y = jnp.square(jnp.maximum(y, 0.0))
        return y

    width = acc.shape[1]
    if head_major:
        for c in range(width // HEAD_DIM):
            y = epilogue(acc[:, c * HEAD_DIM:(c + 1) * HEAD_DIM])
            o_ref[col0 // HEAD_DIM + c] = y.astype(o_ref.dtype)
    else:
        o_ref[:, col0:col0 + width] = epilogue(acc).astype(o_ref.dtype)


def _matmul_kernel(*refs, tile_modes, units, head_major, n_k, use_rope, acc_in_out, b_rows):
    refs = list(refs)
    a_ref, b_ref = refs[0], refs[1]
    cos_ref, sin_ref = (refs[2], refs[3]) if use_rope else (None, None)
    o_ref = refs[4] if use_rope else refs[2]
    acc_ref = o_ref if acc_in_out else (refs[-1] if n_k > 1 else None)
    tn = b_ref.shape[0] if b_rows else b_ref.shape[1]

    def product(col0, width):
        if b_rows:
            return _dot_nt(a_ref[...], b_ref[col0:col0 + width, :].astype(BF16))
        return _dot(a_ref[...], b_ref[:, col0:col0 + width].astype(BF16))

    if n_k == 1:
        unit_cols = tn // units
        width = min(MXU_COLS, unit_cols)
        for col0 in range(0, tn, width):
            per_tile = tile_modes[col0 // unit_cols::units]
            _write_columns(product(col0, width), o_ref, col0, per_tile, head_major, cos_ref, sin_ref)
        return
    assert units == 1

    @pl.when(pl.program_id(2) == 0)
    def _():
        acc_ref[...] = jnp.zeros(acc_ref.shape, F32)

    acc_ref[...] += product(0, tn)
    if not acc_in_out:
        pl.when(pl.program_id(2) == n_k - 1)(
            lambda: _write_columns(acc_ref[...], o_ref, 0, tile_modes, head_major, cos_ref, sin_ref))


def _matmul(a, b, *, name, out_dtype, tile_modes, tm, tn, tk=None, layer=None, col0=0,
            head_major=False, rope_tables=None, b_rows=False, mode_cols=None):
    m, kdim = a.shape
    mode_cols = tn if mode_cols is None else mode_cols
    n = mode_cols * len(tile_modes)
    assert tn % mode_cols == 0 and n % tn == 0
    tm = min(tm, m)
    tk = kdim if tk is None else min(tk, kdim)
    assert m % tm == 0 and kdim % tk == 0 and col0 % tn == 0
    n_k = kdim // tk
    col_tile0 = col0 // tn
    use_rope = any(mode.startswith('rope') for mode in tile_modes)
    acc_in_out = n_k > 1 and out_dtype == F32 and set(tile_modes) == {'none'} and not head_major
    if b_rows:
        assert layer is not None
        b_spec = pl.BlockSpec((None, tn, tk), lambda i, j, k: (layer, col_tile0 + j, k))
    elif layer is None:
        b_spec = pl.BlockSpec((tk, tn), lambda i, j, k: (k, col_tile0 + j))
    else:
        b_spec = pl.BlockSpec((None, tk, tn), lambda i, j, k: (layer, k, col_tile0 + j))
    in_specs = [pl.BlockSpec((tm, tk), lambda i, j, k: (i, k)), b_spec]
    operands = [a, b]
    if use_rope:
        assert head_major
        cos, sin = rope_tables
        seq_tiles = cos.shape[0] // tm
        assert cos.shape[0] % tm == 0
        spec = pl.BlockSpec((tm, HEAD_DIM), lambda i, j, k: (i % seq_tiles, 0))
        in_specs += [spec, spec]
        operands += [cos, sin]
    if head_major:
        out_shape = jax.ShapeDtypeStruct((n // HEAD_DIM, m, HEAD_DIM), out_dtype)
        out_spec = pl.BlockSpec((tn // HEAD_DIM, tm, HEAD_DIM), lambda i, j, k: (j, i, 0))
    else:
        out_shape = jax.ShapeDtypeStruct((m, n), out_dtype)
        out_spec = pl.BlockSpec((tm, tn), lambda i, j, k: (i, j))
    scratch = [pltpu.VMEM((tm, tn), F32)] if (n_k > 1 and not acc_in_out) else []
    return pl.pallas_call(
        functools.partial(_matmul_kernel, tile_modes=tuple(tile_modes), units=tn // mode_cols,
                          head_major=head_major, n_k=n_k, use_rope=use_rope, acc_in_out=acc_in_out,
                          b_rows=b_rows),
        grid=(m // tm, n // tn, n_k),
        in_specs=in_specs,
        out_specs=out_spec,
        out_shape=out_shape,
        scratch_shapes=scratch,
        compiler_params=_params(("parallel", "parallel", "arbitrary")),
        name=name,
    )(*operands)


def _gather_rows_kernel(w_ref, o_ref, *, tn, k_chunks, n_layers):
    stride = k_chunks * n_layers
    by_chunk = pltpu.einshape("nck->cnk", w_ref[...].reshape(tn, stride, HEAD_DIM))
    for layer in range(n_layers):
        pieces = [by_chunk[c * n_layers + layer] for c in range(k_chunks)]
        o_ref[layer] = jnp.concatenate(pieces, axis=1).astype(o_ref.dtype)


def _input_weight_rows(w_in, aligned_cols, n_gates, tn=HEAD_DIM):
    n_layers, kdim, n = w_in.shape
    k_chunks = kdim // HEAD_DIM
    tail = n - aligned_cols - n_gates
    assert aligned_cols % tn == 0 and tail % tn == 0 and kdim % HEAD_DIM == 0
    aligned_tiles, tail_tiles = aligned_cols // tn, tail // tn
    rows_per_col = k_chunks * n_layers
    flat = w_in.reshape(n_layers, k_chunks, HEAD_DIM, n).transpose(3, 1, 0, 2).reshape(n * rows_per_col, HEAD_DIM)

    def source_row(j):
        col = jnp.where(j < aligned_tiles, tn * j,
                        jnp.where(j < aligned_tiles + tail_tiles,
                                  aligned_cols + n_gates + tn * (j - aligned_tiles), aligned_cols))
        return (col * rows_per_col, 0)

    tiles = aligned_tiles + tail_tiles + 1
    return pl.pallas_call(
        functools.partial(_gather_rows_kernel, tn=tn, k_chunks=k_chunks, n_layers=n_layers),
        grid=(tiles,),
        in_specs=[pl.BlockSpec((pl.Element(tn * rows_per_col), pl.Element(HEAD_DIM)), source_row)],
        out_specs=pl.BlockSpec((n_layers, tn, kdim), lambda j: (0, j, 0)),
        out_shape=jax.ShapeDtypeStruct((n_layers, tiles * tn, kdim), BF16),
        compiler_params=_params(("parallel",)),
        name="input_weight_rows",
    )(flat)


def _window_start(t0, back, span, seq, align):
    start = jnp.minimum(jnp.maximum(t0 - back, 0), seq - span)
    return pl.multiple_of(start, align)


def _by_residue(x, dil):
    rows = x.shape[0] // dil
    return pltpu.einshape("ldk->dlk", x.reshape(rows, dil, HEAD_DIM))


def _dilated_group_kernel(*refs, dil, rows, band, n_rows, n_others):
    q_ref, k_ref, v_ref = refs[:3]
    others = refs[3:3 + 2 * n_others]
    if n_others:
        y_ref = refs[3 + 2 * n_others]
        kr_ref, vr_ref, o_ref, lse_ref = refs[4 + 2 * n_others:]
    else:
        o_ref, lse_ref = refs[3:5]
        kr_ref, vr_ref = refs[5:] if dil > 1 else (None, None)
    j = pl.program_id(2)
    if dil > 1:

        @pl.when(j == 0)
        def _():
            kr_ref[...] = _by_residue(k_ref[...], dil)
            vr_ref[...] = _by_residue(v_ref[...], dil)

        q_by = _by_residue(q_ref[...], dil)
    sub = min(band, rows) if dil == 1 else rows
    span = min(band + sub, n_rows)
    for t in range(rows // sub):
        l0 = j * rows + t * sub
        start = _window_start(l0, band, span, n_rows, math.gcd(sub, band))
        dist = (l0 + lax.broadcasted_iota(jnp.int32, (sub, 1), 0)) - (start + lax.broadcasted_iota(jnp.int32, (1, span), 1))
        keep = (dist >= 0) & (dist <= band)
        for r in range(dil):
            if dil == 1:
                q = q_ref[t * sub:(t + 1) * sub, :]
                keys, vals = k_ref[pl.ds(start, span), :], v_ref[pl.ds(start, span), :]
            else:
                q, keys, vals = q_by[r], kr_ref[r, pl.ds(start, span), :], vr_ref[r, pl.ds(start, span), :]
            s = jnp.where(keep, _dot_nt(q, keys), -jnp.inf)
            m = jnp.max(s, axis=1, keepdims=True)
            p = jnp.exp(s - m)
            l = jnp.sum(p, axis=1, keepdims=True)
            o = _dot(p.astype(BF16), vals) / l
            lse = jnp.broadcast_to(m + jnp.log(l), (sub, HEAD_DIM))
            if dil == 1:
                o_ref[t * sub:(t + 1) * sub, :] = o
                lse_ref[t * sub:(t + 1) * sub, :] = lse
            else:
                o_ref[pl.ds(r, rows, stride=dil), :] = o
                lse_ref[pl.ds(r, rows, stride=dil), :] = lse
    if n_others:
        outs = [others[2 * g][...] for g in range(n_others)] + [o_ref[...]]
        lses = [others[2 * g + 1][...] for g in range(n_others)] + [lse_ref[...]]
        top = functools.reduce(jnp.maximum, lses)
        ws = [jnp.exp(lse - top) for lse in lses]
        total = functools.reduce(lambda a, b: a + b, ws)
        mixed = functools.reduce(lambda a, b: a + b, [w * o for w, o in zip(ws, outs)])
        y_ref[...] = (mixed / total).astype(y_ref.dtype)


def _head_spec(rows, head0):
    return pl.BlockSpec((None, None, rows, HEAD_DIM), lambda b, h, n: (head0 + h, b, n, 0))


def _seq_spec(seq, head0):
    return pl.BlockSpec((None, None, seq, HEAD_DIM), lambda b, h, n: (head0 + h, b, 0, 0))


def _dilated_group(att, q_head0, k_head0, v_head0, window, dil, others=()):
    _, nb, seq, _ = att.shape
    n_rows = seq // dil
    band = window // dil
    rows = min(DIL_ROWS if dil > 1 else DIL_ROWS_DENSE, n_rows)
    assert seq % dil == 0 and n_rows % rows == 0 and (dil > 1 or not others)
    pair_spec = _head_spec(rows * dil, 0)
    pair_shape = jax.ShapeDtypeStruct((A_SLOTS, nb, seq, HEAD_DIM), F32)
    scratch = [pltpu.VMEM((dil, n_rows, HEAD_DIM), BF16)] * 2 if dil > 1 else []
    if others:
        out_specs = pl.BlockSpec((None, rows * dil, HEAD_DIM), lambda b, s, j: (b, j, s))
        out_shape = jax.ShapeDtypeStruct((nb, seq, A_SLOTS * HEAD_DIM), BF16)
        scratch = scratch + [pltpu.VMEM((rows * dil, HEAD_DIM), F32)] * 2
    else:
        out_specs, out_shape = [pair_spec, pair_spec], [pair_shape, pair_shape]
    return pl.pallas_call(
        functools.partial(_dilated_group_kernel, dil=dil, rows=rows, band=band, n_rows=n_rows,
                          n_others=len(others) // 2),
        grid=(nb, A_SLOTS, n_rows // rows),
        in_specs=[_head_spec(rows * dil, q_head0), _seq_spec(seq, k_head0), _seq_spec(seq, v_head0)]
        + [pair_spec] * len(others),
        out_specs=out_specs,
        out_shape=out_shape,
        scratch_shapes=scratch,
        compiler_params=_params(("parallel", "parallel", "arbitrary")),
        name="dilated_group_%d" % dil,
    )(att, att, att, *others)


def _dilated_mixer(att, q_head0, k_head0, v_head0):
    pairs = []
    for g, (window, dil) in enumerate(DIL_GROUPS):
        last = g == len(DIL_GROUPS) - 1
        result = _dilated_group(att, q_head0 + g * A_SLOTS, k_head0, v_head0, window, dil,
                                others=tuple(pairs) if last else ())
        if last:
            return result
        pairs += list(result)


def _gelu_tanh(x):
    return 0.5 * x * (1.0 + jnp.tanh(math.sqrt(2.0 / math.pi) * (x + 0.044715 * (x * x * x))))


def _compress_kernel(x_ref, pe_ref, w1_ref, w2_ref, o_ref):
    by_token = _by_residue(x_ref[...], CMP_STRIDE)
    x = jnp.concatenate([by_token[r] for r in range(CMP_STRIDE)], axis=1).astype(F32)
    first = _dot((x + pe_ref[0:1, :]).astype(BF16), w1_ref[0])
    second = _dot((x + pe_ref[1:2, :]).astype(BF16), w1_ref[1])
    chunks = x.shape[0]
    hidden = first + pltpu.roll(second, chunks - 1, 0)
    o_ref[...] = _dot(_gelu_tanh(hidden).astype(BF16), w2_ref[...]).astype(o_ref.dtype)


def _compress(t, head0, pe, w1, w2, name):
    _, nb, seq, _ = t.shape
    nh = B_KV_HEADS
    chunks = seq // CMP_STRIDE
    width = CMP_STRIDE * HEAD_DIM
    ratio = CMP_LEN // CMP_STRIDE
    return pl.pallas_call(
        _compress_kernel,
        grid=(nh, nb),
        in_specs=[pl.BlockSpec((None, None, seq, HEAD_DIM), lambda h, b: (head0 + h, b, 0, 0)),
                  pl.BlockSpec((ratio, width), lambda h, b: (0, 0)),
                  pl.BlockSpec((ratio, width, HEAD_DIM), lambda h, b: (0, 0, 0)),
                  pl.BlockSpec((HEAD_DIM, HEAD_DIM), lambda h, b: (0, 0))],
        out_specs=pl.BlockSpec((None, None, chunks, HEAD_DIM), lambda h, b: (h, b, 0, 0)),
        out_shape=jax.ShapeDtypeStruct((nh, nb, chunks, HEAD_DIM), BF16),
        compiler_params=_params(("parallel", "parallel")),
        name=name,
    )(t, pe.reshape(ratio, width), w1.reshape(ratio, width, HEAD_DIM).astype(BF16), w2.astype(BF16))


def _group_q(q_refs, row0=0, rows=None):
    rows = q_refs[0].shape[0] if rows is None else rows
    return jnp.concatenate([r[row0:row0 + rows, :] for r in q_refs], axis=0)


def _group_q_specs(tq, head0):
    return [pl.BlockSpec((None, None, tq, HEAD_DIM),
                         functools.partial(lambda b, h, n, g: (head0 + h * B_GROUP + g, b, n, 0), g=g))
            for g in range(B_GROUP)]


def _cmp_select_kernel(q0_ref, q1_ref, q2_ref, kc_ref, vc_ref, ov_ref, o_ref, sel_ref, *, tq, tiles):
    for u in range(tiles):
        _cmp_select_tile((q0_ref, q1_ref, q2_ref), kc_ref, vc_ref, ov_ref, o_ref, sel_ref,
                         (pl.program_id(2) * tiles + u) * tq, u * tq, tq)


def _cmp_select_tile(q_refs, kc_ref, vc_ref, ov_ref, o_ref, sel_ref, t0, row0, tq):
    n_cmp = kc_ref.shape[0]
    rows = B_GROUP * tq
    s = _dot_nt(_group_q(q_refs, row0, tq), kc_ref[...])
    tpos3 = t0 + (lax.broadcasted_iota(jnp.int32, (rows, 1), 0) & (tq - 1))
    c_end = lax.broadcasted_iota(jnp.int32, (1, n_cmp), 1) * CMP_STRIDE + (CMP_LEN - 1)
    s = jnp.where(c_end - tpos3 <= 0, s, -jnp.inf)
    m = jnp.max(s, axis=1, keepdims=True)
    m = jnp.where(jnp.abs(m) < jnp.inf, m, 0.0)
    e = jnp.exp(s - m)
    den = jnp.sum(e, axis=1, keepdims=True)
    p = e / jnp.where(den > 0, den, 1.0)
    o = _dot(p.astype(BF16), vc_ref[...]).reshape(B_GROUP, tq, HEAD_DIM)
    o_ref[:, row0:row0 + tq, :] = o.astype(o_ref.dtype)

    p_sum = p[0:tq] + p[tq:2 * tq] + p[2 * tq:3 * tq]
    hi, lo = _split_bf16(p_sum)
    imp = _dot(hi, ov_ref[...]) + _dot(lo, ov_ref[...])
    tpos = t0 + lax.broadcasted_iota(jnp.int32, (tq, 1), 0)
    rel = lax.broadcasted_iota(jnp.int32, (1, HEAD_DIM), 1) - (tpos >> SEL_SHIFT)
    j_abs = jnp.broadcast_to(lax.broadcasted_iota(jnp.int32, (1, HEAD_DIM), 1), rel.shape)
    forced = (j_abs == 0) | (rel == 0) | (rel == -1)
    imp = jnp.where(forced, jnp.inf, jnp.where(rel <= 0, imp, -jnp.inf))

    n_sel = HEAD_DIM // 2
    imp_t = imp.T
    mine = imp_t[0:n_sel]
    j_idx = lax.broadcasted_iota(jnp.int32, (n_sel, tq), 0)
    beaten = jnp.zeros((n_sel, tq), F32)
    for kk in range(n_sel):
        other = imp_t[kk:kk + 1, :]
        wins = (other > mine) | ((other == mine) & (j_idx > kk))
        beaten = beaten + jnp.where(wins, 1.0, 0.0)
    chosen = jnp.where(beaten < SEL_TOPK, 0.0, MASKED)
    chosen = jnp.concatenate([chosen, jnp.full((HEAD_DIM - n_sel, tq), MASKED, F32)], axis=0)
    sel_ref[:, row0:row0 + tq] = chosen.astype(sel_ref.dtype)


def _overlap_matrix(n_cmp_rows, n_sel):
    c_start = np.arange(n_cmp_rows) * CMP_STRIDE
    c_end = c_start + CMP_LEN - 1
    s_start = np.arange(HEAD_DIM) * SEL_LEN
    ov = (c_start[:, None] <= s_start[None, :] + SEL_LEN - 1) & (c_end[:, None] >= s_start[None, :])
    ov &= (np.arange(HEAD_DIM) < n_sel)[None, :]
    return jnp.asarray(ov.astype(np.float32), dtype=BF16)


def _group_o_spec(tq):
    return pl.BlockSpec((B_GROUP, None, tq, HEAD_DIM), lambda b, h, n: (h, b, n, 0))


def _sel_spec(tq):
    return pl.BlockSpec((None, None, HEAD_DIM, tq), lambda b, h, n: (h, b, 0, n))


def _cmp_select(q, q_head0, kc, vc):
    _, nb, seq, _ = q.shape
    assert seq // SEL_LEN <= HEAD_DIM // 2
    tq = min(ATT_TQ, seq)
    tiles = min(ATT_TILES, seq // tq)
    n_cmp = kc.shape[2]
    kv_spec = pl.BlockSpec((None, None, n_cmp, HEAD_DIM), lambda b, h, n: (h, b, 0, 0))
    return pl.pallas_call(
        functools.partial(_cmp_select_kernel, tq=tq, tiles=tiles),
        grid=(nb, B_KV_HEADS, seq // (tq * tiles)),
        in_specs=_group_q_specs(tq * tiles, q_head0) + [kv_spec, kv_spec,
                                                        pl.BlockSpec((n_cmp, HEAD_DIM), lambda b, h, n: (0, 0))],
        out_specs=[_group_o_spec(tq * tiles), _sel_spec(tq * tiles)],
        out_shape=[jax.ShapeDtypeStruct((B_HEADS, nb, seq, HEAD_DIM), BF16),
                   jax.ShapeDtypeStruct((B_KV_HEADS, nb, HEAD_DIM, seq), BF16)],
        compiler_params=_params(("parallel", "parallel", "arbitrary")),
        name="nsa_compressed_select",
    )(q, q, q, kc, vc, _overlap_matrix(n_cmp, seq // SEL_LEN))


def _selected_kernel(q0_ref, q1_ref, q2_ref, k_ref, v_ref, sel_ref, o_ref, vt_ref, acc_ref, *, tq, kc, tiles):
    n = pl.program_id(2)
    t_first = n * (tq * tiles)
    cols = B_GROUP * tq

    @pl.when(n == 0)
    def _():
        for c in range(v_ref.shape[0] // kc):
            vt_ref[c] = v_ref[c * kc:(c + 1) * kc, :].astype(F32).T.astype(BF16)

    qs = [_group_q((q0_ref, q1_ref, q2_ref), u * tq, tq) for u in range(tiles)]
    qposs = [t_first + u * tq + lax.broadcasted_iota(jnp.int32, (1, tq), 1) for u in range(tiles)]
    key_in_chunk = lax.broadcasted_iota(jnp.int32, (kc, 1), 0)
    blocks = kc // SEL_LEN
    acc_ref[...] = jnp.zeros(acc_ref.shape, F32)

    def chunk(i, carry, diagonal):
        k0 = pl.multiple_of(i * kc, kc)
        kpos = k0 + key_in_chunk
        keys = k_ref[pl.ds(k0, kc), :]
        blk0 = pl.multiple_of(i * blocks, blocks)
        out = []
        for u in range(tiles):
            m_old, l_old = carry[u]
            n_keys = tq * (u + 1) if (diagonal and tq * tiles == kc) else kc
            rows = sel_ref[pl.ds(blk0, blocks), u * tq:(u + 1) * tq].astype(F32)
            bias = jnp.broadcast_to(rows[:, None, :], (blocks, SEL_LEN, tq)).reshape(kc, tq)[:n_keys]
            if diagonal:
                bias = jnp.where(kpos[:n_keys] - qposs[u] <= 0, bias, MASKED)
            s = _dot_nt(keys[:n_keys], qs[u]) + jnp.concatenate([bias] * B_GROUP, axis=1)
            m_new = jnp.maximum(m_old, jnp.max(s, axis=0, keepdims=True))
            p = jnp.exp(s - m_new)
            alpha = jnp.exp(m_old - m_new)
            acc_ref[u] = alpha * acc_ref[u] + _dot(vt_ref[i, :, :n_keys], p.astype(BF16))
            out.append((m_new, alpha * l_old + jnp.sum(p, axis=0, keepdims=True)))
        return tuple(out)

    init = tuple((jnp.full((1, cols), MASKED, F32), jnp.zeros((1, cols), F32)) for _ in range(tiles))
    last = t_first // kc
    carry = lax.fori_loop(0, last, lambda i, c: chunk(i, c, False), init)
    final = chunk(last, carry, True)
    for u in range(tiles):
        o_t = acc_ref[u] / final[u][1]
        for g in range(B_GROUP):
            o_ref[g, u * tq:(u + 1) * tq, :] = o_t[:, g * tq:(g + 1) * tq].T.astype(o_ref.dtype)


def _selected(q, q_head0, k, k_head0, v, v_head0, sel):
    _, nb, seq, _ = q.shape
    tq = min(SEL_TQ, seq)
    kc = min(SEL_CHUNK, seq)
    tiles = min(SEL_TILES, kc // tq)
    assert kc % (tq * tiles) == 0 and seq % kc == 0 and tq == HEAD_DIM
    return pl.pallas_call(
        functools.partial(_selected_kernel, tq=tq, kc=kc, tiles=tiles),
        grid=(nb, B_KV_HEADS, seq // (tq * tiles)),
        in_specs=_group_q_specs(tq * tiles, q_head0) + [_seq_spec(seq, k_head0), _seq_spec(seq, v_head0),
                                                        _sel_spec(tq * tiles)],
        out_specs=_group_o_spec(tq * tiles),
        out_shape=jax.ShapeDtypeStruct((B_HEADS, nb, seq, HEAD_DIM), BF16),
        scratch_shapes=[pltpu.VMEM((seq // kc, HEAD_DIM, kc), BF16),
                        pltpu.VMEM((tiles, HEAD_DIM, B_GROUP * tq), F32)],
        compiler_params=_params(("parallel", "parallel", "arbitrary")),
        name="nsa_selected",
    )(q, q, q, k, v, sel)


def _window_kernel(q0_ref, q1_ref, q2_ref, k_ref, v_ref, o_ref, *, tq, tiles, seq):
    rows = B_GROUP * tq
    span = min(WIN_LEN + tq, seq)
    for u in range(tiles):
        t0 = (pl.program_id(2) * tiles + u) * tq
        start = _window_start(t0, WIN_LEN, span, seq, tq)
        s = _dot_nt(_group_q((q0_ref, q1_ref, q2_ref), u * tq, tq), k_ref[pl.ds(start, span), :])
        qpos = t0 + (lax.broadcasted_iota(jnp.int32, (rows, 1), 0) & (tq - 1))
        kpos = start + lax.broadcasted_iota(jnp.int32, (1, span), 1)
        dist = qpos - kpos
        s = jnp.where((dist >= 0) & (dist <= WIN_LEN - 1), s, -jnp.inf)
        m = jnp.max(s, axis=1, keepdims=True)
        p = jnp.exp(s - m)
        l = jnp.sum(p, axis=1, keepdims=True)
        o = _dot(p.astype(BF16), v_ref[pl.ds(start, span), :]) / l
        o_ref[:, u * tq:(u + 1) * tq, :] = o.reshape(B_GROUP, tq, HEAD_DIM).astype(o_ref.dtype)


def _window(q, q_head0, k, k_head0, v, v_head0):
    _, nb, seq, _ = q.shape
    tq = min(ATT_TQ, seq)
    tiles = min(ATT_TILES, seq // tq)
    assert WIN_LEN % tq == 0
    return pl.pallas_call(
        functools.partial(_window_kernel, tq=tq, tiles=tiles, seq=seq),
        grid=(nb, B_KV_HEADS, seq // (tq * tiles)),
        in_specs=_group_q_specs(tq * tiles, q_head0) + [_seq_spec(seq, k_head0), _seq_spec(seq, v_head0)],
        out_specs=_group_o_spec(tq * tiles),
        out_shape=jax.ShapeDtypeStruct((B_HEADS, nb, seq, HEAD_DIM), BF16),
        compiler_params=_params(("parallel", "parallel", "arbitrary")),
        name="nsa_window",
    )(q, q, q, k, v)


def _nsa_gate_kernel(g_ref, oc_ref, os_ref, ow_ref, y_ref):
    g = g_ref[...]
    for h in range(B_HEADS):
        y = (g[:, 3 * h:3 * h + 1] * oc_ref[h].astype(F32)
             + g[:, 3 * h + 1:3 * h + 2] * os_ref[h].astype(F32)
             + g[:, 3 * h + 2:3 * h + 3] * ow_ref[h].astype(F32))
        y_ref[:, h * HEAD_DIM:(h + 1) * HEAD_DIM] = y.astype(y_ref.dtype)


def _nsa_gate(gates, o_cmp, o_sel, o_win):
    rows = gates.shape[0]
    tm = min(256, rows)
    o_spec = pl.BlockSpec((B_HEADS, tm, HEAD_DIM), lambda i: (0, i, 0))
    return pl.pallas_call(
        _nsa_gate_kernel,
        grid=(rows // tm,),
        in_specs=[pl.BlockSpec((tm, HEAD_DIM), lambda i: (i, 0)), o_spec, o_spec, o_spec],
        out_specs=pl.BlockSpec((tm, B_HEADS * HEAD_DIM), lambda i: (i, 0)),
        out_shape=jax.ShapeDtypeStruct((rows, B_HEADS * HEAD_DIM), BF16),
        compiler_params=_params(("parallel",)),
        name="nsa_gate",
    )(gates, o_cmp, o_sel, o_win)


def _stick_kernel(*refs, tq, heads):
    q_refs, k_refs, v_refs = refs[:heads], refs[heads:2 * heads], refs[2 * heads:3 * heads]
    o_ref, acc_ref, car_ref = refs[3 * heads:]
    n = pl.program_id(2)
    sub = min(CUM_TILE, tq)
    n_sub = tq // sub
    r_idx = lax.broadcasted_iota(jnp.int32, (2 * sub, 2 * sub), 0) & (sub - 1)
    c_idx = lax.broadcasted_iota(jnp.int32, (2 * sub, 2 * sub), 1)
    suffix_and_total = jnp.where((c_idx >= sub) | (r_idx > c_idx), 1.0, 0.0).astype(BF16)
    k_in_sub = lax.broadcasted_iota(jnp.int32, (1, sub), 1)
    acc_ref[...] = jnp.zeros(acc_ref.shape, F32)
    car_ref[...] = jnp.zeros(car_ref.shape, F32)

    q_in_tile = lax.broadcasted_iota(jnp.int32, (tq, 1), 0)

    def tile(k0, diagonal):
        for h in range(heads):
            z = _dot_nt(q_refs[h][...], k_refs[h][pl.ds(k0, tq), :])
            soft = jnp.log(1.0 + jnp.exp(-jnp.abs(z)))
            log_beta = jnp.minimum(z, 0.0) - soft
            log_rest = log_beta - z
            carried = car_ref[h]
            parts = [None] * n_sub
            for u in reversed(range(n_sub)):
                cols = slice(u * sub, (u + 1) * sub)
                rest_u = log_rest[:, cols]
                if diagonal:
                    before = (u * sub + k_in_sub) - q_in_tile < 0
                    rest_u = jnp.where(before, rest_u, 0.0)
                sums = _dot(jnp.concatenate(_split_bf16(rest_u), axis=1), suffix_and_total)
                a = jnp.exp(log_beta[:, cols] + (sums[:, :sub] + carried))
                if diagonal:
                    a = jnp.where(before, a, 0.0)
                carried = carried + sums[:, sub:]
                parts[u] = a.astype(BF16)
            car_ref[h] = carried
            acc_ref[h] += _dot(jnp.concatenate(parts, axis=1), v_refs[h][pl.ds(k0, tq), :])

    tile(pl.multiple_of(n * tq, tq), True)

    def step(i, carry):
        tile(pl.multiple_of((n - 1 - i) * tq, tq), False)
        return carry

    lax.fori_loop(0, n, step, 0)
    for h in range(heads):
        o_ref[:, h * HEAD_DIM:(h + 1) * HEAD_DIM] = acc_ref[h].astype(o_ref.dtype)


def _stick_breaking(q, q_head0, k, k_head0, v, v_head0):
    _, nb, seq, _ = q.shape
    tq = min(STICK_TILE, seq)
    heads = STICK_HEADS
    sub = min(CUM_TILE, tq)

    def per_head(make, head0):
        return [make(head0 + h) for h in range(heads)]

    def q_spec(head):
        return pl.BlockSpec((None, None, tq, HEAD_DIM), lambda b, g, n: (head + g * heads, b, n, 0))

    def kv_spec(head):
        return pl.BlockSpec((None, None, seq, HEAD_DIM), lambda b, g, n: (head + g * heads, b, 0, 0))

    return pl.pallas_call(
        functools.partial(_stick_kernel, tq=tq, heads=heads),
        grid=(nb, C_HEADS // heads, seq // tq),
        in_specs=per_head(q_spec, q_head0) + per_head(kv_spec, k_head0) + per_head(kv_spec, v_head0),
        out_specs=pl.BlockSpec((None, tq, heads * HEAD_DIM), lambda b, g, n: (b, n, g)),
        out_shape=jax.ShapeDtypeStruct((nb, seq, C_HEADS * HEAD_DIM), BF16),
        scratch_shapes=[pltpu.VMEM((heads, tq, HEAD_DIM), F32), pltpu.VMEM((heads, tq, sub), F32)],
        compiler_params=_params(("parallel", "parallel", "arbitrary")),
        name="stick_breaking",
    )(*([q] * heads + [k] * heads + [v] * heads))


def _merge_kernel(ya_ref, gn_ref, oc_ref, os_ref, ow_ref, yc_ref, wa_ref, wb_ref, wc_ref,
                  ga_ref, gb_ref, gc_ref, o_ref, yb_ref):
    @pl.when(pl.program_id(1) == 0)
    def _():
        _nsa_gate_kernel(gn_ref, oc_ref, os_ref, ow_ref, yb_ref)

    mixed = (ga_ref[...].astype(F32) * _dot(ya_ref[...], wa_ref[...])
             + gb_ref[...].astype(F32) * _dot(yb_ref[...], wb_ref[...])
             + gc_ref[...].astype(F32) * _dot(yc_ref[...], wc_ref[...]))
    o_ref[...] = mixed.astype(o_ref.dtype)


def _merge(ya, g_nsa, o_cmp, o_sel, o_win, yc, wa, wb, wc, gates, d_model):
    m = ya.shape[0]
    tm, tn = min(512, m), 1024
    col_tiles = d_model // tn

    def y_spec(y):
        return pl.BlockSpec((tm, y.shape[1]), lambda i, j: (i, 0))

    def w_spec(w):
        return pl.BlockSpec((w.shape[0], tn), lambda i, j: (0, j))

    def g_spec(branch):
        return pl.BlockSpec((tm, tn), lambda i, j: (i, branch * col_tiles + j))

    o_spec = pl.BlockSpec((B_HEADS, tm, HEAD_DIM), lambda i, j: (0, i, 0))
    return pl.pallas_call(
        _merge_kernel,
        grid=(m // tm, col_tiles),
        in_specs=[y_spec(ya), y_spec(g_nsa), o_spec, o_spec, o_spec, y_spec(yc),
                  w_spec(wa), w_spec(wb), w_spec(wc), g_spec(0), g_spec(1), g_spec(2)],
        out_specs=pl.BlockSpec((tm, tn), lambda i, j: (i, j)),
        out_shape=jax.ShapeDtypeStruct((m, d_model), BF16),
        scratch_shapes=[pltpu.VMEM((tm, B_HEADS * HEAD_DIM), BF16)],
        compiler_params=_params(("parallel", "arbitrary")),
        name="branch_merge",
    )(ya, g_nsa, o_cmp, o_sel, o_win, yc, wa, wb, wc, gates, gates, gates)


def _ln_kernel(x_ref, y_ref, g_ref, b_ref, o_ref, ob_ref):
    z = ALPHA * x_ref[...] + y_ref[...]
    mu = jnp.mean(z, axis=1, keepdims=True)
    zc = z - mu
    var = jnp.mean(zc * zc, axis=1, keepdims=True)
    out = zc * lax.rsqrt(var + LN_EPS) * g_ref[...] + b_ref[...]
    o_ref[...] = out
    ob_ref[...] = out.astype(BF16)


def _residual_ln(x, y, g, b):
    m, d = x.shape
    tm = min(256, m)
    row = pl.BlockSpec((tm, d), lambda i: (i, 0))
    vec = pl.BlockSpec((1, d), lambda i: (0, 0))
    return pl.pallas_call(
        _ln_kernel,
        grid=(m // tm,),
        in_specs=[row, row, vec, vec],
        out_specs=[row, row],
        out_shape=[jax.ShapeDtypeStruct((m, d), F32), jax.ShapeDtypeStruct((m, d), BF16)],
        compiler_params=_params(("parallel",)),
        name="residual_layer_norm",
    )(x, y, g.reshape(1, d), b.reshape(1, d))


def _rope_tables(seq):
    half = HEAD_DIM // 2
    inv = 1.0 / (ROPE_THETA ** (jnp.arange(half, dtype=F32) / half))
    ang = jnp.arange(seq).astype(F32)[:, None] * inv[None, :]
    cos, sin = jnp.cos(ang), jnp.sin(ang)
    return jnp.concatenate([cos, cos], axis=1), jnp.concatenate([-sin, sin], axis=1)


_A_KV, _B_KV = A_SLOTS, B_KV_HEADS
_ALIGNED_FIELDS = (('q_a', A_HEADS, 'rope_scale'), ('k_a', _A_KV, 'rope'), ('v_a', _A_KV, 'none'),
                   ('q_b', B_HEADS, 'rope_scale'), ('kc_b', _B_KV, 'rope'), ('vc_b', _B_KV, 'none'),
                   ('ks_b', _B_KV, 'rope'), ('vs_b', _B_KV, 'none'), ('kw_b', _B_KV, 'rope'),
                   ('vw_b', _B_KV, 'none'))
_STICK_FIELDS = (('q_c', C_HEADS, 'scale'), ('k_c', C_HEADS, 'none'), ('v_c', C_HEADS, 'none'))
N_NSA_GATES = 3 * B_HEADS


def _field_layout(fields, heads_per_tile):
    head0, modes, start = {}, [], 0
    for name, heads, mode in fields:
        assert heads % heads_per_tile == 0
        head0[name] = start
        modes += [mode] * (heads // heads_per_tile)
        start += heads
    return head0, tuple(modes), start


def _layer(layer, x, xb, nb, seq, tables, w_rows, cmp_pe_k, cmp_wk1, cmp_wk2, cmp_pe_v, cmp_wv1, cmp_wv2,
           w_br_a, w_br_b, w_br_c, w_out, ln1_g, ln1_b, w_up, w_down, ln2_g, ln2_b):
    m, d_model = x.shape

    att_mode_cols = 4 * HEAD_DIM
    at, att_modes, att_heads = _field_layout(_ALIGNED_FIELDS, att_mode_cols // HEAD_DIM)
    att = _matmul(xb, w_rows, layer=layer, b_rows=True, name="proj_attention", out_dtype=BF16,
                  tile_modes=att_modes, mode_cols=att_mode_cols, tm=1024, tn=1024, head_major=True,
                  rope_tables=tables)
    tn_tail = 1024
    st, stick_modes, stick_heads = _field_layout(_STICK_FIELDS, tn_tail // HEAD_DIM)
    stick_col0 = att_heads * HEAD_DIM
    gates_col0 = stick_col0 + stick_heads * HEAD_DIM
    nsa_col0 = gates_col0 + 3 * d_model
    stick = _matmul(xb, w_rows, layer=layer, b_rows=True, col0=stick_col0, name="proj_stick",
                    out_dtype=BF16, tile_modes=stick_modes, tm=1024, tn=tn_tail, head_major=True)
    gates = _matmul(xb, w_rows, layer=layer, b_rows=True, col0=gates_col0, name="proj_branch_gates",
                    out_dtype=BF16, tile_modes=('sigmoid',) * (3 * d_model // tn_tail), tm=1024, tn=tn_tail)
    g_nsa = _matmul(xb, w_rows, layer=layer, b_rows=True, col0=nsa_col0, name="proj_nsa_gates",
                    out_dtype=F32, tile_modes=('sigmoid',), tm=1024, tn=HEAD_DIM)

    att = att.reshape(att_heads, nb, seq, HEAD_DIM)
    stick = stick.reshape(stick_heads, nb, seq, HEAD_DIM)

    y_a = _dilated_mixer(att, at['q_a'], at['k_a'], at['v_a'])

    kc = _compress(att, at['kc_b'], cmp_pe_k, cmp_wk1, cmp_wk2, "nsa_compress_k")
    vc = _compress(att, at['vc_b'], cmp_pe_v, cmp_wv1, cmp_wv2, "nsa_compress_v")
    o_cmp, sel = _cmp_select(att, at['q_b'], kc, vc)
    o_sel = _selected(att, at['q_b'], att, at['ks_b'], att, at['vs_b'], sel)
    o_win = _window(att, at['q_b'], att, at['kw_b'], att, at['vw_b'])
    y_c = _stick_breaking(stick, st['q_c'], stick, st['k_c'], stick, st['v_c'])

    merged = _merge(y_a.reshape(m, -1), g_nsa, o_cmp.reshape(B_HEADS, m, HEAD_DIM),
                    o_sel.reshape(B_HEADS, m, HEAD_DIM), o_win.reshape(B_HEADS, m, HEAD_DIM),
                    y_c.reshape(m, -1), w_br_a.astype(BF16), w_br_b.astype(BF16), w_br_c.astype(BF16),
                    gates, d_model)
    mixed = _matmul(merged, w_out, layer=layer, name="out_proj", out_dtype=F32,
                    tile_modes=('none',) * (d_model // 512), tm=1024, tn=512)
    x1, x1b = _residual_ln(x, mixed, ln1_g, ln1_b)

    d_ff = w_up.shape[2]
    hidden = _matmul(x1b, w_up, layer=layer, name="mlp_up", out_dtype=BF16,
                     tile_modes=('relu2',) * (d_ff // 512), tm=1024, tn=512)
    down = _matmul(hidden, w_down, layer=layer, name="mlp_down", out_dtype=F32,
                   tile_modes=('none',) * (d_model // 1024), tm=2048, tn=1024, tk=1024)
    return _residual_ln(x1, down, ln2_g, ln2_b)


def kernel(x, w_in, cmp_pe_k, cmp_wk1, cmp_wk2, cmp_pe_v, cmp_wv1, cmp_wv2, w_br_a, w_br_b, w_br_c,
           w_out, ln1_g, ln1_b, w_up, w_down, ln2_g, ln2_b):
    nb, seq, d_model = x.shape
    tables = _rope_tables(seq)
    xf = x.reshape(nb * seq, d_model)
    xb = xf.astype(BF16)
    aligned_cols = sum(heads for _, heads, _ in _ALIGNED_FIELDS) * HEAD_DIM
    w_rows = _input_weight_rows(w_in, aligned_cols, N_NSA_GATES)
    for l in range(w_in.shape[0]):
        xf, xb = _layer(l, xf, xb, nb, seq, tables, w_rows, cmp_pe_k[l], cmp_wk1[l], cmp_wk2[l],
                        cmp_pe_v[l], cmp_wv1[l], cmp_wv2[l], w_br_a[l], w_br_b[l], w_br_c[l],
                        w_out, ln1_g[l], ln1_b[l], w_up, w_down, ln2_g[l], ln2_b[l])
    return xf.reshape(nb, seq, d_model)
```
